```python
import math
import jax, jax.numpy as jnp
from jax import lax
import numpy as np

D_MODEL = 1024
BATCH = 8
SEQ = 8192
DEPTH = 1

ATTN_HEADS = 8
ATTN_KV_HEADS = 2
ATTN_HEAD_DIM = 64
Q_PER_KV = ATTN_HEADS // ATTN_KV_HEADS
WINDOW = 128
ATTN_BLOCK = WINDOW
ATTN_WIDTH = ATTN_HEADS * ATTN_HEAD_DIM
KV_WIDTH = ATTN_KV_HEADS * ATTN_HEAD_DIM
HGRN_HEADS = 4
HGRN_KEY_DIM = 128
HGRN_VAL_DIM = 128
HGRN_WIDTH = HGRN_HEADS * HGRN_VAL_DIM
HGRN_KEY_WIDTH = HGRN_HEADS * HGRN_KEY_DIM
HGRN_CHUNK = 32
MIX_WIDTH = ATTN_WIDTH + HGRN_WIDTH
IN_SPLITS = [ATTN_WIDTH, KV_WIDTH, KV_WIDTH, HGRN_KEY_WIDTH, HGRN_KEY_WIDTH, HGRN_WIDTH, HGRN_WIDTH]
IN_WIDTH = sum(IN_SPLITS)
N_GROUPS = 4
EXPERTS_PER_GROUP = 8
N_EXPERTS = N_GROUPS * EXPERTS_PER_GROUP
TOP_K = 2
EXPERT_FF = D_MODEL // 4

NORM_EPS = 1e-6
NEG_INF = -1e30

kernel_name = "hymba_swa_sink_hgrn2_hmoe_layer"


def rmsnorm(x, g):
    xf = x.astype(jnp.float32)
    y = xf * lax.rsqrt(jnp.mean(xf * xf, axis=-1, keepdims=True) + NORM_EPS)
    return (y * g.astype(jnp.float32)).astype(x.dtype)


def swa_sink_alibi(q, k, v, sinks):
    B, T, _ = q.shape
    NB = T // ATTN_BLOCK
    qb = q.reshape(B, NB, ATTN_BLOCK, ATTN_KV_HEADS, Q_PER_KV, ATTN_HEAD_DIM)
    pad = ((0, 0), (ATTN_BLOCK, 0), (0, 0))
    kp = jnp.pad(k, pad).reshape(B, NB + 1, ATTN_BLOCK, ATTN_KV_HEADS, ATTN_HEAD_DIM)
    vp = jnp.pad(v, pad).reshape(B, NB + 1, ATTN_BLOCK, ATTN_KV_HEADS, ATTN_HEAD_DIM)
    kb = jnp.concatenate([kp[:, :-1], kp[:, 1:]], axis=2)
    vb = jnp.concatenate([vp[:, :-1], vp[:, 1:]], axis=2)
    scale = 1.0 / math.sqrt(ATTN_HEAD_DIM)
    s = jnp.einsum('bnqgrd,bnkgd->bngrqk', qb, kb, preferred_element_type=jnp.float32) * scale
    qi = jnp.arange(ATTN_BLOCK)[:, None]
    kj = jnp.arange(2 * ATTN_BLOCK)[None, :]
    dist = qi + ATTN_BLOCK - kj
    in_win = (dist >= 0) & (dist < WINDOW)
    not_pad = (jnp.arange(NB)[:, None, None] > 0) | (kj >= ATTN_BLOCK)[None]
    valid = in_win[None] & not_pad
    slopes = jnp.exp2(-8.0 * (jnp.arange(ATTN_HEADS, dtype=jnp.float32) + 1.0) / ATTN_HEADS)
    slopes = slopes.reshape(ATTN_KV_HEADS, Q_PER_KV)[:, :, None, None]
    s = s - slopes * dist.astype(jnp.float32)
    s = jnp.where(valid[None, :, None, None], s, NEG_INF)
    sink = sinks.astype(jnp.float32).reshape(ATTN_KV_HEADS, Q_PER_KV)[:, :, None, None]
    m = jnp.maximum(jnp.max(s, axis=-1, keepdims=True), sink)
    p = jnp.exp(s - m)
    p = p / (jnp.sum(p, axis=-1, keepdims=True) + jnp.exp(sink - m))
    o = jnp.einsum('bngrqk,bnkgd->bnqgrd', p.astype(v.dtype), vb)
    return o.reshape(B, T, ATTN_WIDTH)


def hgrn2(q_raw, f_raw, i_raw, lb):
    B, T, _ = q_raw.shape
    NC = T // HGRN_CHUNK
    shp_k = (B, NC, HGRN_CHUNK, HGRN_HEADS, HGRN_KEY_DIM)
    q = jax.nn.silu(q_raw.astype(jnp.float32)).reshape(shp_k)
    lbf = lb.astype(jnp.float32)
    f = lbf + (1.0 - lbf) * jax.nn.sigmoid(f_raw.astype(jnp.float32))
    k = (1.0 - f).reshape(shp_k)
    logf = jnp.log(f).reshape(shp_k)
    v = i_raw.astype(jnp.float32).reshape(B, NC, HGRN_CHUNK, HGRN_HEADS, HGRN_VAL_DIM)
    b = jnp.cumsum(logf, axis=2)
    b_last = b[:, :, -1:]
    q_dec = q * jnp.exp(b)
    k_dec = k * jnp.exp(-b)
    k_end = k * jnp.exp(b_last - b)
    causal = jnp.tril(jnp.ones((HGRN_CHUNK, HGRN_CHUNK), dtype=bool))
    a = jnp.einsum('bnchk,bnshk->bnhcs', q_dec, k_dec)
    a = jnp.where(causal, a, 0.0)
    o_intra = jnp.einsum('bnhcs,bnshv->bnchv', a, v)

    def step(S, inp):
        qd, ke, vv, decay = inp
        o = jnp.einsum('bchk,bhkv->bchv', qd, S)
        S = S * decay[..., None] + jnp.einsum('bchk,bchv->bhkv', ke, vv)
        return S, o

    S0 = jnp.zeros((B, HGRN_HEADS, HGRN_KEY_DIM, HGRN_VAL_DIM), jnp.float32)
    xs = (jnp.moveaxis(q_dec, 1, 0), jnp.moveaxis(k_end, 1, 0), jnp.moveaxis(v, 1, 0),
          jnp.moveaxis(jnp.exp(b_last[:, :, 0]), 1, 0))
    _, o_inter = lax.scan(step, S0, xs)
    o = o_intra + jnp.moveaxis(o_inter, 0, 1)
    return o.reshape(B, T, HGRN_HEADS, HGRN_VAL_DIM).astype(q_raw.dtype)


def hierarchical_moe(h, w_rg, b_rg, w_re, b_re, w_gate, w_up, w_down):
    B, T, D = h.shape
    tok = h.reshape(-1, D)
    N = tok.shape[0]
    g_prob = jax.nn.softmax((tok @ w_rg + b_rg).astype(jnp.float32), axis=-1)
    g_w, g_idx = lax.top_k(g_prob, 1)
    e_logits = (tok @ w_re + b_re).astype(jnp.float32).reshape(N, N_GROUPS, EXPERTS_PER_GROUP)
    e_sel = e_logits[jnp.arange(N), g_idx[:, 0]]
    e_w, e_idx = lax.top_k(jax.nn.softmax(e_sel, axis=-1), TOP_K)
    e_w = e_w / jnp.sum(e_w, axis=-1, keepdims=True)
    weights = (g_w * e_w).astype(h.dtype)
    expert = (g_idx * EXPERTS_PER_GROUP + e_idx).reshape(-1)
    order = jnp.argsort(expert)
    inv = jnp.argsort(order)
    xs = tok[order // TOP_K]
    sizes = jnp.bincount(expert, length=N_EXPERTS).astype(jnp.int32)
    hg = lax.ragged_dot(xs, w_gate, sizes)
    hu = lax.ragged_dot(xs, w_up, sizes)
    ys = lax.ragged_dot(jax.nn.silu(hg) * hu, w_down, sizes)
    ys = ys[inv].reshape(N, TOP_K, D)
    out = jnp.einsum('nkd,nk->nd', ys, weights)
    return out.reshape(B, T, D)


def setup_inputs(seed: int = 0) -> dict:
    key = jax.random.key(seed)
    ks = jax.random.split(key, 24)
    f32 = jnp.float32
    L, D = DEPTH, D_MODEL

    def nrm(k, shape, scale):
        return jax.random.normal(k, shape, f32) * scale

    return {
        "x": nrm(ks[0], (BATCH, SEQ, D), 1.0),
        "c": nrm(ks[1], (BATCH, D), 1.0),
        "ln1_pre": 1.0 + nrm(ks[2], (L, D), 0.02),
        "ln1_post": 1.0 + nrm(ks[3], (L, D), 0.02),
        "ln2_pre": 1.0 + nrm(ks[4], (L, D), 0.02),
        "ln2_post": 1.0 + nrm(ks[5], (L, D), 0.02),
        "w_ada": nrm(ks[6], (L, D, 6 * D), 0.25 * D ** -0.5),
        "b_ada": nrm(ks[7], (L, 6 * D), 0.01),
        "w_in": nrm(ks[8], (L, D, IN_WIDTH), D ** -0.5),
        "attn_sinks": nrm(ks[9], (L, ATTN_HEADS), 0.5),
        "attn_out_norm": 1.0 + nrm(ks[10], (L, ATTN_WIDTH), 0.02),
        "hgrn_lb": nrm(ks[11], (L + 1, HGRN_KEY_WIDTH), 0.1),
        "hgrn_out_norm": 1.0 + nrm(ks[12], (L, HGRN_WIDTH), 0.02),
        "w_out": nrm(ks[13], (L, MIX_WIDTH, D), MIX_WIDTH ** -0.5),
        "w_router_group": nrm(ks[14], (L, D, N_GROUPS), D ** -0.5),
        "b_router_group": nrm(ks[15], (L, N_GROUPS), 0.01),
        "w_router_expert": nrm(ks[16], (L, D, N_EXPERTS), D ** -0.5),
        "b_router_expert": nrm(ks[17], (L, N_EXPERTS), 0.01),
        "w_exp_gate": nrm(ks[18], (L, N_EXPERTS, D, EXPERT_FF), D ** -0.5),
        "w_exp_up": nrm(ks[19], (L, N_EXPERTS, D, EXPERT_FF), D ** -0.5),
        "w_exp_down": nrm(ks[20], (L, N_EXPERTS, EXPERT_FF, D), EXPERT_FF ** -0.5),
    }


def reference(x, c, ln1_pre, ln1_post, ln2_pre, ln2_post, w_ada, b_ada, w_in, attn_sinks,
              attn_out_norm, hgrn_lb, hgrn_out_norm, w_out, w_router_group, b_router_group,
              w_router_expert, b_router_expert, w_exp_gate, w_exp_up, w_exp_down):
    B, T, D = x.shape
    lb_all = jnp.cumsum(jax.nn.softmax(hgrn_lb.astype(jnp.float32), axis=0), axis=0)
    offsets = list(np.cumsum(IN_SPLITS)[:-1])
    c_act = jax.nn.silu(c)
    for l in range(DEPTH):
        mod = (c_act @ w_ada[l] + b_ada[l])[:, None, :]
        sh1, sc1, ga1, sh2, sc2, ga2 = jnp.split(mod, 6, axis=-1)
        h = rmsnorm(x, ln1_pre[l]) * (1.0 + sc1) + sh1
        proj = h @ w_in[l]
        q_a, k_a, v_a, q_h, f_h, i_h, g_h = jnp.split(proj, offsets, axis=-1)
        attn = rmsnorm(swa_sink_alibi(q_a, k_a, v_a, attn_sinks[l]), attn_out_norm[l])
        hg = hgrn2(q_h, f_h, i_h, lb_all[l])
        hg = rmsnorm(hg, hgrn_out_norm[l].reshape(HGRN_HEADS, HGRN_VAL_DIM)).reshape(B, T, HGRN_WIDTH)
        hg = hg * jax.nn.silu(g_h)
        mix = jnp.concatenate([attn, hg], axis=-1) @ w_out[l]
        x = x + ga1 * rmsnorm(mix, ln1_post[l])
        h = rmsnorm(x, ln2_pre[l]) * (1.0 + sc2) + sh2
        y = hierarchical_moe(h, w_router_group[l], b_router_group[l], w_router_expert[l],
                             b_router_expert[l], w_exp_gate[l], w_exp_up[l], w_exp_down[l])
        x = x + ga2 * rmsnorm(y, ln2_post[l])
    return x
```

```python
import functools

import numpy as np
import jax
import jax.numpy as jnp
from jax import lax
from jax.experimental import pallas as pl
from jax.experimental.pallas import tpu as pltpu

F32 = jnp.float32
BF16 = jnp.bfloat16
I32 = jnp.int32

D_MODEL = 1024
ATTN_HEADS = 8
HEAD_DIM = 64
WINDOW = 128
ATTN_W = 512
KV_W = 128
HG_HEADS = 4
HG_DIM = 128
HG_W = 512
HG_CHUNK = 32
IN_W = 2816
N_GROUPS = 4
EPG = 8
N_EXPERTS = 32
FF = 256
N_PAIRS = EPG * (EPG - 1) // 2
N_BUCKETS = N_GROUPS * N_PAIRS
EPS = 1e-6
NEG = -1e30

LANES = 128
EXT_W = LANES
ROW_W = D_MODEL + EXT_W
ROUTER_ROWS = 8 + N_EXPERTS

TQ = 256
TM = 128
VMEM_LIMIT = 56 * 1024 * 1024

OFF_Q, OFF_K, OFF_V, OFF_HQ, OFF_HF, OFF_HI, OFF_HG = 0, 512, 640, 768, 1280, 1792, 2304

_PAIR_A = np.array([a for a in range(EPG) for b in range(a + 1, EPG)], np.int32)
_PAIR_B = np.array([b for a in range(EPG) for b in range(a + 1, EPG)], np.int32)


def _dot(a, b):
    return jnp.dot(a, b, preferred_element_type=F32)


def _dot_nt(a, b):
    return lax.dot_general(a, b, (((1,), (1,)), ((), ())), preferred_element_type=F32)


def _dot_tn(a, b):
    return lax.dot_general(a, b, (((0,), (0,)), ((), ())), preferred_element_type=F32)


def _split(a):
    hi = a.astype(BF16)
    lo = (a - hi.astype(F32)).astype(BF16)
    return hi, lo


def _rms(x):
    return x * lax.rsqrt(jnp.mean(x * x, axis=-1, keepdims=True) + EPS)


def _sigmoid(x):
    return 1.0 / (1.0 + jnp.exp(-x))


def _ada_kernel(c_ref, w_ref, b_ref, o_ref):
    c = c_ref[...]
    ca = c * _sigmoid(c)
    c_hi, c_lo = _split(ca)
    w_hi, w_lo = _split(w_ref[...])
    o_ref[...] = _dot(c_hi, w_hi) + _dot(c_lo, w_hi) + _dot(c_hi, w_lo) + b_ref[...]


def _ada(c, w, b):
    bsz, d = c.shape
    n_out = w.shape[1]
    return pl.pallas_call(
        _ada_kernel,
        grid=(n_out // d,),
        in_specs=[pl.BlockSpec((bsz, d), lambda j: (0, 0)),
                  pl.BlockSpec((d, d), lambda j: (0, j)),
                  pl.BlockSpec((1, d), lambda j: (0, j))],
        out_specs=pl.BlockSpec((bsz, d), lambda j: (0, j)),
        out_shape=jax.ShapeDtypeStruct((bsz, n_out), F32),
        compiler_params=pltpu.CompilerParams(dimension_semantics=("arbitrary",),
                                             vmem_limit_bytes=VMEM_LIMIT),
        name="adaln",
    )(c, w, b.reshape(1, n_out))


def _attention(proj, kprev_ref, vprev_ref, sinks_ref, t):
    tq = proj.shape[0]
    q = (proj[:, OFF_Q:OFF_Q + ATTN_W] * (HEAD_DIM ** -0.5)).astype(BF16)
    kf = jnp.concatenate([kprev_ref[...], proj[:, OFF_K:OFF_K + KV_W]], axis=0)
    vf = jnp.concatenate([vprev_ref[...], proj[:, OFF_V:OFF_V + KV_W]], axis=0)
    kprev_ref[...] = proj[tq - WINDOW:, OFF_K:OFF_K + KV_W]
    vprev_ref[...] = proj[tq - WINDOW:, OFF_V:OFF_V + KV_W]

    lo = lax.broadcasted_iota(I32, kf.shape, 1) < HEAD_DIM
    kr = pltpu.roll(kf, HEAD_DIM, axis=1)
    vr = pltpu.roll(vf, HEAD_DIM, axis=1)

    def variants(a, ar):
        return [[jnp.where(lo, a, 0.0).astype(BF16), jnp.where(lo, 0.0, ar).astype(BF16)],
                [jnp.where(lo, ar, 0.0).astype(BF16), jnp.where(lo, 0.0, a).astype(BF16)]]

    kvar = variants(kf, kr)
    vvar = variants(vf, vr)

    qi = lax.broadcasted_iota(I32, (WINDOW, 2 * WINDOW), 0)
    kj = lax.broadcasted_iota(I32, (WINDOW, 2 * WINDOW), 1)
    dist = qi + WINDOW - kj
    in_win = (dist >= 0) & (dist < WINDOW)
    distf = dist.astype(F32)
    kj_first = kj + jnp.where(t > 0, WINDOW, 0)

    blocks = []
    for j in range(tq // WINDOW):
        valid = (in_win & (kj_first >= WINDOW)) if j == 0 else in_win
        pairs = []
        for p in range(ATTN_HEADS // 2):
            g = p // 2
            qp = q[j * WINDOW:(j + 1) * WINDOW, p * LANES:(p + 1) * LANES]
            acc = jnp.zeros((WINDOW, LANES), F32)
            for par in range(2):
                h = 2 * p + par
                slope = 2.0 ** (-8.0 * (h + 1) / ATTN_HEADS)
                s = _dot_nt(qp, kvar[g][par][j * WINDOW:(j + 2) * WINDOW])
                s = jnp.where(valid, s - slope * distf, NEG)
                sink = sinks_ref[h]
                m = jnp.maximum(jnp.max(s, axis=-1, keepdims=True), sink)
                pe = jnp.exp(s - m)
                den = jnp.sum(pe, axis=-1, keepdims=True) + jnp.exp(sink - m)
                pn = (pe / den).astype(BF16)
                acc = acc + _dot(pn, vvar[g][par][j * WINDOW:(j + 2) * WINDOW])
            pairs.append(acc)
        blocks.append(jnp.concatenate(pairs, axis=1))
    return jnp.concatenate(blocks, axis=0)


def _hgrn2(proj, lb, st_ref, hnorm):
    tq = proj.shape[0]
    nc = tq // HG_CHUNK
    qr = proj[:, OFF_HQ:OFF_HQ + HG_W]
    fr = proj[:, OFF_HF:OFF_HF + HG_W]
    iv = proj[:, OFF_HI:OFF_HI + HG_W]
    gr = proj[:, OFF_HG:OFF_HG + HG_W]
    qh = qr * _sigmoid(qr)
    f = lb + (1.0 - lb) * _sigmoid(fr)
    kk = 1.0 - f
    logf = jnp.log(f)

    rmod = lax.broadcasted_iota(I32, (tq, HG_W), 0) & (HG_CHUNK - 1)
    bc = logf
    s = 1
    while s < HG_CHUNK:
        bc = bc + jnp.where(rmod >= s, pltpu.roll(bc, s, axis=0), 0.0)
        s *= 2

    b3 = bc.reshape(nc, HG_CHUNK, HG_W)
    blast = b3[:, HG_CHUNK - 1:HG_CHUNK, :]
    kend = (kk.reshape(nc, HG_CHUNK, HG_W) * jnp.exp(blast - b3)).reshape(tq, HG_W)
    decay = jnp.exp(blast).reshape(nc, HG_W)
    qdec = (qh * jnp.exp(bc)).astype(BF16)
    kdec = (kk * jnp.exp(-bc)).astype(BF16)
    kend = kend.astype(BF16)
    ivb = iv.astype(BF16)

    ri = lax.broadcasted_iota(I32, (tq, tq), 0)
    ci = lax.broadcasted_iota(I32, (tq, tq), 1)
    cmask = ((ri // HG_CHUNK) == (ci // HG_CHUNK)) & (ri >= ci)

    outs = []
    for hh in range(HG_HEADS):
        sl = slice(hh * HG_DIM, (hh + 1) * HG_DIM)
        a = _dot_nt(qdec[:, sl], kdec[:, sl])
        a = jnp.where(cmask, a, 0.0).astype(BF16)
        o_intra = _dot(a, ivb[:, sl])
        st = st_ref[hh]
        inter = []
        for n in range(nc):
            rs = slice(n * HG_CHUNK, (n + 1) * HG_CHUNK)
            inter.append(_dot_nt(qdec[rs, sl], st.astype(BF16)))
            u = _dot_tn(ivb[rs, sl], kend[rs, sl])
            st = st * decay[n:n + 1, sl] + u
        st_ref[hh] = st
        o = o_intra + jnp.concatenate(inter, axis=0)
        o = _rms(o) * hnorm[:, sl]
        g = gr[:, sl]
        outs.append(o * (g * _sigmoid(g)))
    return jnp.concatenate(outs, axis=1)


def _route(h2, wr_hi, wr_lo, br, carry_ref, bidx):
    tq = h2.shape[0]
    h_hi, h_lo = _split(h2)
    lt = _dot_nt(wr_hi, h_hi) + _dot_nt(wr_hi, h_lo) + _dot_nt(wr_lo, h_hi) + br
    sub = lax.broadcasted_iota(I32, (8, tq), 0).astype(F32)

    gl = lt[0:8]
    gm = jnp.max(gl, axis=0, keepdims=True)
    gidx = jnp.min(jnp.where(gl == gm, sub, 8.0), axis=0, keepdims=True)
    g_w = 1.0 / jnp.sum(jnp.exp(gl - gm), axis=0, keepdims=True)

    es = lt[8:16]
    for g in range(1, N_GROUPS):
        es = jnp.where(gidx == float(g), lt[8 + 8 * g:16 + 8 * g], es)
    m1 = jnp.max(es, axis=0, keepdims=True)
    i1 = jnp.min(jnp.where(es == m1, sub, 8.0), axis=0, keepdims=True)
    e2 = jnp.where(sub == i1, NEG, es)
    m2 = jnp.max(e2, axis=0, keepdims=True)
    i2 = jnp.min(jnp.where(e2 == m2, sub, 8.0), axis=0, keepdims=True)
    dd = jnp.exp(m2 - m1)
    w1 = g_w / (1.0 + dd)
    w2 = g_w * dd / (1.0 + dd)
    first_low = i1 < i2
    ea = jnp.minimum(i1, i2)
    eb = jnp.maximum(i1, i2)
    w_lo = jnp.where(first_low, w1, w2)
    w_hi = jnp.where(first_low, w2, w1)
    pair = ea * (15.0 - ea) * 0.5 + (eb - ea - 1.0)
    bucket = gidx * float(N_PAIRS) + pair

    brow = lax.broadcasted_iota(I32, (LANES, tq), 0).astype(F32)
    onehot = brow == bucket
    oh = jnp.where(onehot, 1.0, 0.0)
    ti = lax.broadcasted_iota(I32, (tq, tq), 0)
    tj = lax.broadcasted_iota(I32, (tq, tq), 1)
    upper = jnp.where(ti < tj, 1.0, 0.0).astype(BF16)
    before = _dot(oh.astype(BF16), upper) + carry_ref[...]
    rank = jnp.sum(jnp.where(onehot, before, 0.0), axis=0, keepdims=True)
    carry_ref[...] = carry_ref[...] + jnp.sum(oh, axis=1, keepdims=True)

    info = jnp.concatenate([w_lo, w_hi, jnp.full((1, tq), bidx, F32),
                            jnp.zeros((LANES - 3, tq), F32)], axis=0)
    return bucket, rank, info.T


def _mixer_kernel(sinks_ref, x_ref, mod_ref, ln1pre_ref, ln1post_ref, ln2pre_ref, anorm_ref, hnorm_ref,
                  lb_ref, win_ref, wout_ref, wrhi_ref, wrlo_ref, br_ref,
                  x1_ref, info_ref, cnt_ref,
                  kprev_ref, vprev_ref, st_ref, carry_ref):
    b = pl.program_id(0)
    t = pl.program_id(1)

    @pl.when(t == 0)
    def _():
        st_ref[...] = jnp.zeros_like(st_ref)
        kprev_ref[...] = jnp.zeros_like(kprev_ref)
        vprev_ref[...] = jnp.zeros_like(vprev_ref)

    @pl.when((t == 0) & (b == 0))
    def _():
        carry_ref[...] = jnp.zeros_like(carry_ref)

    x = x_ref[...]
    mod = mod_ref[0]
    sh1, sc1, ga1, sh2, sc2 = mod[0:1], mod[1:2], mod[2:3], mod[3:4], mod[4:5]

    h = _rms(x) * ln1pre_ref[...] * (1.0 + sc1) + sh1
    proj = _dot(h.astype(BF16), win_ref[...])

    attn = _rms(_attention(proj, kprev_ref, vprev_ref, sinks_ref, t)) * anorm_ref[...]

    lbr = lb_ref[...]
    le = jnp.exp(lbr - jnp.max(lbr, axis=0, keepdims=True))
    lb = le[0:1] / jnp.sum(le, axis=0, keepdims=True)
    hg = _hgrn2(proj, lb, st_ref, hnorm_ref[...])

    mix = _dot(jnp.concatenate([attn, hg], axis=1).astype(BF16), wout_ref[...])
    x1 = x + ga1 * (_rms(mix) * ln1post_ref[...])

    h2 = _rms(x1) * ln2pre_ref[...] * (1.0 + sc2) + sh2
    bucket, rank, ext = _route(h2, wrhi_ref[...], wrlo_ref[...], br_ref[...], carry_ref, b.astype(F32))

    x1_ref[:, 0:D_MODEL] = x1
    x1_ref[:, D_MODEL:ROW_W] = ext
    tq = x.shape[0]
    info_ref[0] = jnp.concatenate([bucket, rank, jnp.zeros((6, tq), F32)], axis=0).astype(I32)
    cnt_ref[...] = jnp.broadcast_to(carry_ref[...], cnt_ref.shape)


def _mixer(x2, mod3, sinks, ln1pre, ln1post, ln2pre, anorm, hnorm, lb, win, wout, wr_hi, wr_lo, br,
           bsz, seq):
    n = bsz * seq
    nt = seq // TQ
    tok = lambda b, t: (b * nt + t, 0)
    const = lambda b, t: (0, 0)
    full = lambda a: pl.BlockSpec(a.shape, const)
    return pl.pallas_call(
        _mixer_kernel,
        grid=(bsz, nt),
        in_specs=[pl.BlockSpec(memory_space=pltpu.SMEM),
                  pl.BlockSpec((TQ, D_MODEL), tok),
                  pl.BlockSpec((1, 6, D_MODEL), lambda b, t: (b, 0, 0)),
                  full(ln1pre), full(ln1post), full(ln2pre), full(anorm), full(hnorm), full(lb),
                  full(win), full(wout), full(wr_hi), full(wr_lo), full(br)],
        out_specs=[pl.BlockSpec((TQ, ROW_W), tok),
                   pl.BlockSpec((1, 8, TQ), lambda b, t: (b * nt + t, 0, 0)),
                   pl.BlockSpec((LANES, LANES), const)],
        out_shape=[jax.ShapeDtypeStruct((n, ROW_W), F32),
                   jax.ShapeDtypeStruct((n // TQ, 8, TQ), I32),
                   jax.ShapeDtypeStruct((LANES, LANES), F32)],
        scratch_shapes=[pltpu.VMEM((WINDOW, KV_W), F32),
                        pltpu.VMEM((WINDOW, KV_W), F32),
                        pltpu.VMEM((HG_HEADS, HG_DIM, HG_DIM), F32),
                        pltpu.VMEM((LANES, 1), F32)],
        compiler_params=pltpu.CompilerParams(dimension_semantics=("arbitrary", "arbitrary"),
                                             vmem_limit_bytes=VMEM_LIMIT),
        name="mixer",
    )(sinks, x2, mod3, ln1pre, ln1post, ln2pre, anorm, hnorm, lb, win, wout, wr_hi, wr_lo, br)


PERM_STEPS = 64


def _perm_kernel(pos_hbm, perm_ref, chunk_ref, sem):
    pid = pl.program_id(0)
    n_init = perm_ref.shape[0] // PERM_STEPS
    n_tok = chunk_ref.shape[0]

    @pl.when(pid < PERM_STEPS)
    def _():
        def init(i, carry):
            perm_ref[pid * n_init + i] = 0
            return carry

        lax.fori_loop(0, n_init, init, 0)

    @pl.when(pid >= PERM_STEPS)
    def _():
        c = pid - PERM_STEPS
        cp = pltpu.make_async_copy(pos_hbm.at[c], chunk_ref, sem)
        cp.start()
        cp.wait()

        def body(i, carry):
            perm_ref[chunk_ref[i]] = c * n_tok + i
            return carry

        lax.fori_loop(0, n_tok, body, 0)


def _perm(pos, n_rows):
    n = pos.shape[0]
    assert n % PERM_STEPS == 0 and n_rows % PERM_STEPS == 0
    return pl.pallas_call(
        _perm_kernel,
        grid=(2 * PERM_STEPS,),
        in_specs=[pl.BlockSpec(memory_space=pl.ANY)],
        out_specs=pl.BlockSpec(memory_space=pltpu.SMEM),
        out_shape=jax.ShapeDtypeStruct((n_rows,), I32),
        scratch_shapes=[pltpu.SMEM((n // PERM_STEPS,), I32), pltpu.SemaphoreType.DMA(())],
        compiler_params=pltpu.CompilerParams(dimension_semantics=("arbitrary",)),
        name="perm",
    )(pos.reshape(PERM_STEPS, n // PERM_STEPS))


def _moe_kernel(nt_ref, nv_ref, ea_ref, eb_ref, perm_ref,
                x1_hbm, mod_ref, ln2pre_ref, ln2post_ref, wgu_a_ref, wd_a_ref, wgu_b_ref, wd_b_ref,
                out_hbm, xbuf, obuf, gsem, ssem):
    i = pl.program_id(0)
    nt = nt_ref[0]
    slot = i % 2

    def gather(tile, sl):
        for r in range(TM):
            tok = perm_ref[tile * TM + r]
            pltpu.make_async_copy(x1_hbm.at[pl.ds(tok, 1)], xbuf.at[sl, pl.ds(r, 1)], gsem.at[sl]).start()

    def wait_gather(sl):
        pltpu.make_async_copy(x1_hbm.at[pl.ds(0, TM)], xbuf.at[sl], gsem.at[sl]).wait()

    def wait_scatter(sl, nv):
        @pl.when(nv == TM)
        def _():
            pltpu.make_async_copy(obuf.at[sl], out_hbm.at[pl.ds(0, TM)], ssem.at[sl]).wait()

        @pl.when(nv < TM)
        def _():
            def one(r, carry):
                pltpu.make_async_copy(obuf.at[sl, pl.ds(0, 1)], out_hbm.at[pl.ds(0, 1)], ssem.at[sl]).wait()
                return carry

            lax.fori_loop(0, nv, one, 0)

    @pl.when(i == 0)
    def _():
        gather(0, 0)

    @pl.when(i + 1 < nt)
    def _():
        gather(i + 1, 1 - slot)

    @pl.when(i < nt)
    def _():
        nv = nv_ref[i]
        wait_gather(slot)

        @pl.when(i >= 2)
        def _():
            wait_scatter(slot, nv_ref[jnp.maximum(i - 2, 0)])

        xb = xbuf[slot]
        x1 = xb[:, 0:D_MODEL]
        ext = xb[:, D_MODEL:ROW_W]
        w_lo, w_hi, bf = ext[:, 0:1], ext[:, 1:2], ext[:, 2:3]
        mod = mod_ref[...]
        nb = mod.shape[1]

        def pick(k):
            acc = jnp.zeros((TM, D_MODEL), F32)
            for bb in range(nb):
                acc = jnp.where(bf == float(bb), mod[k, bb:bb + 1], acc)
            return acc

        sh2, sc2, ga2 = pick(0), pick(1), pick(2)
        h2 = (_rms(x1) * ln2pre_ref[...] * (1.0 + sc2) + sh2).astype(BF16)

        def expert(wgu_ref, wd_ref):
            gu = _dot(h2, wgu_ref[0])
            hg, hu = gu[:, 0:FF], gu[:, FF:2 * FF]
            act = (hg * _sigmoid(hg)) * hu
            return _dot(act.astype(BF16), wd_ref[0])

        y = w_lo * expert(wgu_a_ref, wd_a_ref) + w_hi * expert(wgu_b_ref, wd_b_ref)
        obuf[slot] = x1 + ga2 * (_rms(y) * ln2post_ref[...])

        for r in range(TM):
            @pl.when(r < nv)
            def _():
                tok = perm_ref[i * TM + r]
                pltpu.make_async_copy(obuf.at[slot, pl.ds(r, 1)], out_hbm.at[pl.ds(tok, 1)],
                                      ssem.at[slot]).start()

        @pl.when(i == nt - 1)
        def _():
            wait_scatter(slot, nv)

            @pl.when(i >= 1)
            def _():
                wait_scatter(1 - slot, nv_ref[jnp.maximum(i - 1, 0)])


def _moe(x1ext, mod2, ln2pre, ln2post, wgu, wd, nt, nv, ea, eb, perm, n_tiles):
    n = x1ext.shape[0]
    const2 = lambda i, *_: (0, 0)
    grid_spec = pltpu.PrefetchScalarGridSpec(
        num_scalar_prefetch=5,
        grid=(n_tiles,),
        in_specs=[pl.BlockSpec(memory_space=pl.ANY),
                  pl.BlockSpec(mod2.shape, lambda i, *_: (0, 0, 0)),
                  pl.BlockSpec(ln2pre.shape, const2),
                  pl.BlockSpec(ln2post.shape, const2),
                  pl.BlockSpec((1, D_MODEL, 2 * FF), lambda i, nt, nv, ea, eb, perm: (ea[i], 0, 0)),
                  pl.BlockSpec((1, FF, D_MODEL), lambda i, nt, nv, ea, eb, perm: (ea[i], 0, 0)),
                  pl.BlockSpec((1, D_MODEL, 2 * FF), lambda i, nt, nv, ea, eb, perm: (eb[i], 0, 0)),
                  pl.BlockSpec((1, FF, D_MODEL), lambda i, nt, nv, ea, eb, perm: (eb[i], 0, 0))],
        out_specs=pl.BlockSpec(memory_space=pl.ANY),
        scratch_shapes=[pltpu.VMEM((2, TM, ROW_W), F32),
                        pltpu.VMEM((2, TM, D_MODEL), F32),
                        pltpu.SemaphoreType.DMA((2,)),
                        pltpu.SemaphoreType.DMA((2,))],
    )
    return pl.pallas_call(
        _moe_kernel,
        grid_spec=grid_spec,
        out_shape=jax.ShapeDtypeStruct((n, D_MODEL), F32),
        compiler_params=pltpu.CompilerParams(dimension_semantics=("arbitrary",),
                                             vmem_limit_bytes=VMEM_LIMIT),
        name="moe",
    )(nt, nv, ea, eb, perm, x1ext, mod2, ln2pre, ln2post, wgu, wd, wgu, wd)


def kernel(x, c, ln1_pre, ln1_post, ln2_pre, ln2_post, w_ada, b_ada, w_in, attn_sinks, attn_out_norm,
           hgrn_lb, hgrn_out_norm, w_out, w_router_group, b_router_group, w_router_expert,
           b_router_expert, w_exp_gate, w_exp_up, w_exp_down):
    bsz, seq, d = x.shape
    assert d == D_MODEL and seq % TQ == 0 and w_ada.shape[0] == 1 and hgrn_lb.shape[0] == 2
    n = bsz * seq

    mod = _ada(c, w_ada[0], b_ada[0])
    mod3 = mod.reshape(bsz, 6, d)

    wr = jnp.concatenate([w_router_group[0].T, jnp.zeros((8 - N_GROUPS, d), F32), w_router_expert[0].T], axis=0)
    br = jnp.concatenate([b_router_group[0], jnp.full((8 - N_GROUPS,), NEG, F32), b_router_expert[0]])
    wr_hi = wr.astype(BF16)
    wr_lo = (wr - wr_hi.astype(F32)).astype(BF16)

    x1ext, info, cnt = _mixer(
        x.reshape(n, d), mod3, attn_sinks[0], ln1_pre, ln1_post, ln2_pre, attn_out_norm, hgrn_out_norm,
        hgrn_lb, w_in[0].astype(BF16), w_out[0].astype(BF16), wr_hi, wr_lo, br.reshape(ROUTER_ROWS, 1),
        bsz, seq)

    n_tiles = n // TM + N_BUCKETS
    counts = cnt[:N_BUCKETS, 0].astype(I32)
    tiles_per = (counts + TM - 1) // TM
    tile_end = jnp.cumsum(tiles_per)
    tile_start = tile_end - tiles_per
    nt = tile_end[-1]
    bucket = info[:, 0, :].reshape(n)
    rank = info[:, 1, :].reshape(n)
    pos = tile_start[bucket] * TM + rank
    tid = jnp.arange(n_tiles, dtype=I32)
    tb = jnp.searchsorted(tile_end, tid, side="right").astype(I32)
    tb = jnp.where(tid < nt, tb, tb[nt - 1])
    nv = jnp.where(tid < nt, jnp.clip(counts[tb] - (tid - tile_start[tb]) * TM, 0, TM), 0).astype(I32)
    grp, pair = tb // N_PAIRS, tb % N_PAIRS
    ea = (grp * EPG + jnp.asarray(_PAIR_A)[pair]).astype(I32)
    eb = (grp * EPG + jnp.asarray(_PAIR_B)[pair]).astype(I32)

    perm = _perm(pos, n_tiles * TM)

    wgu = jnp.concatenate([w_exp_gate[0], w_exp_up[0]], axis=-1).astype(BF16)
    wd = w_exp_down[0].astype(BF16)
    mod2 = jnp.transpose(mod3[:, 3:6, :], (1, 0, 2))
    out = _moe(x1ext, mod2, ln2_pre, ln2_post, wgu, wd, nt.reshape(1), nv, ea, eb, perm, n_tiles)
    return out.reshape(bsz, seq, d)
```

```python
import functools

import numpy as np
import jax
import jax.numpy as jnp
from jax import lax
from jax.experimental import pallas as pl
from jax.experimental.pallas import tpu as pltpu

F32 = jnp.float32
BF16 = jnp.bfloat16
I32 = jnp.int32

D_MODEL = 1024
ATTN_HEADS = 8
HEAD_DIM = 64
WINDOW = 128
ATTN_W = 512
KV_W = 128
HG_HEADS = 4
HG_DIM = 128
HG_W = 512
HG_CHUNK = 32
IN_W = 2816
N_GROUPS = 4
EPG = 8
N_EXPERTS = 32
FF = 256
N_PAIRS = EPG * (EPG - 1) // 2
N_BUCKETS = N_GROUPS * N_PAIRS
EPS = 1e-6
NEG = -1e30

LANES = 128
EXT_W = LANES
ROW_W = D_MODEL + EXT_W
ROUTER_ROWS = 8 + N_EXPERTS

TQ = 256
TM = 128
VMEM_LIMIT = 56 * 1024 * 1024

OFF_Q, OFF_K, OFF_V, OFF_HQ, OFF_HF, OFF_HI, OFF_HG = 0, 512, 640, 768, 1280, 1792, 2304

_PAIR_A = np.array([a for a in range(EPG) for b in range(a + 1, EPG)], np.int32)
_PAIR_B = np.array([b for a in range(EPG) for b in range(a + 1, EPG)], np.int32)


def _dot(a, b):
    return jnp.dot(a, b, preferred_element_type=F32)


def _dot_nt(a, b):
    return lax.dot_general(a, b, (((1,), (1,)), ((), ())), preferred_element_type=F32)


def _dot_tn(a, b):
    return lax.dot_general(a, b, (((0,), (0,)), ((), ())), preferred_element_type=F32)


def _split(a):
    hi = a.astype(BF16)
    lo = (a - hi.astype(F32)).astype(BF16)
    return hi, lo


def _rms(x):
    return x * lax.rsqrt(jnp.mean(x * x, axis=-1, keepdims=True) + EPS)


def _sigmoid(x):
    return 1.0 / (1.0 + jnp.exp(-x))


def _ada_kernel(c_ref, w_ref, b_ref, o_ref):
    c = c_ref[...]
    ca = c * _sigmoid(c)
    c_hi, c_lo = _split(ca)
    w_hi, w_lo = _split(w_ref[...])
    o_ref[...] = _dot(c_hi, w_hi) + _dot(c_lo, w_hi) + _dot(c_hi, w_lo) + b_ref[...]


def _ada(c, w, b):
    bsz, d = c.shape
    n_out = w.shape[1]
    return pl.pallas_call(
        _ada_kernel,
        grid=(n_out // d,),
        in_specs=[pl.BlockSpec((bsz, d), lambda j: (0, 0)),
                  pl.BlockSpec((d, d), lambda j: (0, j)),
                  pl.BlockSpec((1, d), lambda j: (0, j))],
        out_specs=pl.BlockSpec((bsz, d), lambda j: (0, j)),
        out_shape=jax.ShapeDtypeStruct((bsz, n_out), F32),
        compiler_params=pltpu.CompilerParams(dimension_semantics=("arbitrary",),
                                             vmem_limit_bytes=VMEM_LIMIT),
        name="adaln",
    )(c, w, b.reshape(1, n_out))


def _attention(proj, kprev_ref, vprev_ref, sinks_ref, t):
    tq = proj.shape[0]
    q = (proj[:, OFF_Q:OFF_Q + ATTN_W] * (HEAD_DIM ** -0.5)).astype(BF16)
    kf = jnp.concatenate([kprev_ref[...], proj[:, OFF_K:OFF_K + KV_W]], axis=0)
    vf = jnp.concatenate([vprev_ref[...], proj[:, OFF_V:OFF_V + KV_W]], axis=0)
    kprev_ref[...] = proj[tq - WINDOW:, OFF_K:OFF_K + KV_W]
    vprev_ref[...] = proj[tq - WINDOW:, OFF_V:OFF_V + KV_W]

    lo = lax.broadcasted_iota(I32, kf.shape, 1) < HEAD_DIM
    kr = pltpu.roll(kf, HEAD_DIM, axis=1)
    vr = pltpu.roll(vf, HEAD_DIM, axis=1)

    def variants(a, ar):
        return [[jnp.where(lo, a, 0.0).astype(BF16), jnp.where(lo, 0.0, ar).astype(BF16)],
                [jnp.where(lo, ar, 0.0).astype(BF16), jnp.where(lo, 0.0, a).astype(BF16)]]

    kvar = variants(kf, kr)
    vvar = variants(vf, vr)

    qi = lax.broadcasted_iota(I32, (WINDOW, 2 * WINDOW), 0)
    kj = lax.broadcasted_iota(I32, (WINDOW, 2 * WINDOW), 1)
    dist = qi + WINDOW - kj
    in_win = (dist >= 0) & (dist < WINDOW)
    distf = dist.astype(F32)
    kj_first = kj + jnp.where(t > 0, WINDOW, 0)

    blocks = []
    for j in range(tq // WINDOW):
        valid = (in_win & (kj_first >= WINDOW)) if j == 0 else in_win
        pairs = []
        for p in range(ATTN_HEADS // 2):
            g = p // 2
            qp = q[j * WINDOW:(j + 1) * WINDOW, p * LANES:(p + 1) * LANES]
            acc = jnp.zeros((WINDOW, LANES), F32)
            for par in range(2):
                h = 2 * p + par
                slope = 2.0 ** (-8.0 * (h + 1) / ATTN_HEADS)
                s = _dot_nt(qp, kvar[g][par][j * WINDOW:(j + 2) * WINDOW])
                s = jnp.where(valid, s - slope * distf, NEG)
                sink = sinks_ref[h]
                m = jnp.maximum(jnp.max(s, axis=-1, keepdims=True), sink)
                pe = jnp.exp(s - m)
                den = jnp.sum(pe, axis=-1, keepdims=True) + jnp.exp(sink - m)
                pn = (pe / den).astype(BF16)
                acc = acc + _dot(pn, vvar[g][par][j * WINDOW:(j + 2) * WINDOW])
            pairs.append(acc)
        blocks.append(jnp.concatenate(pairs, axis=1))
    return jnp.concatenate(blocks, axis=0)


def _hgrn2(proj, lb, st_ref, hnorm):
    tq = proj.shape[0]
    nc = tq // HG_CHUNK
    qr = proj[:, OFF_HQ:OFF_HQ + HG_W]
    fr = proj[:, OFF_HF:OFF_HF + HG_W]
    iv = proj[:, OFF_HI:OFF_HI + HG_W]
    gr = proj[:, OFF_HG:OFF_HG + HG_W]
    qh = qr * _sigmoid(qr)
    f = lb + (1.0 - lb) * _sigmoid(fr)
    kk = 1.0 - f
    logf = jnp.log(f)

    rmod = lax.broadcasted_iota(I32, (tq, HG_W), 0) & (HG_CHUNK - 1)
    bc = logf
    s = 1
    while s < HG_CHUNK:
        bc = bc + jnp.where(rmod >= s, pltpu.roll(bc, s, axis=0), 0.0)
        s *= 2

    b3 = bc.reshape(nc, HG_CHUNK, HG_W)
    blast = b3[:, HG_CHUNK - 1:HG_CHUNK, :]
    kend = (kk.reshape(nc, HG_CHUNK, HG_W) * jnp.exp(blast - b3)).reshape(tq, HG_W)
    decay = jnp.exp(blast).reshape(nc, HG_W)
    qdec = (qh * jnp.exp(bc)).astype(BF16)
    kdec = (kk * jnp.exp(-bc)).astype(BF16)
    kend = kend.astype(BF16)
    ivb = iv.astype(BF16)

    ri = lax.broadcasted_iota(I32, (tq, tq), 0)
    ci = lax.broadcasted_iota(I32, (tq, tq), 1)
    cmask = ((ri // HG_CHUNK) == (ci // HG_CHUNK)) & (ri >= ci)

    outs = []
    for hh in range(HG_HEADS):
        sl = slice(hh * HG_DIM, (hh + 1) * HG_DIM)
        a = _dot_nt(qdec[:, sl], kdec[:, sl])
        a = jnp.where(cmask, a, 0.0).astype(BF16)
        o_intra = _dot(a, ivb[:, sl])
        st = st_ref[hh]
        inter = []
        for n in range(nc):
            rs = slice(n * HG_CHUNK, (n + 1) * HG_CHUNK)
            inter.append(_dot_nt(qdec[rs, sl], st.astype(BF16)))
            u = _dot_tn(ivb[rs, sl], kend[rs, sl])
            st = st * decay[n:n + 1, sl] + u
        st_ref[hh] = st
        o = o_intra + jnp.concatenate(inter, axis=0)
        o = _rms(o) * hnorm[:, sl]
        g = gr[:, sl]
        outs.append(o * (g * _sigmoid(g)))
    return jnp.concatenate(outs, axis=1)


def _route(h2, wr_hi, wr_lo, br, carry_ref, bidx):
    tq = h2.shape[0]
    h_hi, h_lo = _split(h2)
    lt = _dot_nt(wr_hi, h_hi) + _dot_nt(wr_hi, h_lo) + _dot_nt(wr_lo, h_hi) + br
    sub = lax.broadcasted_iota(I32, (8, tq), 0).astype(F32)

    gl = lt[0:8]
    gm = jnp.max(gl, axis=0, keepdims=True)
    gidx = jnp.min(jnp.where(gl == gm, sub, 8.0), axis=0, keepdims=True)
    g_w = 1.0 / jnp.sum(jnp.exp(gl - gm), axis=0, keepdims=True)

    es = lt[8:16]
    for g in range(1, N_GROUPS):
        es = jnp.where(gidx == float(g), lt[8 + 8 * g:16 + 8 * g], es)
    m1 = jnp.max(es, axis=0, keepdims=True)
    i1 = jnp.min(jnp.where(es == m1, sub, 8.0), axis=0, keepdims=True)
    e2 = jnp.where(sub == i1, NEG, es)
    m2 = jnp.max(e2, axis=0, keepdims=True)
    i2 = jnp.min(jnp.where(e2 == m2, sub, 8.0), axis=0, keepdims=True)
    dd = jnp.exp(m2 - m1)
    w1 = g_w / (1.0 + dd)
    w2 = g_w * dd / (1.0 + dd)
    first_low = i1 < i2
    ea = jnp.minimum(i1, i2)
    eb = jnp.maximum(i1, i2)
    w_lo = jnp.where(first_low, w1, w2)
    w_hi = jnp.where(first_low, w2, w1)
    pair = ea * (15.0 - ea) * 0.5 + (eb - ea - 1.0)
    bucket = gidx * float(N_PAIRS) + pair

    brow = lax.broadcasted_iota(I32, (LANES, tq), 0).astype(F32)
    onehot = brow == bucket
    oh = jnp.where(onehot, 1.0, 0.0)
    ti = lax.broadcasted_iota(I32, (tq, tq), 0)
    tj = lax.broadcasted_iota(I32, (tq, tq), 1)
    upper = jnp.where(ti < tj, 1.0, 0.0).astype(BF16)
    before = _dot(oh.astype(BF16), upper) + carry_ref[...]
    rank = jnp.sum(jnp.where(onehot, before, 0.0), axis=0, keepdims=True)
    carry_ref[...] = carry_ref[...] + jnp.sum(oh, axis=1, keepdims=True)

    info = jnp.concatenate([w_lo, w_hi, jnp.full((1, tq), bidx, F32),
                            jnp.zeros((LANES - 3, tq), F32)], axis=0)
    return bucket, rank, info.T


def _mixer_kernel(sinks_ref, x_ref, mod_ref, ln1pre_ref, ln1post_ref, ln2pre_ref, anorm_ref, hnorm_ref,
                  lb_ref, win_ref, wout_ref, wrhi_ref, wrlo_ref, br_ref,
                  x1_ref, info_ref, cnt_ref,
                  kprev_ref, vprev_ref, st_ref, carry_ref):
    b = pl.program_id(0)
    t = pl.program_id(1)

    @pl.when(t == 0)
    def _():
        st_ref[...] = jnp.zeros_like(st_ref)
        kprev_ref[...] = jnp.zeros_like(kprev_ref)
        vprev_ref[...] = jnp.zeros_like(vprev_ref)

    @pl.when((t == 0) & (b == 0))
    def _():
        carry_ref[...] = jnp.zeros_like(carry_ref)

    x = x_ref[...]
    mod = mod_ref[0]
    sh1, sc1, ga1, sh2, sc2 = mod[0:1], mod[1:2], mod[2:3], mod[3:4], mod[4:5]

    h = _rms(x) * ln1pre_ref[...] * (1.0 + sc1) + sh1
    proj = _dot(h.astype(BF16), win_ref[...])

    attn = _rms(_attention(proj, kprev_ref, vprev_ref, sinks_ref, t)) * anorm_ref[...]

    lbr = lb_ref[...]
    le = jnp.exp(lbr - jnp.max(lbr, axis=0, keepdims=True))
    lb = le[0:1] / jnp.sum(le, axis=0, keepdims=True)
    hg = _hgrn2(proj, lb, st_ref, hnorm_ref[...])

    mix = _dot(jnp.concatenate([attn, hg], axis=1).astype(BF16), wout_ref[...])
    x1 = x + ga1 * (_rms(mix) * ln1post_ref[...])

    h2 = _rms(x1) * ln2pre_ref[...] * (1.0 + sc2) + sh2
    bucket, rank, ext = _route(h2, wrhi_ref[...], wrlo_ref[...], br_ref[...], carry_ref, b.astype(F32))

    x1_ref[:, 0:D_MODEL] = x1
    x1_ref[:, D_MODEL:ROW_W] = ext
    tq = x.shape[0]
    info_ref[0] = jnp.concatenate([bucket, rank, jnp.zeros((6, tq), F32)], axis=0).astype(I32)
    cnt_ref[...] = jnp.broadcast_to(carry_ref[...], cnt_ref.shape)


def _mixer(x2, mod3, sinks, ln1pre, ln1post, ln2pre, anorm, hnorm, lb, win, wout, wr_hi, wr_lo, br,
           bsz, seq):
    n = bsz * seq
    nt = seq // TQ
    tok = lambda b, t: (b * nt + t, 0)
    const = lambda b, t: (0, 0)
    full = lambda a: pl.BlockSpec(a.shape, const)
    return pl.pallas_call(
        _mixer_kernel,
        grid=(bsz, nt),
        in_specs=[pl.BlockSpec(memory_space=pltpu.SMEM),
                  pl.BlockSpec((TQ, D_MODEL), tok),
                  pl.BlockSpec((1, 6, D_MODEL), lambda b, t: (b, 0, 0)),
                  full(ln1pre), full(ln1post), full(ln2pre), full(anorm), full(hnorm), full(lb),
                  full(win), full(wout), full(wr_hi), full(wr_lo), full(br)],
        out_specs=[pl.BlockSpec((TQ, ROW_W), tok),
                   pl.BlockSpec((1, 8, TQ), lambda b, t: (b * nt + t, 0, 0)),
                   pl.BlockSpec((LANES, LANES), const)],
        out_shape=[jax.ShapeDtypeStruct((n, ROW_W), F32),
                   jax.ShapeDtypeStruct((n // TQ, 8, TQ), I32),
                   jax.ShapeDtypeStruct((LANES, LANES), F32)],
        scratch_shapes=[pltpu.VMEM((WINDOW, KV_W), F32),
                        pltpu.VMEM((WINDOW, KV_W), F32),
                        pltpu.VMEM((HG_HEADS, HG_DIM, HG_DIM), F32),
                        pltpu.VMEM((LANES, 1), F32)],
        compiler_params=pltpu.CompilerParams(dimension_semantics=("arbitrary", "arbitrary"),
                                             vmem_limit_bytes=VMEM_LIMIT),
        name="mixer",
    )(sinks, x2, mod3, ln1pre, ln1post, ln2pre, anorm, hnorm, lb, win, wout, wr_hi, wr_lo, br)


PERM_STEPS = 8
PERM_UNROLL = 8


def _perm_kernel(rs_ref, cnt_ref, bucket_hbm, rank_hbm, perm_ref, bchunk, rchunk, sem):
    pid = pl.program_id(0)
    n_tok = bchunk.shape[0]
    cb = pltpu.make_async_copy(bucket_hbm.at[pid], bchunk, sem.at[0])
    cr = pltpu.make_async_copy(rank_hbm.at[pid], rchunk, sem.at[1])
    cb.start()
    cr.start()

    @pl.when(pid == 0)
    def _():
        def per_bucket(b, carry):
            start = rs_ref[b]
            cnt = cnt_ref[b]

            def pad(r, c2):
                perm_ref[start + r] = 0
                return c2

            lax.fori_loop(cnt, ((cnt + TM - 1) // TM) * TM, pad, 0)
            return carry

        lax.fori_loop(0, N_BUCKETS, per_bucket, 0)

        def tail(r, carry):
            perm_ref[r] = 0
            return carry

        lax.fori_loop(rs_ref[N_BUCKETS], perm_ref.shape[0], tail, 0)

    cb.wait()
    cr.wait()

    def body(j, carry):
        for u in range(PERM_UNROLL):
            i = j * PERM_UNROLL + u
            perm_ref[rs_ref[bchunk[i]] + rchunk[i]] = pid * n_tok + i
        return carry

    lax.fori_loop(0, n_tok // PERM_UNROLL, body, 0)


def _perm(row_start, counts, bucket, rank, n_rows):
    n = bucket.shape[0]
    chunk = n // PERM_STEPS
    assert n % (PERM_STEPS * PERM_UNROLL) == 0
    return pl.pallas_call(
        _perm_kernel,
        grid=(PERM_STEPS,),
        in_specs=[pl.BlockSpec(memory_space=pltpu.SMEM),
                  pl.BlockSpec(memory_space=pltpu.SMEM),
                  pl.BlockSpec(memory_space=pl.ANY),
                  pl.BlockSpec(memory_space=pl.ANY)],
        out_specs=pl.BlockSpec(memory_space=pltpu.SMEM),
        out_shape=jax.ShapeDtypeStruct((n_rows,), I32),
        scratch_shapes=[pltpu.SMEM((chunk,), I32), pltpu.SMEM((chunk,), I32), pltpu.SemaphoreType.DMA((2,))],
        compiler_params=pltpu.CompilerParams(dimension_semantics=("arbitrary",)),
        name="perm",
    )(row_start, counts, bucket.reshape(PERM_STEPS, chunk), rank.reshape(PERM_STEPS, chunk))


def _moe_kernel(nt_ref, nv_ref, ea_ref, eb_ref, perm_ref,
                x1_hbm, mod_ref, ln2pre_ref, ln2post_ref, wgu_a_ref, wd_a_ref, wgu_b_ref, wd_b_ref,
                out_hbm, xbuf, obuf, gsem, ssem):
    i = pl.program_id(0)
    nt = nt_ref[0]
    slot = i % 2

    def gather(tile, sl):
        for r in range(TM):
            tok = perm_ref[tile * TM + r]
            pltpu.make_async_copy(x1_hbm.at[pl.ds(tok, 1)], xbuf.at[sl, pl.ds(r, 1)], gsem.at[sl]).start()

    def wait_gather(sl):
        pltpu.make_async_copy(x1_hbm.at[pl.ds(0, TM)], xbuf.at[sl], gsem.at[sl]).wait()

    def wait_scatter(sl, nv):
        @pl.when(nv == TM)
        def _():
            pltpu.make_async_copy(obuf.at[sl], out_hbm.at[pl.ds(0, TM)], ssem.at[sl]).wait()

        @pl.when(nv < TM)
        def _():
            def one(r, carry):
                pltpu.make_async_copy(obuf.at[sl, pl.ds(0, 1)], out_hbm.at[pl.ds(0, 1)], ssem.at[sl]).wait()
                return carry

            lax.fori_loop(0, nv, one, 0)

    @pl.when(i == 0)
    def _():
        gather(0, 0)

    @pl.when(i < nt)
    def _():
        nv = nv_ref[i]
        wait_gather(slot)

        @pl.when(i >= 2)
        def _():
            wait_scatter(slot, nv_ref[jnp.maximum(i - 2, 0)])

        has_next = i + 1 < nt
        nxt = jnp.minimum(i + 1, pl.num_programs(0) - 1) * TM
        for r in range(TM):
            @pl.when(has_next)
            def _():
                tok = perm_ref[nxt + r]
                pltpu.make_async_copy(x1_hbm.at[pl.ds(tok, 1)], xbuf.at[1 - slot, pl.ds(r, 1)],
                                      gsem.at[1 - slot]).start()

        xb = xbuf[slot]
        x1 = xb[:, 0:D_MODEL]
        ext = xb[:, D_MODEL:ROW_W]
        w_lo, w_hi, bf = ext[:, 0:1], ext[:, 1:2], ext[:, 2:3]
        mod = mod_ref[...]
        nb = mod.shape[1]

        def pick(k):
            acc = jnp.zeros((TM, D_MODEL), F32)
            for bb in range(nb):
                acc = jnp.where(bf == float(bb), mod[k, bb:bb + 1], acc)
            return acc

        sh2, sc2, ga2 = pick(0), pick(1), pick(2)
        h2 = (_rms(x1) * ln2pre_ref[...] * (1.0 + sc2) + sh2).astype(BF16)

        def expert(wgu_ref, wd_ref):
            gu = _dot(h2, wgu_ref[0])
            hg, hu = gu[:, 0:FF], gu[:, FF:2 * FF]
            act = (hg * _sigmoid(hg)) * hu
            return _dot(act.astype(BF16), wd_ref[0])

        y = w_lo * expert(wgu_a_ref, wd_a_ref) + w_hi * expert(wgu_b_ref, wd_b_ref)
        obuf[slot] = x1 + ga2 * (_rms(y) * ln2post_ref[...])

        for r in range(TM):
            @pl.when(r < nv)
            def _():
                tok = perm_ref[i * TM + r]
                pltpu.make_async_copy(obuf.at[slot, pl.ds(r, 1)], out_hbm.at[pl.ds(tok, 1)],
                                      ssem.at[slot]).start()

        @pl.when(i == nt - 1)
        def _():
            wait_scatter(slot, nv)

            @pl.when(i >= 1)
            def _():
                wait_scatter(1 - slot, nv_ref[jnp.maximum(i - 1, 0)])


def _moe(x1ext, mod2, ln2pre, ln2post, wgu, wd, nt, nv, ea, eb, perm, n_tiles):
    n = x1ext.shape[0]
    const2 = lambda i, *_: (0, 0)
    grid_spec = pltpu.PrefetchScalarGridSpec(
        num_scalar_prefetch=5,
        grid=(n_tiles,),
        in_specs=[pl.BlockSpec(memory_space=pl.ANY),
                  pl.BlockSpec(mod2.shape, lambda i, *_: (0, 0, 0)),
                  pl.BlockSpec(ln2pre.shape, const2),
                  pl.BlockSpec(ln2post.shape, const2),
                  pl.BlockSpec((1, D_MODEL, 2 * FF), lambda i, nt, nv, ea, eb, perm: (ea[i], 0, 0)),
                  pl.BlockSpec((1, FF, D_MODEL), lambda i, nt, nv, ea, eb, perm: (ea[i], 0, 0)),
                  pl.BlockSpec((1, D_MODEL, 2 * FF), lambda i, nt, nv, ea, eb, perm: (eb[i], 0, 0)),
                  pl.BlockSpec((1, FF, D_MODEL), lambda i, nt, nv, ea, eb, perm: (eb[i], 0, 0))],
        out_specs=pl.BlockSpec(memory_space=pl.ANY),
        scratch_shapes=[pltpu.VMEM((2, TM, ROW_W), F32),
                        pltpu.VMEM((2, TM, D_MODEL), F32),
                        pltpu.SemaphoreType.DMA((2,)),
                        pltpu.SemaphoreType.DMA((2,))],
    )
    return pl.pallas_call(
        _moe_kernel,
        grid_spec=grid_spec,
        out_shape=jax.ShapeDtypeStruct((n, D_MODEL), F32),
        compiler_params=pltpu.CompilerParams(dimension_semantics=("arbitrary",),
                                             vmem_limit_bytes=VMEM_LIMIT),
        name="moe",
    )(nt, nv, ea, eb, perm, x1ext, mod2, ln2pre, ln2post, wgu, wd, wgu, wd)


def kernel(x, c, ln1_pre, ln1_post, ln2_pre, ln2_post, w_ada, b_ada, w_in, attn_sinks, attn_out_norm,
           hgrn_lb, hgrn_out_norm, w_out, w_router_group, b_router_group, w_router_expert,
           b_router_expert, w_exp_gate, w_exp_up, w_exp_down):
    bsz, seq, d = x.shape
    assert d == D_MODEL and seq % TQ == 0 and w_ada.shape[0] == 1 and hgrn_lb.shape[0] == 2
    n = bsz * seq

    mod = _ada(c, w_ada[0], b_ada[0])
    mod3 = mod.reshape(bsz, 6, d)

    wr = jnp.concatenate([w_router_group[0].T, jnp.zeros((8 - N_GROUPS, d), F32), w_router_expert[0].T], axis=0)
    br = jnp.concatenate([b_router_group[0], jnp.full((8 - N_GROUPS,), NEG, F32), b_router_expert[0]])
    wr_hi = wr.astype(BF16)
    wr_lo = (wr - wr_hi.astype(F32)).astype(BF16)

    x1ext, info, cnt = _mixer(
        x.reshape(n, d), mod3, attn_sinks[0], ln1_pre, ln1_post, ln2_pre, attn_out_norm, hgrn_out_norm,
        hgrn_lb, w_in[0].astype(BF16), w_out[0].astype(BF16), wr_hi, wr_lo, br.reshape(ROUTER_ROWS, 1),
        bsz, seq)

    n_tiles = n // TM + N_BUCKETS
    counts = cnt[:N_BUCKETS, 0].astype(I32)
    tiles_per = (counts + TM - 1) // TM
    tile_end = jnp.cumsum(tiles_per)
    tile_start = tile_end - tiles_per
    nt = tile_end[-1]
    bucket = info[:, 0, :].reshape(n)
    rank = info[:, 1, :].reshape(n)
    tid = jnp.arange(n_tiles, dtype=I32)
    tb = jnp.searchsorted(tile_end, tid, side="right").astype(I32)
    tb = jnp.where(tid < nt, tb, tb[nt - 1])
    nv = jnp.where(tid < nt, jnp.clip(counts[tb] - (tid - tile_start[tb]) * TM, 0, TM), 0).astype(I32)
    grp, pair = tb // N_PAIRS, tb % N_PAIRS
    ea = (grp * EPG + jnp.asarray(_PAIR_A)[pair]).astype(I32)
    eb = (grp * EPG + jnp.asarray(_PAIR_B)[pair]).astype(I32)

    pad128 = lambda a: jnp.concatenate([a, jnp.zeros((LANES - a.shape[0],), I32)])
    row_start = jnp.concatenate([tile_start, nt.reshape(1)]) * TM
    perm = _perm(pad128(row_start), pad128(counts), bucket, rank, n_tiles * TM)

    wgu = jnp.concatenate([w_exp_gate[0], w_exp_up[0]], axis=-1).astype(BF16)
    wd = w_exp_down[0].astype(BF16)
    mod2 = jnp.transpose(mod3[:, 3:6, :], (1, 0, 2))
    out = _moe(x1ext, mod2, ln2_pre, ln2_post, wgu, wd, nt.reshape(1), nv, ea, eb, perm, n_tiles)
    return out.reshape(bsz, seq, d)
```

```python
import functools

import numpy as np
import jax
import jax.numpy as jnp
from jax import lax
from jax.experimental import pallas as pl
from jax.experimental.pallas import tpu as pltpu

F32 = jnp.float32
BF16 = jnp.bfloat16
I32 = jnp.int32

D_MODEL = 1024
ATTN_HEADS = 8
HEAD_DIM = 64
WINDOW = 128
ATTN_W = 512
KV_W = 128
HG_HEADS = 4
HG_DIM = 128
HG_W = 512
HG_CHUNK = 32
IN_W = 2816
N_GROUPS = 4
EPG = 8
N_EXPERTS = 32
FF = 256
N_PAIRS = EPG * (EPG - 1) // 2
N_BUCKETS = N_GROUPS * N_PAIRS
EPS = 1e-6
NEG = -1e30

LANES = 128
H2P_W = D_MODEL // 2
EXT_W = LANES
OFF_H2P = D_MODEL
OFF_EXT = D_MODEL + H2P_W
ROW_W = OFF_EXT + EXT_W
EXT_BATCH0 = 8
U32 = jnp.uint32
ROUTER_ROWS = 8 + N_EXPERTS

TQ = 256
TM = 128
VMEM_LIMIT = 56 * 1024 * 1024

OFF_Q, OFF_K, OFF_V, OFF_HQ, OFF_HF, OFF_HI, OFF_HG = 0, 512, 640, 768, 1280, 1792, 2304

_PAIR_A = np.array([a for a in range(EPG) for b in range(a + 1, EPG)], np.int32)
_PAIR_B = np.array([b for a in range(EPG) for b in range(a + 1, EPG)], np.int32)


def _dot(a, b):
    return jnp.dot(a, b, preferred_element_type=F32)


def _dot_nt(a, b):
    return lax.dot_general(a, b, (((1,), (1,)), ((), ())), preferred_element_type=F32)


def _dot_tn(a, b):
    return lax.dot_general(a, b, (((0,), (0,)), ((), ())), preferred_element_type=F32)


def _split(a):
    hi = a.astype(BF16)
    lo = (a - hi.astype(F32)).astype(BF16)
    return hi, lo


def _rms(x):
    return x * lax.rsqrt(jnp.mean(x * x, axis=-1, keepdims=True) + EPS)


def _sigmoid(x):
    return 1.0 / (1.0 + jnp.exp(-x))


def _ada_kernel(c_ref, w_ref, b_ref, o_ref):
    c = c_ref[...]
    ca = c * _sigmoid(c)
    c_hi, c_lo = _split(ca)
    w_hi, w_lo = _split(w_ref[...])
    o_ref[...] = _dot(c_hi, w_hi) + _dot(c_lo, w_hi) + _dot(c_hi, w_lo) + b_ref[...]


def _ada(c, w, b):
    bsz, d = c.shape
    n_out = w.shape[1]
    return pl.pallas_call(
        _ada_kernel,
        grid=(n_out // d,),
        in_specs=[pl.BlockSpec((bsz, d), lambda j: (0, 0)),
                  pl.BlockSpec((d, d), lambda j: (0, j)),
                  pl.BlockSpec((1, d), lambda j: (0, j))],
        out_specs=pl.BlockSpec((bsz, d), lambda j: (0, j)),
        out_shape=jax.ShapeDtypeStruct((bsz, n_out), F32),
        compiler_params=pltpu.CompilerParams(dimension_semantics=("arbitrary",),
                                             vmem_limit_bytes=VMEM_LIMIT),
        name="adaln",
    )(c, w, b.reshape(1, n_out))


def _attention(proj, kprev_ref, vprev_ref, sinks_ref, t, s_scr, m_scr, p_scr):
    tq = proj.shape[0]
    q = (proj[:, OFF_Q:OFF_Q + ATTN_W] * (HEAD_DIM ** -0.5)).astype(BF16)
    kf = jnp.concatenate([kprev_ref[...], proj[:, OFF_K:OFF_K + KV_W]], axis=0)
    vf = jnp.concatenate([vprev_ref[...], proj[:, OFF_V:OFF_V + KV_W]], axis=0)
    kprev_ref[...] = proj[tq - WINDOW:, OFF_K:OFF_K + KV_W]
    vprev_ref[...] = proj[tq - WINDOW:, OFF_V:OFF_V + KV_W]

    lo = lax.broadcasted_iota(I32, kf.shape, 1) < HEAD_DIM
    kr = pltpu.roll(kf, HEAD_DIM, axis=1)
    vr = pltpu.roll(vf, HEAD_DIM, axis=1)

    def variants(a, ar):
        return [[jnp.where(lo, a, 0.0).astype(BF16), jnp.where(lo, 0.0, ar).astype(BF16)],
                [jnp.where(lo, ar, 0.0).astype(BF16), jnp.where(lo, 0.0, a).astype(BF16)]]

    kvar = variants(kf, kr)
    vvar = variants(vf, vr)

    qi = lax.broadcasted_iota(I32, (WINDOW, 2 * WINDOW), 0)
    kj = lax.broadcasted_iota(I32, (WINDOW, 2 * WINDOW), 1)
    dist = qi + WINDOW - kj
    in_win = (dist >= 0) & (dist < WINDOW)
    distf = dist.astype(F32)
    kj_first = kj + jnp.where(t > 0, WINDOW, 0)

    nblk = tq // WINDOW
    idx = lambda j, h: j * ATTN_HEADS + h
    keys = lambda a, j: a[j * WINDOW:(j + 2) * WINDOW]

    for j in range(nblk):
        valid = (in_win & (kj_first >= WINDOW)) if j == 0 else in_win
        for h in range(ATTN_HEADS):
            p, par = h // 2, h % 2
            slope = 2.0 ** (-8.0 * (h + 1) / ATTN_HEADS)
            qp = q[j * WINDOW:(j + 1) * WINDOW, p * LANES:(p + 1) * LANES]
            s = _dot_nt(qp, keys(kvar[p // 2][par], j))
            s = jnp.where(valid, s - slope * distf, NEG)
            s_scr[idx(j, h)] = s
            m_scr[idx(j, h)] = jnp.maximum(jnp.max(s, axis=-1, keepdims=True), sinks_ref[h])

    for j in range(nblk):
        for h in range(ATTN_HEADS):
            m = m_scr[idx(j, h)]
            pe = jnp.exp(s_scr[idx(j, h)] - m)
            p_scr[idx(j, h)] = pe.astype(BF16)
            m_scr[idx(j, h)] = 1.0 / (jnp.sum(pe, axis=-1, keepdims=True) + jnp.exp(sinks_ref[h] - m))

    blocks = []
    for j in range(nblk):
        pairs = []
        for p in range(ATTN_HEADS // 2):
            acc = None
            for par in range(2):
                h = 2 * p + par
                o = _dot(p_scr[idx(j, h)], keys(vvar[p // 2][par], j)) * m_scr[idx(j, h)]
                acc = o if acc is None else acc + o
            pairs.append(acc)
        blocks.append(jnp.concatenate(pairs, axis=1))
    return jnp.concatenate(blocks, axis=0)


def _hgrn2(proj, lb, st_ref, hnorm, u_scr, stb_scr):
    tq = proj.shape[0]
    nc = tq // HG_CHUNK
    qr = proj[:, OFF_HQ:OFF_HQ + HG_W]
    fr = proj[:, OFF_HF:OFF_HF + HG_W]
    iv = proj[:, OFF_HI:OFF_HI + HG_W]
    gr = proj[:, OFF_HG:OFF_HG + HG_W]
    qh = qr * _sigmoid(qr)
    f = lb + (1.0 - lb) * _sigmoid(fr)
    kk = 1.0 - f
    logf = jnp.log(f)

    rmod = lax.broadcasted_iota(I32, (tq, HG_W), 0) & (HG_CHUNK - 1)
    bc = logf
    s = 1
    while s < HG_CHUNK:
        bc = bc + jnp.where(rmod >= s, pltpu.roll(bc, s, axis=0), 0.0)
        s *= 2

    b3 = bc.reshape(nc, HG_CHUNK, HG_W)
    blast = b3[:, HG_CHUNK - 1:HG_CHUNK, :]
    kend = (kk.reshape(nc, HG_CHUNK, HG_W) * jnp.exp(blast - b3)).reshape(tq, HG_W)
    decay = jnp.exp(blast).reshape(nc, HG_W)
    qdec = (qh * jnp.exp(bc)).astype(BF16)
    kdec = (kk * jnp.exp(-bc)).astype(BF16)
    kend = kend.astype(BF16)
    ivb = iv.astype(BF16)

    ri = lax.broadcasted_iota(I32, (tq, tq), 0)
    ci = lax.broadcasted_iota(I32, (tq, tq), 1)
    cmask = ((ri // HG_CHUNK) == (ci // HG_CHUNK)) & (ri >= ci)

    heads = [slice(hh * HG_DIM, (hh + 1) * HG_DIM) for hh in range(HG_HEADS)]
    chunks = [slice(n * HG_CHUNK, (n + 1) * HG_CHUNK) for n in range(nc)]

    lane_head = lax.broadcasted_iota(I32, (HG_CHUNK, HG_W), 1) // HG_DIM
    for n, rs in enumerate(chunks):
        vstack = jnp.concatenate([ivb[rs, sl] for sl in heads], axis=0)
        kblk = jnp.concatenate([jnp.where(lane_head == hh, kend[rs], 0.0).astype(BF16)
                                for hh in range(HG_HEADS)], axis=0)
        u_scr[n] = _dot_tn(vstack, kblk)

    st = st_ref[...]
    for n in range(nc):
        stb_scr[n] = st.astype(BF16)
        st = st * decay[n:n + 1] + u_scr[n]
    st_ref[...] = st

    outs = []
    for hh, sl in enumerate(heads):
        a = _dot_nt(qdec[:, sl], kdec[:, sl])
        a = jnp.where(cmask, a, 0.0).astype(BF16)
        o_intra = _dot(a, ivb[:, sl])
        inter = [_dot_nt(qdec[rs, sl], stb_scr[n, :, sl]) for n, rs in enumerate(chunks)]
        o = o_intra + jnp.concatenate(inter, axis=0)
        o = _rms(o) * hnorm[:, sl]
        g = gr[:, sl]
        outs.append(o * (g * _sigmoid(g)))
    return jnp.concatenate(outs, axis=1)


def _route_topk(h2, wr_hi, wr_lo, br):
    tq = h2.shape[0]
    h_hi, h_lo = _split(h2)
    lt = _dot_nt(wr_hi, h_hi) + _dot_nt(wr_hi, h_lo) + _dot_nt(wr_lo, h_hi) + br
    sub = lax.broadcasted_iota(I32, (8, tq), 0).astype(F32)

    gl = lt[0:8]
    gm = jnp.max(gl, axis=0, keepdims=True)
    gidx = jnp.min(jnp.where(gl == gm, sub, 8.0), axis=0, keepdims=True)
    g_w = 1.0 / jnp.sum(jnp.exp(gl - gm), axis=0, keepdims=True)

    es = lt[8:16]
    for g in range(1, N_GROUPS):
        es = jnp.where(gidx == float(g), lt[8 + 8 * g:16 + 8 * g], es)
    m1 = jnp.max(es, axis=0, keepdims=True)
    i1 = jnp.min(jnp.where(es == m1, sub, 8.0), axis=0, keepdims=True)
    e2 = jnp.where(sub == i1, NEG, es)
    m2 = jnp.max(e2, axis=0, keepdims=True)
    i2 = jnp.min(jnp.where(e2 == m2, sub, 8.0), axis=0, keepdims=True)
    dd = jnp.exp(m2 - m1)
    w1 = g_w / (1.0 + dd)
    w2 = g_w * dd / (1.0 + dd)
    first_low = i1 < i2
    ea = jnp.minimum(i1, i2)
    eb = jnp.maximum(i1, i2)
    w_lo = jnp.where(first_low, w1, w2)
    w_hi = jnp.where(first_low, w2, w1)
    pair = ea * (15.0 - ea) * 0.5 + (eb - ea - 1.0)
    bucket = gidx * float(N_PAIRS) + pair
    return bucket, w_lo, w_hi


def _route_rank(bucket, w_lo, w_hi, carry_ref, bidx, live):
    tq = bucket.shape[1]
    brow = lax.broadcasted_iota(I32, (LANES, tq), 0).astype(F32)
    onehot = brow == bucket
    oh = jnp.where(onehot, live, 0.0)
    ti = lax.broadcasted_iota(I32, (tq, tq), 0)
    tj = lax.broadcasted_iota(I32, (tq, tq), 1)
    upper = jnp.where(ti < tj, 1.0, 0.0).astype(BF16)
    before = _dot(oh.astype(BF16), upper) + carry_ref[...]
    rank = jnp.sum(jnp.where(onehot, before, 0.0), axis=0, keepdims=True)
    carry_ref[...] = carry_ref[...] + jnp.sum(oh, axis=1, keepdims=True)

    lane_row = lax.broadcasted_iota(I32, (LANES - EXT_BATCH0, tq), 0)
    onehot_b = jnp.where(lane_row == bidx, 1.0, 0.0)
    info = jnp.concatenate([w_lo, w_hi, jnp.zeros((EXT_BATCH0 - 2, tq), F32), onehot_b], axis=0)
    return rank, info.T


def _mixer_kernel(sinks_ref, x_ref, mod_ref, ln1pre_ref, ln1post_ref, ln2pre_ref, anorm_ref, hnorm_ref,
                  lb_ref, win_ref, wout_ref, wrhi_ref, wrlo_ref, br_ref,
                  rows_hbm, info_ref, cnt_ref,
                  kprev_ref, vprev_ref, st_ref, carry_ref, s_scr, m_scr, p_scr, u_scr, stb_scr,
                  h2_scr, proj_scr, rowbuf, rsem, *, tiles_per_seq):
    s = pl.program_id(0)
    n_tiles = pl.num_programs(0) - 1
    tq = x_ref.shape[0]
    t = lax.rem(jnp.minimum(s, n_tiles - 1), tiles_per_seq)
    slot = lax.rem(s, 2)
    bits = lambda a: pltpu.bitcast(a, U32)

    def row_copy(step_slot, tile):
        return pltpu.make_async_copy(rowbuf.at[step_slot], rows_hbm.at[pl.ds(pl.multiple_of(tile * tq, tq), tq)],
                                     rsem.at[step_slot])

    @pl.when(s == 0)
    def _():
        carry_ref[...] = jnp.zeros_like(carry_ref)
        h2_scr[...] = jnp.zeros_like(h2_scr)

    @pl.when(t == 0)
    def _():
        st_ref[...] = jnp.zeros_like(st_ref)
        kprev_ref[...] = jnp.zeros_like(kprev_ref)
        vprev_ref[...] = jnp.zeros_like(vprev_ref)

    x = x_ref[...]
    mod = mod_ref[0]
    sh1, sc1, ga1, sh2, sc2 = mod[0:1], mod[1:2], mod[2:3], mod[3:4], mod[4:5]
    prev = jnp.maximum(s - 1, 0)

    @pl.when(s <= n_tiles)
    def _():
        bucket, w_lo, w_hi = _route_topk(h2_scr[...], wrhi_ref[...], wrlo_ref[...], br_ref[...])
        h = _rms(x) * ln1pre_ref[...] * (1.0 + sc1) + sh1
        proj_scr[...] = _dot(h.astype(BF16), win_ref[...])
        live = jnp.where(s >= 1, 1.0, 0.0)
        rank, ext = _route_rank(bucket, w_lo, w_hi, carry_ref, prev // tiles_per_seq, live)
        rowbuf[1 - slot, :, OFF_EXT:ROW_W] = bits(ext)
        info_ref[0] = jnp.concatenate([bucket, rank, jnp.zeros((6, tq), F32)], axis=0).astype(I32)
        cnt_ref[...] = jnp.broadcast_to(carry_ref[...], cnt_ref.shape)

    proj = proj_scr[...]
    attn = _rms(_attention(proj, kprev_ref, vprev_ref, sinks_ref, t, s_scr, m_scr, p_scr)) * anorm_ref[...]

    lbr = lb_ref[...]
    le = jnp.exp(lbr - jnp.max(lbr, axis=0, keepdims=True))
    lb = le[0:1] / jnp.sum(le, axis=0, keepdims=True)
    hg = _hgrn2(proj, lb, st_ref, hnorm_ref[...], u_scr, stb_scr)

    mix = _dot(jnp.concatenate([attn, hg], axis=1).astype(BF16), wout_ref[...])
    x1 = x + ga1 * (_rms(mix) * ln1post_ref[...])

    h2 = _rms(x1) * ln2pre_ref[...] * (1.0 + sc2) + sh2
    h2r = h2.astype(BF16).astype(F32)

    @pl.when(s >= 2)
    def _():
        row_copy(slot, s - 2).wait()

    rowbuf[slot, :, 0:D_MODEL] = bits(x1)
    rowbuf[slot, :, OFF_H2P:OFF_EXT] = ((bits(h2r[:, 0:H2P_W]) >> 16)
                                        | (bits(h2r[:, H2P_W:D_MODEL]) & jnp.uint32(0xFFFF0000)))
    h2_scr[...] = h2

    @pl.when(s >= 1)
    def _():
        row_copy(1 - slot, prev).start()

    @pl.when(s == n_tiles)
    def _():
        row_copy(1 - slot, prev).wait()


def _mixer(x2, mod3, sinks, ln1pre, ln1post, ln2pre, anorm, hnorm, lb, win, wout, wr_hi, wr_lo, br,
           bsz, seq):
    n = bsz * seq
    nt = seq // TQ
    n_tiles = bsz * nt
    n_sc = (TQ // WINDOW) * ATTN_HEADS
    cur = lambda s: jnp.minimum(s, n_tiles - 1)
    const = lambda s: (0, 0)
    full = lambda a: pl.BlockSpec(a.shape, const)
    return pl.pallas_call(
        functools.partial(_mixer_kernel, tiles_per_seq=nt),
        grid=(n_tiles + 1,),
        in_specs=[pl.BlockSpec(memory_space=pltpu.SMEM),
                  pl.BlockSpec((TQ, D_MODEL), lambda s: (cur(s), 0)),
                  pl.BlockSpec((1, 6, D_MODEL), lambda s: (cur(s) // nt, 0, 0)),
                  full(ln1pre), full(ln1post), full(ln2pre), full(anorm), full(hnorm), full(lb),
                  full(win), full(wout), full(wr_hi), full(wr_lo), full(br)],
        out_specs=[pl.BlockSpec(memory_space=pl.ANY),
                   pl.BlockSpec((1, 8, TQ), lambda s: (jnp.maximum(s - 1, 0), 0, 0)),
                   pl.BlockSpec((LANES, LANES), const)],
        out_shape=[jax.ShapeDtypeStruct((n, ROW_W), U32),
                   jax.ShapeDtypeStruct((n // TQ, 8, TQ), I32),
                   jax.ShapeDtypeStruct((LANES, LANES), F32)],
        scratch_shapes=[pltpu.VMEM((WINDOW, KV_W), F32),
                        pltpu.VMEM((WINDOW, KV_W), F32),
                        pltpu.VMEM((HG_DIM, HG_W), F32),
                        pltpu.VMEM((LANES, 1), F32),
                        pltpu.VMEM((n_sc, WINDOW, 2 * WINDOW), F32),
                        pltpu.VMEM((n_sc, WINDOW, 1), F32),
                        pltpu.VMEM((n_sc, WINDOW, 2 * WINDOW), BF16),
                        pltpu.VMEM((TQ // HG_CHUNK, HG_DIM, HG_W), F32),
                        pltpu.VMEM((TQ // HG_CHUNK, HG_DIM, HG_W), BF16),
                        pltpu.VMEM((TQ, D_MODEL), F32),
                        pltpu.VMEM((TQ, IN_W), F32),
                        pltpu.VMEM((2, TQ, ROW_W), U32),
                        pltpu.SemaphoreType.DMA((2,))],
        compiler_params=pltpu.CompilerParams(dimension_semantics=("arbitrary",),
                                             vmem_limit_bytes=VMEM_LIMIT),
        name="mixer",
    )(sinks, x2, mod3, ln1pre, ln1post, ln2pre, anorm, hnorm, lb, win, wout, wr_hi, wr_lo, br)


PERM_STEPS = 8
PERM_UNROLL = 8


def _perm_kernel(rs_ref, cnt_ref, bucket_hbm, rank_hbm, perm_ref, bchunk, rchunk, sem):
    pid = pl.program_id(0)
    n_tok = bchunk.shape[0]
    cb = pltpu.make_async_copy(bucket_hbm.at[pid], bchunk, sem.at[0])
    cr = pltpu.make_async_copy(rank_hbm.at[pid], rchunk, sem.at[1])
    cb.start()
    cr.start()

    @pl.when(pid == 0)
    def _():
        def per_bucket(b, carry):
            start = rs_ref[b]
            cnt = cnt_ref[b]

            def pad(r, c2):
                perm_ref[start + r] = 0
                return c2

            lax.fori_loop(cnt, ((cnt + TM - 1) // TM) * TM, pad, 0)
            return carry

        lax.fori_loop(0, N_BUCKETS, per_bucket, 0)

        def tail(r, carry):
            perm_ref[r] = 0
            return carry

        lax.fori_loop(rs_ref[N_BUCKETS], perm_ref.shape[0], tail, 0)

    cb.wait()
    cr.wait()

    def body(j, carry):
        for u in range(PERM_UNROLL):
            i = j * PERM_UNROLL + u
            perm_ref[rs_ref[bchunk[i]] + rchunk[i]] = pid * n_tok + i
        return carry

    lax.fori_loop(0, n_tok // PERM_UNROLL, body, 0)


def _perm(row_start, counts, bucket, rank, n_rows):
    n = bucket.shape[0]
    chunk = n // PERM_STEPS
    assert n % (PERM_STEPS * PERM_UNROLL) == 0
    return pl.pallas_call(
        _perm_kernel,
        grid=(PERM_STEPS,),
        in_specs=[pl.BlockSpec(memory_space=pltpu.SMEM),
                  pl.BlockSpec(memory_space=pltpu.SMEM),
                  pl.BlockSpec(memory_space=pl.ANY),
                  pl.BlockSpec(memory_space=pl.ANY)],
        out_specs=pl.BlockSpec(memory_space=pltpu.SMEM),
        out_shape=jax.ShapeDtypeStruct((n_rows,), I32),
        scratch_shapes=[pltpu.SMEM((chunk,), I32), pltpu.SMEM((chunk,), I32), pltpu.SemaphoreType.DMA((2,))],
        compiler_params=pltpu.CompilerParams(dimension_semantics=("arbitrary",)),
        name="perm",
    )(row_start, counts, bucket.reshape(PERM_STEPS, chunk), rank.reshape(PERM_STEPS, chunk))


def _moe_kernel(nt_ref, nv_ref, ea_ref, eb_ref, perm_ref,
                rows_hbm, gtab_hi_ref, gtab_lo_ref, ln2post_ref,
                wgu_a0, wd_a0, wgu_b0, wd_b0, wgu_a1, wd_a1, wgu_b1, wd_b1,
                out_hbm, xbuf0, xbuf1, obuf0, obuf1, gsem, ssem):
    j = pl.program_id(0)
    nt = nt_ref[0]
    last_tile = 2 * pl.num_programs(0) - 1
    xbufs, obufs = (xbuf0, xbuf1), (obuf0, obuf1)
    weights = ((wgu_a0, wd_a0, wgu_b0, wd_b0), (wgu_a1, wd_a1, wgu_b1, wd_b1))

    def start_gather(tile, s, pred):
        base = jnp.minimum(tile, last_tile) * TM
        for r in range(TM):
            @pl.when(pred)
            def _():
                tok = perm_ref[base + r]
                pltpu.make_async_copy(rows_hbm.at[pl.ds(tok, 1)], xbufs[s].at[pl.ds(r, 1)], gsem.at[s]).start()

    def wait_gather(s):
        pltpu.make_async_copy(rows_hbm.at[pl.ds(0, TM)], xbufs[s], gsem.at[s]).wait()

    def wait_scatter(s, nv):
        @pl.when(nv == TM)
        def _():
            pltpu.make_async_copy(obufs[s], out_hbm.at[pl.ds(0, TM)], ssem.at[s]).wait()

        @pl.when(nv < TM)
        def _():
            def one(r, carry):
                pltpu.make_async_copy(obufs[s].at[pl.ds(0, 1)], out_hbm.at[pl.ds(0, 1)], ssem.at[s]).wait()
                return carry

            lax.fori_loop(0, nv, one, 0)

    def compute(xbuf, obuf, wgu_a, wd_a, wgu_b, wd_b):
        xb = xbuf[...]
        x1 = pltpu.bitcast(xb[:, 0:D_MODEL], F32)
        hp = xb[:, OFF_H2P:OFF_EXT]
        h2a = pltpu.bitcast(hp << 16, F32).astype(BF16)
        h2b = pltpu.bitcast(hp & jnp.uint32(0xFFFF0000), F32).astype(BF16)
        ext = pltpu.bitcast(xb[:, OFF_EXT:ROW_W], F32)
        w_lo, w_hi = ext[:, 0:1], ext[:, 1:2]
        sel = ext.astype(BF16)
        ga2 = _dot(sel, gtab_hi_ref[...]) + _dot(sel, gtab_lo_ref[...])

        def expert(wgu_ref, wd_ref):
            gu = _dot(h2a, wgu_ref[0, 0:H2P_W]) + _dot(h2b, wgu_ref[0, H2P_W:D_MODEL])
            hg, hu = gu[:, 0:FF], gu[:, FF:2 * FF]
            act = (hg * _sigmoid(hg)) * hu
            return _dot(act.astype(BF16), wd_ref[0])

        y = w_lo * expert(wgu_a, wd_a) + w_hi * expert(wgu_b, wd_b)
        obuf[...] = x1 + ga2 * (_rms(y) * ln2post_ref[...])

    @pl.when(j == 0)
    def _():
        start_gather(0, 0, nt > 0)

    for s in range(2):
        tile = 2 * j + s

        @pl.when(tile < nt)
        def _():
            nv = nv_ref[tile]
            wait_gather(s)

            @pl.when(tile >= 2)
            def _():
                wait_scatter(s, nv_ref[jnp.maximum(tile - 2, 0)])

            start_gather(tile + 1, 1 - s, tile + 1 < nt)
            compute(xbufs[s], obufs[s], *weights[s])

            for r in range(TM):
                @pl.when(r < nv)
                def _():
                    tok = perm_ref[tile * TM + r]
                    pltpu.make_async_copy(obufs[s].at[pl.ds(r, 1)], out_hbm.at[pl.ds(tok, 1)], ssem.at[s]).start()

            @pl.when(tile == nt - 1)
            def _():
                wait_scatter(s, nv)

                @pl.when(tile >= 1)
                def _():
                    wait_scatter(1 - s, nv_ref[jnp.maximum(tile - 1, 0)])


def _moe(rows, gtab_hi, gtab_lo, ln2post, wgu, wd, nt, nv, ea, eb, perm, n_tiles):
    n = rows.shape[0]
    assert n_tiles % 2 == 0
    const2 = lambda j, *_: (0, 0)

    def wspec(shape, which, s):
        return pl.BlockSpec(shape, lambda j, nt, nv, ea, eb, perm: ((ea, eb)[which][2 * j + s], 0, 0))

    wspecs = []
    for s in range(2):
        for which in range(2):
            wspecs += [wspec((1, D_MODEL, 2 * FF), which, s), wspec((1, FF, D_MODEL), which, s)]
    grid_spec = pltpu.PrefetchScalarGridSpec(
        num_scalar_prefetch=5,
        grid=(n_tiles // 2,),
        in_specs=[pl.BlockSpec(memory_space=pl.ANY),
                  pl.BlockSpec(gtab_hi.shape, const2),
                  pl.BlockSpec(gtab_lo.shape, const2),
                  pl.BlockSpec(ln2post.shape, const2)] + wspecs,
        out_specs=pl.BlockSpec(memory_space=pl.ANY),
        scratch_shapes=[pltpu.VMEM((TM, ROW_W), U32),
                        pltpu.VMEM((TM, ROW_W), U32),
                        pltpu.VMEM((TM, D_MODEL), F32),
                        pltpu.VMEM((TM, D_MODEL), F32),
                        pltpu.SemaphoreType.DMA((2,)),
                        pltpu.SemaphoreType.DMA((2,))],
    )
    return pl.pallas_call(
        _moe_kernel,
        grid_spec=grid_spec,
        out_shape=jax.ShapeDtypeStruct((n, D_MODEL), F32),
        compiler_params=pltpu.CompilerParams(dimension_semantics=("arbitrary",),
                                             vmem_limit_bytes=VMEM_LIMIT),
        name="moe",
    )(nt, nv, ea, eb, perm, rows, gtab_hi, gtab_lo, ln2post, wgu, wd, wgu, wd, wgu, wd, wgu, wd)


def kernel(x, c, ln1_pre, ln1_post, ln2_pre, ln2_post, w_ada, b_ada, w_in, attn_sinks, attn_out_norm,
           hgrn_lb, hgrn_out_norm, w_out, w_router_group, b_router_group, w_router_expert,
           b_router_expert, w_exp_gate, w_exp_up, w_exp_down):
    bsz, seq, d = x.shape
    assert d == D_MODEL and seq % TQ == 0 and w_ada.shape[0] == 1 and hgrn_lb.shape[0] == 2
    n = bsz * seq

    mod = _ada(c, w_ada[0], b_ada[0])
    mod3 = mod.reshape(bsz, 6, d)

    wr = jnp.concatenate([w_router_group[0].T, jnp.zeros((8 - N_GROUPS, d), F32), w_router_expert[0].T], axis=0)
    br = jnp.concatenate([b_router_group[0], jnp.full((8 - N_GROUPS,), NEG, F32), b_router_expert[0]])
    wr_hi = wr.astype(BF16)
    wr_lo = (wr - wr_hi.astype(F32)).astype(BF16)

    x1ext, info, cnt = _mixer(
        x.reshape(n, d), mod3, attn_sinks[0], ln1_pre, ln1_post, ln2_pre, attn_out_norm, hgrn_out_norm,
        hgrn_lb, w_in[0].astype(BF16), w_out[0].astype(BF16), wr_hi, wr_lo, br.reshape(ROUTER_ROWS, 1),
        bsz, seq)

    n_tiles = n // TM + N_BUCKETS
    counts = cnt[:N_BUCKETS, 0].astype(I32)
    tiles_per = (counts + TM - 1) // TM
    tile_end = jnp.cumsum(tiles_per)
    tile_start = tile_end - tiles_per
    nt = tile_end[-1]
    bucket = info[:, 0, :].reshape(n)
    rank = info[:, 1, :].reshape(n)
    tid = jnp.arange(n_tiles, dtype=I32)
    tb = jnp.searchsorted(tile_end, tid, side="right").astype(I32)
    tb = jnp.where(tid < nt, tb, tb[nt - 1])
    nv = jnp.where(tid < nt, jnp.clip(counts[tb] - (tid - tile_start[tb]) * TM, 0, TM), 0).astype(I32)
    grp, pair = tb // N_PAIRS, tb % N_PAIRS
    ea = (grp * EPG + jnp.asarray(_PAIR_A)[pair]).astype(I32)
    eb = (grp * EPG + jnp.asarray(_PAIR_B)[pair]).astype(I32)

    pad128 = lambda a: jnp.concatenate([a, jnp.zeros((LANES - a.shape[0],), I32)])
    row_start = jnp.concatenate([tile_start, nt.reshape(1)]) * TM
    perm = _perm(pad128(row_start), pad128(counts), bucket, rank, n_tiles * TM)

    wgu = jnp.concatenate([w_exp_gate[0], w_exp_up[0]], axis=-1).astype(BF16)
    wd = w_exp_down[0].astype(BF16)
    gtab = jnp.zeros((LANES, d), F32).at[EXT_BATCH0:EXT_BATCH0 + bsz].set(mod3[:, 5, :])
    gtab_hi = gtab.astype(BF16)
    gtab_lo = (gtab - gtab_hi.astype(F32)).astype(BF16)
    out = _moe(x1ext, gtab_hi, gtab_lo, ln2_post, wgu, wd, nt.reshape(1), nv, ea, eb, perm, n_tiles)
    return out.reshape(bsz, seq, d)
```

```python
import functools

import numpy as np
import jax
import jax.numpy as jnp
from jax import lax
from jax.experimental import pallas as pl
from jax.experimental.pallas import tpu as pltpu

F32 = jnp.float32
BF16 = jnp.bfloat16
I32 = jnp.int32

D_MODEL = 1024
ATTN_HEADS = 8
HEAD_DIM = 64
WINDOW = 128
ATTN_W = 512
KV_W = 128
HG_HEADS = 4
HG_DIM = 128
HG_W = 512
HG_CHUNK = 32
IN_W = 2816
N_GROUPS = 4
EPG = 8
N_EXPERTS = 32
FF = 256
N_PAIRS = EPG * (EPG - 1) // 2
N_BUCKETS = N_GROUPS * N_PAIRS
EPS = 1e-6
NEG = -1e30

LANES = 128
H2P_W = D_MODEL // 2
EXT_W = LANES
OFF_H2P = D_MODEL
OFF_EXT = D_MODEL + H2P_W
ROW_W = OFF_EXT + EXT_W
EXT_BATCH0 = 8
U32 = jnp.uint32
ROUTER_ROWS = 8 + N_EXPERTS

TQ = 256
TM = 128
VMEM_LIMIT = 56 * 1024 * 1024

OFF_Q, OFF_K, OFF_V, OFF_HQ, OFF_HF, OFF_HI, OFF_HG = 0, 512, 640, 768, 1280, 1792, 2304

_PAIR_A = np.array([a for a in range(EPG) for b in range(a + 1, EPG)], np.int32)
_PAIR_B = np.array([b for a in range(EPG) for b in range(a + 1, EPG)], np.int32)


def _dot(a, b):
    return jnp.dot(a, b, preferred_element_type=F32)


def _dot_nt(a, b):
    return lax.dot_general(a, b, (((1,), (1,)), ((), ())), preferred_element_type=F32)


def _dot_tn(a, b):
    return lax.dot_general(a, b, (((0,), (0,)), ((), ())), preferred_element_type=F32)


def _split(a):
    hi = a.astype(BF16)
    lo = (a - hi.astype(F32)).astype(BF16)
    return hi, lo


def _rms(x):
    return x * lax.rsqrt(jnp.mean(x * x, axis=-1, keepdims=True) + EPS)


def _sigmoid(x):
    return 1.0 / (1.0 + jnp.exp(-x))


def _ada_kernel(c_ref, w_ref, b_ref, o_ref):
    c = c_ref[...]
    ca = c * _sigmoid(c)
    c_hi, c_lo = _split(ca)
    w_hi, w_lo = _split(w_ref[...])
    o_ref[...] = _dot(c_hi, w_hi) + _dot(c_lo, w_hi) + _dot(c_hi, w_lo) + b_ref[...]


def _ada(c, w, b):
    bsz, d = c.shape
    n_out = w.shape[1]
    return pl.pallas_call(
        _ada_kernel,
        grid=(n_out // d,),
        in_specs=[pl.BlockSpec((bsz, d), lambda j: (0, 0)),
                  pl.BlockSpec((d, d), lambda j: (0, j)),
                  pl.BlockSpec((1, d), lambda j: (0, j))],
        out_specs=pl.BlockSpec((bsz, d), lambda j: (0, j)),
        out_shape=jax.ShapeDtypeStruct((bsz, n_out), F32),
        compiler_params=pltpu.CompilerParams(dimension_semantics=("arbitrary",),
                                             vmem_limit_bytes=VMEM_LIMIT),
        name="adaln",
    )(c, w, b.reshape(1, n_out))


def _attention(proj, kprev_ref, vprev_ref, sinks_ref, t, s_scr, m_scr, p_scr):
    tq = proj.shape[0]
    q = (proj[:, OFF_Q:OFF_Q + ATTN_W] * (HEAD_DIM ** -0.5)).astype(BF16)
    kf = jnp.concatenate([kprev_ref[...], proj[:, OFF_K:OFF_K + KV_W]], axis=0)
    vf = jnp.concatenate([vprev_ref[...], proj[:, OFF_V:OFF_V + KV_W]], axis=0)
    kprev_ref[...] = proj[tq - WINDOW:, OFF_K:OFF_K + KV_W]
    vprev_ref[...] = proj[tq - WINDOW:, OFF_V:OFF_V + KV_W]

    lo = lax.broadcasted_iota(I32, kf.shape, 1) < HEAD_DIM
    kr = pltpu.roll(kf, HEAD_DIM, axis=1)
    vr = pltpu.roll(vf, HEAD_DIM, axis=1)

    def variants(a, ar):
        return [[jnp.where(lo, a, 0.0).astype(BF16), jnp.where(lo, 0.0, ar).astype(BF16)],
                [jnp.where(lo, ar, 0.0).astype(BF16), jnp.where(lo, 0.0, a).astype(BF16)]]

    kvar = variants(kf, kr)
    vvar = variants(vf, vr)

    qi = lax.broadcasted_iota(I32, (WINDOW, 2 * WINDOW), 0)
    kj = lax.broadcasted_iota(I32, (WINDOW, 2 * WINDOW), 1)
    dist = qi + WINDOW - kj
    in_win = (dist >= 0) & (dist < WINDOW)
    distf = dist.astype(F32)
    kj_first = kj + jnp.where(t > 0, WINDOW, 0)

    nblk = tq // WINDOW
    idx = lambda j, h: j * ATTN_HEADS + h
    keys = lambda a, j: a[j * WINDOW:(j + 2) * WINDOW]

    def scores():
        for j in range(nblk):
            valid = (in_win & (kj_first >= WINDOW)) if j == 0 else in_win
            for h in range(ATTN_HEADS):
                p, par = h // 2, h % 2
                slope = 2.0 ** (-8.0 * (h + 1) / ATTN_HEADS)
                qp = q[j * WINDOW:(j + 1) * WINDOW, p * LANES:(p + 1) * LANES]
                s = _dot_nt(qp, keys(kvar[p // 2][par], j))
                s = jnp.where(valid, s - slope * distf, NEG)
                s_scr[idx(j, h)] = s
                m_scr[idx(j, h)] = jnp.maximum(jnp.max(s, axis=-1, keepdims=True), sinks_ref[h])

    def exps():
        for j in range(nblk):
            for h in range(ATTN_HEADS):
                m = m_scr[idx(j, h)]
                pe = jnp.exp(s_scr[idx(j, h)] - m)
                p_scr[idx(j, h)] = pe.astype(BF16)
                m_scr[idx(j, h)] = 1.0 / (jnp.sum(pe, axis=-1, keepdims=True) + jnp.exp(sinks_ref[h] - m))

    def values():
        blocks = []
        for j in range(nblk):
            pairs = []
            for p in range(ATTN_HEADS // 2):
                acc = None
                for par in range(2):
                    h = 2 * p + par
                    o = _dot(p_scr[idx(j, h)], keys(vvar[p // 2][par], j)) * m_scr[idx(j, h)]
                    acc = o if acc is None else acc + o
                pairs.append(acc)
            blocks.append(jnp.concatenate(pairs, axis=1))
        return jnp.concatenate(blocks, axis=0)

    return scores, exps, values


def _hgrn2(proj, lb, st_ref, hnorm, u_scr, stb_scr):
    tq = proj.shape[0]
    nc = tq // HG_CHUNK
    qr = proj[:, OFF_HQ:OFF_HQ + HG_W]
    fr = proj[:, OFF_HF:OFF_HF + HG_W]
    iv = proj[:, OFF_HI:OFF_HI + HG_W]
    gr = proj[:, OFF_HG:OFF_HG + HG_W]
    qh = qr * _sigmoid(qr)
    f = lb + (1.0 - lb) * _sigmoid(fr)
    kk = 1.0 - f
    logf = jnp.log(f)

    rmod = lax.broadcasted_iota(I32, (tq, HG_W), 0) & (HG_CHUNK - 1)
    bc = logf
    s = 1
    while s < HG_CHUNK:
        bc = bc + jnp.where(rmod >= s, pltpu.roll(bc, s, axis=0), 0.0)
        s *= 2

    b3 = bc.reshape(nc, HG_CHUNK, HG_W)
    blast = b3[:, HG_CHUNK - 1:HG_CHUNK, :]
    kend = (kk.reshape(nc, HG_CHUNK, HG_W) * jnp.exp(blast - b3)).reshape(tq, HG_W)
    decay = jnp.exp(blast).reshape(nc, HG_W)
    qdec = (qh * jnp.exp(bc)).astype(BF16)
    kdec = (kk * jnp.exp(-bc)).astype(BF16)
    kend = kend.astype(BF16)
    ivb = iv.astype(BF16)

    ri = lax.broadcasted_iota(I32, (tq, tq), 0)
    ci = lax.broadcasted_iota(I32, (tq, tq), 1)
    cmask = ((ri // HG_CHUNK) == (ci // HG_CHUNK)) & (ri >= ci)

    heads = [slice(hh * HG_DIM, (hh + 1) * HG_DIM) for hh in range(HG_HEADS)]
    chunks = [slice(n * HG_CHUNK, (n + 1) * HG_CHUNK) for n in range(nc)]

    lane_head = lax.broadcasted_iota(I32, (HG_CHUNK, HG_W), 1) // HG_DIM
    for n, rs in enumerate(chunks):
        vstack = jnp.concatenate([ivb[rs, sl] for sl in heads], axis=0)
        kblk = jnp.concatenate([jnp.where(lane_head == hh, kend[rs], 0.0).astype(BF16)
                                for hh in range(HG_HEADS)], axis=0)
        u_scr[n] = _dot_tn(vstack, kblk)

    st = st_ref[...]
    for n in range(nc):
        stb_scr[n] = st.astype(BF16)
        st = st * decay[n:n + 1] + u_scr[n]
    st_ref[...] = st

    outs = []
    for hh, sl in enumerate(heads):
        a = _dot_nt(qdec[:, sl], kdec[:, sl])
        a = jnp.where(cmask, a, 0.0).astype(BF16)
        o_intra = _dot(a, ivb[:, sl])
        inter = [_dot_nt(qdec[rs, sl], stb_scr[n, :, sl]) for n, rs in enumerate(chunks)]
        o = o_intra + jnp.concatenate(inter, axis=0)
        o = _rms(o) * hnorm[:, sl]
        g = gr[:, sl]
        outs.append(o * (g * _sigmoid(g)))
    return jnp.concatenate(outs, axis=1)


def _route_topk(h2, wr_hi, wr_lo, br):
    tq = h2.shape[0]
    h_hi, h_lo = _split(h2)
    logits = _dot(h_hi, wr_hi) + _dot(h_lo, wr_hi) + _dot(h_hi, wr_lo)
    lt = logits.T[0:ROUTER_ROWS] + br
    sub = lax.broadcasted_iota(I32, (8, tq), 0).astype(F32)

    gl = lt[0:8]
    gm = jnp.max(gl, axis=0, keepdims=True)
    gidx = jnp.min(jnp.where(gl == gm, sub, 8.0), axis=0, keepdims=True)
    g_w = 1.0 / jnp.sum(jnp.exp(gl - gm), axis=0, keepdims=True)

    es = lt[8:16]
    for g in range(1, N_GROUPS):
        es = jnp.where(gidx == float(g), lt[8 + 8 * g:16 + 8 * g], es)
    m1 = jnp.max(es, axis=0, keepdims=True)
    i1 = jnp.min(jnp.where(es == m1, sub, 8.0), axis=0, keepdims=True)
    e2 = jnp.where(sub == i1, NEG, es)
    m2 = jnp.max(e2, axis=0, keepdims=True)
    i2 = jnp.min(jnp.where(e2 == m2, sub, 8.0), axis=0, keepdims=True)
    dd = jnp.exp(m2 - m1)
    w1 = g_w / (1.0 + dd)
    w2 = g_w * dd / (1.0 + dd)
    first_low = i1 < i2
    ea = jnp.minimum(i1, i2)
    eb = jnp.maximum(i1, i2)
    w_lo = jnp.where(first_low, w1, w2)
    w_hi = jnp.where(first_low, w2, w1)
    pair = ea * (15.0 - ea) * 0.5 + (eb - ea - 1.0)
    bucket = gidx * float(N_PAIRS) + pair
    return bucket, w_lo, w_hi


def _route_rank(bucket, w_lo, w_hi, carry_ref, bidx, live):
    tq = bucket.shape[1]
    brow = lax.broadcasted_iota(I32, (LANES, tq), 0).astype(F32)
    onehot = brow == bucket
    oh = jnp.where(onehot, live, 0.0)
    ti = lax.broadcasted_iota(I32, (tq, tq), 0)
    tj = lax.broadcasted_iota(I32, (tq, tq), 1)
    upper = jnp.where(ti < tj, 1.0, 0.0).astype(BF16)
    before = _dot(oh.astype(BF16), upper) + carry_ref[...]
    rank = jnp.sum(jnp.where(onehot, before, 0.0), axis=0, keepdims=True)
    carry_ref[...] = carry_ref[...] + jnp.sum(oh, axis=1, keepdims=True)

    lane_row = lax.broadcasted_iota(I32, (LANES - EXT_BATCH0, tq), 0)
    onehot_b = jnp.where(lane_row == bidx, 1.0, 0.0)
    info = jnp.concatenate([w_lo, w_hi, jnp.zeros((EXT_BATCH0 - 2, tq), F32), onehot_b], axis=0)
    return rank, info.T


def _mixer_kernel(sinks_ref, x_ref, mod_ref, ln1pre_ref, ln1post_ref, ln2pre_ref, anorm_ref, hnorm_ref,
                  lb_ref, win_ref, wout_ref, wrhi_ref, wrlo_ref, br_ref,
                  rows_hbm, info_ref, cnt_ref,
                  kprev_ref, vprev_ref, st_ref, carry_ref, s_scr, m_scr, p_scr, u_scr, stb_scr,
                  h2_scr, proj_scr, rowbuf, rsem, *, tiles_per_seq):
    s = pl.program_id(0)
    n_tiles = pl.num_programs(0) - 1
    tq = x_ref.shape[0]
    t = lax.rem(jnp.minimum(s, n_tiles - 1), tiles_per_seq)
    slot = lax.rem(s, 2)
    bits = lambda a: pltpu.bitcast(a, U32)

    def row_copy(step_slot, tile):
        return pltpu.make_async_copy(rowbuf.at[step_slot], rows_hbm.at[pl.ds(pl.multiple_of(tile * tq, tq), tq)],
                                     rsem.at[step_slot])

    @pl.when(s == 0)
    def _():
        carry_ref[...] = jnp.zeros_like(carry_ref)
        h2_scr[...] = jnp.zeros_like(h2_scr)

    @pl.when(t == 0)
    def _():
        st_ref[...] = jnp.zeros_like(st_ref)
        kprev_ref[...] = jnp.zeros_like(kprev_ref)
        vprev_ref[...] = jnp.zeros_like(vprev_ref)

    x = x_ref[...]
    mod = mod_ref[0]
    sh1, sc1, ga1, sh2, sc2 = mod[0:1], mod[1:2], mod[2:3], mod[3:4], mod[4:5]
    prev = jnp.maximum(s - 1, 0)

    @pl.when(s <= n_tiles)
    def _():
        bucket, w_lo, w_hi = _route_topk(h2_scr[...], wrhi_ref[...], wrlo_ref[...], br_ref[...])
        h = _rms(x) * ln1pre_ref[...] * (1.0 + sc1) + sh1
        proj_scr[...] = _dot(h.astype(BF16), win_ref[...])
        live = jnp.where(s >= 1, 1.0, 0.0)
        rank, ext = _route_rank(bucket, w_lo, w_hi, carry_ref, prev // tiles_per_seq, live)
        rowbuf[1 - slot, :, OFF_EXT:ROW_W] = bits(ext)
        info_ref[0] = jnp.concatenate([bucket, rank, jnp.zeros((6, tq), F32)], axis=0).astype(I32)
        cnt_ref[...] = jnp.broadcast_to(carry_ref[...], cnt_ref.shape)

    proj = proj_scr[...]
    scores, exps, values = _attention(proj, kprev_ref, vprev_ref, sinks_ref, t, s_scr, m_scr, p_scr)
    scores()
    exps()
    attn = _rms(values()) * anorm_ref[...]

    lbr = lb_ref[...]
    le = jnp.exp(lbr - jnp.max(lbr, axis=0, keepdims=True))
    lb = le[0:1] / jnp.sum(le, axis=0, keepdims=True)
    hg = _hgrn2(proj, lb, st_ref, hnorm_ref[...], u_scr, stb_scr)

    mix = _dot(jnp.concatenate([attn, hg], axis=1).astype(BF16), wout_ref[...])
    x1 = x + ga1 * (_rms(mix) * ln1post_ref[...])

    h2 = _rms(x1) * ln2pre_ref[...] * (1.0 + sc2) + sh2
    h2r = h2.astype(BF16).astype(F32)

    @pl.when(s >= 2)
    def _():
        row_copy(slot, s - 2).wait()

    rowbuf[slot, :, 0:D_MODEL] = bits(x1)
    rowbuf[slot, :, OFF_H2P:OFF_EXT] = ((bits(h2r[:, 0:H2P_W]) >> 16)
                                        | (bits(h2r[:, H2P_W:D_MODEL]) & jnp.uint32(0xFFFF0000)))
    h2_scr[...] = h2

    @pl.when(s >= 1)
    def _():
        row_copy(1 - slot, prev).start()

    @pl.when(s == n_tiles)
    def _():
        row_copy(1 - slot, prev).wait()


def _mixer(x2, mod3, sinks, ln1pre, ln1post, ln2pre, anorm, hnorm, lb, win, wout, wr_hi, wr_lo, br,
           bsz, seq):
    n = bsz * seq
    nt = seq // TQ
    n_tiles = bsz * nt
    n_sc = (TQ // WINDOW) * ATTN_HEADS
    cur = lambda s: jnp.minimum(s, n_tiles - 1)
    const = lambda s: (0, 0)
    full = lambda a: pl.BlockSpec(a.shape, const)
    return pl.pallas_call(
        functools.partial(_mixer_kernel, tiles_per_seq=nt),
        grid=(n_tiles + 1,),
        in_specs=[pl.BlockSpec(memory_space=pltpu.SMEM),
                  pl.BlockSpec((TQ, D_MODEL), lambda s: (cur(s), 0)),
                  pl.BlockSpec((1, 6, D_MODEL), lambda s: (cur(s) // nt, 0, 0)),
                  full(ln1pre), full(ln1post), full(ln2pre), full(anorm), full(hnorm), full(lb),
                  full(win), full(wout), full(wr_hi), full(wr_lo), full(br)],
        out_specs=[pl.BlockSpec(memory_space=pl.ANY),
                   pl.BlockSpec((1, 8, TQ), lambda s: (jnp.maximum(s - 1, 0), 0, 0)),
                   pl.BlockSpec((LANES, LANES), const)],
        out_shape=[jax.ShapeDtypeStruct((n, ROW_W), U32),
                   jax.ShapeDtypeStruct((n // TQ, 8, TQ), I32),
                   jax.ShapeDtypeStruct((LANES, LANES), F32)],
        scratch_shapes=[pltpu.VMEM((WINDOW, KV_W), F32),
                        pltpu.VMEM((WINDOW, KV_W), F32),
                        pltpu.VMEM((HG_DIM, HG_W), F32),
                        pltpu.VMEM((LANES, 1), F32),
                        pltpu.VMEM((n_sc, WINDOW, 2 * WINDOW), F32),
                        pltpu.VMEM((n_sc, WINDOW, 1), F32),
                        pltpu.VMEM((n_sc, WINDOW, 2 * WINDOW), BF16),
                        pltpu.VMEM((TQ // HG_CHUNK, HG_DIM, HG_W), F32),
                        pltpu.VMEM((TQ // HG_CHUNK, HG_DIM, HG_W), BF16),
                        pltpu.VMEM((TQ, D_MODEL), F32),
                        pltpu.VMEM((TQ, IN_W), F32),
                        pltpu.VMEM((2, TQ, ROW_W), U32),
                        pltpu.SemaphoreType.DMA((2,))],
        compiler_params=pltpu.CompilerParams(dimension_semantics=("arbitrary",),
                                             vmem_limit_bytes=VMEM_LIMIT),
        name="mixer",
    )(sinks, x2, mod3, ln1pre, ln1post, ln2pre, anorm, hnorm, lb, win, wout, wr_hi, wr_lo, br)


PERM_STEPS = 8
PERM_ROWS = 8
PERM_UNROLL = 8


def _perm_kernel(rs_ref, cnt_ref, bucket_ref, rank_ref, perm_ref, pos_vmem, pos_smem, sem):
    pid = pl.program_id(0)
    rows, cols = pos_vmem.shape

    b = bucket_ref[0]
    start = jnp.zeros_like(b)
    for k in range(N_BUCKETS):
        start = jnp.where(b == k, rs_ref[k], start)
    pos_vmem[...] = start + rank_ref[0]
    copies = [pltpu.make_async_copy(pos_vmem.at[r], pos_smem.at[pl.ds(r * cols, cols)], sem) for r in range(rows)]
    for cp in copies:
        cp.start()

    @pl.when(pid == 0)
    def _():
        def per_bucket(k, carry):
            first = rs_ref[k]
            cnt = cnt_ref[k]

            def pad(r, c2):
                perm_ref[first + r] = 0
                return c2

            lax.fori_loop(cnt, ((cnt + TM - 1) // TM) * TM, pad, 0)
            return carry

        lax.fori_loop(0, N_BUCKETS, per_bucket, 0)

        def tail(r, carry):
            perm_ref[r] = 0
            return carry

        lax.fori_loop(rs_ref[N_BUCKETS], perm_ref.shape[0], tail, 0)

    for cp in copies:
        cp.wait()
    base = pid * (rows * cols)

    def body(j, carry):
        i0 = j * PERM_UNROLL
        for u in range(PERM_UNROLL):
            perm_ref[pos_smem[i0 + u]] = base + i0 + u
        return carry

    lax.fori_loop(0, rows * cols // PERM_UNROLL, body, 0)


def _perm(row_start, counts, bucket, rank, n_rows):
    n = bucket.shape[0]
    cols = n // (PERM_STEPS * PERM_ROWS)
    assert n % (PERM_STEPS * PERM_ROWS * PERM_UNROLL) == 0
    chunked = lambda a: a.reshape(PERM_STEPS, PERM_ROWS, cols)
    chunk_spec = pl.BlockSpec((1, PERM_ROWS, cols), lambda i: (i, 0, 0))
    return pl.pallas_call(
        _perm_kernel,
        grid=(PERM_STEPS,),
        in_specs=[pl.BlockSpec(memory_space=pltpu.SMEM),
                  pl.BlockSpec(memory_space=pltpu.SMEM),
                  chunk_spec, chunk_spec],
        out_specs=pl.BlockSpec(memory_space=pltpu.SMEM),
        out_shape=jax.ShapeDtypeStruct((n_rows,), I32),
        scratch_shapes=[pltpu.VMEM((PERM_ROWS, cols), I32), pltpu.SMEM((PERM_ROWS * cols,), I32),
                        pltpu.SemaphoreType.DMA(())],
        compiler_params=pltpu.CompilerParams(dimension_semantics=("arbitrary",)),
        name="perm",
    )(row_start, counts, chunked(bucket), chunked(rank))


GATHER_DEPTH = 2


def _moe_kernel(nt_ref, nv_ref, ea_ref, eb_ref, perm_ref,
                rows_hbm, gtab_hi_ref, gtab_lo_ref, ln2post_ref, wgu_a, wd_a, wgu_b, wd_b,
                out_hbm, xbuf, obuf, gsem, ssem):
    i = pl.program_id(0)
    nt = nt_ref[0]
    last_tile = pl.num_programs(0) - 1
    nbuf = xbuf.shape[0]

    def start_gather(tile, pred):
        sl = lax.rem(tile, nbuf)
        base = jnp.minimum(tile, last_tile) * TM
        for r in range(TM):
            @pl.when(pred)
            def _():
                tok = perm_ref[base + r]
                pltpu.make_async_copy(rows_hbm.at[pl.ds(tok, 1)], xbuf.at[sl, pl.ds(r, 1)], gsem.at[sl]).start()

    def wait_gather(tile):
        sl = lax.rem(tile, nbuf)
        pltpu.make_async_copy(rows_hbm.at[pl.ds(0, TM)], xbuf.at[sl], gsem.at[sl]).wait()

    def wait_scatter(sl, nv):
        @pl.when(nv == TM)
        def _():
            pltpu.make_async_copy(obuf.at[sl], out_hbm.at[pl.ds(0, TM)], ssem.at[sl]).wait()

        @pl.when(nv < TM)
        def _():
            def one(r, carry):
                pltpu.make_async_copy(obuf.at[sl, pl.ds(0, 1)], out_hbm.at[pl.ds(0, 1)], ssem.at[sl]).wait()
                return carry

            lax.fori_loop(0, nv, one, 0)

    def compute(xb):
        x1 = pltpu.bitcast(xb[:, 0:D_MODEL], F32)
        hp = xb[:, OFF_H2P:OFF_EXT]
        h2a = pltpu.bitcast(hp << 16, F32).astype(BF16)
        h2b = pltpu.bitcast(hp & jnp.uint32(0xFFFF0000), F32).astype(BF16)
        ext = pltpu.bitcast(xb[:, OFF_EXT:ROW_W], F32)
        w_lo, w_hi = ext[:, 0:1], ext[:, 1:2]
        sel = ext.astype(BF16)
        ga2 = _dot(sel, gtab_hi_ref[...]) + _dot(sel, gtab_lo_ref[...])

        def expert(wgu_ref, wd_ref):
            gu = _dot(h2a, wgu_ref[0, 0:H2P_W]) + _dot(h2b, wgu_ref[0, H2P_W:D_MODEL])
            hg, hu = gu[:, 0:FF], gu[:, FF:2 * FF]
            act = (hg * _sigmoid(hg)) * hu
            return _dot(act.astype(BF16), wd_ref[0])

        y = w_lo * expert(wgu_a, wd_a) + w_hi * expert(wgu_b, wd_b)
        return x1 + ga2 * (_rms(y) * ln2post_ref[...])

    @pl.when(i == 0)
    def _():
        for d in range(GATHER_DEPTH):
            start_gather(d, d < nt)

    @pl.when(i < nt)
    def _():
        nv = nv_ref[i]
        osl = lax.rem(i, 2)
        wait_gather(i)

        @pl.when(i >= 2)
        def _():
            wait_scatter(osl, nv_ref[jnp.maximum(i - 2, 0)])

        start_gather(i + GATHER_DEPTH, i + GATHER_DEPTH < nt)
        obuf[osl] = compute(xbuf[lax.rem(i, nbuf)])

        for r in range(TM):
            @pl.when(r < nv)
            def _():
                tok = perm_ref[i * TM + r]
                pltpu.make_async_copy(obuf.at[osl, pl.ds(r, 1)], out_hbm.at[pl.ds(tok, 1)], ssem.at[osl]).start()

        @pl.when(i == nt - 1)
        def _():
            wait_scatter(osl, nv)

            @pl.when(i >= 1)
            def _():
                wait_scatter(1 - osl, nv_ref[jnp.maximum(i - 1, 0)])


def _moe(rows, gtab_hi, gtab_lo, ln2post, wgu, wd, nt, nv, ea, eb, perm, n_tiles):
    n = rows.shape[0]
    const2 = lambda i, *_: (0, 0)
    grid_spec = pltpu.PrefetchScalarGridSpec(
        num_scalar_prefetch=5,
        grid=(n_tiles,),
        in_specs=[pl.BlockSpec(memory_space=pl.ANY),
                  pl.BlockSpec(gtab_hi.shape, const2),
                  pl.BlockSpec(gtab_lo.shape, const2),
                  pl.BlockSpec(ln2post.shape, const2),
                  pl.BlockSpec((1, D_MODEL, 2 * FF), lambda i, nt, nv, ea, eb, perm: (ea[i], 0, 0)),
                  pl.BlockSpec((1, FF, D_MODEL), lambda i, nt, nv, ea, eb, perm: (ea[i], 0, 0)),
                  pl.BlockSpec((1, D_MODEL, 2 * FF), lambda i, nt, nv, ea, eb, perm: (eb[i], 0, 0)),
                  pl.BlockSpec((1, FF, D_MODEL), lambda i, nt, nv, ea, eb, perm: (eb[i], 0, 0))],
        out_specs=pl.BlockSpec(memory_space=pl.ANY),
        scratch_shapes=[pltpu.VMEM((GATHER_DEPTH + 1, TM, ROW_W), U32),
                        pltpu.VMEM((2, TM, D_MODEL), F32),
                        pltpu.SemaphoreType.DMA((GATHER_DEPTH + 1,)),
                        pltpu.SemaphoreType.DMA((2,))],
    )
    return pl.pallas_call(
        _moe_kernel,
        grid_spec=grid_spec,
        out_shape=jax.ShapeDtypeStruct((n, D_MODEL), F32),
        compiler_params=pltpu.CompilerParams(dimension_semantics=("arbitrary",),
                                             vmem_limit_bytes=VMEM_LIMIT),
        name="moe",
    )(nt, nv, ea, eb, perm, rows, gtab_hi, gtab_lo, ln2post, wgu, wd, wgu, wd)


def kernel(x, c, ln1_pre, ln1_post, ln2_pre, ln2_post, w_ada, b_ada, w_in, attn_sinks, attn_out_norm,
           hgrn_lb, hgrn_out_norm, w_out, w_router_group, b_router_group, w_router_expert,
           b_router_expert, w_exp_gate, w_exp_up, w_exp_down):
    bsz, seq, d = x.shape
    assert d == D_MODEL and seq % TQ == 0 and w_ada.shape[0] == 1 and hgrn_lb.shape[0] == 2
    n = bsz * seq

    mod = _ada(c, w_ada[0], b_ada[0])
    mod3 = mod.reshape(bsz, 6, d)

    wr = jnp.concatenate([w_router_group[0], jnp.zeros((d, 8 - N_GROUPS), F32), w_router_expert[0],
                          jnp.zeros((d, LANES - ROUTER_ROWS), F32)], axis=1)
    br = jnp.concatenate([b_router_group[0], jnp.full((8 - N_GROUPS,), NEG, F32), b_router_expert[0]])
    wr_hi = wr.astype(BF16)
    wr_lo = (wr - wr_hi.astype(F32)).astype(BF16)

    x1ext, info, cnt = _mixer(
        x.reshape(n, d), mod3, attn_sinks[0], ln1_pre, ln1_post, ln2_pre, attn_out_norm, hgrn_out_norm,
        hgrn_lb, w_in[0].astype(BF16), w_out[0].astype(BF16), wr_hi, wr_lo, br.reshape(ROUTER_ROWS, 1),
        bsz, seq)

    n_tiles = n // TM + N_BUCKETS
    counts = cnt[:N_BUCKETS, 0].astype(I32)
    tiles_per = (counts + TM - 1) // TM
    tile_end = jnp.cumsum(tiles_per)
    tile_start = tile_end - tiles_per
    nt = tile_end[-1]
    bucket = info[:, 0, :].reshape(n)
    rank = info[:, 1, :].reshape(n)
    tid = jnp.arange(n_tiles, dtype=I32)[None, :]
    member = (tid >= tile_start[:, None]) & (tid < tile_end[:, None])
    pick = lambda per_bucket: jnp.sum(jnp.where(member, per_bucket, 0), axis=0).astype(I32)
    bidx = np.arange(N_BUCKETS, dtype=np.int32)
    ea_of = jnp.asarray((bidx // N_PAIRS) * EPG + _PAIR_A[bidx % N_PAIRS])[:, None]
    eb_of = jnp.asarray((bidx // N_PAIRS) * EPG + _PAIR_B[bidx % N_PAIRS])[:, None]
    last_used = jnp.arange(N_BUCKETS)[:, None] == jnp.max(jnp.where(tiles_per > 0, jnp.arange(N_BUCKETS), 0))
    unused = tid[0] >= nt
    nv = pick(jnp.clip(counts[:, None] - (tid - tile_start[:, None]) * TM, 0, TM))
    ea = jnp.where(unused, jnp.sum(jnp.where(last_used, ea_of, 0)), pick(ea_of)).astype(I32)
    eb = jnp.where(unused, jnp.sum(jnp.where(last_used, eb_of, 0)), pick(eb_of)).astype(I32)

    pad128 = lambda a: jnp.concatenate([a, jnp.zeros((LANES - a.shape[0],), I32)])
    row_start = jnp.concatenate([tile_start, nt.reshape(1)]) * TM
    perm = _perm(pad128(row_start), pad128(counts), bucket, rank, n_tiles * TM)

    wgu = jnp.concatenate([w_exp_gate[0], w_exp_up[0]], axis=-1).astype(BF16)
    wd = w_exp_down[0].astype(BF16)
    gtab = jnp.zeros((LANES, d), F32).at[EXT_BATCH0:EXT_BATCH0 + bsz].set(mod3[:, 5, :])
    gtab_hi = gtab.astype(BF16)
    gtab_lo = (gtab - gtab_hi.astype(F32)).astype(BF16)
    out = _moe(x1ext, gtab_hi, gtab_lo, ln2_post, wgu, wd, nt.reshape(1), nv, ea, eb, perm, n_tiles)
    return out.reshape(bsz, seq, d)
```

```python
import functools

import numpy as np
import jax
import jax.numpy as jnp
from jax import lax
from jax.experimental import pallas as pl
from jax.experimental.pallas import tpu as pltpu

F32 = jnp.float32
BF16 = jnp.bfloat16
I32 = jnp.int32

D_MODEL = 1024
ATTN_HEADS = 8
HEAD_DIM = 64
WINDOW = 128
ATTN_W = 512
KV_W = 128
HG_HEADS = 4
HG_DIM = 128
HG_W = 512
HG_CHUNK = 32
IN_W = 2816
N_GROUPS = 4
EPG = 8
N_EXPERTS = 32
FF = 256
N_PAIRS = EPG * (EPG - 1) // 2
N_BUCKETS = N_GROUPS * N_PAIRS
EPS = 1e-6
NEG = -1e30

LANES = 128
H2P_W = D_MODEL // 2
EXT_W = LANES
OFF_H2P = D_MODEL
OFF_EXT = D_MODEL + H2P_W
ROW_W = OFF_EXT + EXT_W
EXT_BATCH0 = 8
U32 = jnp.uint32
ROUTER_ROWS = 8 + N_EXPERTS

TQ = 256
TM = 128
VMEM_LIMIT = 56 * 1024 * 1024

OFF_Q, OFF_K, OFF_V, OFF_HQ, OFF_HF, OFF_HI, OFF_HG = 0, 512, 640, 768, 1280, 1792, 2304

_PAIR_A = np.array([a for a in range(EPG) for b in range(a + 1, EPG)], np.int32)
_PAIR_B = np.array([b for a in range(EPG) for b in range(a + 1, EPG)], np.int32)


def _dot(a, b):
    return jnp.dot(a, b, preferred_element_type=F32)


def _dot_nt(a, b):
    return lax.dot_general(a, b, (((1,), (1,)), ((), ())), preferred_element_type=F32)


def _dot_tn(a, b):
    return lax.dot_general(a, b, (((0,), (0,)), ((), ())), preferred_element_type=F32)


def _split(a):
    hi = a.astype(BF16)
    lo = (a - hi.astype(F32)).astype(BF16)
    return hi, lo


def _rms(x):
    return x * lax.rsqrt(jnp.mean(x * x, axis=-1, keepdims=True) + EPS)


def _sigmoid(x):
    return 1.0 / (1.0 + jnp.exp(-x))


def _ada_kernel(c_ref, w_ref, b_ref, o_ref):
    c = c_ref[...]
    ca = c * _sigmoid(c)
    c_hi, c_lo = _split(ca)
    w_hi, w_lo = _split(w_ref[...])
    o_ref[...] = _dot(c_hi, w_hi) + _dot(c_lo, w_hi) + _dot(c_hi, w_lo) + b_ref[...]


def _ada(c, w, b):
    bsz, d = c.shape
    n_out = w.shape[1]
    return pl.pallas_call(
        _ada_kernel,
        grid=(n_out // d,),
        in_specs=[pl.BlockSpec((bsz, d), lambda j: (0, 0)),
                  pl.BlockSpec((d, d), lambda j: (0, j)),
                  pl.BlockSpec((1, d), lambda j: (0, j))],
        out_specs=pl.BlockSpec((bsz, d), lambda j: (0, j)),
        out_shape=jax.ShapeDtypeStruct((bsz, n_out), F32),
        compiler_params=pltpu.CompilerParams(dimension_semantics=("arbitrary",),
                                             vmem_limit_bytes=VMEM_LIMIT),
        name="adaln",
    )(c, w, b.reshape(1, n_out))


def _attention(proj, kprev_ref, vprev_ref, sinks_ref, t, s_scr, m_scr, p_scr):
    tq = proj.shape[0]
    q = (proj[:, OFF_Q:OFF_Q + ATTN_W] * (HEAD_DIM ** -0.5)).astype(BF16)
    kf = jnp.concatenate([kprev_ref[...], proj[:, OFF_K:OFF_K + KV_W]], axis=0)
    vf = jnp.concatenate([vprev_ref[...], proj[:, OFF_V:OFF_V + KV_W]], axis=0)
    kprev_ref[...] = proj[tq - WINDOW:, OFF_K:OFF_K + KV_W]
    vprev_ref[...] = proj[tq - WINDOW:, OFF_V:OFF_V + KV_W]

    lo = lax.broadcasted_iota(I32, kf.shape, 1) < HEAD_DIM
    kr = pltpu.roll(kf, HEAD_DIM, axis=1)
    vr = pltpu.roll(vf, HEAD_DIM, axis=1)

    def variants(a, ar):
        return [[jnp.where(lo, a, 0.0).astype(BF16), jnp.where(lo, 0.0, ar).astype(BF16)],
                [jnp.where(lo, ar, 0.0).astype(BF16), jnp.where(lo, 0.0, a).astype(BF16)]]

    kvar = variants(kf, kr)
    vvar = variants(vf, vr)

    qi = lax.broadcasted_iota(I32, (WINDOW, 2 * WINDOW), 0)
    kj = lax.broadcasted_iota(I32, (WINDOW, 2 * WINDOW), 1)
    dist = qi + WINDOW - kj
    in_win = (dist >= 0) & (dist < WINDOW)
    distf = dist.astype(F32)
    kj_first = kj + jnp.where(t > 0, WINDOW, 0)

    nblk = tq // WINDOW
    idx = lambda j, h: j * ATTN_HEADS + h
    keys = lambda a, j: a[j * WINDOW:(j + 2) * WINDOW]

    def scores():
        for j in range(nblk):
            valid = (in_win & (kj_first >= WINDOW)) if j == 0 else in_win
            for h in range(ATTN_HEADS):
                p, par = h // 2, h % 2
                slope = 2.0 ** (-8.0 * (h + 1) / ATTN_HEADS)
                qp = q[j * WINDOW:(j + 1) * WINDOW, p * LANES:(p + 1) * LANES]
                s = _dot_nt(qp, keys(kvar[p // 2][par], j))
                s = jnp.where(valid, s - slope * distf, NEG)
                s_scr[idx(j, h)] = s
                m_scr[idx(j, h)] = jnp.maximum(jnp.max(s, axis=-1, keepdims=True), sinks_ref[h])

    def exps():
        for j in range(nblk):
            for h in range(ATTN_HEADS):
                m = m_scr[idx(j, h)]
                pe = jnp.exp(s_scr[idx(j, h)] - m)
                p_scr[idx(j, h)] = pe.astype(BF16)
                m_scr[idx(j, h)] = 1.0 / (jnp.sum(pe, axis=-1, keepdims=True) + jnp.exp(sinks_ref[h] - m))

    def values():
        blocks = []
        for j in range(nblk):
            pairs = []
            for p in range(ATTN_HEADS // 2):
                acc = None
                for par in range(2):
                    h = 2 * p + par
                    o = _dot(p_scr[idx(j, h)], keys(vvar[p // 2][par], j)) * m_scr[idx(j, h)]
                    acc = o if acc is None else acc + o
                pairs.append(acc)
            blocks.append(jnp.concatenate(pairs, axis=1))
        return jnp.concatenate(blocks, axis=0)

    return scores, exps, values


def _hgrn2(proj, lb, st_ref, hnorm, u_scr, stb_scr):
    tq = proj.shape[0]
    nc = tq // HG_CHUNK
    qr = proj[:, OFF_HQ:OFF_HQ + HG_W]
    fr = proj[:, OFF_HF:OFF_HF + HG_W]
    iv = proj[:, OFF_HI:OFF_HI + HG_W]
    gr = proj[:, OFF_HG:OFF_HG + HG_W]
    qh = qr * _sigmoid(qr)
    f = lb + (1.0 - lb) * _sigmoid(fr)
    kk = 1.0 - f
    logf = jnp.log(f)

    rmod = lax.broadcasted_iota(I32, (tq, HG_W), 0) & (HG_CHUNK - 1)
    bc = logf
    s = 1
    while s < HG_CHUNK:
        bc = bc + jnp.where(rmod >= s, pltpu.roll(bc, s, axis=0), 0.0)
        s *= 2

    b3 = bc.reshape(nc, HG_CHUNK, HG_W)
    blast = b3[:, HG_CHUNK - 1:HG_CHUNK, :]
    kend = (kk.reshape(nc, HG_CHUNK, HG_W) * jnp.exp(blast - b3)).reshape(tq, HG_W)
    decay = jnp.exp(blast).reshape(nc, HG_W)
    qdec = (qh * jnp.exp(bc)).astype(BF16)
    kdec = (kk * jnp.exp(-bc)).astype(BF16)
    kend = kend.astype(BF16)
    ivb = iv.astype(BF16)

    ri = lax.broadcasted_iota(I32, (tq, tq), 0)
    ci = lax.broadcasted_iota(I32, (tq, tq), 1)
    cmask = ((ri // HG_CHUNK) == (ci // HG_CHUNK)) & (ri >= ci)

    heads = [slice(hh * HG_DIM, (hh + 1) * HG_DIM) for hh in range(HG_HEADS)]
    chunks = [slice(n * HG_CHUNK, (n + 1) * HG_CHUNK) for n in range(nc)]

    lane_head = lax.broadcasted_iota(I32, (HG_CHUNK, HG_W), 1) // HG_DIM
    for n, rs in enumerate(chunks):
        vstack = jnp.concatenate([ivb[rs, sl] for sl in heads], axis=0)
        kblk = jnp.concatenate([jnp.where(lane_head == hh, kend[rs], 0.0).astype(BF16)
                                for hh in range(HG_HEADS)], axis=0)
        u_scr[n] = _dot_tn(vstack, kblk)

    st = st_ref[...]
    for n in range(nc):
        stb_scr[n] = st.astype(BF16)
        st = st * decay[n:n + 1] + u_scr[n]
    st_ref[...] = st

    outs = []
    for hh, sl in enumerate(heads):
        a = _dot_nt(qdec[:, sl], kdec[:, sl])
        a = jnp.where(cmask, a, 0.0).astype(BF16)
        o_intra = _dot(a, ivb[:, sl])
        inter = [_dot_nt(qdec[rs, sl], stb_scr[n, :, sl]) for n, rs in enumerate(chunks)]
        o = o_intra + jnp.concatenate(inter, axis=0)
        o = _rms(o) * hnorm[:, sl]
        g = gr[:, sl]
        outs.append(o * (g * _sigmoid(g)))
    return jnp.concatenate(outs, axis=1)


def _route_topk(h2, wr_hi, wr_lo, br):
    tq = h2.shape[0]
    h_hi, h_lo = _split(h2)
    logits = _dot(h_hi, wr_hi) + _dot(h_lo, wr_hi) + _dot(h_hi, wr_lo)
    lt = logits.T[0:ROUTER_ROWS] + br
    sub = lax.broadcasted_iota(I32, (8, tq), 0).astype(F32)

    gl = lt[0:8]
    gm = jnp.max(gl, axis=0, keepdims=True)
    gidx = jnp.min(jnp.where(gl == gm, sub, 8.0), axis=0, keepdims=True)
    g_w = 1.0 / jnp.sum(jnp.exp(gl - gm), axis=0, keepdims=True)

    es = lt[8:16]
    for g in range(1, N_GROUPS):
        es = jnp.where(gidx == float(g), lt[8 + 8 * g:16 + 8 * g], es)
    m1 = jnp.max(es, axis=0, keepdims=True)
    i1 = jnp.min(jnp.where(es == m1, sub, 8.0), axis=0, keepdims=True)
    e2 = jnp.where(sub == i1, NEG, es)
    m2 = jnp.max(e2, axis=0, keepdims=True)
    i2 = jnp.min(jnp.where(e2 == m2, sub, 8.0), axis=0, keepdims=True)
    dd = jnp.exp(m2 - m1)
    w1 = g_w / (1.0 + dd)
    w2 = g_w * dd / (1.0 + dd)
    first_low = i1 < i2
    ea = jnp.minimum(i1, i2)
    eb = jnp.maximum(i1, i2)
    w_lo = jnp.where(first_low, w1, w2)
    w_hi = jnp.where(first_low, w2, w1)
    pair = ea * (15.0 - ea) * 0.5 + (eb - ea - 1.0)
    bucket = gidx * float(N_PAIRS) + pair
    return bucket, w_lo, w_hi


def _route_rank(bucket, w_lo, w_hi, carry_ref, bidx, live):
    tq = bucket.shape[1]
    brow = lax.broadcasted_iota(I32, (LANES, tq), 0).astype(F32)
    onehot = brow == bucket
    oh = jnp.where(onehot, live, 0.0)
    ti = lax.broadcasted_iota(I32, (tq, tq), 0)
    tj = lax.broadcasted_iota(I32, (tq, tq), 1)
    upper = jnp.where(ti < tj, 1.0, 0.0).astype(BF16)
    before = _dot(oh.astype(BF16), upper) + carry_ref[...]
    rank = jnp.sum(jnp.where(onehot, before, 0.0), axis=0, keepdims=True)
    carry_ref[...] = carry_ref[...] + jnp.sum(oh, axis=1, keepdims=True)

    lane_row = lax.broadcasted_iota(I32, (LANES - EXT_BATCH0, tq), 0)
    onehot_b = jnp.where(lane_row == bidx, 1.0, 0.0)
    info = jnp.concatenate([w_lo, w_hi, jnp.zeros((EXT_BATCH0 - 2, tq), F32), onehot_b], axis=0)
    return rank, info.T


def _mixer_kernel(sinks_ref, x_ref, mod_ref, ln1pre_ref, ln1post_ref, ln2pre_ref, anorm_ref, hnorm_ref,
                  lb_ref, win_ref, wout_ref, wrhi_ref, wrlo_ref, br_ref,
                  rows_ref, info_ref, cnt_ref,
                  kprev_ref, vprev_ref, st_ref, carry_ref, s_scr, m_scr, p_scr, u_scr, stb_scr,
                  h2_scr, proj_scr, keep_scr, *, tiles_per_seq):
    s = pl.program_id(0)
    n_tiles = pl.num_programs(0) - 1
    tq = x_ref.shape[0]
    t = lax.rem(jnp.minimum(s, n_tiles - 1), tiles_per_seq)
    bits = lambda a: pltpu.bitcast(a, U32)

    @pl.when(s == 0)
    def _():
        carry_ref[...] = jnp.zeros_like(carry_ref)
        h2_scr[...] = jnp.zeros_like(h2_scr)
        keep_scr[...] = jnp.zeros_like(keep_scr)

    @pl.when(t == 0)
    def _():
        st_ref[...] = jnp.zeros_like(st_ref)
        kprev_ref[...] = jnp.zeros_like(kprev_ref)
        vprev_ref[...] = jnp.zeros_like(vprev_ref)

    x = x_ref[...]
    mod = mod_ref[0]
    sh1, sc1, ga1, sh2, sc2 = mod[0:1], mod[1:2], mod[2:3], mod[3:4], mod[4:5]
    prev = jnp.maximum(s - 1, 0)

    @pl.when(s <= n_tiles)
    def _():
        bucket, w_lo, w_hi = _route_topk(h2_scr[...], wrhi_ref[...], wrlo_ref[...], br_ref[...])
        h = _rms(x) * ln1pre_ref[...] * (1.0 + sc1) + sh1
        proj_scr[...] = _dot(h.astype(BF16), win_ref[...])
        live = jnp.where(s >= 1, 1.0, 0.0)
        rank, ext = _route_rank(bucket, w_lo, w_hi, carry_ref, prev // tiles_per_seq, live)
        rows_ref[:, :, 0:OFF_EXT] = keep_scr[...].reshape(tq, 1, OFF_EXT)
        rows_ref[:, :, OFF_EXT:ROW_W] = bits(ext).reshape(tq, 1, EXT_W)
        info_ref[0] = jnp.concatenate([bucket, rank, jnp.zeros((6, tq), F32)], axis=0).astype(I32)
        cnt_ref[...] = jnp.broadcast_to(carry_ref[...], cnt_ref.shape)

    proj = proj_scr[...]
    scores, exps, values = _attention(proj, kprev_ref, vprev_ref, sinks_ref, t, s_scr, m_scr, p_scr)
    scores()
    exps()
    attn = _rms(values()) * anorm_ref[...]

    lbr = lb_ref[...]
    le = jnp.exp(lbr - jnp.max(lbr, axis=0, keepdims=True))
    lb = le[0:1] / jnp.sum(le, axis=0, keepdims=True)
    hg = _hgrn2(proj, lb, st_ref, hnorm_ref[...], u_scr, stb_scr)

    mix = _dot(jnp.concatenate([attn, hg], axis=1).astype(BF16), wout_ref[...])
    x1 = x + ga1 * (_rms(mix) * ln1post_ref[...])

    h2 = _rms(x1) * ln2pre_ref[...] * (1.0 + sc2) + sh2
    h2r = h2.astype(BF16).astype(F32)

    keep_scr[:, 0:D_MODEL] = bits(x1)
    keep_scr[:, OFF_H2P:OFF_EXT] = ((bits(h2r[:, 0:H2P_W]) >> 16)
                                    | (bits(h2r[:, H2P_W:D_MODEL]) & jnp.uint32(0xFFFF0000)))
    h2_scr[...] = h2


def _mixer(x2, mod3, sinks, ln1pre, ln1post, ln2pre, anorm, hnorm, lb, win, wout, wr_hi, wr_lo, br,
           bsz, seq):
    n = bsz * seq
    nt = seq // TQ
    n_tiles = bsz * nt
    n_sc = (TQ // WINDOW) * ATTN_HEADS
    cur = lambda s: jnp.minimum(s, n_tiles - 1)
    const = lambda s: (0, 0)
    full = lambda a: pl.BlockSpec(a.shape, const)
    return pl.pallas_call(
        functools.partial(_mixer_kernel, tiles_per_seq=nt),
        grid=(n_tiles + 1,),
        in_specs=[pl.BlockSpec(memory_space=pltpu.SMEM),
                  pl.BlockSpec((TQ, D_MODEL), lambda s: (cur(s), 0)),
                  pl.BlockSpec((1, 6, D_MODEL), lambda s: (cur(s) // nt, 0, 0)),
                  full(ln1pre), full(ln1post), full(ln2pre), full(anorm), full(hnorm), full(lb),
                  full(win), full(wout), full(wr_hi), full(wr_lo), full(br)],
        out_specs=[pl.BlockSpec((TQ, 1, ROW_W), lambda s: (jnp.maximum(s - 1, 0), 0, 0)),
                   pl.BlockSpec((1, 8, TQ), lambda s: (jnp.maximum(s - 1, 0), 0, 0)),
                   pl.BlockSpec((LANES, LANES), const)],
        out_shape=[jax.ShapeDtypeStruct((n, 1, ROW_W), U32),
                   jax.ShapeDtypeStruct((n // TQ, 8, TQ), I32),
                   jax.ShapeDtypeStruct((LANES, LANES), F32)],
        scratch_shapes=[pltpu.VMEM((WINDOW, KV_W), F32),
                        pltpu.VMEM((WINDOW, KV_W), F32),
                        pltpu.VMEM((HG_DIM, HG_W), F32),
                        pltpu.VMEM((LANES, 1), F32),
                        pltpu.VMEM((n_sc, WINDOW, 2 * WINDOW), F32),
                        pltpu.VMEM((n_sc, WINDOW, 1), F32),
                        pltpu.VMEM((n_sc, WINDOW, 2 * WINDOW), BF16),
                        pltpu.VMEM((TQ // HG_CHUNK, HG_DIM, HG_W), F32),
                        pltpu.VMEM((TQ // HG_CHUNK, HG_DIM, HG_W), BF16),
                        pltpu.VMEM((TQ, D_MODEL), F32),
                        pltpu.VMEM((TQ, IN_W), F32),
                        pltpu.VMEM((TQ, OFF_EXT), U32)],
        compiler_params=pltpu.CompilerParams(dimension_semantics=("arbitrary",),
                                             vmem_limit_bytes=VMEM_LIMIT),
        name="mixer",
    )(sinks, x2, mod3, ln1pre, ln1post, ln2pre, anorm, hnorm, lb, win, wout, wr_hi, wr_lo, br)


PERM_STEPS = 8
PERM_ROWS = 8
PERM_UNROLL = 8


def _perm_kernel(rs_ref, cnt_ref, bucket_ref, rank_ref, perm_ref, pos_vmem, pos_smem, sem):
    pid = pl.program_id(0)
    rows, cols = pos_vmem.shape

    b = bucket_ref[0]
    start = jnp.zeros_like(b)
    for k in range(N_BUCKETS):
        start = jnp.where(b == k, rs_ref[k], start)
    pos_vmem[...] = start + rank_ref[0]
    copies = [pltpu.make_async_copy(pos_vmem.at[r], pos_smem.at[pl.ds(r * cols, cols)], sem) for r in range(rows)]
    for cp in copies:
        cp.start()

    @pl.when(pid == 0)
    def _():
        def per_bucket(k, carry):
            first = rs_ref[k]
            cnt = cnt_ref[k]

            def pad(r, c2):
                perm_ref[first + r] = 0
                return c2

            lax.fori_loop(cnt, ((cnt + TM - 1) // TM) * TM, pad, 0)
            return carry

        lax.fori_loop(0, N_BUCKETS, per_bucket, 0)

        def tail(r, carry):
            perm_ref[r] = 0
            return carry

        lax.fori_loop(rs_ref[N_BUCKETS], perm_ref.shape[0], tail, 0)

    for cp in copies:
        cp.wait()
    base = pid * (rows * cols)

    def body(j, carry):
        i0 = j * PERM_UNROLL
        for u in range(PERM_UNROLL):
            perm_ref[pos_smem[i0 + u]] = base + i0 + u
        return carry

    lax.fori_loop(0, rows * cols // PERM_UNROLL, body, 0)


def _perm(row_start, counts, bucket, rank, n_rows):
    n = bucket.shape[0]
    cols = n // (PERM_STEPS * PERM_ROWS)
    assert n % (PERM_STEPS * PERM_ROWS * PERM_UNROLL) == 0
    chunked = lambda a: a.reshape(PERM_STEPS, PERM_ROWS, cols)
    chunk_spec = pl.BlockSpec((1, PERM_ROWS, cols), lambda i: (i, 0, 0))
    return pl.pallas_call(
        _perm_kernel,
        grid=(PERM_STEPS,),
        in_specs=[pl.BlockSpec(memory_space=pltpu.SMEM),
                  pl.BlockSpec(memory_space=pltpu.SMEM),
                  chunk_spec, chunk_spec],
        out_specs=pl.BlockSpec(memory_space=pltpu.SMEM),
        out_shape=jax.ShapeDtypeStruct((n_rows,), I32),
        scratch_shapes=[pltpu.VMEM((PERM_ROWS, cols), I32), pltpu.SMEM((PERM_ROWS * cols,), I32),
                        pltpu.SemaphoreType.DMA(())],
        compiler_params=pltpu.CompilerParams(dimension_semantics=("arbitrary",)),
        name="perm",
    )(row_start, counts, chunked(bucket), chunked(rank))


GATHER_DEPTH = 2


def _moe_kernel(nt_ref, nv_ref, ea_ref, eb_ref, perm_ref,
                rows_hbm, gtab_hi_ref, gtab_lo_ref, ln2post_ref, wgu_a, wd_a, wgu_b, wd_b,
                out_hbm, xbuf, x2d, obuf, gsem, ssem):
    i = pl.program_id(0)
    nt = nt_ref[0]
    last_tile = pl.num_programs(0) - 1
    nbuf = xbuf.shape[0] // TM

    def start_gather(tile, pred):
        sl = lax.rem(tile, nbuf)
        base = jnp.minimum(tile, last_tile) * TM
        for r in range(TM):
            @pl.when(pred)
            def _():
                tok = perm_ref[base + r]
                pltpu.make_async_copy(rows_hbm.at[tok], xbuf.at[sl * TM + r], gsem.at[sl]).start()

    def wait_gather(tile):
        sl = lax.rem(tile, nbuf)
        pltpu.make_async_copy(rows_hbm.at[pl.ds(0, TM)], xbuf.at[pl.ds(sl * TM, TM)], gsem.at[sl]).wait()

    def wait_scatter(sl, nv):
        @pl.when(nv == TM)
        def _():
            pltpu.make_async_copy(obuf.at[sl], out_hbm.at[pl.ds(0, TM)], ssem.at[sl]).wait()

        @pl.when(nv < TM)
        def _():
            def one(r, carry):
                pltpu.make_async_copy(obuf.at[sl, pl.ds(0, 1)], out_hbm.at[pl.ds(0, 1)], ssem.at[sl]).wait()
                return carry

            lax.fori_loop(0, nv, one, 0)

    def compute(xb):
        x1 = pltpu.bitcast(xb[:, 0:D_MODEL], F32)
        hp = xb[:, OFF_H2P:OFF_EXT]
        h2a = pltpu.bitcast(hp << 16, F32).astype(BF16)
        h2b = pltpu.bitcast(hp & jnp.uint32(0xFFFF0000), F32).astype(BF16)
        ext = pltpu.bitcast(xb[:, OFF_EXT:ROW_W], F32)
        w_lo, w_hi = ext[:, 0:1], ext[:, 1:2]
        sel = ext.astype(BF16)
        ga2 = _dot(sel, gtab_hi_ref[...]) + _dot(sel, gtab_lo_ref[...])

        def expert(wgu_ref, wd_ref):
            gu = _dot(h2a, wgu_ref[0, 0:H2P_W]) + _dot(h2b, wgu_ref[0, H2P_W:D_MODEL])
            hg, hu = gu[:, 0:FF], gu[:, FF:2 * FF]
            act = (hg * _sigmoid(hg)) * hu
            return _dot(act.astype(BF16), wd_ref[0])

        y = w_lo * expert(wgu_a, wd_a) + w_hi * expert(wgu_b, wd_b)
        return x1 + ga2 * (_rms(y) * ln2post_ref[...])

    @pl.when(i == 0)
    def _():
        for d in range(GATHER_DEPTH):
            start_gather(d, d < nt)

    @pl.when(i < nt)
    def _():
        nv = nv_ref[i]
        osl = lax.rem(i, 2)
        wait_gather(i)

        @pl.when(i >= 2)
        def _():
            wait_scatter(osl, nv_ref[jnp.maximum(i - 2, 0)])

        start_gather(i + GATHER_DEPTH, i + GATHER_DEPTH < nt)
        x2d[...] = xbuf[pl.ds(lax.rem(i, nbuf) * TM, TM)].reshape(TM, ROW_W)
        obuf[osl] = compute(x2d[...])

        for r in range(TM):
            @pl.when(r < nv)
            def _():
                tok = perm_ref[i * TM + r]
                pltpu.make_async_copy(obuf.at[osl, pl.ds(r, 1)], out_hbm.at[pl.ds(tok, 1)], ssem.at[osl]).start()

        @pl.when(i == nt - 1)
        def _():
            wait_scatter(osl, nv)

            @pl.when(i >= 1)
            def _():
                wait_scatter(1 - osl, nv_ref[jnp.maximum(i - 1, 0)])


def _moe(rows, gtab_hi, gtab_lo, ln2post, wgu, wd, nt, nv, ea, eb, perm, n_tiles):
    n = rows.shape[0]
    const2 = lambda i, *_: (0, 0)
    grid_spec = pltpu.PrefetchScalarGridSpec(
        num_scalar_prefetch=5,
        grid=(n_tiles,),
        in_specs=[pl.BlockSpec(memory_space=pl.ANY),
                  pl.BlockSpec(gtab_hi.shape, const2),
                  pl.BlockSpec(gtab_lo.shape, const2),
                  pl.BlockSpec(ln2post.shape, const2),
                  pl.BlockSpec((1, D_MODEL, 2 * FF), lambda i, nt, nv, ea, eb, perm: (ea[i], 0, 0)),
                  pl.BlockSpec((1, FF, D_MODEL), lambda i, nt, nv, ea, eb, perm: (ea[i], 0, 0)),
                  pl.BlockSpec((1, D_MODEL, 2 * FF), lambda i, nt, nv, ea, eb, perm: (eb[i], 0, 0)),
                  pl.BlockSpec((1, FF, D_MODEL), lambda i, nt, nv, ea, eb, perm: (eb[i], 0, 0))],
        out_specs=pl.BlockSpec(memory_space=pl.ANY),
        scratch_shapes=[pltpu.VMEM(((GATHER_DEPTH + 1) * TM, 1, ROW_W), U32),
                        pltpu.VMEM((TM, ROW_W), U32),
                        pltpu.VMEM((2, TM, D_MODEL), F32),
                        pltpu.SemaphoreType.DMA((GATHER_DEPTH + 1,)),
                        pltpu.SemaphoreType.DMA((2,))],
    )
    return pl.pallas_call(
        _moe_kernel,
        grid_spec=grid_spec,
        out_shape=jax.ShapeDtypeStruct((n, D_MODEL), F32),
        compiler_params=pltpu.CompilerParams(dimension_semantics=("arbitrary",),
                                             vmem_limit_bytes=VMEM_LIMIT),
        name="moe",
    )(nt, nv, ea, eb, perm, rows, gtab_hi, gtab_lo, ln2post, wgu, wd, wgu, wd)


def kernel(x, c, ln1_pre, ln1_post, ln2_pre, ln2_post, w_ada, b_ada, w_in, attn_sinks, attn_out_norm,
           hgrn_lb, hgrn_out_norm, w_out, w_router_group, b_router_group, w_router_expert,
           b_router_expert, w_exp_gate, w_exp_up, w_exp_down):
    bsz, seq, d = x.shape
    assert d == D_MODEL and seq % TQ == 0 and w_ada.shape[0] == 1 and hgrn_lb.shape[0] == 2
    n = bsz * seq

    mod = _ada(c, w_ada[0], b_ada[0])
    mod3 = mod.reshape(bsz, 6, d)

    wr = jnp.concatenate([w_router_group[0], jnp.zeros((d, 8 - N_GROUPS), F32), w_router_expert[0],
                          jnp.zeros((d, LANES - ROUTER_ROWS), F32)], axis=1)
    br = jnp.concatenate([b_router_group[0], jnp.full((8 - N_GROUPS,), NEG, F32), b_router_expert[0]])
    wr_hi = wr.astype(BF16)
    wr_lo = (wr - wr_hi.astype(F32)).astype(BF16)

    x1ext, info, cnt = _mixer(
        x.reshape(n, d), mod3, attn_sinks[0], ln1_pre, ln1_post, ln2_pre, attn_out_norm, hgrn_out_norm,
        hgrn_lb, w_in[0].astype(BF16), w_out[0].astype(BF16), wr_hi, wr_lo, br.reshape(ROUTER_ROWS, 1),
        bsz, seq)

    n_tiles = n // TM + N_BUCKETS
    counts = cnt[:N_BUCKETS, 0].astype(I32)
    tiles_per = (counts + TM - 1) // TM
    tile_end = jnp.cumsum(tiles_per)
    tile_start = tile_end - tiles_per
    nt = tile_end[-1]
    bucket = info[:, 0, :].reshape(n)
    rank = info[:, 1, :].reshape(n)
    tid = jnp.arange(n_tiles, dtype=I32)[None, :]
    member = (tid >= tile_start[:, None]) & (tid < tile_end[:, None])
    pick = lambda per_bucket: jnp.sum(jnp.where(member, per_bucket, 0), axis=0).astype(I32)
    bidx = np.arange(N_BUCKETS, dtype=np.int32)
    ea_of = jnp.asarray((bidx // N_PAIRS) * EPG + _PAIR_A[bidx % N_PAIRS])[:, None]
    eb_of = jnp.asarray((bidx // N_PAIRS) * EPG + _PAIR_B[bidx % N_PAIRS])[:, None]
    last_used = jnp.arange(N_BUCKETS)[:, None] == jnp.max(jnp.where(tiles_per > 0, jnp.arange(N_BUCKETS), 0))
    unused = tid[0] >= nt
    nv = pick(jnp.clip(counts[:, None] - (tid - tile_start[:, None]) * TM, 0, TM))
    ea = jnp.where(unused, jnp.sum(jnp.where(last_used, ea_of, 0)), pick(ea_of)).astype(I32)
    eb = jnp.where(unused, jnp.sum(jnp.where(last_used, eb_of, 0)), pick(eb_of)).astype(I32)

    pad128 = lambda a: jnp.concatenate([a, jnp.zeros((LANES - a.shape[0],), I32)])
    row_start = jnp.concatenate([tile_start, nt.reshape(1)]) * TM
    perm = _perm(pad128(row_start), pad128(counts), bucket, rank, n_tiles * TM)

    wgu = jnp.concatenate([w_exp_gate[0], w_exp_up[0]], axis=-1).astype(BF16)
    wd = w_exp_down[0].astype(BF16)
    gtab = jnp.zeros((LANES, d), F32).at[EXT_BATCH0:EXT_BATCH0 + bsz].set(mod3[:, 5, :])
    gtab_hi = gtab.astype(BF16)
    gtab_lo = (gtab - gtab_hi.astype(F32)).astype(BF16)
    out = _moe(x1ext, gtab_hi, gtab_lo, ln2_post, wgu, wd, nt.reshape(1), nv, ea, eb, perm, n_tiles)
    return out.reshape(bsz, seq, d)
```

```python
import functools

import numpy as np
import jax
import jax.numpy as jnp
from jax import lax
from jax.experimental import pallas as pl
from jax.experimental.pallas import tpu as pltpu

F32 = jnp.float32
BF16 = jnp.bfloat16
I32 = jnp.int32

D_MODEL = 1024
ATTN_HEADS = 8
HEAD_DIM = 64
WINDOW = 128
ATTN_W = 512
KV_W = 128
HG_HEADS = 4
HG_DIM = 128
HG_W = 512
HG_CHUNK = 32
IN_W = 2816
N_GROUPS = 4
EPG = 8
N_EXPERTS = 32
FF = 256
N_PAIRS = EPG * (EPG - 1) // 2
N_BUCKETS = N_GROUPS * N_PAIRS
EPS = 1e-6
NEG = -1e30
LOG2E = 1.4426950408889634

LANES = 128
H2P_W = D_MODEL // 2
EXT_W = LANES
OFF_H2P = D_MODEL
OFF_EXT = D_MODEL + H2P_W
ROW_W = OFF_EXT + EXT_W
EXT_BATCH0 = 8
U32 = jnp.uint32
ROUTER_ROWS = 8 + N_EXPERTS

TQ = 256
TM = 256
VMEM_LIMIT = 56 * 1024 * 1024

OFF_Q, OFF_K, OFF_V, OFF_HQ, OFF_HF, OFF_HI, OFF_HG = 0, 512, 640, 768, 1280, 1792, 2304

_PAIR_A = np.array([a for a in range(EPG) for b in range(a + 1, EPG)], np.int32)
_PAIR_B = np.array([b for a in range(EPG) for b in range(a + 1, EPG)], np.int32)


def _dot(a, b):
    return jnp.dot(a, b, preferred_element_type=F32)


def _dot_nt(a, b):
    return lax.dot_general(a, b, (((1,), (1,)), ((), ())), preferred_element_type=F32)


def _dot_tn(a, b):
    return lax.dot_general(a, b, (((0,), (0,)), ((), ())), preferred_element_type=F32)


def _split(a):
    hi = a.astype(BF16)
    lo = (a - hi.astype(F32)).astype(BF16)
    return hi, lo


def _rms(x):
    return x * lax.rsqrt(jnp.mean(x * x, axis=-1, keepdims=True) + EPS)


def _sigmoid(x):
    return 0.5 * jnp.tanh(0.5 * x) + 0.5


def _ada_kernel(c_ref, w_ref, b_ref, o_ref):
    c = c_ref[...]
    ca = c * _sigmoid(c)
    c_hi, c_lo = _split(ca)
    w_hi, w_lo = _split(w_ref[...])
    o_ref[...] = _dot(c_hi, w_hi) + _dot(c_lo, w_hi) + _dot(c_hi, w_lo) + b_ref[...]


def _ada(c, w, b):
    bsz, d = c.shape
    n_out = w.shape[1]
    return pl.pallas_call(
        _ada_kernel,
        grid=(n_out // d,),
        in_specs=[pl.BlockSpec((bsz, d), lambda j: (0, 0)),
                  pl.BlockSpec((d, d), lambda j: (0, j)),
                  pl.BlockSpec((1, d), lambda j: (0, j))],
        out_specs=pl.BlockSpec((bsz, d), lambda j: (0, j)),
        out_shape=jax.ShapeDtypeStruct((bsz, n_out), F32),
        compiler_params=pltpu.CompilerParams(dimension_semantics=("arbitrary",),
                                             vmem_limit_bytes=VMEM_LIMIT),
        name="adaln",
    )(c, w, b.reshape(1, n_out))


def _attention(proj, kprev_ref, vprev_ref, sinks_ref, bias_ref, t, s_scr, m_scr, p_scr):
    tq = proj.shape[0]
    q = (proj[:, OFF_Q:OFF_Q + ATTN_W] * (HEAD_DIM ** -0.5 * LOG2E)).astype(BF16)
    kf = jnp.concatenate([kprev_ref[...], proj[:, OFF_K:OFF_K + KV_W]], axis=0)
    vf = jnp.concatenate([vprev_ref[...], proj[:, OFF_V:OFF_V + KV_W]], axis=0)
    kprev_ref[...] = proj[tq - WINDOW:, OFF_K:OFF_K + KV_W]
    vprev_ref[...] = proj[tq - WINDOW:, OFF_V:OFF_V + KV_W]

    lo = lax.broadcasted_iota(I32, kf.shape, 1) < HEAD_DIM
    kr = pltpu.roll(kf, HEAD_DIM, axis=1)
    vr = pltpu.roll(vf, HEAD_DIM, axis=1)

    def variants(a, ar):
        return [[jnp.where(lo, a, 0.0).astype(BF16), jnp.where(lo, 0.0, ar).astype(BF16)],
                [jnp.where(lo, ar, 0.0).astype(BF16), jnp.where(lo, 0.0, a).astype(BF16)]]

    kvar = variants(kf, kr)
    vvar = variants(vf, vr)

    first = jnp.where(t > 0, 0, 1)

    nblk = tq // WINDOW
    idx = lambda j, h: j * ATTN_HEADS + h
    keys = lambda a, j: a[j * WINDOW:(j + 2) * WINDOW]

    def scores():
        for j in range(nblk):
            for h in range(ATTN_HEADS):
                p, par = h // 2, h % 2
                qp = q[j * WINDOW:(j + 1) * WINDOW, p * LANES:(p + 1) * LANES]
                s = _dot_nt(qp, keys(kvar[p // 2][par], j)) + bias_ref[first if j == 0 else 0, h]
                s_scr[idx(j, h)] = s
                m_scr[idx(j, h)] = jnp.maximum(jnp.max(s, axis=-1, keepdims=True), sinks_ref[h] * LOG2E)

    def exps():
        for j in range(nblk):
            for h in range(ATTN_HEADS):
                m = m_scr[idx(j, h)]
                pe = jnp.exp2(s_scr[idx(j, h)] - m)
                p_scr[idx(j, h)] = pe.astype(BF16)
                m_scr[idx(j, h)] = 1.0 / (jnp.sum(pe, axis=-1, keepdims=True) + jnp.exp2(sinks_ref[h] * LOG2E - m))

    def values():
        blocks = []
        for j in range(nblk):
            pairs = []
            for p in range(ATTN_HEADS // 2):
                acc = None
                for par in range(2):
                    h = 2 * p + par
                    o = _dot(p_scr[idx(j, h)], keys(vvar[p // 2][par], j)) * m_scr[idx(j, h)]
                    acc = o if acc is None else acc + o
                pairs.append(acc)
            blocks.append(jnp.concatenate(pairs, axis=1))
        return jnp.concatenate(blocks, axis=0)

    return scores, exps, values


def _hgrn2(proj, lb, st_ref, hnorm, u_scr, stb_scr):
    tq = proj.shape[0]
    nc = tq // HG_CHUNK
    qr = proj[:, OFF_HQ:OFF_HQ + HG_W]
    fr = proj[:, OFF_HF:OFF_HF + HG_W]
    iv = proj[:, OFF_HI:OFF_HI + HG_W]
    gr = proj[:, OFF_HG:OFF_HG + HG_W]
    qh = qr * _sigmoid(qr)
    f = lb + (1.0 - lb) * _sigmoid(fr)
    kk = 1.0 - f
    logf = jnp.log(f)

    rmod = lax.broadcasted_iota(I32, (tq, HG_W), 0) & (HG_CHUNK - 1)
    bc = logf
    s = 1
    while s < HG_CHUNK:
        bc = bc + jnp.where(rmod >= s, pltpu.roll(bc, s, axis=0), 0.0)
        s *= 2

    b3 = bc.reshape(nc, HG_CHUNK, HG_W)
    blast = b3[:, HG_CHUNK - 1:HG_CHUNK, :]
    kend = (kk.reshape(nc, HG_CHUNK, HG_W) * jnp.exp(blast - b3)).reshape(tq, HG_W)
    decay = jnp.exp(blast).reshape(nc, HG_W)
    qdec = (qh * jnp.exp(bc)).astype(BF16)
    kdec = (kk * jnp.exp(-bc)).astype(BF16)
    kend = kend.astype(BF16)
    ivb = iv.astype(BF16)

    ri = lax.broadcasted_iota(I32, (tq, tq), 0)
    ci = lax.broadcasted_iota(I32, (tq, tq), 1)
    cmask = ((ri // HG_CHUNK) == (ci // HG_CHUNK)) & (ri >= ci)

    heads = [slice(hh * HG_DIM, (hh + 1) * HG_DIM) for hh in range(HG_HEADS)]
    chunks = [slice(n * HG_CHUNK, (n + 1) * HG_CHUNK) for n in range(nc)]

    lane_head = lax.broadcasted_iota(I32, (HG_CHUNK, HG_W), 1) // HG_DIM
    for n, rs in enumerate(chunks):
        vstack = jnp.concatenate([ivb[rs, sl] for sl in heads], axis=0)
        kblk = jnp.concatenate([jnp.where(lane_head == hh, kend[rs], 0.0).astype(BF16)
                                for hh in range(HG_HEADS)], axis=0)
        u_scr[n] = _dot_tn(vstack, kblk)

    st = st_ref[...]
    for n in range(nc):
        stb_scr[n] = st.astype(BF16)
        st = st * decay[n:n + 1] + u_scr[n]
    st_ref[...] = st

    outs = []
    for hh, sl in enumerate(heads):
        a = _dot_nt(qdec[:, sl], kdec[:, sl])
        a = jnp.where(cmask, a, 0.0).astype(BF16)
        o_intra = _dot(a, ivb[:, sl])
        inter = [_dot_nt(qdec[rs, sl], stb_scr[n, :, sl]) for n, rs in enumerate(chunks)]
        o = o_intra + jnp.concatenate(inter, axis=0)
        o = _rms(o) * hnorm[:, sl]
        g = gr[:, sl]
        outs.append(o * (g * _sigmoid(g)))
    return jnp.concatenate(outs, axis=1)


def _route_topk(h2, wr_hi, wr_lo, br):
    tq = h2.shape[0]
    h_hi, h_lo = _split(h2)
    logits = _dot(h_hi, wr_hi) + _dot(h_lo, wr_hi) + _dot(h_hi, wr_lo)
    lt = logits.T[0:ROUTER_ROWS] + br
    sub = lax.broadcasted_iota(I32, (8, tq), 0).astype(F32)

    gl = lt[0:8]
    gm = jnp.max(gl, axis=0, keepdims=True)
    gidx = jnp.min(jnp.where(gl == gm, sub, 8.0), axis=0, keepdims=True)
    g_w = 1.0 / jnp.sum(jnp.exp(gl - gm), axis=0, keepdims=True)

    es = lt[8:16]
    for g in range(1, N_GROUPS):
        es = jnp.where(gidx == float(g), lt[8 + 8 * g:16 + 8 * g], es)
    m1 = jnp.max(es, axis=0, keepdims=True)
    i1 = jnp.min(jnp.where(es == m1, sub, 8.0), axis=0, keepdims=True)
    e2 = jnp.where(sub == i1, NEG, es)
    m2 = jnp.max(e2, axis=0, keepdims=True)
    i2 = jnp.min(jnp.where(e2 == m2, sub, 8.0), axis=0, keepdims=True)
    dd = jnp.exp(m2 - m1)
    w1 = g_w / (1.0 + dd)
    w2 = g_w * dd / (1.0 + dd)
    first_low = i1 < i2
    ea = jnp.minimum(i1, i2)
    eb = jnp.maximum(i1, i2)
    w_lo = jnp.where(first_low, w1, w2)
    w_hi = jnp.where(first_low, w2, w1)
    pair = ea * (15.0 - ea) * 0.5 + (eb - ea - 1.0)
    bucket = gidx * float(N_PAIRS) + pair
    return bucket, w_lo, w_hi


def _route_rank(bucket, w_lo, w_hi, carry_ref, bidx, live):
    tq = bucket.shape[1]
    brow = lax.broadcasted_iota(I32, (LANES, tq), 0).astype(F32)
    onehot = brow == bucket
    oh = jnp.where(onehot, live, 0.0)
    ti = lax.broadcasted_iota(I32, (tq, tq), 0)
    tj = lax.broadcasted_iota(I32, (tq, tq), 1)
    upper = jnp.where(ti < tj, 1.0, 0.0).astype(BF16)
    before = _dot(oh.astype(BF16), upper) + carry_ref[...]
    rank = jnp.sum(jnp.where(onehot, before, 0.0), axis=0, keepdims=True)
    carry_ref[...] = carry_ref[...] + jnp.sum(oh, axis=1, keepdims=True)

    lane_row = lax.broadcasted_iota(I32, (LANES - EXT_BATCH0, tq), 0)
    onehot_b = jnp.where(lane_row == bidx, 1.0, 0.0)
    info = jnp.concatenate([w_lo, w_hi, jnp.zeros((EXT_BATCH0 - 2, tq), F32), onehot_b], axis=0)
    return rank, info.T


def _mixer_kernel(sinks_ref, x_ref, mod_ref, ln1pre_ref, ln1post_ref, ln2pre_ref, anorm_ref, hnorm_ref,
                  lb_ref, win_ref, wout_ref, wrhi_ref, wrlo_ref, br_ref, bias_ref,
                  rows_ref, info_ref, cnt_ref,
                  kprev_ref, vprev_ref, st_ref, carry_ref, s_scr, m_scr, p_scr, u_scr, stb_scr,
                  h2_scr, proj_scr, keep_scr, *, tiles_per_seq):
    s = pl.program_id(0)
    n_tiles = pl.num_programs(0) - 1
    tq = x_ref.shape[0]
    t = lax.rem(jnp.minimum(s, n_tiles - 1), tiles_per_seq)
    bits = lambda a: pltpu.bitcast(a, U32)

    @pl.when(s == 0)
    def _():
        carry_ref[...] = jnp.zeros_like(carry_ref)
        h2_scr[...] = jnp.zeros_like(h2_scr)
        keep_scr[...] = jnp.zeros_like(keep_scr)

    @pl.when(t == 0)
    def _():
        st_ref[...] = jnp.zeros_like(st_ref)
        kprev_ref[...] = jnp.zeros_like(kprev_ref)
        vprev_ref[...] = jnp.zeros_like(vprev_ref)

    x = x_ref[...]
    mod = mod_ref[0]
    sh1, sc1, ga1, sh2, sc2 = mod[0:1], mod[1:2], mod[2:3], mod[3:4], mod[4:5]
    prev = jnp.maximum(s - 1, 0)

    @pl.when(s <= n_tiles)
    def _():
        bucket, w_lo, w_hi = _route_topk(h2_scr[...], wrhi_ref[...], wrlo_ref[...], br_ref[...])
        h = _rms(x) * ln1pre_ref[...] * (1.0 + sc1) + sh1
        proj_scr[...] = _dot(h.astype(BF16), win_ref[...])
        live = jnp.where(s >= 1, 1.0, 0.0)
        rank, ext = _route_rank(bucket, w_lo, w_hi, carry_ref, prev // tiles_per_seq, live)
        rows_ref[:, :, 0:OFF_EXT] = keep_scr[...].reshape(tq, 1, OFF_EXT)
        rows_ref[:, :, OFF_EXT:ROW_W] = bits(ext).reshape(tq, 1, EXT_W)
        info_ref[0] = jnp.concatenate([bucket, rank, jnp.zeros((6, tq), F32)], axis=0).astype(I32)
        cnt_ref[...] = jnp.broadcast_to(carry_ref[...], cnt_ref.shape)

    proj = proj_scr[...]
    scores, exps, values = _attention(proj, kprev_ref, vprev_ref, sinks_ref, bias_ref, t, s_scr, m_scr, p_scr)
    scores()
    exps()
    attn = _rms(values()) * anorm_ref[...]

    lbr = lb_ref[...]
    le = jnp.exp(lbr - jnp.max(lbr, axis=0, keepdims=True))
    lb = le[0:1] / jnp.sum(le, axis=0, keepdims=True)
    hg = _hgrn2(proj, lb, st_ref, hnorm_ref[...], u_scr, stb_scr)

    mix = _dot(jnp.concatenate([attn, hg], axis=1).astype(BF16), wout_ref[...])
    x1 = x + ga1 * (_rms(mix) * ln1post_ref[...])

    h2 = _rms(x1) * ln2pre_ref[...] * (1.0 + sc2) + sh2
    h2r = h2.astype(BF16).astype(F32)

    keep_scr[:, 0:D_MODEL] = bits(x1)
    keep_scr[:, OFF_H2P:OFF_EXT] = ((bits(h2r[:, 0:H2P_W]) >> 16)
                                    | (bits(h2r[:, H2P_W:D_MODEL]) & jnp.uint32(0xFFFF0000)))
    h2_scr[...] = h2


def _attn_bias():
    qi = np.arange(WINDOW)[:, None]
    kj = np.arange(2 * WINDOW)[None, :]
    dist = qi + WINDOW - kj
    in_win = (dist >= 0) & (dist < WINDOW)
    slopes = 2.0 ** (-8.0 * (np.arange(ATTN_HEADS) + 1.0) / ATTN_HEADS)
    b = np.where(in_win[None], -slopes[:, None, None] * dist[None] * LOG2E, NEG)
    b_first = np.where((kj >= WINDOW)[None], b, NEG)
    return jnp.asarray(np.stack([b, b_first]).astype(np.float32))


def _mixer(x2, mod3, sinks, ln1pre, ln1post, ln2pre, anorm, hnorm, lb, win, wout, wr_hi, wr_lo, br,
           bsz, seq):
    bias = _attn_bias()
    n = bsz * seq
    nt = seq // TQ
    n_tiles = bsz * nt
    n_sc = (TQ // WINDOW) * ATTN_HEADS
    cur = lambda s: jnp.minimum(s, n_tiles - 1)
    const = lambda s: (0, 0)
    full = lambda a: pl.BlockSpec(a.shape, const)
    return pl.pallas_call(
        functools.partial(_mixer_kernel, tiles_per_seq=nt),
        grid=(n_tiles + 1,),
        in_specs=[pl.BlockSpec(memory_space=pltpu.SMEM),
                  pl.BlockSpec((TQ, D_MODEL), lambda s: (cur(s), 0)),
                  pl.BlockSpec((1, 6, D_MODEL), lambda s: (cur(s) // nt, 0, 0)),
                  full(ln1pre), full(ln1post), full(ln2pre), full(anorm), full(hnorm), full(lb),
                  full(win), full(wout), full(wr_hi), full(wr_lo), full(br),
                  pl.BlockSpec(bias.shape, lambda s: (0, 0, 0, 0))],
        out_specs=[pl.BlockSpec((TQ, 1, ROW_W), lambda s: (jnp.maximum(s - 1, 0), 0, 0)),
                   pl.BlockSpec((1, 8, TQ), lambda s: (jnp.maximum(s - 1, 0), 0, 0)),
                   pl.BlockSpec((LANES, LANES), const)],
        out_shape=[jax.ShapeDtypeStruct((n, 1, ROW_W), U32),
                   jax.ShapeDtypeStruct((n // TQ, 8, TQ), I32),
                   jax.ShapeDtypeStruct((LANES, LANES), F32)],
        scratch_shapes=[pltpu.VMEM((WINDOW, KV_W), F32),
                        pltpu.VMEM((WINDOW, KV_W), F32),
                        pltpu.VMEM((HG_DIM, HG_W), F32),
                        pltpu.VMEM((LANES, 1), F32),
                        pltpu.VMEM((n_sc, WINDOW, 2 * WINDOW), F32),
                        pltpu.VMEM((n_sc, WINDOW, 1), F32),
                        pltpu.VMEM((n_sc, WINDOW, 2 * WINDOW), BF16),
                        pltpu.VMEM((TQ // HG_CHUNK, HG_DIM, HG_W), F32),
                        pltpu.VMEM((TQ // HG_CHUNK, HG_DIM, HG_W), BF16),
                        pltpu.VMEM((TQ, D_MODEL), F32),
                        pltpu.VMEM((TQ, IN_W), F32),
                        pltpu.VMEM((TQ, OFF_EXT), U32)],
        compiler_params=pltpu.CompilerParams(dimension_semantics=("arbitrary",),
                                             vmem_limit_bytes=VMEM_LIMIT),
        name="mixer",
    )(sinks, x2, mod3, ln1pre, ln1post, ln2pre, anorm, hnorm, lb, win, wout, wr_hi, wr_lo, br, bias)


PERM_STEPS = 8
PERM_ROWS = 8
PERM_UNROLL = 8


def _perm_kernel(rs_ref, cnt_ref, bucket_ref, rank_ref, perm_ref, pos_vmem, pos_smem, sem):
    pid = pl.program_id(0)
    rows, cols = pos_vmem.shape

    b = bucket_ref[0]
    start = jnp.zeros_like(b)
    for k in range(N_BUCKETS):
        start = jnp.where(b == k, rs_ref[k], start)
    pos_vmem[...] = start + rank_ref[0]
    copies = [pltpu.make_async_copy(pos_vmem.at[r], pos_smem.at[pl.ds(r * cols, cols)], sem) for r in range(rows)]
    for cp in copies:
        cp.start()

    @pl.when(pid == 0)
    def _():
        def per_bucket(k, carry):
            first = rs_ref[k]
            cnt = cnt_ref[k]

            def pad(r, c2):
                perm_ref[first + r] = 0
                return c2

            lax.fori_loop(cnt, ((cnt + TM - 1) // TM) * TM, pad, 0)
            return carry

        lax.fori_loop(0, N_BUCKETS, per_bucket, 0)

        def tail(r, carry):
            perm_ref[r] = 0
            return carry

        lax.fori_loop(rs_ref[N_BUCKETS], perm_ref.shape[0], tail, 0)

    for cp in copies:
        cp.wait()
    base = pid * (rows * cols)

    def body(j, carry):
        i0 = j * PERM_UNROLL
        for u in range(PERM_UNROLL):
            perm_ref[pos_smem[i0 + u]] = base + i0 + u
        return carry

    lax.fori_loop(0, rows * cols // PERM_UNROLL, body, 0)


def _perm(row_start, counts, bucket, rank, n_rows):
    n = bucket.shape[0]
    cols = n // (PERM_STEPS * PERM_ROWS)
    assert n % (PERM_STEPS * PERM_ROWS * PERM_UNROLL) == 0
    chunked = lambda a: a.reshape(PERM_STEPS, PERM_ROWS, cols)
    chunk_spec = pl.BlockSpec((1, PERM_ROWS, cols), lambda i: (i, 0, 0))
    return pl.pallas_call(
        _perm_kernel,
        grid=(PERM_STEPS,),
        in_specs=[pl.BlockSpec(memory_space=pltpu.SMEM),
                  pl.BlockSpec(memory_space=pltpu.SMEM),
                  chunk_spec, chunk_spec],
        out_specs=pl.BlockSpec(memory_space=pltpu.SMEM),
        out_shape=jax.ShapeDtypeStruct((n_rows,), I32),
        scratch_shapes=[pltpu.VMEM((PERM_ROWS, cols), I32), pltpu.SMEM((PERM_ROWS * cols,), I32),
                        pltpu.SemaphoreType.DMA(())],
        compiler_params=pltpu.CompilerParams(dimension_semantics=("arbitrary",)),
        name="perm",
    )(row_start, counts, chunked(bucket), chunked(rank))


GATHER_DEPTH = 2


def _moe_kernel(nt_ref, nv_ref, ea_ref, eb_ref, perm_ref,
                rows_hbm, gtab_hi_ref, gtab_lo_ref, ln2post_ref, wgu_a, wd_a, wgu_b, wd_b,
                out_hbm, xbuf, x2d, obuf, gsem, ssem):
    i = pl.program_id(0)
    nt = nt_ref[0]
    last_tile = pl.num_programs(0) - 1
    nbuf = xbuf.shape[0] // TM

    def start_gather(tile, pred):
        sl = lax.rem(tile, nbuf)
        base = jnp.minimum(tile, last_tile) * TM
        for r in range(TM):
            @pl.when(pred)
            def _():
                tok = perm_ref[base + r]
                pltpu.make_async_copy(rows_hbm.at[tok], xbuf.at[sl * TM + r], gsem.at[sl]).start()

    def wait_gather(tile):
        sl = lax.rem(tile, nbuf)
        pltpu.make_async_copy(rows_hbm.at[pl.ds(0, TM)], xbuf.at[pl.ds(sl * TM, TM)], gsem.at[sl]).wait()

    def wait_scatter(sl, nv):
        @pl.when(nv == TM)
        def _():
            pltpu.make_async_copy(obuf.at[sl], out_hbm.at[pl.ds(0, TM)], ssem.at[sl]).wait()

        @pl.when(nv < TM)
        def _():
            def one(r, carry):
                pltpu.make_async_copy(obuf.at[sl, pl.ds(0, 1)], out_hbm.at[pl.ds(0, 1)], ssem.at[sl]).wait()
                return carry

            lax.fori_loop(0, nv, one, 0)

    def compute(xb):
        x1 = pltpu.bitcast(xb[:, 0:D_MODEL], F32)
        hp = xb[:, OFF_H2P:OFF_EXT]
        h2a = pltpu.bitcast(hp << 16, F32).astype(BF16)
        h2b = pltpu.bitcast(hp & jnp.uint32(0xFFFF0000), F32).astype(BF16)
        ext = pltpu.bitcast(xb[:, OFF_EXT:ROW_W], F32)
        w_lo, w_hi = ext[:, 0:1], ext[:, 1:2]
        sel = ext.astype(BF16)
        ga2 = _dot(sel, gtab_hi_ref[...]) + _dot(sel, gtab_lo_ref[...])

        def expert(wgu_ref, wd_ref):
            gu = _dot(h2a, wgu_ref[0, 0:H2P_W]) + _dot(h2b, wgu_ref[0, H2P_W:D_MODEL])
            hg, hu = gu[:, 0:FF], gu[:, FF:2 * FF]
            act = (hg * _sigmoid(hg)) * hu
            return _dot(act.astype(BF16), wd_ref[0])

        y = w_lo * expert(wgu_a, wd_a) + w_hi * expert(wgu_b, wd_b)
        return x1 + ga2 * (_rms(y) * ln2post_ref[...])

    @pl.when(i == 0)
    def _():
        for d in range(GATHER_DEPTH):
            start_gather(d, d < nt)

    @pl.when(i < nt)
    def _():
        nv = nv_ref[i]
        osl = lax.rem(i, 2)
        wait_gather(i)

        @pl.when(i >= 2)
        def _():
            wait_scatter(osl, nv_ref[jnp.maximum(i - 2, 0)])

        start_gather(i + GATHER_DEPTH, i + GATHER_DEPTH < nt)
        x2d[...] = xbuf[pl.ds(lax.rem(i, nbuf) * TM, TM)].reshape(TM, ROW_W)
        obuf[osl] = compute(x2d[...])

        for r in range(TM):
            @pl.when(r < nv)
            def _():
                tok = perm_ref[i * TM + r]
                pltpu.make_async_copy(obuf.at[osl, pl.ds(r, 1)], out_hbm.at[pl.ds(tok, 1)], ssem.at[osl]).start()

        @pl.when(i == nt - 1)
        def _():
            wait_scatter(osl, nv)

            @pl.when(i >= 1)
            def _():
                wait_scatter(1 - osl, nv_ref[jnp.maximum(i - 1, 0)])


def _moe(rows, gtab_hi, gtab_lo, ln2post, wgu, wd, nt, nv, ea, eb, perm, n_tiles):
    n = rows.shape[0]
    const2 = lambda i, *_: (0, 0)
    grid_spec = pltpu.PrefetchScalarGridSpec(
        num_scalar_prefetch=5,
        grid=(n_tiles,),
        in_specs=[pl.BlockSpec(memory_space=pl.ANY),
                  pl.BlockSpec(gtab_hi.shape, const2),
                  pl.BlockSpec(gtab_lo.shape, const2),
                  pl.BlockSpec(ln2post.shape, const2),
                  pl.BlockSpec((1, D_MODEL, 2 * FF), lambda i, nt, nv, ea, eb, perm: (ea[i], 0, 0)),
                  pl.BlockSpec((1, FF, D_MODEL), lambda i, nt, nv, ea, eb, perm: (ea[i], 0, 0)),
                  pl.BlockSpec((1, D_MODEL, 2 * FF), lambda i, nt, nv, ea, eb, perm: (eb[i], 0, 0)),
                  pl.BlockSpec((1, FF, D_MODEL), lambda i, nt, nv, ea, eb, perm: (eb[i], 0, 0))],
        out_specs=pl.BlockSpec(memory_space=pl.ANY),
        scratch_shapes=[pltpu.VMEM(((GATHER_DEPTH + 1) * TM, 1, ROW_W), U32),
                        pltpu.VMEM((TM, ROW_W), U32),
                        pltpu.VMEM((2, TM, D_MODEL), F32),
                        pltpu.SemaphoreType.DMA((GATHER_DEPTH + 1,)),
                        pltpu.SemaphoreType.DMA((2,))],
    )
    return pl.pallas_call(
        _moe_kernel,
        grid_spec=grid_spec,
        out_shape=jax.ShapeDtypeStruct((n, D_MODEL), F32),
        compiler_params=pltpu.CompilerParams(dimension_semantics=("arbitrary",),
                                             vmem_limit_bytes=VMEM_LIMIT),
        name="moe",
    )(nt, nv, ea, eb, perm, rows, gtab_hi, gtab_lo, ln2post, wgu, wd, wgu, wd)


def kernel(x, c, ln1_pre, ln1_post, ln2_pre, ln2_post, w_ada, b_ada, w_in, attn_sinks, attn_out_norm,
           hgrn_lb, hgrn_out_norm, w_out, w_router_group, b_router_group, w_router_expert,
           b_router_expert, w_exp_gate, w_exp_up, w_exp_down):
    bsz, seq, d = x.shape
    assert d == D_MODEL and seq % TQ == 0 and w_ada.shape[0] == 1 and hgrn_lb.shape[0] == 2
    n = bsz * seq

    mod = _ada(c, w_ada[0], b_ada[0])
    mod3 = mod.reshape(bsz, 6, d)

    wr = jnp.concatenate([w_router_group[0], jnp.zeros((d, 8 - N_GROUPS), F32), w_router_expert[0],
                          jnp.zeros((d, LANES - ROUTER_ROWS), F32)], axis=1)
    br = jnp.concatenate([b_router_group[0], jnp.full((8 - N_GROUPS,), NEG, F32), b_router_expert[0]])
    wr_hi = wr.astype(BF16)
    wr_lo = (wr - wr_hi.astype(F32)).astype(BF16)

    x1ext, info, cnt = _mixer(
        x.reshape(n, d), mod3, attn_sinks[0], ln1_pre, ln1_post, ln2_pre, attn_out_norm, hgrn_out_norm,
        hgrn_lb, w_in[0].astype(BF16), w_out[0].astype(BF16), wr_hi, wr_lo, br.reshape(ROUTER_ROWS, 1),
        bsz, seq)

    n_tiles = n // TM + N_BUCKETS
    counts = cnt[:N_BUCKETS, 0].astype(I32)
    tiles_per = (counts + TM - 1) // TM
    tile_end = jnp.cumsum(tiles_per)
    tile_start = tile_end - tiles_per
    nt = tile_end[-1]
    bucket = info[:, 0, :].reshape(n)
    rank = info[:, 1, :].reshape(n)
    tid = jnp.arange(n_tiles, dtype=I32)[None, :]
    member = (tid >= tile_start[:, None]) & (tid < tile_end[:, None])
    pick = lambda per_bucket: jnp.sum(jnp.where(member, per_bucket, 0), axis=0).astype(I32)
    bidx = np.arange(N_BUCKETS, dtype=np.int32)
    ea_of = jnp.asarray((bidx // N_PAIRS) * EPG + _PAIR_A[bidx % N_PAIRS])[:, None]
    eb_of = jnp.asarray((bidx // N_PAIRS) * EPG + _PAIR_B[bidx % N_PAIRS])[:, None]
    last_used = jnp.arange(N_BUCKETS)[:, None] == jnp.max(jnp.where(tiles_per > 0, jnp.arange(N_BUCKETS), 0))
    unused = tid[0] >= nt
    nv = pick(jnp.clip(counts[:, None] - (tid - tile_start[:, None]) * TM, 0, TM))
    ea = jnp.where(unused, jnp.sum(jnp.where(last_used, ea_of, 0)), pick(ea_of)).astype(I32)
    eb = jnp.where(unused, jnp.sum(jnp.where(last_used, eb_of, 0)), pick(eb_of)).astype(I32)

    pad128 = lambda a: jnp.concatenate([a, jnp.zeros((LANES - a.shape[0],), I32)])
    row_start = jnp.concatenate([tile_start, nt.reshape(1)]) * TM
    perm = _perm(pad128(row_start), pad128(counts), bucket, rank, n_tiles * TM)

    wgu = jnp.concatenate([w_exp_gate[0], w_exp_up[0]], axis=-1).astype(BF16)
    wd = w_exp_down[0].astype(BF16)
    gtab = jnp.zeros((LANES, d), F32).at[EXT_BATCH0:EXT_BATCH0 + bsz].set(mod3[:, 5, :])
    gtab_hi = gtab.astype(BF16)
    gtab_lo = (gtab - gtab_hi.astype(F32)).astype(BF16)
    out = _moe(x1ext, gtab_hi, gtab_lo, ln2_post, wgu, wd, nt.reshape(1), nv, ea, eb, perm, n_tiles)
    return out.reshape(bsz, seq, d)
```

```python
import functools

import numpy as np
import jax
import jax.numpy as jnp
from jax import lax
from jax.experimental import pallas as pl
from jax.experimental.pallas import tpu as pltpu

F32 = jnp.float32
BF16 = jnp.bfloat16
I32 = jnp.int32

D_MODEL = 1024
ATTN_HEADS = 8
HEAD_DIM = 64
WINDOW = 128
ATTN_W = 512
KV_W = 128
HG_HEADS = 4
HG_DIM = 128
HG_W = 512
HG_CHUNK = 32
IN_W = 2816
N_GROUPS = 4
EPG = 8
N_EXPERTS = 32
FF = 256
N_PAIRS = EPG * (EPG - 1) // 2
N_BUCKETS = N_GROUPS * N_PAIRS
EPS = 1e-6
NEG = -1e30
LOG2E = 1.4426950408889634

LANES = 128
H2P_W = D_MODEL // 2
EXT_W = LANES
OFF_H2P = D_MODEL
OFF_EXT = D_MODEL + H2P_W
ROW_W = OFF_EXT + EXT_W
EXT_BATCH0 = 8
U32 = jnp.uint32
ROUTER_ROWS = 8 + N_EXPERTS

TQ = 256
TM = 128
VMEM_LIMIT = 56 * 1024 * 1024

OFF_Q, OFF_K, OFF_V, OFF_HQ, OFF_HF, OFF_HI, OFF_HG = 0, 512, 640, 768, 1280, 1792, 2304

_PAIR_A = np.array([a for a in range(EPG) for b in range(a + 1, EPG)], np.int32)
_PAIR_B = np.array([b for a in range(EPG) for b in range(a + 1, EPG)], np.int32)


def _dot(a, b):
    return jnp.dot(a, b, preferred_element_type=F32)


def _dot_nt(a, b):
    return lax.dot_general(a, b, (((1,), (1,)), ((), ())), preferred_element_type=F32)


def _dot_tn(a, b):
    return lax.dot_general(a, b, (((0,), (0,)), ((), ())), preferred_element_type=F32)


def _split(a):
    hi = a.astype(BF16)
    lo = (a - hi.astype(F32)).astype(BF16)
    return hi, lo


def _rms(x):
    return x * lax.rsqrt(jnp.mean(x * x, axis=-1, keepdims=True) + EPS)


def _sigmoid(x):
    return 0.5 * jnp.tanh(0.5 * x) + 0.5


def _ada_kernel(c_ref, w_ref, b_ref, o_ref):
    c = c_ref[...]
    ca = c * _sigmoid(c)
    c_hi, c_lo = _split(ca)
    w_hi, w_lo = _split(w_ref[...])
    o_ref[...] = _dot(c_hi, w_hi) + _dot(c_lo, w_hi) + _dot(c_hi, w_lo) + b_ref[...]


def _ada(c, w, b):
    bsz, d = c.shape
    n_out = w.shape[1]
    return pl.pallas_call(
        _ada_kernel,
        grid=(n_out // d,),
        in_specs=[pl.BlockSpec((bsz, d), lambda j: (0, 0)),
                  pl.BlockSpec((d, d), lambda j: (0, j)),
                  pl.BlockSpec((1, d), lambda j: (0, j))],
        out_specs=pl.BlockSpec((bsz, d), lambda j: (0, j)),
        out_shape=jax.ShapeDtypeStruct((bsz, n_out), F32),
        compiler_params=pltpu.CompilerParams(dimension_semantics=("arbitrary",),
                                             vmem_limit_bytes=VMEM_LIMIT),
        name="adaln",
    )(c, w, b.reshape(1, n_out))


def _attention(proj, kprev_ref, vprev_ref, sinks_ref, bias_ref, t, s_scr, m_scr, p_scr):
    tq = proj.shape[0]
    q = (proj[:, OFF_Q:OFF_Q + ATTN_W] * (HEAD_DIM ** -0.5 * LOG2E)).astype(BF16)
    kf = jnp.concatenate([kprev_ref[...], proj[:, OFF_K:OFF_K + KV_W]], axis=0)
    vf = jnp.concatenate([vprev_ref[...], proj[:, OFF_V:OFF_V + KV_W]], axis=0)
    kprev_ref[...] = proj[tq - WINDOW:, OFF_K:OFF_K + KV_W]
    vprev_ref[...] = proj[tq - WINDOW:, OFF_V:OFF_V + KV_W]

    lo = lax.broadcasted_iota(I32, kf.shape, 1) < HEAD_DIM
    kr = pltpu.roll(kf, HEAD_DIM, axis=1)
    vr = pltpu.roll(vf, HEAD_DIM, axis=1)

    def variants(a, ar):
        return [[jnp.where(lo, a, 0.0).astype(BF16), jnp.where(lo, 0.0, ar).astype(BF16)],
                [jnp.where(lo, ar, 0.0).astype(BF16), jnp.where(lo, 0.0, a).astype(BF16)]]

    kvar = variants(kf, kr)
    vvar = variants(vf, vr)

    first = jnp.where(t > 0, 0, 1)

    nblk = tq // WINDOW
    idx = lambda j, h: j * ATTN_HEADS + h
    keys = lambda a, j: a[j * WINDOW:(j + 2) * WINDOW]

    def scores():
        for j in range(nblk):
            for h in range(ATTN_HEADS):
                p, par = h // 2, h % 2
                qp = q[j * WINDOW:(j + 1) * WINDOW, p * LANES:(p + 1) * LANES]
                s = _dot_nt(qp, keys(kvar[p // 2][par], j)) + bias_ref[first if j == 0 else 0, h]
                s_scr[idx(j, h)] = s
                m_scr[idx(j, h)] = jnp.maximum(jnp.max(s, axis=-1, keepdims=True), sinks_ref[h] * LOG2E)

    def exps():
        for j in range(nblk):
            for h in range(ATTN_HEADS):
                m = m_scr[idx(j, h)]
                pe = jnp.exp2(s_scr[idx(j, h)] - m)
                p_scr[idx(j, h)] = pe.astype(BF16)
                m_scr[idx(j, h)] = 1.0 / (jnp.sum(pe, axis=-1, keepdims=True) + jnp.exp2(sinks_ref[h] * LOG2E - m))

    def values():
        blocks = []
        for j in range(nblk):
            pairs = []
            for p in range(ATTN_HEADS // 2):
                acc = None
                for par in range(2):
                    h = 2 * p + par
                    o = _dot(p_scr[idx(j, h)], keys(vvar[p // 2][par], j)) * m_scr[idx(j, h)]
                    acc = o if acc is None else acc + o
                pairs.append(acc)
            blocks.append(jnp.concatenate(pairs, axis=1))
        return jnp.concatenate(blocks, axis=0)

    return scores, exps, values


def _hgrn2(proj, lb, st_ref, hnorm, u_scr, stb_scr):
    tq = proj.shape[0]
    nc = tq // HG_CHUNK
    qr = proj[:, OFF_HQ:OFF_HQ + HG_W]
    fr = proj[:, OFF_HF:OFF_HF + HG_W]
    iv = proj[:, OFF_HI:OFF_HI + HG_W]
    gr = proj[:, OFF_HG:OFF_HG + HG_W]
    qh = qr * _sigmoid(qr)
    f = lb + (1.0 - lb) * _sigmoid(fr)
    kk = 1.0 - f
    logf = jnp.log(f)

    rmod = lax.broadcasted_iota(I32, (tq, HG_W), 0) & (HG_CHUNK - 1)
    bc = logf
    s = 1
    while s < HG_CHUNK:
        bc = bc + jnp.where(rmod >= s, pltpu.roll(bc, s, axis=0), 0.0)
        s *= 2

    b3 = bc.reshape(nc, HG_CHUNK, HG_W)
    blast = b3[:, HG_CHUNK - 1:HG_CHUNK, :]
    kend = (kk.reshape(nc, HG_CHUNK, HG_W) * jnp.exp(blast - b3)).reshape(tq, HG_W)
    decay = jnp.exp(blast).reshape(nc, HG_W)
    qdec = (qh * jnp.exp(bc)).astype(BF16)
    kdec = (kk * jnp.exp(-bc)).astype(BF16)
    kend = kend.astype(BF16)
    ivb = iv.astype(BF16)

    ri = lax.broadcasted_iota(I32, (tq, tq), 0)
    ci = lax.broadcasted_iota(I32, (tq, tq), 1)
    cmask = ((ri // HG_CHUNK) == (ci // HG_CHUNK)) & (ri >= ci)

    heads = [slice(hh * HG_DIM, (hh + 1) * HG_DIM) for hh in range(HG_HEADS)]
    chunks = [slice(n * HG_CHUNK, (n + 1) * HG_CHUNK) for n in range(nc)]

    lane_head = lax.broadcasted_iota(I32, (HG_CHUNK, HG_W), 1) // HG_DIM
    for n, rs in enumerate(chunks):
        vstack = jnp.concatenate([ivb[rs, sl] for sl in heads], axis=0)
        kblk = jnp.concatenate([jnp.where(lane_head == hh, kend[rs], 0.0).astype(BF16)
                                for hh in range(HG_HEADS)], axis=0)
        u_scr[n] = _dot_tn(vstack, kblk)

    st = st_ref[...]
    for n in range(nc):
        stb_scr[n] = st.astype(BF16)
        st = st * decay[n:n + 1] + u_scr[n]
    st_ref[...] = st

    outs = []
    for hh, sl in enumerate(heads):
        a = _dot_nt(qdec[:, sl], kdec[:, sl])
        a = jnp.where(cmask, a, 0.0).astype(BF16)
        o_intra = _dot(a, ivb[:, sl])
        inter = [_dot_nt(qdec[rs, sl], stb_scr[n, :, sl]) for n, rs in enumerate(chunks)]
        o = o_intra + jnp.concatenate(inter, axis=0)
        o = _rms(o) * hnorm[:, sl]
        g = gr[:, sl]
        outs.append(o * (g * _sigmoid(g)))
    return jnp.concatenate(outs, axis=1)


def _route_topk(h2, wr_hi, wr_lo, br):
    tq = h2.shape[0]
    h_hi, h_lo = _split(h2)
    logits = _dot(h_hi, wr_hi) + _dot(h_lo, wr_hi) + _dot(h_hi, wr_lo)
    lt = logits.T[0:ROUTER_ROWS] + br
    sub = lax.broadcasted_iota(I32, (8, tq), 0).astype(F32)

    gl = lt[0:8]
    gm = jnp.max(gl, axis=0, keepdims=True)
    gidx = jnp.min(jnp.where(gl == gm, sub, 8.0), axis=0, keepdims=True)
    g_w = 1.0 / jnp.sum(jnp.exp(gl - gm), axis=0, keepdims=True)

    es = lt[8:16]
    for g in range(1, N_GROUPS):
        es = jnp.where(gidx == float(g), lt[8 + 8 * g:16 + 8 * g], es)
    m1 = jnp.max(es, axis=0, keepdims=True)
    i1 = jnp.min(jnp.where(es == m1, sub, 8.0), axis=0, keepdims=True)
    e2 = jnp.where(sub == i1, NEG, es)
    m2 = jnp.max(e2, axis=0, keepdims=True)
    i2 = jnp.min(jnp.where(e2 == m2, sub, 8.0), axis=0, keepdims=True)
    dd = jnp.exp(m2 - m1)
    w1 = g_w / (1.0 + dd)
    w2 = g_w * dd / (1.0 + dd)
    first_low = i1 < i2
    ea = jnp.minimum(i1, i2)
    eb = jnp.maximum(i1, i2)
    w_lo = jnp.where(first_low, w1, w2)
    w_hi = jnp.where(first_low, w2, w1)
    pair = ea * (15.0 - ea) * 0.5 + (eb - ea - 1.0)
    bucket = gidx * float(N_PAIRS) + pair
    return bucket, w_lo, w_hi


def _route_rank(bucket, w_lo, w_hi, carry_ref, bidx, live):
    tq = bucket.shape[1]
    brow = lax.broadcasted_iota(I32, (LANES, tq), 0).astype(F32)
    onehot = brow == bucket
    oh = jnp.where(onehot, live, 0.0)
    ti = lax.broadcasted_iota(I32, (tq, tq), 0)
    tj = lax.broadcasted_iota(I32, (tq, tq), 1)
    upper = jnp.where(ti < tj, 1.0, 0.0).astype(BF16)
    before = _dot(oh.astype(BF16), upper) + carry_ref[...]
    rank = jnp.sum(jnp.where(onehot, before, 0.0), axis=0, keepdims=True)
    carry_ref[...] = carry_ref[...] + jnp.sum(oh, axis=1, keepdims=True)

    lane_row = lax.broadcasted_iota(I32, (LANES - EXT_BATCH0, tq), 0)
    onehot_b = jnp.where(lane_row == bidx, 1.0, 0.0)
    info = jnp.concatenate([w_lo, w_hi, jnp.zeros((EXT_BATCH0 - 2, tq), F32), onehot_b], axis=0)
    return rank, info.T


def _mixer_kernel(sinks_ref, x_ref, mod_ref, ln1pre_ref, ln1post_ref, ln2pre_ref, anorm_ref, hnorm_ref,
                  lb_ref, win_ref, wout_ref, wrhi_ref, wrlo_ref, br_ref, bias_ref,
                  rows_ref, info_ref, cnt_ref,
                  kprev_ref, vprev_ref, st_ref, carry_ref, s_scr, m_scr, p_scr, u_scr, stb_scr,
                  h2_scr, proj_scr, keep_scr, *, tiles_per_seq):
    s = pl.program_id(0)
    n_tiles = pl.num_programs(0) - 1
    tq = x_ref.shape[0]
    t = lax.rem(jnp.minimum(s, n_tiles - 1), tiles_per_seq)
    bits = lambda a: pltpu.bitcast(a, U32)

    @pl.when(s == 0)
    def _():
        carry_ref[...] = jnp.zeros_like(carry_ref)
        h2_scr[...] = jnp.zeros_like(h2_scr)
        keep_scr[...] = jnp.zeros_like(keep_scr)

    @pl.when(t == 0)
    def _():
        st_ref[...] = jnp.zeros_like(st_ref)
        kprev_ref[...] = jnp.zeros_like(kprev_ref)
        vprev_ref[...] = jnp.zeros_like(vprev_ref)

    x = x_ref[...]
    mod = mod_ref[0]
    sh1, sc1, ga1, sh2, sc2 = mod[0:1], mod[1:2], mod[2:3], mod[3:4], mod[4:5]
    prev = jnp.maximum(s - 1, 0)

    @pl.when(s <= n_tiles)
    def _():
        bucket, w_lo, w_hi = _route_topk(h2_scr[...], wrhi_ref[...], wrlo_ref[...], br_ref[...])
        h = _rms(x) * ln1pre_ref[...] * (1.0 + sc1) + sh1
        proj_scr[...] = _dot(h.astype(BF16), win_ref[...])
        live = jnp.where(s >= 1, 1.0, 0.0)
        rank, ext = _route_rank(bucket, w_lo, w_hi, carry_ref, prev // tiles_per_seq, live)
        rows_ref[:, :, 0:OFF_EXT] = keep_scr[...].reshape(tq, 1, OFF_EXT)
        rows_ref[:, :, OFF_EXT:ROW_W] = bits(ext).reshape(tq, 1, EXT_W)
        info_ref[0] = jnp.concatenate([bucket, rank, jnp.zeros((6, tq), F32)], axis=0).astype(I32)
        cnt_ref[...] = jnp.broadcast_to(carry_ref[...], cnt_ref.shape)

    proj = proj_scr[...]
    scores, exps, values = _attention(proj, kprev_ref, vprev_ref, sinks_ref, bias_ref, t, s_scr, m_scr, p_scr)
    scores()
    exps()
    attn = _rms(values()) * anorm_ref[...]

    lbr = lb_ref[...]
    le = jnp.exp(lbr - jnp.max(lbr, axis=0, keepdims=True))
    lb = le[0:1] / jnp.sum(le, axis=0, keepdims=True)
    hg = _hgrn2(proj, lb, st_ref, hnorm_ref[...], u_scr, stb_scr)

    mix = _dot(jnp.concatenate([attn, hg], axis=1).astype(BF16), wout_ref[...])
    x1 = x + ga1 * (_rms(mix) * ln1post_ref[...])

    h2 = _rms(x1) * ln2pre_ref[...] * (1.0 + sc2) + sh2
    h2r = h2.astype(BF16).astype(F32)

    keep_scr[:, 0:D_MODEL] = bits(x1)
    keep_scr[:, OFF_H2P:OFF_EXT] = ((bits(h2r[:, 0:H2P_W]) >> 16)
                                    | (bits(h2r[:, H2P_W:D_MODEL]) & jnp.uint32(0xFFFF0000)))
    h2_scr[...] = h2


def _attn_bias():
    qi = np.arange(WINDOW)[:, None]
    kj = np.arange(2 * WINDOW)[None, :]
    dist = qi + WINDOW - kj
    in_win = (dist >= 0) & (dist < WINDOW)
    slopes = 2.0 ** (-8.0 * (np.arange(ATTN_HEADS) + 1.0) / ATTN_HEADS)
    b = np.where(in_win[None], -slopes[:, None, None] * dist[None] * LOG2E, NEG)
    b_first = np.where((kj >= WINDOW)[None], b, NEG)
    return jnp.asarray(np.stack([b, b_first]).astype(np.float32))


def _mixer(x2, mod3, sinks, ln1pre, ln1post, ln2pre, anorm, hnorm, lb, win, wout, wr_hi, wr_lo, br,
           bsz, seq):
    bias = _attn_bias()
    n = bsz * seq
    nt = seq // TQ
    n_tiles = bsz * nt
    n_sc = (TQ // WINDOW) * ATTN_HEADS
    cur = lambda s: jnp.minimum(s, n_tiles - 1)
    const = lambda s: (0, 0)
    full = lambda a: pl.BlockSpec(a.shape, const)
    return pl.pallas_call(
        functools.partial(_mixer_kernel, tiles_per_seq=nt),
        grid=(n_tiles + 1,),
        in_specs=[pl.BlockSpec(memory_space=pltpu.SMEM),
                  pl.BlockSpec((TQ, D_MODEL), lambda s: (cur(s), 0)),
                  pl.BlockSpec((1, 6, D_MODEL), lambda s: (cur(s) // nt, 0, 0)),
                  full(ln1pre), full(ln1post), full(ln2pre), full(anorm), full(hnorm), full(lb),
                  full(win), full(wout), full(wr_hi), full(wr_lo), full(br),
                  pl.BlockSpec(bias.shape, lambda s: (0, 0, 0, 0))],
        out_specs=[pl.BlockSpec((TQ, 1, ROW_W), lambda s: (jnp.maximum(s - 1, 0), 0, 0)),
                   pl.BlockSpec((1, 8, TQ), lambda s: (jnp.maximum(s - 1, 0), 0, 0)),
                   pl.BlockSpec((LANES, LANES), const)],
        out_shape=[jax.ShapeDtypeStruct((n, 1, ROW_W), U32),
                   jax.ShapeDtypeStruct((n // TQ, 8, TQ), I32),
                   jax.ShapeDtypeStruct((LANES, LANES), F32)],
        scratch_shapes=[pltpu.VMEM((WINDOW, KV_W), F32),
                        pltpu.VMEM((WINDOW, KV_W), F32),
                        pltpu.VMEM((HG_DIM, HG_W), F32),
                        pltpu.VMEM((LANES, 1), F32),
                        pltpu.VMEM((n_sc, WINDOW, 2 * WINDOW), F32),
                        pltpu.VMEM((n_sc, WINDOW, 1), F32),
                        pltpu.VMEM((n_sc, WINDOW, 2 * WINDOW), BF16),
                        pltpu.VMEM((TQ // HG_CHUNK, HG_DIM, HG_W), F32),
                        pltpu.VMEM((TQ // HG_CHUNK, HG_DIM, HG_W), BF16),
                        pltpu.VMEM((TQ, D_MODEL), F32),
                        pltpu.VMEM((TQ, IN_W), F32),
                        pltpu.VMEM((TQ, OFF_EXT), U32)],
        compiler_params=pltpu.CompilerParams(dimension_semantics=("arbitrary",),
                                             vmem_limit_bytes=VMEM_LIMIT),
        name="mixer",
    )(sinks, x2, mod3, ln1pre, ln1post, ln2pre, anorm, hnorm, lb, win, wout, wr_hi, wr_lo, br, bias)


PERM_STEPS = 8
PERM_ROWS = 8
PERM_UNROLL = 8


def _perm_kernel(rs_ref, cnt_ref, bucket_ref, rank_ref, perm_ref, pos_vmem, pos_smem, sem):
    pid = pl.program_id(0)
    rows, cols = pos_vmem.shape

    b = bucket_ref[0]
    start = jnp.zeros_like(b)
    for k in range(N_BUCKETS):
        start = jnp.where(b == k, rs_ref[k], start)
    pos_vmem[...] = start + rank_ref[0]
    copies = [pltpu.make_async_copy(pos_vmem.at[r], pos_smem.at[pl.ds(r * cols, cols)], sem) for r in range(rows)]
    for cp in copies:
        cp.start()

    @pl.when(pid == 0)
    def _():
        def per_bucket(k, carry):
            first = rs_ref[k]
            cnt = cnt_ref[k]

            def pad(r, c2):
                perm_ref[first + r] = 0
                return c2

            lax.fori_loop(cnt, ((cnt + TM - 1) // TM) * TM, pad, 0)
            return carry

        lax.fori_loop(0, N_BUCKETS, per_bucket, 0)

        def tail(r, carry):
            perm_ref[r] = 0
            return carry

        lax.fori_loop(rs_ref[N_BUCKETS], perm_ref.shape[0], tail, 0)

    for cp in copies:
        cp.wait()
    base = pid * (rows * cols)

    def body(j, carry):
        i0 = j * PERM_UNROLL
        for u in range(PERM_UNROLL):
            perm_ref[pos_smem[i0 + u]] = base + i0 + u
        return carry

    lax.fori_loop(0, rows * cols // PERM_UNROLL, body, 0)


def _perm(row_start, counts, bucket, rank, n_rows):
    n = bucket.shape[0]
    cols = n // (PERM_STEPS * PERM_ROWS)
    assert n % (PERM_STEPS * PERM_ROWS * PERM_UNROLL) == 0
    chunked = lambda a: a.reshape(PERM_STEPS, PERM_ROWS, cols)
    chunk_spec = pl.BlockSpec((1, PERM_ROWS, cols), lambda i: (i, 0, 0))
    return pl.pallas_call(
        _perm_kernel,
        grid=(PERM_STEPS,),
        in_specs=[pl.BlockSpec(memory_space=pltpu.SMEM),
                  pl.BlockSpec(memory_space=pltpu.SMEM),
                  chunk_spec, chunk_spec],
        out_specs=pl.BlockSpec(memory_space=pltpu.SMEM),
        out_shape=jax.ShapeDtypeStruct((n_rows,), I32),
        scratch_shapes=[pltpu.VMEM((PERM_ROWS, cols), I32), pltpu.SMEM((PERM_ROWS * cols,), I32),
                        pltpu.SemaphoreType.DMA(())],
        compiler_params=pltpu.CompilerParams(dimension_semantics=("arbitrary",)),
        name="perm",
    )(row_start, counts, chunked(bucket), chunked(rank))


GATHER_DEPTH = 3


def _moe_kernel(nt_ref, nv_ref, ea_ref, eb_ref, perm_ref,
                rows_hbm, gtab_hi_ref, gtab_lo_ref, ln2post_ref, wgu_a, wd_a, wgu_b, wd_b,
                out_hbm, xbuf, x2d, obuf, gsem, ssem):
    i = pl.program_id(0)
    nt = nt_ref[0]
    last_tile = pl.num_programs(0) - 1
    nbuf = xbuf.shape[0] // TM

    def start_gather(tile, pred):
        sl = lax.rem(tile, nbuf)
        base = jnp.minimum(tile, last_tile) * TM
        for r in range(TM):
            @pl.when(pred)
            def _():
                tok = perm_ref[base + r]
                pltpu.make_async_copy(rows_hbm.at[tok], xbuf.at[sl * TM + r], gsem.at[sl]).start()

    def wait_gather(tile):
        sl = lax.rem(tile, nbuf)
        pltpu.make_async_copy(rows_hbm.at[pl.ds(0, TM)], xbuf.at[pl.ds(sl * TM, TM)], gsem.at[sl]).wait()

    def wait_scatter(sl, nv):
        @pl.when(nv == TM)
        def _():
            pltpu.make_async_copy(obuf.at[sl], out_hbm.at[pl.ds(0, TM)], ssem.at[sl]).wait()

        @pl.when(nv < TM)
        def _():
            def one(r, carry):
                pltpu.make_async_copy(obuf.at[sl, pl.ds(0, 1)], out_hbm.at[pl.ds(0, 1)], ssem.at[sl]).wait()
                return carry

            lax.fori_loop(0, nv, one, 0)

    def compute(xb):
        x1 = pltpu.bitcast(xb[:, 0:D_MODEL], F32)
        hp = xb[:, OFF_H2P:OFF_EXT]
        h2a = pltpu.bitcast(hp << 16, F32).astype(BF16)
        h2b = pltpu.bitcast(hp & jnp.uint32(0xFFFF0000), F32).astype(BF16)
        ext = pltpu.bitcast(xb[:, OFF_EXT:ROW_W], F32)
        w_lo, w_hi = ext[:, 0:1], ext[:, 1:2]
        sel = ext.astype(BF16)
        ga2 = _dot(sel, gtab_hi_ref[...]) + _dot(sel, gtab_lo_ref[...])

        def expert(wgu_ref, wd_ref):
            gu = _dot(h2a, wgu_ref[0, 0:H2P_W]) + _dot(h2b, wgu_ref[0, H2P_W:D_MODEL])
            hg, hu = gu[:, 0:FF], gu[:, FF:2 * FF]
            act = (hg * _sigmoid(hg)) * hu
            return _dot(act.astype(BF16), wd_ref[0])

        y = w_lo * expert(wgu_a, wd_a) + w_hi * expert(wgu_b, wd_b)
        return x1 + ga2 * (_rms(y) * ln2post_ref[...])

    @pl.when(i == 0)
    def _():
        for d in range(GATHER_DEPTH):
            start_gather(d, d < nt)

    @pl.when(i < nt)
    def _():
        nv = nv_ref[i]
        osl = lax.rem(i, 2)
        wait_gather(i)

        @pl.when(i >= 2)
        def _():
            wait_scatter(osl, nv_ref[jnp.maximum(i - 2, 0)])

        start_gather(i + GATHER_DEPTH, i + GATHER_DEPTH < nt)
        x2d[...] = xbuf[pl.ds(lax.rem(i, nbuf) * TM, TM)].reshape(TM, ROW_W)
        result = compute(x2d[...])

        def scatter_row(k, r):
            tok = perm_ref[i * TM + r]
            pltpu.make_async_copy(obuf.at[k, pl.ds(r, 1)], out_hbm.at[pl.ds(tok, 1)], ssem.at[k]).start()

        for k in range(2):
            @pl.when((osl == k) & (nv == TM))
            def _():
                obuf[k] = result
                for r in range(TM):
                    scatter_row(k, r)

            @pl.when((osl == k) & (nv < TM))
            def _():
                obuf[k] = result
                for r in range(TM):
                    pl.when(r < nv)(functools.partial(scatter_row, k, r))

        @pl.when(i == nt - 1)
        def _():
            wait_scatter(osl, nv)

            @pl.when(i >= 1)
            def _():
                wait_scatter(1 - osl, nv_ref[jnp.maximum(i - 1, 0)])


def _moe(rows, gtab_hi, gtab_lo, ln2post, wgu, wd, nt, nv, ea, eb, perm, n_tiles):
    n = rows.shape[0]
    const2 = lambda i, *_: (0, 0)
    grid_spec = pltpu.PrefetchScalarGridSpec(
        num_scalar_prefetch=5,
        grid=(n_tiles,),
        in_specs=[pl.BlockSpec(memory_space=pl.ANY),
                  pl.BlockSpec(gtab_hi.shape, const2),
                  pl.BlockSpec(gtab_lo.shape, const2),
                  pl.BlockSpec(ln2post.shape, const2),
                  pl.BlockSpec((1, D_MODEL, 2 * FF), lambda i, nt, nv, ea, eb, perm: (ea[i], 0, 0)),
                  pl.BlockSpec((1, FF, D_MODEL), lambda i, nt, nv, ea, eb, perm: (ea[i], 0, 0)),
                  pl.BlockSpec((1, D_MODEL, 2 * FF), lambda i, nt, nv, ea, eb, perm: (eb[i], 0, 0)),
                  pl.BlockSpec((1, FF, D_MODEL), lambda i, nt, nv, ea, eb, perm: (eb[i], 0, 0))],
        out_specs=pl.BlockSpec(memory_space=pl.ANY),
        scratch_shapes=[pltpu.VMEM(((GATHER_DEPTH + 1) * TM, 1, ROW_W), U32),
                        pltpu.VMEM((TM, ROW_W), U32),
                        pltpu.VMEM((2, TM, D_MODEL), F32),
                        pltpu.SemaphoreType.DMA((GATHER_DEPTH + 1,)),
                        pltpu.SemaphoreType.DMA((2,))],
    )
    return pl.pallas_call(
        _moe_kernel,
        grid_spec=grid_spec,
        out_shape=jax.ShapeDtypeStruct((n, D_MODEL), F32),
        compiler_params=pltpu.CompilerParams(dimension_semantics=("arbitrary",),
                                             vmem_limit_bytes=VMEM_LIMIT),
        name="moe",
    )(nt, nv, ea, eb, perm, rows, gtab_hi, gtab_lo, ln2post, wgu, wd, wgu, wd)


def kernel(x, c, ln1_pre, ln1_post, ln2_pre, ln2_post, w_ada, b_ada, w_in, attn_sinks, attn_out_norm,
           hgrn_lb, hgrn_out_norm, w_out, w_router_group, b_router_group, w_router_expert,
           b_router_expert, w_exp_gate, w_exp_up, w_exp_down):
    bsz, seq, d = x.shape
    assert d == D_MODEL and seq % TQ == 0 and w_ada.shape[0] == 1 and hgrn_lb.shape[0] == 2
    n = bsz * seq

    mod = _ada(c, w_ada[0], b_ada[0])
    mod3 = mod.reshape(bsz, 6, d)

    wr = jnp.concatenate([w_router_group[0], jnp.zeros((d, 8 - N_GROUPS), F32), w_router_expert[0],
                          jnp.zeros((d, LANES - ROUTER_ROWS), F32)], axis=1)
    br = jnp.concatenate([b_router_group[0], jnp.full((8 - N_GROUPS,), NEG, F32), b_router_expert[0]])
    wr_hi = wr.astype(BF16)
    wr_lo = (wr - wr_hi.astype(F32)).astype(BF16)

    x1ext, info, cnt = _mixer(
        x.reshape(n, d), mod3, attn_sinks[0], ln1_pre, ln1_post, ln2_pre, attn_out_norm, hgrn_out_norm,
        hgrn_lb, w_in[0].astype(BF16), w_out[0].astype(BF16), wr_hi, wr_lo, br.reshape(ROUTER_ROWS, 1),
        bsz, seq)

    n_tiles = n // TM + N_BUCKETS
    counts = cnt[:N_BUCKETS, 0].astype(I32)
    tiles_per = (counts + TM - 1) // TM
    tile_end = jnp.cumsum(tiles_per)
    tile_start = tile_end - tiles_per
    nt = tile_end[-1]
    bucket = info[:, 0, :].reshape(n)
    rank = info[:, 1, :].reshape(n)
    tid = jnp.arange(n_tiles, dtype=I32)[None, :]
    member = (tid >= tile_start[:, None]) & (tid < tile_end[:, None])
    pick = lambda per_bucket: jnp.sum(jnp.where(member, per_bucket, 0), axis=0).astype(I32)
    bidx = np.arange(N_BUCKETS, dtype=np.int32)
    ea_of = jnp.asarray((bidx // N_PAIRS) * EPG + _PAIR_A[bidx % N_PAIRS])[:, None]
    eb_of = jnp.asarray((bidx // N_PAIRS) * EPG + _PAIR_B[bidx % N_PAIRS])[:, None]
    last_used = jnp.arange(N_BUCKETS)[:, None] == jnp.max(jnp.where(tiles_per > 0, jnp.arange(N_BUCKETS), 0))
    unused = tid[0] >= nt
    nv = pick(jnp.clip(counts[:, None] - (tid - tile_start[:, None]) * TM, 0, TM))
    ea = jnp.where(unused, jnp.sum(jnp.where(last_used, ea_of, 0)), pick(ea_of)).astype(I32)
    eb = jnp.where(unused, jnp.sum(jnp.where(last_used, eb_of, 0)), pick(eb_of)).astype(I32)

    pad128 = lambda a: jnp.concatenate([a, jnp.zeros((LANES - a.shape[0],), I32)])
    row_start = jnp.concatenate([tile_start, nt.reshape(1)]) * TM
    perm = _perm(pad128(row_start), pad128(counts), bucket, rank, n_tiles * TM)

    wgu = jnp.concatenate([w_exp_gate[0], w_exp_up[0]], axis=-1).astype(BF16)
    wd = w_exp_down[0].astype(BF16)
    gtab = jnp.zeros((LANES, d), F32).at[EXT_BATCH0:EXT_BATCH0 + bsz].set(mod3[:, 5, :])
    gtab_hi = gtab.astype(BF16)
    gtab_lo = (gtab - gtab_hi.astype(F32)).astype(BF16)
    out = _moe(x1ext, gtab_hi, gtab_lo, ln2_post, wgu, wd, nt.reshape(1), nv, ea, eb, perm, n_tiles)
    return out.reshape(bsz, seq, d)
```

```python
import functools

import numpy as np
import jax
import jax.numpy as jnp
from jax import lax
from jax.experimental import pallas as pl
from jax.experimental.pallas import tpu as pltpu

F32 = jnp.float32
BF16 = jnp.bfloat16
I32 = jnp.int32

D_MODEL = 1024
ATTN_HEADS = 8
HEAD_DIM = 64
WINDOW = 128
ATTN_W = 512
KV_W = 128
HG_HEADS = 4
HG_DIM = 128
HG_W = 512
HG_CHUNK = 32
IN_W = 2816
N_GROUPS = 4
EPG = 8
N_EXPERTS = 32
FF = 256
N_PAIRS = EPG * (EPG - 1) // 2
N_BUCKETS = N_GROUPS * N_PAIRS
EPS = 1e-6
NEG = -1e30
LOG2E = 1.4426950408889634

LANES = 128
H2P_W = D_MODEL // 2
EXT_W = LANES
OFF_H2P = D_MODEL
OFF_EXT = D_MODEL + H2P_W
ROW_W = OFF_EXT + EXT_W
EXT_BATCH0 = 8
U32 = jnp.uint32
ROUTER_ROWS = 8 + N_EXPERTS

TQ = 256
TM = 128
VMEM_LIMIT = 56 * 1024 * 1024

OFF_Q, OFF_K, OFF_V, OFF_HQ, OFF_HF, OFF_HI, OFF_HG = 0, 512, 640, 768, 1280, 1792, 2304

_PAIR_A = np.array([a for a in range(EPG) for b in range(a + 1, EPG)], np.int32)
_PAIR_B = np.array([b for a in range(EPG) for b in range(a + 1, EPG)], np.int32)


def _dot(a, b):
    return jnp.dot(a, b, preferred_element_type=F32)


def _dot_nt(a, b):
    return lax.dot_general(a, b, (((1,), (1,)), ((), ())), preferred_element_type=F32)


def _dot_tn(a, b):
    return lax.dot_general(a, b, (((0,), (0,)), ((), ())), preferred_element_type=F32)


def _split(a):
    hi = a.astype(BF16)
    lo = (a - hi.astype(F32)).astype(BF16)
    return hi, lo


def _rms(x):
    return x * lax.rsqrt(jnp.mean(x * x, axis=-1, keepdims=True) + EPS)


def _sigmoid(x):
    return 0.5 * jnp.tanh(0.5 * x) + 0.5


def _ada_kernel(c_ref, w_ref, b_ref, o_ref):
    c = c_ref[...]
    ca = c * _sigmoid(c)
    c_hi, c_lo = _split(ca)
    w_hi, w_lo = _split(w_ref[...])
    o_ref[...] = _dot(c_hi, w_hi) + _dot(c_lo, w_hi) + _dot(c_hi, w_lo) + b_ref[...]


def _ada(c, w, b):
    bsz, d = c.shape
    n_out = w.shape[1]
    return pl.pallas_call(
        _ada_kernel,
        grid=(n_out // d,),
        in_specs=[pl.BlockSpec((bsz, d), lambda j: (0, 0)),
                  pl.BlockSpec((d, d), lambda j: (0, j)),
                  pl.BlockSpec((1, d), lambda j: (0, j))],
        out_specs=pl.BlockSpec((bsz, d), lambda j: (0, j)),
        out_shape=jax.ShapeDtypeStruct((bsz, n_out), F32),
        compiler_params=pltpu.CompilerParams(dimension_semantics=("arbitrary",),
                                             vmem_limit_bytes=VMEM_LIMIT),
        name="adaln",
    )(c, w, b.reshape(1, n_out))


def _attention(proj, kprev_ref, vprev_ref, sinks_ref, bias_ref, t, s_scr, m_scr, p_scr):
    tq = proj.shape[0]
    q = (proj[:, OFF_Q:OFF_Q + ATTN_W] * (HEAD_DIM ** -0.5 * LOG2E)).astype(BF16)
    kf = jnp.concatenate([kprev_ref[...], proj[:, OFF_K:OFF_K + KV_W]], axis=0)
    vf = jnp.concatenate([vprev_ref[...], proj[:, OFF_V:OFF_V + KV_W]], axis=0)
    kprev_ref[...] = proj[tq - WINDOW:, OFF_K:OFF_K + KV_W]
    vprev_ref[...] = proj[tq - WINDOW:, OFF_V:OFF_V + KV_W]

    lo = lax.broadcasted_iota(I32, kf.shape, 1) < HEAD_DIM
    kr = pltpu.roll(kf, HEAD_DIM, axis=1)
    vr = pltpu.roll(vf, HEAD_DIM, axis=1)

    def variants(a, ar):
        return [[jnp.where(lo, a, 0.0).astype(BF16), jnp.where(lo, 0.0, ar).astype(BF16)],
                [jnp.where(lo, ar, 0.0).astype(BF16), jnp.where(lo, 0.0, a).astype(BF16)]]

    kvar = variants(kf, kr)
    vvar = variants(vf, vr)

    first = jnp.where(t > 0, 0, 1)

    nblk = tq // WINDOW
    idx = lambda j, h: j * ATTN_HEADS + h
    keys = lambda a, j: a[j * WINDOW:(j + 2) * WINDOW]

    def scores():
        for j in range(nblk):
            for h in range(ATTN_HEADS):
                p, par = h // 2, h % 2
                qp = q[j * WINDOW:(j + 1) * WINDOW, p * LANES:(p + 1) * LANES]
                s = _dot_nt(qp, keys(kvar[p // 2][par], j)) + bias_ref[first if j == 0 else 0, h]
                s_scr[idx(j, h)] = s
                m_scr[idx(j, h)] = jnp.maximum(jnp.max(s, axis=-1, keepdims=True), sinks_ref[h] * LOG2E)

    def exps():
        for j in range(nblk):
            for h in range(ATTN_HEADS):
                m = m_scr[idx(j, h)]
                pe = jnp.exp2(s_scr[idx(j, h)] - m)
                p_scr[idx(j, h)] = pe.astype(BF16)
                m_scr[idx(j, h)] = 1.0 / (jnp.sum(pe, axis=-1, keepdims=True) + jnp.exp2(sinks_ref[h] * LOG2E - m))

    def values():
        blocks = []
        for j in range(nblk):
            pairs = []
            for p in range(ATTN_HEADS // 2):
                acc = None
                for par in range(2):
                    h = 2 * p + par
                    o = _dot(p_scr[idx(j, h)], keys(vvar[p // 2][par], j)) * m_scr[idx(j, h)]
                    acc = o if acc is None else acc + o
                pairs.append(acc)
            blocks.append(jnp.concatenate(pairs, axis=1))
        return jnp.concatenate(blocks, axis=0)

    return scores, exps, values


def _hgrn2(proj, lb, st_ref, hnorm, u_scr, stb_scr):
    tq = proj.shape[0]
    nc = tq // HG_CHUNK
    qr = proj[:, OFF_HQ:OFF_HQ + HG_W]
    fr = proj[:, OFF_HF:OFF_HF + HG_W]
    iv = proj[:, OFF_HI:OFF_HI + HG_W]
    gr = proj[:, OFF_HG:OFF_HG + HG_W]
    qh = qr * _sigmoid(qr)
    f = lb + (1.0 - lb) * _sigmoid(fr)
    kk = 1.0 - f
    logf = jnp.log(f)

    rmod = lax.broadcasted_iota(I32, (tq, HG_W), 0) & (HG_CHUNK - 1)
    bc = logf
    s = 1
    while s < HG_CHUNK:
        bc = bc + jnp.where(rmod >= s, pltpu.roll(bc, s, axis=0), 0.0)
        s *= 2

    b3 = bc.reshape(nc, HG_CHUNK, HG_W)
    blast = b3[:, HG_CHUNK - 1:HG_CHUNK, :]
    kend = (kk.reshape(nc, HG_CHUNK, HG_W) * jnp.exp(blast - b3)).reshape(tq, HG_W)
    decay = jnp.exp(blast).reshape(nc, HG_W)
    qdec = (qh * jnp.exp(bc)).astype(BF16)
    kdec = (kk * jnp.exp(-bc)).astype(BF16)
    kend = kend.astype(BF16)
    ivb = iv.astype(BF16)

    ri = lax.broadcasted_iota(I32, (tq, tq), 0)
    ci = lax.broadcasted_iota(I32, (tq, tq), 1)
    cmask = ((ri // HG_CHUNK) == (ci // HG_CHUNK)) & (ri >= ci)

    heads = [slice(hh * HG_DIM, (hh + 1) * HG_DIM) for hh in range(HG_HEADS)]
    chunks = [slice(n * HG_CHUNK, (n + 1) * HG_CHUNK) for n in range(nc)]

    lane_head = lax.broadcasted_iota(I32, (HG_CHUNK, HG_W), 1) // HG_DIM
    for n, rs in enumerate(chunks):
        vstack = jnp.concatenate([ivb[rs, sl] for sl in heads], axis=0)
        kblk = jnp.concatenate([jnp.where(lane_head == hh, kend[rs], 0.0).astype(BF16)
                                for hh in range(HG_HEADS)], axis=0)
        u_scr[n] = _dot_tn(vstack, kblk)

    st = st_ref[...]
    for n in range(nc):
        stb_scr[n] = st.astype(BF16)
        st = st * decay[n:n + 1] + u_scr[n]
    st_ref[...] = st

    outs = []
    for hh, sl in enumerate(heads):
        a = _dot_nt(qdec[:, sl], kdec[:, sl])
        a = jnp.where(cmask, a, 0.0).astype(BF16)
        o_intra = _dot(a, ivb[:, sl])
        inter = [_dot_nt(qdec[rs, sl], stb_scr[n, :, sl]) for n, rs in enumerate(chunks)]
        o = o_intra + jnp.concatenate(inter, axis=0)
        o = _rms(o) * hnorm[:, sl]
        g = gr[:, sl]
        outs.append(o * (g * _sigmoid(g)))
    return jnp.concatenate(outs, axis=1)


def _route_topk(h2, wr_hi, wr_lo, br):
    tq = h2.shape[0]
    h_hi, h_lo = _split(h2)
    logits = _dot(h_hi, wr_hi) + _dot(h_lo, wr_hi) + _dot(h_hi, wr_lo)
    lt = logits.T[0:ROUTER_ROWS] + br
    sub = lax.broadcasted_iota(I32, (8, tq), 0).astype(F32)

    gl = lt[0:8]
    gm = jnp.max(gl, axis=0, keepdims=True)
    gidx = jnp.min(jnp.where(gl == gm, sub, 8.0), axis=0, keepdims=True)
    g_w = 1.0 / jnp.sum(jnp.exp(gl - gm), axis=0, keepdims=True)

    es = lt[8:16]
    for g in range(1, N_GROUPS):
        es = jnp.where(gidx == float(g), lt[8 + 8 * g:16 + 8 * g], es)
    m1 = jnp.max(es, axis=0, keepdims=True)
    i1 = jnp.min(jnp.where(es == m1, sub, 8.0), axis=0, keepdims=True)
    e2 = jnp.where(sub == i1, NEG, es)
    m2 = jnp.max(e2, axis=0, keepdims=True)
    i2 = jnp.min(jnp.where(e2 == m2, sub, 8.0), axis=0, keepdims=True)
    dd = jnp.exp(m2 - m1)
    w1 = g_w / (1.0 + dd)
    w2 = g_w * dd / (1.0 + dd)
    first_low = i1 < i2
    ea = jnp.minimum(i1, i2)
    eb = jnp.maximum(i1, i2)
    w_lo = jnp.where(first_low, w1, w2)
    w_hi = jnp.where(first_low, w2, w1)
    pair = ea * (15.0 - ea) * 0.5 + (eb - ea - 1.0)
    bucket = gidx * float(N_PAIRS) + pair
    return bucket, w_lo, w_hi


def _route_rank(bucket, w_lo, w_hi, carry_ref, bidx, live):
    tq = bucket.shape[1]
    brow = lax.broadcasted_iota(I32, (LANES, tq), 0).astype(F32)
    onehot = brow == bucket
    oh = jnp.where(onehot, live, 0.0)
    ti = lax.broadcasted_iota(I32, (tq, tq), 0)
    tj = lax.broadcasted_iota(I32, (tq, tq), 1)
    upper = jnp.where(ti < tj, 1.0, 0.0).astype(BF16)
    before = _dot(oh.astype(BF16), upper) + carry_ref[...]
    rank = jnp.sum(jnp.where(onehot, before, 0.0), axis=0, keepdims=True)
    carry_ref[...] = carry_ref[...] + jnp.sum(oh, axis=1, keepdims=True)

    lane_row = lax.broadcasted_iota(I32, (LANES - EXT_BATCH0, tq), 0)
    onehot_b = jnp.where(lane_row == bidx, 1.0, 0.0)
    info = jnp.concatenate([w_lo, w_hi, jnp.zeros((EXT_BATCH0 - 2, tq), F32), onehot_b], axis=0)
    return rank, info.T


def _mixer_kernel(sinks_ref, x_ref, mod_ref, ln1pre_ref, ln1post_ref, ln2pre_ref, anorm_ref, hnorm_ref,
                  lb_ref, win_ref, wout_ref, wrhi_ref, wrlo_ref, br_ref, bias_ref,
                  rows_ref, info_ref, cnt_ref,
                  kprev_ref, vprev_ref, st_ref, carry_ref, s_scr, m_scr, p_scr, u_scr, stb_scr,
                  h2_scr, proj_scr, keep_scr, *, tiles_per_seq):
    s = pl.program_id(0)
    n_tiles = pl.num_programs(0) - 1
    tq = x_ref.shape[0]
    t = lax.rem(jnp.minimum(s, n_tiles - 1), tiles_per_seq)
    bits = lambda a: pltpu.bitcast(a, U32)

    @pl.when(s == 0)
    def _():
        carry_ref[...] = jnp.zeros_like(carry_ref)
        h2_scr[...] = jnp.zeros_like(h2_scr)
        keep_scr[...] = jnp.zeros_like(keep_scr)

    @pl.when(t == 0)
    def _():
        st_ref[...] = jnp.zeros_like(st_ref)
        kprev_ref[...] = jnp.zeros_like(kprev_ref)
        vprev_ref[...] = jnp.zeros_like(vprev_ref)

    x = x_ref[...]
    mod = mod_ref[0]
    sh1, sc1, ga1, sh2, sc2 = mod[0:1], mod[1:2], mod[2:3], mod[3:4], mod[4:5]
    prev = jnp.maximum(s - 1, 0)

    @pl.when(s <= n_tiles)
    def _():
        bucket, w_lo, w_hi = _route_topk(h2_scr[...], wrhi_ref[...], wrlo_ref[...], br_ref[...])
        h = _rms(x) * ln1pre_ref[...] * (1.0 + sc1) + sh1
        proj_scr[...] = _dot(h.astype(BF16), win_ref[...])
        live = jnp.where(s >= 1, 1.0, 0.0)
        rank, ext = _route_rank(bucket, w_lo, w_hi, carry_ref, prev // tiles_per_seq, live)
        rows_ref[:, :, 0:OFF_EXT] = keep_scr[...].reshape(tq, 1, OFF_EXT)
        rows_ref[:, :, OFF_EXT:ROW_W] = bits(ext).reshape(tq, 1, EXT_W)
        info_ref[0] = jnp.concatenate([bucket, rank, jnp.zeros((6, tq), F32)], axis=0).astype(I32)
        cnt_ref[...] = jnp.broadcast_to(carry_ref[...], cnt_ref.shape)

    proj = proj_scr[...]
    scores, exps, values = _attention(proj, kprev_ref, vprev_ref, sinks_ref, bias_ref, t, s_scr, m_scr, p_scr)
    scores()
    exps()
    attn = _rms(values()) * anorm_ref[...]

    lbr = lb_ref[...]
    le = jnp.exp(lbr - jnp.max(lbr, axis=0, keepdims=True))
    lb = le[0:1] / jnp.sum(le, axis=0, keepdims=True)
    hg = _hgrn2(proj, lb, st_ref, hnorm_ref[...], u_scr, stb_scr)

    mix = _dot(jnp.concatenate([attn, hg], axis=1).astype(BF16), wout_ref[...])
    x1 = x + ga1 * (_rms(mix) * ln1post_ref[...])

    h2 = _rms(x1) * ln2pre_ref[...] * (1.0 + sc2) + sh2
    h2r = h2.astype(BF16).astype(F32)

    keep_scr[:, 0:D_MODEL] = bits(x1)
    keep_scr[:, OFF_H2P:OFF_EXT] = ((bits(h2r[:, 0:H2P_W]) >> 16)
                                    | (bits(h2r[:, H2P_W:D_MODEL]) & jnp.uint32(0xFFFF0000)))
    h2_scr[...] = h2


def _attn_bias():
    qi = np.arange(WINDOW)[:, None]
    kj = np.arange(2 * WINDOW)[None, :]
    dist = qi + WINDOW - kj
    in_win = (dist >= 0) & (dist < WINDOW)
    slopes = 2.0 ** (-8.0 * (np.arange(ATTN_HEADS) + 1.0) / ATTN_HEADS)
    b = np.where(in_win[None], -slopes[:, None, None] * dist[None] * LOG2E, NEG)
    b_first = np.where((kj >= WINDOW)[None], b, NEG)
    return jnp.asarray(np.stack([b, b_first]).astype(np.float32))


def _mixer(x2, mod3, sinks, ln1pre, ln1post, ln2pre, anorm, hnorm, lb, win, wout, wr_hi, wr_lo, br,
           bsz, seq):
    bias = _attn_bias()
    n = bsz * seq
    nt = seq // TQ
    n_tiles = bsz * nt
    n_sc = (TQ // WINDOW) * ATTN_HEADS
    cur = lambda s: jnp.minimum(s, n_tiles - 1)
    const = lambda s: (0, 0)
    full = lambda a: pl.BlockSpec(a.shape, const)
    return pl.pallas_call(
        functools.partial(_mixer_kernel, tiles_per_seq=nt),
        grid=(n_tiles + 1,),
        in_specs=[pl.BlockSpec(memory_space=pltpu.SMEM),
                  pl.BlockSpec((TQ, D_MODEL), lambda s: (cur(s), 0)),
                  pl.BlockSpec((1, 6, D_MODEL), lambda s: (cur(s) // nt, 0, 0)),
                  full(ln1pre), full(ln1post), full(ln2pre), full(anorm), full(hnorm), full(lb),
                  full(win), full(wout), full(wr_hi), full(wr_lo), full(br),
                  pl.BlockSpec(bias.shape, lambda s: (0, 0, 0, 0))],
        out_specs=[pl.BlockSpec((TQ, 1, ROW_W), lambda s: (jnp.maximum(s - 1, 0), 0, 0)),
                   pl.BlockSpec((1, 8, TQ), lambda s: (jnp.maximum(s - 1, 0), 0, 0)),
                   pl.BlockSpec((LANES, LANES), const)],
        out_shape=[jax.ShapeDtypeStruct((n, 1, ROW_W), U32),
                   jax.ShapeDtypeStruct((n // TQ, 8, TQ), I32),
                   jax.ShapeDtypeStruct((LANES, LANES), F32)],
        scratch_shapes=[pltpu.VMEM((WINDOW, KV_W), F32),
                        pltpu.VMEM((WINDOW, KV_W), F32),
                        pltpu.VMEM((HG_DIM, HG_W), F32),
                        pltpu.VMEM((LANES, 1), F32),
                        pltpu.VMEM((n_sc, WINDOW, 2 * WINDOW), F32),
                        pltpu.VMEM((n_sc, WINDOW, 1), F32),
                        pltpu.VMEM((n_sc, WINDOW, 2 * WINDOW), BF16),
                        pltpu.VMEM((TQ // HG_CHUNK, HG_DIM, HG_W), F32),
                        pltpu.VMEM((TQ // HG_CHUNK, HG_DIM, HG_W), BF16),
                        pltpu.VMEM((TQ, D_MODEL), F32),
                        pltpu.VMEM((TQ, IN_W), F32),
                        pltpu.VMEM((TQ, OFF_EXT), U32)],
        compiler_params=pltpu.CompilerParams(dimension_semantics=("arbitrary",),
                                             vmem_limit_bytes=VMEM_LIMIT),
        name="mixer",
    )(sinks, x2, mod3, ln1pre, ln1post, ln2pre, anorm, hnorm, lb, win, wout, wr_hi, wr_lo, br, bias)


PERM_STEPS = 8
PERM_ROWS = 8
PERM_UNROLL = 8


def _perm_kernel(rs_ref, cnt_ref, bucket_ref, rank_ref, perm_ref, pos_vmem, pos_smem, sem):
    pid = pl.program_id(0)
    rows, cols = pos_vmem.shape

    b = bucket_ref[0]
    start = jnp.zeros_like(b)
    for k in range(N_BUCKETS):
        start = jnp.where(b == k, rs_ref[k], start)
    pos_vmem[...] = start + rank_ref[0]
    copies = [pltpu.make_async_copy(pos_vmem.at[r], pos_smem.at[pl.ds(r * cols, cols)], sem) for r in range(rows)]
    for cp in copies:
        cp.start()

    @pl.when(pid == 0)
    def _():
        def per_bucket(k, carry):
            first = rs_ref[k]
            cnt = cnt_ref[k]

            def pad(r, c2):
                perm_ref[first + r] = 0
                return c2

            lax.fori_loop(cnt, ((cnt + TM - 1) // TM) * TM, pad, 0)
            return carry

        lax.fori_loop(0, N_BUCKETS, per_bucket, 0)

        def tail(r, carry):
            perm_ref[r] = 0
            return carry

        lax.fori_loop(rs_ref[N_BUCKETS], perm_ref.shape[0], tail, 0)

    for cp in copies:
        cp.wait()
    base = pid * (rows * cols)

    def body(j, carry):
        i0 = j * PERM_UNROLL
        for u in range(PERM_UNROLL):
            perm_ref[pos_smem[i0 + u]] = base + i0 + u
        return carry

    lax.fori_loop(0, rows * cols // PERM_UNROLL, body, 0)


def _perm(row_start, counts, bucket, rank, n_rows):
    n = bucket.shape[0]
    cols = n // (PERM_STEPS * PERM_ROWS)
    assert n % (PERM_STEPS * PERM_ROWS * PERM_UNROLL) == 0
    chunked = lambda a: a.reshape(PERM_STEPS, PERM_ROWS, cols)
    chunk_spec = pl.BlockSpec((1, PERM_ROWS, cols), lambda i: (i, 0, 0))
    return pl.pallas_call(
        _perm_kernel,
        grid=(PERM_STEPS,),
        in_specs=[pl.BlockSpec(memory_space=pltpu.SMEM),
                  pl.BlockSpec(memory_space=pltpu.SMEM),
                  chunk_spec, chunk_spec],
        out_specs=pl.BlockSpec(memory_space=pltpu.SMEM),
        out_shape=jax.ShapeDtypeStruct((n_rows,), I32),
        scratch_shapes=[pltpu.VMEM((PERM_ROWS, cols), I32), pltpu.SMEM((PERM_ROWS * cols,), I32),
                        pltpu.SemaphoreType.DMA(())],
        compiler_params=pltpu.CompilerParams(dimension_semantics=("arbitrary",)),
        name="perm",
    )(row_start, counts, chunked(bucket), chunked(rank))


GATHER_DEPTH = 3
TILES_PER_STEP = GATHER_DEPTH + 1


def _moe_kernel(nt_ref, nv_ref, ea_ref, eb_ref, perm_ref,
                rows_hbm, gtab_hi_ref, gtab_lo_ref, ln2post_ref, *refs):
    weights = [refs[4 * g:4 * g + 4] for g in range(TILES_PER_STEP)]
    out_hbm, xbuf, x2d, obuf, gsem, ssem = refs[4 * TILES_PER_STEP:]
    j = pl.program_id(0)
    nt = nt_ref[0]
    last_tile = TILES_PER_STEP * pl.num_programs(0) - 1

    def start_gather(tile, sl, pred):
        base = jnp.minimum(tile, last_tile) * TM
        for r in range(TM):
            @pl.when(pred)
            def _():
                tok = perm_ref[base + r]
                pltpu.make_async_copy(rows_hbm.at[tok], xbuf.at[sl * TM + r], gsem.at[sl]).start()

    def wait_gather(sl):
        pltpu.make_async_copy(rows_hbm.at[pl.ds(0, TM)], xbuf.at[pl.ds(sl * TM, TM)], gsem.at[sl]).wait()

    def wait_scatter(sl, nv):
        @pl.when(nv == TM)
        def _():
            pltpu.make_async_copy(obuf.at[sl], out_hbm.at[pl.ds(0, TM)], ssem.at[sl]).wait()

        @pl.when(nv < TM)
        def _():
            def one(r, carry):
                pltpu.make_async_copy(obuf.at[sl, pl.ds(0, 1)], out_hbm.at[pl.ds(0, 1)], ssem.at[sl]).wait()
                return carry

            lax.fori_loop(0, nv, one, 0)

    def compute(xb, wgu_a, wd_a, wgu_b, wd_b):
        x1 = pltpu.bitcast(xb[:, 0:D_MODEL], F32)
        hp = xb[:, OFF_H2P:OFF_EXT]
        h2a = pltpu.bitcast(hp << 16, F32).astype(BF16)
        h2b = pltpu.bitcast(hp & jnp.uint32(0xFFFF0000), F32).astype(BF16)
        ext = pltpu.bitcast(xb[:, OFF_EXT:ROW_W], F32)
        w_lo, w_hi = ext[:, 0:1], ext[:, 1:2]
        sel = ext.astype(BF16)
        ga2 = _dot(sel, gtab_hi_ref[...]) + _dot(sel, gtab_lo_ref[...])

        def expert(wgu_ref, wd_ref):
            gu = _dot(h2a, wgu_ref[0, 0:H2P_W]) + _dot(h2b, wgu_ref[0, H2P_W:D_MODEL])
            hg, hu = gu[:, 0:FF], gu[:, FF:2 * FF]
            act = (hg * _sigmoid(hg)) * hu
            return _dot(act.astype(BF16), wd_ref[0])

        y = w_lo * expert(wgu_a, wd_a) + w_hi * expert(wgu_b, wd_b)
        return x1 + ga2 * (_rms(y) * ln2post_ref[...])

    @pl.when(j == 0)
    def _():
        for d in range(GATHER_DEPTH):
            start_gather(d, d, d < nt)

    for g in range(TILES_PER_STEP):
        tile = TILES_PER_STEP * j + g
        osl = g % 2

        @pl.when(tile < nt)
        def _():
            nv = nv_ref[tile]
            wait_gather(g)

            @pl.when(tile >= 2)
            def _():
                wait_scatter(osl, nv_ref[jnp.maximum(tile - 2, 0)])

            ahead = tile + GATHER_DEPTH
            start_gather(ahead, (g + GATHER_DEPTH) % TILES_PER_STEP, ahead < nt)
            x2d[...] = xbuf[pl.ds(g * TM, TM)].reshape(TM, ROW_W)
            result = compute(x2d[...], *weights[g])

            def scatter_row(r):
                tok = perm_ref[tile * TM + r]
                pltpu.make_async_copy(obuf.at[osl, pl.ds(r, 1)], out_hbm.at[pl.ds(tok, 1)], ssem.at[osl]).start()

            @pl.when(nv == TM)
            def _():
                obuf[osl] = result
                for r in range(TM):
                    scatter_row(r)

            @pl.when(nv < TM)
            def _():
                obuf[osl] = result
                for r in range(TM):
                    pl.when(r < nv)(functools.partial(scatter_row, r))

            @pl.when(tile == nt - 1)
            def _():
                wait_scatter(osl, nv)

                @pl.when(tile >= 1)
                def _():
                    wait_scatter(1 - osl, nv_ref[jnp.maximum(tile - 1, 0)])


def _moe(rows, gtab_hi, gtab_lo, ln2post, wgu, wd, nt, nv, ea, eb, perm, n_tiles):
    n = rows.shape[0]
    assert n_tiles % TILES_PER_STEP == 0
    const2 = lambda j, *_: (0, 0)

    def wspec(shape, which, g):
        return pl.BlockSpec(shape, lambda j, nt, nv, ea, eb, perm: ((ea, eb)[which][TILES_PER_STEP * j + g], 0, 0))

    wspecs = []
    for g in range(TILES_PER_STEP):
        for which in range(2):
            wspecs += [wspec((1, D_MODEL, 2 * FF), which, g), wspec((1, FF, D_MODEL), which, g)]
    grid_spec = pltpu.PrefetchScalarGridSpec(
        num_scalar_prefetch=5,
        grid=(n_tiles // TILES_PER_STEP,),
        in_specs=[pl.BlockSpec(memory_space=pl.ANY),
                  pl.BlockSpec(gtab_hi.shape, const2),
                  pl.BlockSpec(gtab_lo.shape, const2),
                  pl.BlockSpec(ln2post.shape, const2)] + wspecs,
        out_specs=pl.BlockSpec(memory_space=pl.ANY),
        scratch_shapes=[pltpu.VMEM((TILES_PER_STEP * TM, 1, ROW_W), U32),
                        pltpu.VMEM((TM, ROW_W), U32),
                        pltpu.VMEM((2, TM, D_MODEL), F32),
                        pltpu.SemaphoreType.DMA((TILES_PER_STEP,)),
                        pltpu.SemaphoreType.DMA((2,))],
    )
    return pl.pallas_call(
        _moe_kernel,
        grid_spec=grid_spec,
        out_shape=jax.ShapeDtypeStruct((n, D_MODEL), F32),
        compiler_params=pltpu.CompilerParams(dimension_semantics=("arbitrary",),
                                             vmem_limit_bytes=VMEM_LIMIT),
        name="moe",
    )(nt, nv, ea, eb, perm, rows, gtab_hi, gtab_lo, ln2post, *([wgu, wd, wgu, wd] * TILES_PER_STEP))


def kernel(x, c, ln1_pre, ln1_post, ln2_pre, ln2_post, w_ada, b_ada, w_in, attn_sinks, attn_out_norm,
           hgrn_lb, hgrn_out_norm, w_out, w_router_group, b_router_group, w_router_expert,
           b_router_expert, w_exp_gate, w_exp_up, w_exp_down):
    bsz, seq, d = x.shape
    assert d == D_MODEL and seq % TQ == 0 and w_ada.shape[0] == 1 and hgrn_lb.shape[0] == 2
    n = bsz * seq

    mod = _ada(c, w_ada[0], b_ada[0])
    mod3 = mod.reshape(bsz, 6, d)

    wr = jnp.concatenate([w_router_group[0], jnp.zeros((d, 8 - N_GROUPS), F32), w_router_expert[0],
                          jnp.zeros((d, LANES - ROUTER_ROWS), F32)], axis=1)
    br = jnp.concatenate([b_router_group[0], jnp.full((8 - N_GROUPS,), NEG, F32), b_router_expert[0]])
    wr_hi = wr.astype(BF16)
    wr_lo = (wr - wr_hi.astype(F32)).astype(BF16)

    x1ext, info, cnt = _mixer(
        x.reshape(n, d), mod3, attn_sinks[0], ln1_pre, ln1_post, ln2_pre, attn_out_norm, hgrn_out_norm,
        hgrn_lb, w_in[0].astype(BF16), w_out[0].astype(BF16), wr_hi, wr_lo, br.reshape(ROUTER_ROWS, 1),
        bsz, seq)

    n_tiles = n // TM + N_BUCKETS
    counts = cnt[:N_BUCKETS, 0].astype(I32)
    tiles_per = (counts + TM - 1) // TM
    tile_end = jnp.cumsum(tiles_per)
    tile_start = tile_end - tiles_per
    nt = tile_end[-1]
    bucket = info[:, 0, :].reshape(n)
    rank = info[:, 1, :].reshape(n)
    tid = jnp.arange(n_tiles, dtype=I32)[None, :]
    member = (tid >= tile_start[:, None]) & (tid < tile_end[:, None])
    pick = lambda per_bucket: jnp.sum(jnp.where(member, per_bucket, 0), axis=0).astype(I32)
    bidx = np.arange(N_BUCKETS, dtype=np.int32)
    ea_of = jnp.asarray((bidx // N_PAIRS) * EPG + _PAIR_A[bidx % N_PAIRS])[:, None]
    eb_of = jnp.asarray((bidx // N_PAIRS) * EPG + _PAIR_B[bidx % N_PAIRS])[:, None]
    last_used = jnp.arange(N_BUCKETS)[:, None] == jnp.max(jnp.where(tiles_per > 0, jnp.arange(N_BUCKETS), 0))
    unused = tid[0] >= nt
    nv = pick(jnp.clip(counts[:, None] - (tid - tile_start[:, None]) * TM, 0, TM))
    ea = jnp.where(unused, jnp.sum(jnp.where(last_used, ea_of, 0)), pick(ea_of)).astype(I32)
    eb = jnp.where(unused, jnp.sum(jnp.where(last_used, eb_of, 0)), pick(eb_of)).astype(I32)

    pad128 = lambda a: jnp.concatenate([a, jnp.zeros((LANES - a.shape[0],), I32)])
    row_start = jnp.concatenate([tile_start, nt.reshape(1)]) * TM
    perm = _perm(pad128(row_start), pad128(counts), bucket, rank, n_tiles * TM)

    wgu = jnp.concatenate([w_exp_gate[0], w_exp_up[0]], axis=-1).astype(BF16)
    wd = w_exp_down[0].astype(BF16)
    gtab = jnp.zeros((LANES, d), F32).at[EXT_BATCH0:EXT_BATCH0 + bsz].set(mod3[:, 5, :])
    gtab_hi = gtab.astype(BF16)
    gtab_lo = (gtab - gtab_hi.astype(F32)).astype(BF16)
    out = _moe(x1ext, gtab_hi, gtab_lo, ln2_post, wgu, wd, nt.reshape(1), nv, ea, eb, perm, n_tiles)
    return out.reshape(bsz, seq, d)
```

```python
import functools

import numpy as np
import jax
import jax.numpy as jnp
from jax import lax
from jax.experimental import pallas as pl
from jax.experimental.pallas import tpu as pltpu

F32 = jnp.float32
BF16 = jnp.bfloat16
I32 = jnp.int32

D_MODEL = 1024
ATTN_HEADS = 8
HEAD_DIM = 64
WINDOW = 128
ATTN_W = 512
KV_W = 128
HG_HEADS = 4
HG_DIM = 128
HG_W = 512
HG_CHUNK = 32
IN_W = 2816
N_GROUPS = 4
EPG = 8
N_EXPERTS = 32
FF = 256
N_PAIRS = EPG * (EPG - 1) // 2
N_BUCKETS = N_GROUPS * N_PAIRS
EPS = 1e-6
NEG = -1e30
LOG2E = 1.4426950408889634

LANES = 128
H2P_W = D_MODEL // 2
EXT_W = LANES
OFF_H2P = D_MODEL
OFF_EXT = D_MODEL + H2P_W
ROW_W = OFF_EXT + EXT_W
EXT_BATCH0 = 8
U32 = jnp.uint32
ROUTER_ROWS = 8 + N_EXPERTS

TQ = 256
TM = 128
VMEM_LIMIT = 56 * 1024 * 1024

OFF_Q, OFF_K, OFF_V, OFF_HQ, OFF_HF, OFF_HI, OFF_HG = 0, 512, 640, 768, 1280, 1792, 2304

_PAIR_A = np.array([a for a in range(EPG) for b in range(a + 1, EPG)], np.int32)
_PAIR_B = np.array([b for a in range(EPG) for b in range(a + 1, EPG)], np.int32)


def _dot(a, b):
    return jnp.dot(a, b, preferred_element_type=F32)


def _dot_nt(a, b):
    return lax.dot_general(a, b, (((1,), (1,)), ((), ())), preferred_element_type=F32)


def _dot_tn(a, b):
    return lax.dot_general(a, b, (((0,), (0,)), ((), ())), preferred_element_type=F32)


def _split(a):
    hi = a.astype(BF16)
    lo = (a - hi.astype(F32)).astype(BF16)
    return hi, lo


def _rms(x):
    return x * lax.rsqrt(jnp.mean(x * x, axis=-1, keepdims=True) + EPS)


def _sigmoid(x):
    return 0.5 * jnp.tanh(0.5 * x) + 0.5


def _ada_kernel(c_ref, w_ref, b_ref, o_ref):
    c = c_ref[...]
    ca = c * _sigmoid(c)
    c_hi, c_lo = _split(ca)
    w_hi, w_lo = _split(w_ref[...])
    o_ref[...] = _dot(c_hi, w_hi) + _dot(c_lo, w_hi) + _dot(c_hi, w_lo) + b_ref[...]


def _ada(c, w, b):
    bsz, d = c.shape
    n_out = w.shape[1]
    return pl.pallas_call(
        _ada_kernel,
        grid=(n_out // d,),
        in_specs=[pl.BlockSpec((bsz, d), lambda j: (0, 0)),
                  pl.BlockSpec((d, d), lambda j: (0, j)),
                  pl.BlockSpec((1, d), lambda j: (0, j))],
        out_specs=pl.BlockSpec((bsz, d), lambda j: (0, j)),
        out_shape=jax.ShapeDtypeStruct((bsz, n_out), F32),
        compiler_params=pltpu.CompilerParams(dimension_semantics=("arbitrary",),
                                             vmem_limit_bytes=VMEM_LIMIT),
        name="adaln",
    )(c, w, b.reshape(1, n_out))


def _attention(proj, kprev_ref, vprev_ref, sinks_ref, bias_ref, t, s_scr, m_scr, p_scr):
    tq = proj.shape[0]
    q = (proj[:, OFF_Q:OFF_Q + ATTN_W] * (HEAD_DIM ** -0.5 * LOG2E)).astype(BF16)
    kf = jnp.concatenate([kprev_ref[...], proj[:, OFF_K:OFF_K + KV_W]], axis=0)
    vf = jnp.concatenate([vprev_ref[...], proj[:, OFF_V:OFF_V + KV_W]], axis=0)
    kprev_ref[...] = proj[tq - WINDOW:, OFF_K:OFF_K + KV_W]
    vprev_ref[...] = proj[tq - WINDOW:, OFF_V:OFF_V + KV_W]

    lo = lax.broadcasted_iota(I32, kf.shape, 1) < HEAD_DIM
    kr = pltpu.roll(kf, HEAD_DIM, axis=1)
    vr = pltpu.roll(vf, HEAD_DIM, axis=1)

    def variants(a, ar):
        return [[jnp.where(lo, a, 0.0).astype(BF16), jnp.where(lo, 0.0, ar).astype(BF16)],
                [jnp.where(lo, ar, 0.0).astype(BF16), jnp.where(lo, 0.0, a).astype(BF16)]]

    kvar = variants(kf, kr)
    vvar = variants(vf, vr)

    first = jnp.where(t > 0, 0, 1)

    nblk = tq // WINDOW
    idx = lambda j, h: j * ATTN_HEADS + h
    keys = lambda a, j: a[j * WINDOW:(j + 2) * WINDOW]

    def scores():
        for j in range(nblk):
            for h in range(ATTN_HEADS):
                p, par = h // 2, h % 2
                qp = q[j * WINDOW:(j + 1) * WINDOW, p * LANES:(p + 1) * LANES]
                s = _dot_nt(qp, keys(kvar[p // 2][par], j)) + bias_ref[first if j == 0 else 0, h]
                s_scr[idx(j, h)] = s
                m_scr[idx(j, h)] = jnp.maximum(jnp.max(s, axis=-1, keepdims=True), sinks_ref[h] * LOG2E)

    def exps():
        for j in range(nblk):
            for h in range(ATTN_HEADS):
                m = m_scr[idx(j, h)]
                pe = jnp.exp2(s_scr[idx(j, h)] - m)
                p_scr[idx(j, h)] = pe.astype(BF16)
                m_scr[idx(j, h)] = 1.0 / (jnp.sum(pe, axis=-1, keepdims=True) + jnp.exp2(sinks_ref[h] * LOG2E - m))

    def values():
        blocks = []
        for j in range(nblk):
            pairs = []
            for p in range(ATTN_HEADS // 2):
                acc = None
                for par in range(2):
                    h = 2 * p + par
                    o = _dot(p_scr[idx(j, h)], keys(vvar[p // 2][par], j)) * m_scr[idx(j, h)]
                    acc = o if acc is None else acc + o
                pairs.append(acc)
            blocks.append(jnp.concatenate(pairs, axis=1))
        return jnp.concatenate(blocks, axis=0)

    return scores, exps, values


def _hgrn2(proj, lb, st_ref, hnorm, u_scr, stb_scr):
    tq = proj.shape[0]
    nc = tq // HG_CHUNK
    qr = proj[:, OFF_HQ:OFF_HQ + HG_W]
    fr = proj[:, OFF_HF:OFF_HF + HG_W]
    iv = proj[:, OFF_HI:OFF_HI + HG_W]
    gr = proj[:, OFF_HG:OFF_HG + HG_W]
    qh = qr * _sigmoid(qr)
    f = lb + (1.0 - lb) * _sigmoid(fr)
    kk = 1.0 - f
    logf = jnp.log(f)

    rmod = lax.broadcasted_iota(I32, (tq, HG_W), 0) & (HG_CHUNK - 1)
    bc = logf
    s = 1
    while s < HG_CHUNK:
        bc = bc + jnp.where(rmod >= s, pltpu.roll(bc, s, axis=0), 0.0)
        s *= 2

    b3 = bc.reshape(nc, HG_CHUNK, HG_W)
    blast = b3[:, HG_CHUNK - 1:HG_CHUNK, :]
    kend = (kk.reshape(nc, HG_CHUNK, HG_W) * jnp.exp(blast - b3)).reshape(tq, HG_W)
    decay = jnp.exp(blast).reshape(nc, HG_W)
    qdec = (qh * jnp.exp(bc)).astype(BF16)
    kdec = (kk * jnp.exp(-bc)).astype(BF16)
    kend = kend.astype(BF16)
    ivb = iv.astype(BF16)

    ri = lax.broadcasted_iota(I32, (tq, tq), 0)
    ci = lax.broadcasted_iota(I32, (tq, tq), 1)
    cmask = ((ri // HG_CHUNK) == (ci // HG_CHUNK)) & (ri >= ci)

    heads = [slice(hh * HG_DIM, (hh + 1) * HG_DIM) for hh in range(HG_HEADS)]
    chunks = [slice(n * HG_CHUNK, (n + 1) * HG_CHUNK) for n in range(nc)]

    lane_head = lax.broadcasted_iota(I32, (HG_CHUNK, HG_W), 1) // HG_DIM
    for n, rs in enumerate(chunks):
        vstack = jnp.concatenate([ivb[rs, sl] for sl in heads], axis=0)
        kblk = jnp.concatenate([jnp.where(lane_head == hh, kend[rs], 0.0).astype(BF16)
                                for hh in range(HG_HEADS)], axis=0)
        u_scr[n] = _dot_tn(vstack, kblk)

    st = st_ref[...]
    for n in range(nc):
        stb_scr[n] = st.astype(BF16)
        st = st * decay[n:n + 1] + u_scr[n]
    st_ref[...] = st

    outs = []
    for hh, sl in enumerate(heads):
        a = _dot_nt(qdec[:, sl], kdec[:, sl])
        a = jnp.where(cmask, a, 0.0).astype(BF16)
        o_intra = _dot(a, ivb[:, sl])
        inter = [_dot_nt(qdec[rs, sl], stb_scr[n, :, sl]) for n, rs in enumerate(chunks)]
        o = o_intra + jnp.concatenate(inter, axis=0)
        o = _rms(o) * hnorm[:, sl]
        g = gr[:, sl]
        outs.append(o * (g * _sigmoid(g)))
    return jnp.concatenate(outs, axis=1)


def _route_topk(h2, wr_hi, wr_lo, br):
    tq = h2.shape[0]
    h_hi, h_lo = _split(h2)
    logits = _dot(h_hi, wr_hi) + _dot(h_lo, wr_hi) + _dot(h_hi, wr_lo)
    lt = logits.T[0:ROUTER_ROWS] + br
    sub = lax.broadcasted_iota(I32, (8, tq), 0).astype(F32)

    gl = lt[0:8]
    gm = jnp.max(gl, axis=0, keepdims=True)
    gidx = jnp.min(jnp.where(gl == gm, sub, 8.0), axis=0, keepdims=True)
    g_w = 1.0 / jnp.sum(jnp.exp(gl - gm), axis=0, keepdims=True)

    es = lt[8:16]
    for g in range(1, N_GROUPS):
        es = jnp.where(gidx == float(g), lt[8 + 8 * g:16 + 8 * g], es)
    m1 = jnp.max(es, axis=0, keepdims=True)
    i1 = jnp.min(jnp.where(es == m1, sub, 8.0), axis=0, keepdims=True)
    e2 = jnp.where(sub == i1, NEG, es)
    m2 = jnp.max(e2, axis=0, keepdims=True)
    i2 = jnp.min(jnp.where(e2 == m2, sub, 8.0), axis=0, keepdims=True)
    dd = jnp.exp(m2 - m1)
    w1 = g_w / (1.0 + dd)
    w2 = g_w * dd / (1.0 + dd)
    first_low = i1 < i2
    ea = jnp.minimum(i1, i2)
    eb = jnp.maximum(i1, i2)
    w_lo = jnp.where(first_low, w1, w2)
    w_hi = jnp.where(first_low, w2, w1)
    pair = ea * (15.0 - ea) * 0.5 + (eb - ea - 1.0)
    bucket = gidx * float(N_PAIRS) + pair
    return bucket, w_lo, w_hi


def _route_rank(bucket, w_lo, w_hi, carry_ref, bidx, live):
    tq = bucket.shape[1]
    brow = lax.broadcasted_iota(I32, (LANES, tq), 0).astype(F32)
    onehot = brow == bucket
    oh = jnp.where(onehot, live, 0.0)
    ti = lax.broadcasted_iota(I32, (tq, tq), 0)
    tj = lax.broadcasted_iota(I32, (tq, tq), 1)
    upper = jnp.where(ti < tj, 1.0, 0.0).astype(BF16)
    before = _dot(oh.astype(BF16), upper) + carry_ref[...]
    rank = jnp.sum(jnp.where(onehot, before, 0.0), axis=0, keepdims=True)
    carry_ref[...] = carry_ref[...] + jnp.sum(oh, axis=1, keepdims=True)

    lane_row = lax.broadcasted_iota(I32, (LANES - EXT_BATCH0, tq), 0)
    onehot_b = jnp.where(lane_row == bidx, 1.0, 0.0)
    info = jnp.concatenate([w_lo, w_hi, jnp.zeros((EXT_BATCH0 - 2, tq), F32), onehot_b], axis=0)
    return rank, info.T


def _mixer_kernel(sinks_ref, x_ref, mod_ref, ln1pre_ref, ln1post_ref, ln2pre_ref, anorm_ref, hnorm_ref,
                  lb_ref, win_ref, wout_ref, wrhi_ref, wrlo_ref, br_ref, bias_ref,
                  rows_ref, info_ref, cnt_ref,
                  kprev_ref, vprev_ref, st_ref, carry_ref, s_scr, m_scr, p_scr, u_scr, stb_scr,
                  h2_scr, proj_scr, keep_scr, *, tiles_per_seq):
    s = pl.program_id(0)
    n_tiles = pl.num_programs(0) - 1
    tq = x_ref.shape[0]
    t = lax.rem(jnp.minimum(s, n_tiles - 1), tiles_per_seq)
    bits = lambda a: pltpu.bitcast(a, U32)

    @pl.when(s == 0)
    def _():
        carry_ref[...] = jnp.zeros_like(carry_ref)
        h2_scr[...] = jnp.zeros_like(h2_scr)
        keep_scr[...] = jnp.zeros_like(keep_scr)

    @pl.when(t == 0)
    def _():
        st_ref[...] = jnp.zeros_like(st_ref)
        kprev_ref[...] = jnp.zeros_like(kprev_ref)
        vprev_ref[...] = jnp.zeros_like(vprev_ref)

    x = x_ref[...]
    mod = mod_ref[0]
    sh1, sc1, ga1, sh2, sc2 = mod[0:1], mod[1:2], mod[2:3], mod[3:4], mod[4:5]
    prev = jnp.maximum(s - 1, 0)

    @pl.when(s <= n_tiles)
    def _():
        bucket, w_lo, w_hi = _route_topk(h2_scr[...], wrhi_ref[...], wrlo_ref[...], br_ref[...])
        h = _rms(x) * ln1pre_ref[...] * (1.0 + sc1) + sh1
        proj_scr[...] = _dot(h.astype(BF16), win_ref[...])
        live = jnp.where(s >= 1, 1.0, 0.0)
        rank, ext = _route_rank(bucket, w_lo, w_hi, carry_ref, prev // tiles_per_seq, live)
        rows_ref[:, :, 0:OFF_EXT] = keep_scr[...].reshape(tq, 1, OFF_EXT)
        rows_ref[:, :, OFF_EXT:ROW_W] = bits(ext).reshape(tq, 1, EXT_W)
        info_ref[0] = jnp.concatenate([bucket, rank, jnp.zeros((6, tq), F32)], axis=0).astype(I32)
        cnt_ref[...] = jnp.broadcast_to(carry_ref[...], cnt_ref.shape)

    proj = proj_scr[...]
    scores, exps, values = _attention(proj, kprev_ref, vprev_ref, sinks_ref, bias_ref, t, s_scr, m_scr, p_scr)
    scores()
    exps()
    attn = _rms(values()) * anorm_ref[...]

    lbr = lb_ref[...]
    le = jnp.exp(lbr - jnp.max(lbr, axis=0, keepdims=True))
    lb = le[0:1] / jnp.sum(le, axis=0, keepdims=True)
    hg = _hgrn2(proj, lb, st_ref, hnorm_ref[...], u_scr, stb_scr)

    mix = _dot(jnp.concatenate([attn, hg], axis=1).astype(BF16), wout_ref[...])
    x1 = x + ga1 * (_rms(mix) * ln1post_ref[...])

    h2 = _rms(x1) * ln2pre_ref[...] * (1.0 + sc2) + sh2
    h2r = h2.astype(BF16).astype(F32)

    keep_scr[:, 0:D_MODEL] = bits(x1)
    keep_scr[:, OFF_H2P:OFF_EXT] = ((bits(h2r[:, 0:H2P_W]) >> 16)
                                    | (bits(h2r[:, H2P_W:D_MODEL]) & jnp.uint32(0xFFFF0000)))
    h2_scr[...] = h2


def _attn_bias():
    qi = np.arange(WINDOW)[:, None]
    kj = np.arange(2 * WINDOW)[None, :]
    dist = qi + WINDOW - kj
    in_win = (dist >= 0) & (dist < WINDOW)
    slopes = 2.0 ** (-8.0 * (np.arange(ATTN_HEADS) + 1.0) / ATTN_HEADS)
    b = np.where(in_win[None], -slopes[:, None, None] * dist[None] * LOG2E, NEG)
    b_first = np.where((kj >= WINDOW)[None], b, NEG)
    return jnp.asarray(np.stack([b, b_first]).astype(np.float32))


def _mixer(x2, mod3, sinks, ln1pre, ln1post, ln2pre, anorm, hnorm, lb, win, wout, wr_hi, wr_lo, br,
           bsz, seq):
    bias = _attn_bias()
    n = bsz * seq
    nt = seq // TQ
    n_tiles = bsz * nt
    n_sc = (TQ // WINDOW) * ATTN_HEADS
    cur = lambda s: jnp.minimum(s, n_tiles - 1)
    const = lambda s: (0, 0)
    full = lambda a: pl.BlockSpec(a.shape, const)
    return pl.pallas_call(
        functools.partial(_mixer_kernel, tiles_per_seq=nt),
        grid=(n_tiles + 1,),
        in_specs=[pl.BlockSpec(memory_space=pltpu.SMEM),
                  pl.BlockSpec((TQ, D_MODEL), lambda s: (cur(s), 0)),
                  pl.BlockSpec((1, 6, D_MODEL), lambda s: (cur(s) // nt, 0, 0)),
                  full(ln1pre), full(ln1post), full(ln2pre), full(anorm), full(hnorm), full(lb),
                  full(win), full(wout), full(wr_hi), full(wr_lo), full(br),
                  pl.BlockSpec(bias.shape, lambda s: (0, 0, 0, 0))],
        out_specs=[pl.BlockSpec((TQ, 1, ROW_W), lambda s: (jnp.maximum(s - 1, 0), 0, 0)),
                   pl.BlockSpec((1, 8, TQ), lambda s: (jnp.maximum(s - 1, 0), 0, 0)),
                   pl.BlockSpec((LANES, LANES), const)],
        out_shape=[jax.ShapeDtypeStruct((n, 1, ROW_W), U32),
                   jax.ShapeDtypeStruct((n // TQ, 8, TQ), I32),
                   jax.ShapeDtypeStruct((LANES, LANES), F32)],
        scratch_shapes=[pltpu.VMEM((WINDOW, KV_W), F32),
                        pltpu.VMEM((WINDOW, KV_W), F32),
                        pltpu.VMEM((HG_DIM, HG_W), F32),
                        pltpu.VMEM((LANES, 1), F32),
                        pltpu.VMEM((n_sc, WINDOW, 2 * WINDOW), F32),
                        pltpu.VMEM((n_sc, WINDOW, 1), F32),
                        pltpu.VMEM((n_sc, WINDOW, 2 * WINDOW), BF16),
                        pltpu.VMEM((TQ // HG_CHUNK, HG_DIM, HG_W), F32),
                        pltpu.VMEM((TQ // HG_CHUNK, HG_DIM, HG_W), BF16),
                        pltpu.VMEM((TQ, D_MODEL), F32),
                        pltpu.VMEM((TQ, IN_W), F32),
                        pltpu.VMEM((TQ, OFF_EXT), U32)],
        compiler_params=pltpu.CompilerParams(dimension_semantics=("arbitrary",),
                                             vmem_limit_bytes=VMEM_LIMIT),
        name="mixer",
    )(sinks, x2, mod3, ln1pre, ln1post, ln2pre, anorm, hnorm, lb, win, wout, wr_hi, wr_lo, br, bias)


PERM_STEPS = 8
PERM_ROWS = 8
PERM_UNROLL = 8


def _perm_kernel(rs_ref, cnt_ref, bucket_ref, rank_ref, perm_ref, pos_vmem, pos_smem, sem):
    pid = pl.program_id(0)
    rows, cols = pos_vmem.shape

    b = bucket_ref[0]
    start = jnp.zeros_like(b)
    for k in range(N_BUCKETS):
        start = jnp.where(b == k, rs_ref[k], start)
    pos_vmem[...] = start + rank_ref[0]
    copies = [pltpu.make_async_copy(pos_vmem.at[r], pos_smem.at[pl.ds(r * cols, cols)], sem) for r in range(rows)]
    for cp in copies:
        cp.start()

    @pl.when(pid == 0)
    def _():
        def per_bucket(k, carry):
            first = rs_ref[k]
            cnt = cnt_ref[k]

            def pad(r, c2):
                perm_ref[first + r] = 0
                return c2

            lax.fori_loop(cnt, ((cnt + TM - 1) // TM) * TM, pad, 0)
            return carry

        lax.fori_loop(0, N_BUCKETS, per_bucket, 0)

        def tail(r, carry):
            perm_ref[r] = 0
            return carry

        lax.fori_loop(rs_ref[N_BUCKETS], perm_ref.shape[0], tail, 0)

    for cp in copies:
        cp.wait()
    base = pid * (rows * cols)

    def body(j, carry):
        i0 = j * PERM_UNROLL
        for u in range(PERM_UNROLL):
            perm_ref[pos_smem[i0 + u]] = base + i0 + u
        return carry

    lax.fori_loop(0, rows * cols // PERM_UNROLL, body, 0)


def _perm(row_start, counts, bucket, rank, n_rows):
    n = bucket.shape[0]
    cols = n // (PERM_STEPS * PERM_ROWS)
    assert n % (PERM_STEPS * PERM_ROWS * PERM_UNROLL) == 0
    chunked = lambda a: a.reshape(PERM_STEPS, PERM_ROWS, cols)
    chunk_spec = pl.BlockSpec((1, PERM_ROWS, cols), lambda i: (i, 0, 0))
    return pl.pallas_call(
        _perm_kernel,
        grid=(PERM_STEPS,),
        in_specs=[pl.BlockSpec(memory_space=pltpu.SMEM),
                  pl.BlockSpec(memory_space=pltpu.SMEM),
                  chunk_spec, chunk_spec],
        out_specs=pl.BlockSpec(memory_space=pltpu.SMEM),
        out_shape=jax.ShapeDtypeStruct((n_rows,), I32),
        scratch_shapes=[pltpu.VMEM((PERM_ROWS, cols), I32), pltpu.SMEM((PERM_ROWS * cols,), I32),
                        pltpu.SemaphoreType.DMA(())],
        compiler_params=pltpu.CompilerParams(dimension_semantics=("arbitrary",)),
        name="perm",
    )(row_start, counts, chunked(bucket), chunked(rank))


GATHER_DEPTH = 3
SCATTER_DMA_PRIORITY = 1


def _moe_kernel(nt_ref, nv_ref, ea_ref, eb_ref, perm_ref,
                rows_hbm, gtab_hi_ref, gtab_lo_ref, ln2post_ref, wgu_a, wd_a, wgu_b, wd_b,
                out_hbm, xbuf, x2d, obuf, gsem, ssem):
    i = pl.program_id(0)
    nt = nt_ref[0]
    last_tile = pl.num_programs(0) - 1
    nbuf = xbuf.shape[0] // TM

    def start_gather(tile, pred):
        sl = lax.rem(tile, nbuf)
        base = jnp.minimum(tile, last_tile) * TM
        for r in range(TM):
            @pl.when(pred)
            def _():
                tok = perm_ref[base + r]
                pltpu.make_async_copy(rows_hbm.at[tok], xbuf.at[sl * TM + r], gsem.at[sl]).start()

    def wait_gather(tile):
        sl = lax.rem(tile, nbuf)
        pltpu.make_async_copy(rows_hbm.at[pl.ds(0, TM)], xbuf.at[pl.ds(sl * TM, TM)], gsem.at[sl]).wait()

    def wait_scatter(sl, nv):
        @pl.when(nv == TM)
        def _():
            pltpu.make_async_copy(obuf.at[sl], out_hbm.at[pl.ds(0, TM)], ssem.at[sl]).wait()

        @pl.when(nv < TM)
        def _():
            def one(r, carry):
                pltpu.make_async_copy(obuf.at[sl, pl.ds(0, 1)], out_hbm.at[pl.ds(0, 1)], ssem.at[sl]).wait()
                return carry

            lax.fori_loop(0, nv, one, 0)

    def compute(xb):
        x1 = pltpu.bitcast(xb[:, 0:D_MODEL], F32)
        hp = xb[:, OFF_H2P:OFF_EXT]
        h2a = pltpu.bitcast(hp << 16, F32).astype(BF16)
        h2b = pltpu.bitcast(hp & jnp.uint32(0xFFFF0000), F32).astype(BF16)
        ext = pltpu.bitcast(xb[:, OFF_EXT:ROW_W], F32)
        w_lo, w_hi = ext[:, 0:1], ext[:, 1:2]
        sel = ext.astype(BF16)
        ga2 = _dot(sel, gtab_hi_ref[...]) + _dot(sel, gtab_lo_ref[...])

        def expert(wgu_ref, wd_ref):
            gu = _dot(h2a, wgu_ref[0, 0:H2P_W]) + _dot(h2b, wgu_ref[0, H2P_W:D_MODEL])
            hg, hu = gu[:, 0:FF], gu[:, FF:2 * FF]
            act = (hg * _sigmoid(hg)) * hu
            return _dot(act.astype(BF16), wd_ref[0])

        y = w_lo * expert(wgu_a, wd_a) + w_hi * expert(wgu_b, wd_b)
        return x1 + ga2 * (_rms(y) * ln2post_ref[...])

    @pl.when(i == 0)
    def _():
        for d in range(GATHER_DEPTH):
            start_gather(d, d < nt)

    @pl.when(i < nt)
    def _():
        nv = nv_ref[i]
        osl = lax.rem(i, 2)
        wait_gather(i)

        @pl.when(i >= 2)
        def _():
            wait_scatter(osl, nv_ref[jnp.maximum(i - 2, 0)])

        start_gather(i + GATHER_DEPTH, i + GATHER_DEPTH < nt)
        x2d[...] = xbuf[pl.ds(lax.rem(i, nbuf) * TM, TM)].reshape(TM, ROW_W)
        result = compute(x2d[...])

        def scatter_row(k, r):
            tok = perm_ref[i * TM + r]
            pltpu.make_async_copy(obuf.at[k, pl.ds(r, 1)], out_hbm.at[pl.ds(tok, 1)],
                                  ssem.at[k]).start(priority=SCATTER_DMA_PRIORITY)

        for k in range(2):
            @pl.when((osl == k) & (nv == TM))
            def _():
                obuf[k] = result
                for r in range(TM):
                    scatter_row(k, r)

            @pl.when((osl == k) & (nv < TM))
            def _():
                obuf[k] = result
                for r in range(TM):
                    pl.when(r < nv)(functools.partial(scatter_row, k, r))

        @pl.when(i == nt - 1)
        def _():
            wait_scatter(osl, nv)

            @pl.when(i >= 1)
            def _():
                wait_scatter(1 - osl, nv_ref[jnp.maximum(i - 1, 0)])


def _moe(rows, gtab_hi, gtab_lo, ln2post, wgu, wd, nt, nv, ea, eb, perm, n_tiles):
    n = rows.shape[0]
    const2 = lambda i, *_: (0, 0)
    grid_spec = pltpu.PrefetchScalarGridSpec(
        num_scalar_prefetch=5,
        grid=(n_tiles,),
        in_specs=[pl.BlockSpec(memory_space=pl.ANY),
                  pl.BlockSpec(gtab_hi.shape, const2),
                  pl.BlockSpec(gtab_lo.shape, const2),
                  pl.BlockSpec(ln2post.shape, const2),
                  pl.BlockSpec((1, D_MODEL, 2 * FF), lambda i, nt, nv, ea, eb, perm: (ea[i], 0, 0)),
                  pl.BlockSpec((1, FF, D_MODEL), lambda i, nt, nv, ea, eb, perm: (ea[i], 0, 0)),
                  pl.BlockSpec((1, D_MODEL, 2 * FF), lambda i, nt, nv, ea, eb, perm: (eb[i], 0, 0)),
                  pl.BlockSpec((1, FF, D_MODEL), lambda i, nt, nv, ea, eb, perm: (eb[i], 0, 0))],
        out_specs=pl.BlockSpec(memory_space=pl.ANY),
        scratch_shapes=[pltpu.VMEM(((GATHER_DEPTH + 1) * TM, 1, ROW_W), U32),
                        pltpu.VMEM((TM, ROW_W), U32),
                        pltpu.VMEM((2, TM, D_MODEL), F32),
                        pltpu.SemaphoreType.DMA((GATHER_DEPTH + 1,)),
                        pltpu.SemaphoreType.DMA((2,))],
    )
    return pl.pallas_call(
        _moe_kernel,
        grid_spec=grid_spec,
        out_shape=jax.ShapeDtypeStruct((n, D_MODEL), F32),
        compiler_params=pltpu.CompilerParams(dimension_semantics=("arbitrary",),
                                             vmem_limit_bytes=VMEM_LIMIT),
        name="moe",
    )(nt, nv, ea, eb, perm, rows, gtab_hi, gtab_lo, ln2post, wgu, wd, wgu, wd)


def kernel(x, c, ln1_pre, ln1_post, ln2_pre, ln2_post, w_ada, b_ada, w_in, attn_sinks, attn_out_norm,
           hgrn_lb, hgrn_out_norm, w_out, w_router_group, b_router_group, w_router_expert,
           b_router_expert, w_exp_gate, w_exp_up, w_exp_down):
    bsz, seq, d = x.shape
    assert d == D_MODEL and seq % TQ == 0 and w_ada.shape[0] == 1 and hgrn_lb.shape[0] == 2
    n = bsz * seq

    mod = _ada(c, w_ada[0], b_ada[0])
    mod3 = mod.reshape(bsz, 6, d)

    wr = jnp.concatenate([w_router_group[0], jnp.zeros((d, 8 - N_GROUPS), F32), w_router_expert[0],
                          jnp.zeros((d, LANES - ROUTER_ROWS), F32)], axis=1)
    br = jnp.concatenate([b_router_group[0], jnp.full((8 - N_GROUPS,), NEG, F32), b_router_expert[0]])
    wr_hi = wr.astype(BF16)
    wr_lo = (wr - wr_hi.astype(F32)).astype(BF16)

    x1ext, info, cnt = _mixer(
        x.reshape(n, d), mod3, attn_sinks[0], ln1_pre, ln1_post, ln2_pre, attn_out_norm, hgrn_out_norm,
        hgrn_lb, w_in[0].astype(BF16), w_out[0].astype(BF16), wr_hi, wr_lo, br.reshape(ROUTER_ROWS, 1),
        bsz, seq)

    n_tiles = n // TM + N_BUCKETS
    counts = cnt[:N_BUCKETS, 0].astype(I32)
    tiles_per = (counts + TM - 1) // TM
    tile_end = jnp.cumsum(tiles_per)
    tile_start = tile_end - tiles_per
    nt = tile_end[-1]
    bucket = info[:, 0, :].reshape(n)
    rank = info[:, 1, :].reshape(n)
    tid = jnp.arange(n_tiles, dtype=I32)[None, :]
    member = (tid >= tile_start[:, None]) & (tid < tile_end[:, None])
    pick = lambda per_bucket: jnp.sum(jnp.where(member, per_bucket, 0), axis=0).astype(I32)
    bidx = np.arange(N_BUCKETS, dtype=np.int32)
    ea_of = jnp.asarray((bidx // N_PAIRS) * EPG + _PAIR_A[bidx % N_PAIRS])[:, None]
    eb_of = jnp.asarray((bidx // N_PAIRS) * EPG + _PAIR_B[bidx % N_PAIRS])[:, None]
    last_used = jnp.arange(N_BUCKETS)[:, None] == jnp.max(jnp.where(tiles_per > 0, jnp.arange(N_BUCKETS), 0))
    unused = tid[0] >= nt
    nv = pick(jnp.clip(counts[:, None] - (tid - tile_start[:, None]) * TM, 0, TM))
    ea = jnp.where(unused, jnp.sum(jnp.where(last_used, ea_of, 0)), pick(ea_of)).astype(I32)
    eb = jnp.where(unused, jnp.sum(jnp.where(last_used, eb_of, 0)), pick(eb_of)).astype(I32)

    pad128 = lambda a: jnp.concatenate([a, jnp.zeros((LANES - a.shape[0],), I32)])
    row_start = jnp.concatenate([tile_start, nt.reshape(1)]) * TM
    perm = _perm(pad128(row_start), pad128(counts), bucket, rank, n_tiles * TM)

    wgu = jnp.concatenate([w_exp_gate[0], w_exp_up[0]], axis=-1).astype(BF16)
    wd = w_exp_down[0].astype(BF16)
    gtab = jnp.zeros((LANES, d), F32).at[EXT_BATCH0:EXT_BATCH0 + bsz].set(mod3[:, 5, :])
    gtab_hi = gtab.astype(BF16)
    gtab_lo = (gtab - gtab_hi.astype(F32)).astype(BF16)
    out = _moe(x1ext, gtab_hi, gtab_lo, ln2_post, wgu, wd, nt.reshape(1), nv, ea, eb, perm, n_tiles)
    return out.reshape(bsz, seq, d)
```

```python
import functools

import numpy as np
import jax
import jax.numpy as jnp
from jax import lax
from jax.experimental import pallas as pl
from jax.experimental.pallas import tpu as pltpu

F32 = jnp.float32
BF16 = jnp.bfloat16
I32 = jnp.int32

D_MODEL = 1024
ATTN_HEADS = 8
HEAD_DIM = 64
WINDOW = 128
ATTN_W = 512
KV_W = 128
HG_HEADS = 4
HG_DIM = 128
HG_W = 512
HG_CHUNK = 32
IN_W = 2816
N_GROUPS = 4
EPG = 8
N_EXPERTS = 32
FF = 256
N_PAIRS = EPG * (EPG - 1) // 2
N_BUCKETS = N_GROUPS * N_PAIRS
EPS = 1e-6
NEG = -1e30
LOG2E = 1.4426950408889634

LANES = 128
H2P_W = D_MODEL // 2
EXT_W = LANES
OFF_H2P = D_MODEL
OFF_EXT = D_MODEL + H2P_W
ROW_W = OFF_EXT + EXT_W
EXT_BATCH0 = 8
U32 = jnp.uint32
ROUTER_ROWS = 8 + N_EXPERTS

TQ = 256
TM = 128
VMEM_LIMIT = 56 * 1024 * 1024

OFF_Q, OFF_K, OFF_V, OFF_HQ, OFF_HF, OFF_HI, OFF_HG = 0, 512, 640, 768, 1280, 1792, 2304

_PAIR_A = np.array([a for a in range(EPG) for b in range(a + 1, EPG)], np.int32)
_PAIR_B = np.array([b for a in range(EPG) for b in range(a + 1, EPG)], np.int32)


def _dot(a, b):
    return jnp.dot(a, b, preferred_element_type=F32)


def _dot_nt(a, b):
    return lax.dot_general(a, b, (((1,), (1,)), ((), ())), preferred_element_type=F32)


def _dot_tn(a, b):
    return lax.dot_general(a, b, (((0,), (0,)), ((), ())), preferred_element_type=F32)


def _split(a):
    hi = a.astype(BF16)
    lo = (a - hi.astype(F32)).astype(BF16)
    return hi, lo


def _rms(x):
    return x * lax.rsqrt(jnp.mean(x * x, axis=-1, keepdims=True) + EPS)


def _sigmoid(x):
    return 0.5 * jnp.tanh(0.5 * x) + 0.5


def _ada_kernel(c_ref, w_ref, b_ref, o_ref):
    c = c_ref[...]
    ca = c * _sigmoid(c)
    c_hi, c_lo = _split(ca)
    w_hi, w_lo = _split(w_ref[...])
    o_ref[...] = _dot(c_hi, w_hi) + _dot(c_lo, w_hi) + _dot(c_hi, w_lo) + b_ref[...]


def _ada(c, w, b):
    bsz, d = c.shape
    n_out = w.shape[1]
    return pl.pallas_call(
        _ada_kernel,
        grid=(n_out // d,),
        in_specs=[pl.BlockSpec((bsz, d), lambda j: (0, 0)),
                  pl.BlockSpec((d, d), lambda j: (0, j)),
                  pl.BlockSpec((1, d), lambda j: (0, j))],
        out_specs=pl.BlockSpec((bsz, d), lambda j: (0, j)),
        out_shape=jax.ShapeDtypeStruct((bsz, n_out), F32),
        compiler_params=pltpu.CompilerParams(dimension_semantics=("arbitrary",),
                                             vmem_limit_bytes=VMEM_LIMIT),
        name="adaln",
    )(c, w, b.reshape(1, n_out))


def _attention(proj, kprev_ref, vprev_ref, sinks_ref, bias_ref, t, s_scr, m_scr, p_scr):
    tq = proj.shape[0]
    q = (proj[:, OFF_Q:OFF_Q + ATTN_W] * (HEAD_DIM ** -0.5 * LOG2E)).astype(BF16)
    kf = jnp.concatenate([kprev_ref[...], proj[:, OFF_K:OFF_K + KV_W]], axis=0)
    vf = jnp.concatenate([vprev_ref[...], proj[:, OFF_V:OFF_V + KV_W]], axis=0)
    kprev_ref[...] = proj[tq - WINDOW:, OFF_K:OFF_K + KV_W]
    vprev_ref[...] = proj[tq - WINDOW:, OFF_V:OFF_V + KV_W]

    lo = lax.broadcasted_iota(I32, kf.shape, 1) < HEAD_DIM
    kr = pltpu.roll(kf, HEAD_DIM, axis=1)
    vr = pltpu.roll(vf, HEAD_DIM, axis=1)

    def variants(a, ar):
        return [[jnp.where(lo, a, 0.0).astype(BF16), jnp.where(lo, 0.0, ar).astype(BF16)],
                [jnp.where(lo, ar, 0.0).astype(BF16), jnp.where(lo, 0.0, a).astype(BF16)]]

    kvar = variants(kf, kr)
    vvar = variants(vf, vr)

    first = jnp.where(t > 0, 0, 1)

    nblk = tq // WINDOW
    idx = lambda j, h: j * ATTN_HEADS + h
    keys = lambda a, j: a[j * WINDOW:(j + 2) * WINDOW]

    def scores():
        for j in range(nblk):
            for h in range(ATTN_HEADS):
                p, par = h // 2, h % 2
                qp = q[j * WINDOW:(j + 1) * WINDOW, p * LANES:(p + 1) * LANES]
                s = _dot_nt(qp, keys(kvar[p // 2][par], j)) + bias_ref[first if j == 0 else 0, h]
                s_scr[idx(j, h)] = s
                m_scr[idx(j, h)] = jnp.maximum(jnp.max(s, axis=-1, keepdims=True), sinks_ref[h] * LOG2E)

    def exps():
        for j in range(nblk):
            for h in range(ATTN_HEADS):
                m = m_scr[idx(j, h)]
                pe = jnp.exp2(s_scr[idx(j, h)] - m)
                p_scr[idx(j, h)] = pe.astype(BF16)
                m_scr[idx(j, h)] = 1.0 / (jnp.sum(pe, axis=-1, keepdims=True) + jnp.exp2(sinks_ref[h] * LOG2E - m))

    def values():
        blocks = []
        for j in range(nblk):
            pairs = []
            for p in range(ATTN_HEADS // 2):
                acc = None
                for par in range(2):
                    h = 2 * p + par
                    o = _dot(p_scr[idx(j, h)], keys(vvar[p // 2][par], j)) * m_scr[idx(j, h)]
                    acc = o if acc is None else acc + o
                pairs.append(acc)
            blocks.append(jnp.concatenate(pairs, axis=1))
        return jnp.concatenate(blocks, axis=0)

    return scores, exps, values


def _hgrn2(proj, lb, st_ref, hnorm, u_scr, stb_scr):
    tq = proj.shape[0]
    nc = tq // HG_CHUNK
    qr = proj[:, OFF_HQ:OFF_HQ + HG_W]
    fr = proj[:, OFF_HF:OFF_HF + HG_W]
    iv = proj[:, OFF_HI:OFF_HI + HG_W]
    gr = proj[:, OFF_HG:OFF_HG + HG_W]
    qh = qr * _sigmoid(qr)
    f = lb + (1.0 - lb) * _sigmoid(fr)
    kk = 1.0 - f
    logf = jnp.log(f)

    rmod = lax.broadcasted_iota(I32, (tq, HG_W), 0) & (HG_CHUNK - 1)
    bc = logf
    s = 1
    while s < HG_CHUNK:
        bc = bc + jnp.where(rmod >= s, pltpu.roll(bc, s, axis=0), 0.0)
        s *= 2

    b3 = bc.reshape(nc, HG_CHUNK, HG_W)
    blast = b3[:, HG_CHUNK - 1:HG_CHUNK, :]
    kend = (kk.reshape(nc, HG_CHUNK, HG_W) * jnp.exp(blast - b3)).reshape(tq, HG_W)
    decay = jnp.exp(blast).reshape(nc, HG_W)
    qdec = (qh * jnp.exp(bc)).astype(BF16)
    kdec = (kk * jnp.exp(-bc)).astype(BF16)
    kend = kend.astype(BF16)
    ivb = iv.astype(BF16)

    ri = lax.broadcasted_iota(I32, (tq, tq), 0)
    ci = lax.broadcasted_iota(I32, (tq, tq), 1)
    cmask = ((ri // HG_CHUNK) == (ci // HG_CHUNK)) & (ri >= ci)

    heads = [slice(hh * HG_DIM, (hh + 1) * HG_DIM) for hh in range(HG_HEADS)]
    chunks = [slice(n * HG_CHUNK, (n + 1) * HG_CHUNK) for n in range(nc)]

    lane_head = lax.broadcasted_iota(I32, (HG_CHUNK, HG_W), 1) // HG_DIM
    for n, rs in enumerate(chunks):
        vstack = jnp.concatenate([ivb[rs, sl] for sl in heads], axis=0)
        kblk = jnp.concatenate([jnp.where(lane_head == hh, kend[rs], 0.0).astype(BF16)
                                for hh in range(HG_HEADS)], axis=0)
        u_scr[n] = _dot_tn(vstack, kblk)

    st = st_ref[...]
    for n in range(nc):
        stb_scr[n] = st.astype(BF16)
        st = st * decay[n:n + 1] + u_scr[n]
    st_ref[...] = st

    outs = []
    for hh, sl in enumerate(heads):
        a = _dot_nt(qdec[:, sl], kdec[:, sl])
        a = jnp.where(cmask, a, 0.0).astype(BF16)
        o_intra = _dot(a, ivb[:, sl])
        inter = [_dot_nt(qdec[rs, sl], stb_scr[n, :, sl]) for n, rs in enumerate(chunks)]
        o = o_intra + jnp.concatenate(inter, axis=0)
        o = _rms(o) * hnorm[:, sl]
        g = gr[:, sl]
        outs.append(o * (g * _sigmoid(g)))
    return jnp.concatenate(outs, axis=1)


def _route_topk(h2, wr_hi, wr_lo, br):
    tq = h2.shape[0]
    h_hi, h_lo = _split(h2)
    logits = _dot(h_hi, wr_hi) + _dot(h_lo, wr_hi) + _dot(h_hi, wr_lo)
    lt = logits.T[0:ROUTER_ROWS] + br
    sub = lax.broadcasted_iota(I32, (8, tq), 0).astype(F32)

    gl = lt[0:8]
    gm = jnp.max(gl, axis=0, keepdims=True)
    gidx = jnp.min(jnp.where(gl == gm, sub, 8.0), axis=0, keepdims=True)
    g_w = 1.0 / jnp.sum(jnp.exp(gl - gm), axis=0, keepdims=True)

    es = lt[8:16]
    for g in range(1, N_GROUPS):
        es = jnp.where(gidx == float(g), lt[8 + 8 * g:16 + 8 * g], es)
    m1 = jnp.max(es, axis=0, keepdims=True)
    i1 = jnp.min(jnp.where(es == m1, sub, 8.0), axis=0, keepdims=True)
    e2 = jnp.where(sub == i1, NEG, es)
    m2 = jnp.max(e2, axis=0, keepdims=True)
    i2 = jnp.min(jnp.where(e2 == m2, sub, 8.0), axis=0, keepdims=True)
    dd = jnp.exp(m2 - m1)
    w1 = g_w / (1.0 + dd)
    w2 = g_w * dd / (1.0 + dd)
    first_low = i1 < i2
    ea = jnp.minimum(i1, i2)
    eb = jnp.maximum(i1, i2)
    w_lo = jnp.where(first_low, w1, w2)
    w_hi = jnp.where(first_low, w2, w1)
    pair = ea * (15.0 - ea) * 0.5 + (eb - ea - 1.0)
    bucket = gidx * float(N_PAIRS) + pair
    return bucket, w_lo, w_hi


def _route_rank(bucket, w_lo, w_hi, carry_ref, bidx, live):
    tq = bucket.shape[1]
    brow = lax.broadcasted_iota(I32, (LANES, tq), 0).astype(F32)
    onehot = brow == bucket
    oh = jnp.where(onehot, live, 0.0)
    ti = lax.broadcasted_iota(I32, (tq, tq), 0)
    tj = lax.broadcasted_iota(I32, (tq, tq), 1)
    upper = jnp.where(ti < tj, 1.0, 0.0).astype(BF16)
    before = _dot(oh.astype(BF16), upper) + carry_ref[...]
    rank = jnp.sum(jnp.where(onehot, before, 0.0), axis=0, keepdims=True)
    carry_ref[...] = carry_ref[...] + jnp.sum(oh, axis=1, keepdims=True)

    lane_row = lax.broadcasted_iota(I32, (LANES - EXT_BATCH0, tq), 0)
    onehot_b = jnp.where(lane_row == bidx, 1.0, 0.0)
    info = jnp.concatenate([w_lo, w_hi, jnp.zeros((EXT_BATCH0 - 2, tq), F32), onehot_b], axis=0)
    return rank, info.T


def _mixer_kernel(sinks_ref, x_ref, mod_ref, ln1pre_ref, ln1post_ref, ln2pre_ref, anorm_ref, hnorm_ref,
                  lb_ref, win_ref, wout_ref, wrhi_ref, wrlo_ref, br_ref, bias_ref,
                  rows_ref, info_ref, cnt_ref,
                  kprev_ref, vprev_ref, st_ref, carry_ref, s_scr, m_scr, p_scr, u_scr, stb_scr,
                  h2_scr, proj_scr, keep_scr, *, tiles_per_seq):
    s = pl.program_id(0)
    n_tiles = pl.num_programs(0) - 1
    tq = x_ref.shape[0]
    t = lax.rem(jnp.minimum(s, n_tiles - 1), tiles_per_seq)
    bits = lambda a: pltpu.bitcast(a, U32)

    @pl.when(s == 0)
    def _():
        carry_ref[...] = jnp.zeros_like(carry_ref)
        h2_scr[...] = jnp.zeros_like(h2_scr)
        keep_scr[...] = jnp.zeros_like(keep_scr)

    @pl.when(t == 0)
    def _():
        st_ref[...] = jnp.zeros_like(st_ref)
        kprev_ref[...] = jnp.zeros_like(kprev_ref)
        vprev_ref[...] = jnp.zeros_like(vprev_ref)

    x = x_ref[...]
    mod = mod_ref[0]
    sh1, sc1, ga1, sh2, sc2 = mod[0:1], mod[1:2], mod[2:3], mod[3:4], mod[4:5]
    prev = jnp.maximum(s - 1, 0)

    @pl.when(s <= n_tiles)
    def _():
        bucket, w_lo, w_hi = _route_topk(h2_scr[...], wrhi_ref[...], wrlo_ref[...], br_ref[...])
        h = _rms(x) * ln1pre_ref[...] * (1.0 + sc1) + sh1
        proj_scr[...] = _dot(h.astype(BF16), win_ref[...])
        live = jnp.where(s >= 1, 1.0, 0.0)
        rank, ext = _route_rank(bucket, w_lo, w_hi, carry_ref, prev // tiles_per_seq, live)
        rows_ref[:, :, 0:OFF_EXT] = keep_scr[...].reshape(tq, 1, OFF_EXT)
        rows_ref[:, :, OFF_EXT:ROW_W] = bits(ext).reshape(tq, 1, EXT_W)
        info_ref[0] = jnp.concatenate([bucket, rank, jnp.zeros((6, tq), F32)], axis=0).astype(I32)
        cnt_ref[...] = jnp.broadcast_to(carry_ref[...], cnt_ref.shape)

    proj = proj_scr[...]
    scores, exps, values = _attention(proj, kprev_ref, vprev_ref, sinks_ref, bias_ref, t, s_scr, m_scr, p_scr)
    scores()
    exps()
    attn = _rms(values()) * anorm_ref[...]

    lbr = lb_ref[...]
    le = jnp.exp(lbr - jnp.max(lbr, axis=0, keepdims=True))
    lb = le[0:1] / jnp.sum(le, axis=0, keepdims=True)
    hg = _hgrn2(proj, lb, st_ref, hnorm_ref[...], u_scr, stb_scr)

    mix = _dot(jnp.concatenate([attn, hg], axis=1).astype(BF16), wout_ref[...])
    x1 = x + ga1 * (_rms(mix) * ln1post_ref[...])

    h2 = _rms(x1) * ln2pre_ref[...] * (1.0 + sc2) + sh2
    h2r = h2.astype(BF16).astype(F32)

    keep_scr[:, 0:D_MODEL] = bits(x1)
    keep_scr[:, OFF_H2P:OFF_EXT] = ((bits(h2r[:, 0:H2P_W]) >> 16)
                                    | (bits(h2r[:, H2P_W:D_MODEL]) & jnp.uint32(0xFFFF0000)))
    h2_scr[...] = h2


def _attn_bias():
    qi = np.arange(WINDOW)[:, None]
    kj = np.arange(2 * WINDOW)[None, :]
    dist = qi + WINDOW - kj
    in_win = (dist >= 0) & (dist < WINDOW)
    slopes = 2.0 ** (-8.0 * (np.arange(ATTN_HEADS) + 1.0) / ATTN_HEADS)
    b = np.where(in_win[None], -slopes[:, None, None] * dist[None] * LOG2E, NEG)
    b_first = np.where((kj >= WINDOW)[None], b, NEG)
    return jnp.asarray(np.stack([b, b_first]).astype(np.float32))


def _mixer(x2, mod3, sinks, ln1pre, ln1post, ln2pre, anorm, hnorm, lb, win, wout, wr_hi, wr_lo, br,
           bsz, seq):
    bias = _attn_bias()
    n = bsz * seq
    nt = seq // TQ
    n_tiles = bsz * nt
    n_sc = (TQ // WINDOW) * ATTN_HEADS
    cur = lambda s: jnp.minimum(s, n_tiles - 1)
    const = lambda s: (0, 0)
    full = lambda a: pl.BlockSpec(a.shape, const)
    return pl.pallas_call(
        functools.partial(_mixer_kernel, tiles_per_seq=nt),
        grid=(n_tiles + 1,),
        in_specs=[pl.BlockSpec(memory_space=pltpu.SMEM),
                  pl.BlockSpec((TQ, D_MODEL), lambda s: (cur(s), 0)),
                  pl.BlockSpec((1, 6, D_MODEL), lambda s: (cur(s) // nt, 0, 0)),
                  full(ln1pre), full(ln1post), full(ln2pre), full(anorm), full(hnorm), full(lb),
                  full(win), full(wout), full(wr_hi), full(wr_lo), full(br),
                  pl.BlockSpec(bias.shape, lambda s: (0, 0, 0, 0))],
        out_specs=[pl.BlockSpec((TQ, 1, ROW_W), lambda s: (jnp.maximum(s - 1, 0), 0, 0)),
                   pl.BlockSpec((1, 8, TQ), lambda s: (jnp.maximum(s - 1, 0), 0, 0)),
                   pl.BlockSpec((LANES, LANES), const)],
        out_shape=[jax.ShapeDtypeStruct((n, 1, ROW_W), U32),
                   jax.ShapeDtypeStruct((n // TQ, 8, TQ), I32),
                   jax.ShapeDtypeStruct((LANES, LANES), F32)],
        scratch_shapes=[pltpu.VMEM((WINDOW, KV_W), F32),
                        pltpu.VMEM((WINDOW, KV_W), F32),
                        pltpu.VMEM((HG_DIM, HG_W), F32),
                        pltpu.VMEM((LANES, 1), F32),
                        pltpu.VMEM((n_sc, WINDOW, 2 * WINDOW), F32),
                        pltpu.VMEM((n_sc, WINDOW, 1), F32),
                        pltpu.VMEM((n_sc, WINDOW, 2 * WINDOW), BF16),
                        pltpu.VMEM((TQ // HG_CHUNK, HG_DIM, HG_W), F32),
                        pltpu.VMEM((TQ // HG_CHUNK, HG_DIM, HG_W), BF16),
                        pltpu.VMEM((TQ, D_MODEL), F32),
                        pltpu.VMEM((TQ, IN_W), F32),
                        pltpu.VMEM((TQ, OFF_EXT), U32)],
        compiler_params=pltpu.CompilerParams(dimension_semantics=("arbitrary",),
                                             vmem_limit_bytes=VMEM_LIMIT),
        name="mixer",
    )(sinks, x2, mod3, ln1pre, ln1post, ln2pre, anorm, hnorm, lb, win, wout, wr_hi, wr_lo, br, bias)


PERM_STEPS = 8
PERM_ROWS = 8
PERM_UNROLL = 8


def _perm_kernel(rs_ref, cnt_ref, bucket_ref, rank_ref, perm_ref, pos_vmem, pos_smem, sem):
    pid = pl.program_id(0)
    rows, cols = pos_vmem.shape

    b = bucket_ref[0]
    start = jnp.zeros_like(b)
    for k in range(N_BUCKETS):
        start = jnp.where(b == k, rs_ref[k], start)
    pos_vmem[...] = start + rank_ref[0]
    copies = [pltpu.make_async_copy(pos_vmem.at[r], pos_smem.at[pl.ds(r * cols, cols)], sem) for r in range(rows)]
    for cp in copies:
        cp.start()

    @pl.when(pid == 0)
    def _():
        def per_bucket(k, carry):
            first = rs_ref[k]
            cnt = cnt_ref[k]

            def pad(r, c2):
                perm_ref[first + r] = 0
                return c2

            lax.fori_loop(cnt, ((cnt + TM - 1) // TM) * TM, pad, 0)
            return carry

        lax.fori_loop(0, N_BUCKETS, per_bucket, 0)

        def tail(r, carry):
            perm_ref[r] = 0
            return carry

        lax.fori_loop(rs_ref[N_BUCKETS], perm_ref.shape[0], tail, 0)

    for cp in copies:
        cp.wait()
    base = pid * (rows * cols)

    def body(j, carry):
        i0 = j * PERM_UNROLL
        for u in range(PERM_UNROLL):
            perm_ref[pos_smem[i0 + u]] = base + i0 + u
        return carry

    lax.fori_loop(0, rows * cols // PERM_UNROLL, body, 0)


def _perm(row_start, counts, bucket, rank, n_rows):
    n = bucket.shape[0]
    cols = n // (PERM_STEPS * PERM_ROWS)
    assert n % (PERM_STEPS * PERM_ROWS * PERM_UNROLL) == 0
    chunked = lambda a: a.reshape(PERM_STEPS, PERM_ROWS, cols)
    chunk_spec = pl.BlockSpec((1, PERM_ROWS, cols), lambda i: (i, 0, 0))
    return pl.pallas_call(
        _perm_kernel,
        grid=(PERM_STEPS,),
        in_specs=[pl.BlockSpec(memory_space=pltpu.SMEM),
                  pl.BlockSpec(memory_space=pltpu.SMEM),
                  chunk_spec, chunk_spec],
        out_specs=pl.BlockSpec(memory_space=pltpu.SMEM),
        out_shape=jax.ShapeDtypeStruct((n_rows,), I32),
        scratch_shapes=[pltpu.VMEM((PERM_ROWS, cols), I32), pltpu.SMEM((PERM_ROWS * cols,), I32),
                        pltpu.SemaphoreType.DMA(())],
        compiler_params=pltpu.CompilerParams(dimension_semantics=("arbitrary",)),
        name="perm",
    )(row_start, counts, chunked(bucket), chunked(rank))


GATHER_DEPTH = 3
DMA_QUEUES = 2


def _moe_kernel(nt_ref, nv_ref, ea_ref, eb_ref, perm_ref,
                rows_hbm, gtab_hi_ref, gtab_lo_ref, ln2post_ref, wgu_a, wd_a, wgu_b, wd_b,
                out_hbm, xbuf, x2d, obuf, gsem, ssem):
    i = pl.program_id(0)
    nt = nt_ref[0]
    last_tile = pl.num_programs(0) - 1
    nbuf = xbuf.shape[0] // TM

    def start_gather(tile, pred):
        sl = lax.rem(tile, nbuf)
        base = jnp.minimum(tile, last_tile) * TM
        for r in range(TM):
            @pl.when(pred)
            def _():
                tok = perm_ref[base + r]
                pltpu.make_async_copy(rows_hbm.at[tok], xbuf.at[sl * TM + r],
                                      gsem.at[sl]).start(priority=r % DMA_QUEUES)

    def wait_gather(tile):
        sl = lax.rem(tile, nbuf)
        pltpu.make_async_copy(rows_hbm.at[pl.ds(0, TM)], xbuf.at[pl.ds(sl * TM, TM)], gsem.at[sl]).wait()

    def wait_scatter(sl, nv):
        @pl.when(nv == TM)
        def _():
            pltpu.make_async_copy(obuf.at[sl], out_hbm.at[pl.ds(0, TM)], ssem.at[sl]).wait()

        @pl.when(nv < TM)
        def _():
            def one(r, carry):
                pltpu.make_async_copy(obuf.at[sl, pl.ds(0, 1)], out_hbm.at[pl.ds(0, 1)], ssem.at[sl]).wait()
                return carry

            lax.fori_loop(0, nv, one, 0)

    def compute(xb):
        x1 = pltpu.bitcast(xb[:, 0:D_MODEL], F32)
        hp = xb[:, OFF_H2P:OFF_EXT]
        h2a = pltpu.bitcast(hp << 16, F32).astype(BF16)
        h2b = pltpu.bitcast(hp & jnp.uint32(0xFFFF0000), F32).astype(BF16)
        ext = pltpu.bitcast(xb[:, OFF_EXT:ROW_W], F32)
        w_lo, w_hi = ext[:, 0:1], ext[:, 1:2]
        sel = ext.astype(BF16)
        ga2 = _dot(sel, gtab_hi_ref[...]) + _dot(sel, gtab_lo_ref[...])

        def expert(wgu_ref, wd_ref):
            gu = _dot(h2a, wgu_ref[0, 0:H2P_W]) + _dot(h2b, wgu_ref[0, H2P_W:D_MODEL])
            hg, hu = gu[:, 0:FF], gu[:, FF:2 * FF]
            act = (hg * _sigmoid(hg)) * hu
            return _dot(act.astype(BF16), wd_ref[0])

        y = w_lo * expert(wgu_a, wd_a) + w_hi * expert(wgu_b, wd_b)
        return x1 + ga2 * (_rms(y) * ln2post_ref[...])

    @pl.when(i == 0)
    def _():
        for d in range(GATHER_DEPTH):
            start_gather(d, d < nt)

    @pl.when(i < nt)
    def _():
        nv = nv_ref[i]
        osl = lax.rem(i, 2)
        wait_gather(i)

        @pl.when(i >= 2)
        def _():
            wait_scatter(osl, nv_ref[jnp.maximum(i - 2, 0)])

        start_gather(i + GATHER_DEPTH, i + GATHER_DEPTH < nt)
        x2d[...] = xbuf[pl.ds(lax.rem(i, nbuf) * TM, TM)].reshape(TM, ROW_W)
        result = compute(x2d[...])

        def scatter_row(k, r):
            tok = perm_ref[i * TM + r]
            pltpu.make_async_copy(obuf.at[k, pl.ds(r, 1)], out_hbm.at[pl.ds(tok, 1)],
                                  ssem.at[k]).start(priority=r % DMA_QUEUES)

        for k in range(2):
            @pl.when((osl == k) & (nv == TM))
            def _():
                obuf[k] = result
                for r in range(TM):
                    scatter_row(k, r)

            @pl.when((osl == k) & (nv < TM))
            def _():
                obuf[k] = result
                for r in range(TM):
                    pl.when(r < nv)(functools.partial(scatter_row, k, r))

        @pl.when(i == nt - 1)
        def _():
            wait_scatter(osl, nv)

            @pl.when(i >= 1)
            def _():
                wait_scatter(1 - osl, nv_ref[jnp.maximum(i - 1, 0)])


def _moe(rows, gtab_hi, gtab_lo, ln2post, wgu, wd, nt, nv, ea, eb, perm, n_tiles):
    n = rows.shape[0]
    const2 = lambda i, *_: (0, 0)
    grid_spec = pltpu.PrefetchScalarGridSpec(
        num_scalar_prefetch=5,
        grid=(n_tiles,),
        in_specs=[pl.BlockSpec(memory_space=pl.ANY),
                  pl.BlockSpec(gtab_hi.shape, const2),
                  pl.BlockSpec(gtab_lo.shape, const2),
                  pl.BlockSpec(ln2post.shape, const2),
                  pl.BlockSpec((1, D_MODEL, 2 * FF), lambda i, nt, nv, ea, eb, perm: (ea[i], 0, 0)),
                  pl.BlockSpec((1, FF, D_MODEL), lambda i, nt, nv, ea, eb, perm: (ea[i], 0, 0)),
                  pl.BlockSpec((1, D_MODEL, 2 * FF), lambda i, nt, nv, ea, eb, perm: (eb[i], 0, 0)),
                  pl.BlockSpec((1, FF, D_MODEL), lambda i, nt, nv, ea, eb, perm: (eb[i], 0, 0))],
        out_specs=pl.BlockSpec(memory_space=pl.ANY),
        scratch_shapes=[pltpu.VMEM(((GATHER_DEPTH + 1) * TM, 1, ROW_W), U32),
                        pltpu.VMEM((TM, ROW_W), U32),
                        pltpu.VMEM((2, TM, D_MODEL), F32),
                        pltpu.SemaphoreType.DMA((GATHER_DEPTH + 1,)),
                        pltpu.SemaphoreType.DMA((2,))],
    )
    return pl.pallas_call(
        _moe_kernel,
        grid_spec=grid_spec,
        out_shape=jax.ShapeDtypeStruct((n, D_MODEL), F32),
        compiler_params=pltpu.CompilerParams(dimension_semantics=("arbitrary",),
                                             vmem_limit_bytes=VMEM_LIMIT),
        name="moe",
    )(nt, nv, ea, eb, perm, rows, gtab_hi, gtab_lo, ln2post, wgu, wd, wgu, wd)


def kernel(x, c, ln1_pre, ln1_post, ln2_pre, ln2_post, w_ada, b_ada, w_in, attn_sinks, attn_out_norm,
           hgrn_lb, hgrn_out_norm, w_out, w_router_group, b_router_group, w_router_expert,
           b_router_expert, w_exp_gate, w_exp_up, w_exp_down):
    bsz, seq, d = x.shape
    assert d == D_MODEL and seq % TQ == 0 and w_ada.shape[0] == 1 and hgrn_lb.shape[0] == 2
    n = bsz * seq

    mod = _ada(c, w_ada[0], b_ada[0])
    mod3 = mod.reshape(bsz, 6, d)

    wr = jnp.concatenate([w_router_group[0], jnp.zeros((d, 8 - N_GROUPS), F32), w_router_expert[0],
                          jnp.zeros((d, LANES - ROUTER_ROWS), F32)], axis=1)
    br = jnp.concatenate([b_router_group[0], jnp.full((8 - N_GROUPS,), NEG, F32), b_router_expert[0]])
    wr_hi = wr.astype(BF16)
    wr_lo = (wr - wr_hi.astype(F32)).astype(BF16)

    x1ext, info, cnt = _mixer(
        x.reshape(n, d), mod3, attn_sinks[0], ln1_pre, ln1_post, ln2_pre, attn_out_norm, hgrn_out_norm,
        hgrn_lb, w_in[0].astype(BF16), w_out[0].astype(BF16), wr_hi, wr_lo, br.reshape(ROUTER_ROWS, 1),
        bsz, seq)

    n_tiles = n // TM + N_BUCKETS
    counts = cnt[:N_BUCKETS, 0].astype(I32)
    tiles_per = (counts + TM - 1) // TM
    tile_end = jnp.cumsum(tiles_per)
    tile_start = tile_end - tiles_per
    nt = tile_end[-1]
    bucket = info[:, 0, :].reshape(n)
    rank = info[:, 1, :].reshape(n)
    tid = jnp.arange(n_tiles, dtype=I32)[None, :]
    member = (tid >= tile_start[:, None]) & (tid < tile_end[:, None])
    pick = lambda per_bucket: jnp.sum(jnp.where(member, per_bucket, 0), axis=0).astype(I32)
    bidx = np.arange(N_BUCKETS, dtype=np.int32)
    ea_of = jnp.asarray((bidx // N_PAIRS) * EPG + _PAIR_A[bidx % N_PAIRS])[:, None]
    eb_of = jnp.asarray((bidx // N_PAIRS) * EPG + _PAIR_B[bidx % N_PAIRS])[:, None]
    last_used = jnp.arange(N_BUCKETS)[:, None] == jnp.max(jnp.where(tiles_per > 0, jnp.arange(N_BUCKETS), 0))
    unused = tid[0] >= nt
    nv = pick(jnp.clip(counts[:, None] - (tid - tile_start[:, None]) * TM, 0, TM))
    ea = jnp.where(unused, jnp.sum(jnp.where(last_used, ea_of, 0)), pick(ea_of)).astype(I32)
    eb = jnp.where(unused, jnp.sum(jnp.where(last_used, eb_of, 0)), pick(eb_of)).astype(I32)

    pad128 = lambda a: jnp.concatenate([a, jnp.zeros((LANES - a.shape[0],), I32)])
    row_start = jnp.concatenate([tile_start, nt.reshape(1)]) * TM
    perm = _perm(pad128(row_start), pad128(counts), bucket, rank, n_tiles * TM)

    wgu = jnp.concatenate([w_exp_gate[0], w_exp_up[0]], axis=-1).astype(BF16)
    wd = w_exp_down[0].astype(BF16)
    gtab = jnp.zeros((LANES, d), F32).at[EXT_BATCH0:EXT_BATCH0 + bsz].set(mod3[:, 5, :])
    gtab_hi = gtab.astype(BF16)
    gtab_lo = (gtab - gtab_hi.astype(F32)).astype(BF16)
    out = _moe(x1ext, gtab_hi, gtab_lo, ln2_post, wgu, wd, nt.reshape(1), nv, ea, eb, perm, n_tiles)
    return out.reshape(bsz, seq, d)
```

```python
import functools

import numpy as np
import jax
import jax.numpy as jnp
from jax import lax
from jax.experimental import pallas as pl
from jax.experimental.pallas import tpu as pltpu

F32 = jnp.float32
BF16 = jnp.bfloat16
I32 = jnp.int32

D_MODEL = 1024
ATTN_HEADS = 8
HEAD_DIM = 64
WINDOW = 128
ATTN_W = 512
KV_W = 128
HG_HEADS = 4
HG_DIM = 128
HG_W = 512
HG_CHUNK = 32
IN_W = 2816
N_GROUPS = 4
EPG = 8
N_EXPERTS = 32
FF = 256
N_PAIRS = EPG * (EPG - 1) // 2
N_BUCKETS = N_GROUPS * N_PAIRS
EPS = 1e-6
NEG = -1e30
LOG2E = 1.4426950408889634

LANES = 128
H2P_W = D_MODEL // 2
EXT_W = LANES
OFF_H2P = D_MODEL
OFF_EXT = D_MODEL + H2P_W
ROW_W = OFF_EXT + EXT_W
EXT_BATCH0 = 8
U32 = jnp.uint32
SUBLANES = 8
GROUP_ROWS = SUBLANES
ROUTER_ROWS = GROUP_ROWS + N_EXPERTS

TQ = 256
TM = 128
VMEM_LIMIT = 56 * 1024 * 1024

OFF_Q, OFF_K, OFF_V, OFF_HQ, OFF_HF, OFF_HI, OFF_HG = 0, 512, 640, 768, 1280, 1792, 2304

_PAIR_A = np.array([a for a in range(EPG) for b in range(a + 1, EPG)], np.int32)
_PAIR_B = np.array([b for a in range(EPG) for b in range(a + 1, EPG)], np.int32)


def _dot(a, b):
    return jnp.dot(a, b, preferred_element_type=F32)


def _dot_nt(a, b):
    return lax.dot_general(a, b, (((1,), (1,)), ((), ())), preferred_element_type=F32)


def _dot_tn(a, b):
    return lax.dot_general(a, b, (((0,), (0,)), ((), ())), preferred_element_type=F32)


def _split(a):
    hi = a.astype(BF16)
    lo = (a - hi.astype(F32)).astype(BF16)
    return hi, lo


def _rms(x):
    return x * lax.rsqrt(jnp.mean(x * x, axis=-1, keepdims=True) + EPS)


def _sigmoid(x):
    return 0.5 * jnp.tanh(0.5 * x) + 0.5


def _ada_kernel(c_ref, w_ref, b_ref, o_ref):
    c = c_ref[...]
    ca = c * _sigmoid(c)
    c_hi, c_lo = _split(ca)
    w_hi, w_lo = _split(w_ref[...])
    o_ref[...] = _dot(c_hi, w_hi) + _dot(c_lo, w_hi) + _dot(c_hi, w_lo) + b_ref[...]


def _ada(c, w, b):
    bsz, d = c.shape
    n_out = w.shape[1]
    return pl.pallas_call(
        _ada_kernel,
        grid=(n_out // d,),
        in_specs=[pl.BlockSpec((bsz, d), lambda j: (0, 0)),
                  pl.BlockSpec((d, d), lambda j: (0, j)),
                  pl.BlockSpec((1, d), lambda j: (0, j))],
        out_specs=pl.BlockSpec((bsz, d), lambda j: (0, j)),
        out_shape=jax.ShapeDtypeStruct((bsz, n_out), F32),
        compiler_params=pltpu.CompilerParams(dimension_semantics=("arbitrary",),
                                             vmem_limit_bytes=VMEM_LIMIT),
        name="adaln",
    )(c, w, b.reshape(1, n_out))


def _attention(proj, kprev_ref, vprev_ref, sinks_ref, bias_ref, t, s_scr, m_scr, p_scr):
    tq = proj.shape[0]
    q = (proj[:, OFF_Q:OFF_Q + ATTN_W] * (HEAD_DIM ** -0.5 * LOG2E)).astype(BF16)
    kf = jnp.concatenate([kprev_ref[...], proj[:, OFF_K:OFF_K + KV_W]], axis=0)
    vf = jnp.concatenate([vprev_ref[...], proj[:, OFF_V:OFF_V + KV_W]], axis=0)
    kprev_ref[...] = proj[tq - WINDOW:, OFF_K:OFF_K + KV_W]
    vprev_ref[...] = proj[tq - WINDOW:, OFF_V:OFF_V + KV_W]

    lo = lax.broadcasted_iota(I32, kf.shape, 1) < HEAD_DIM
    kr = pltpu.roll(kf, HEAD_DIM, axis=1)
    vr = pltpu.roll(vf, HEAD_DIM, axis=1)

    def variants(a, ar):
        return [[jnp.where(lo, a, 0.0).astype(BF16), jnp.where(lo, 0.0, ar).astype(BF16)],
                [jnp.where(lo, ar, 0.0).astype(BF16), jnp.where(lo, 0.0, a).astype(BF16)]]

    kvar = variants(kf, kr)
    vvar = variants(vf, vr)

    first = jnp.where(t > 0, 0, 1)

    nblk = tq // WINDOW
    idx = lambda j, h: j * ATTN_HEADS + h
    keys = lambda a, j: a[j * WINDOW:(j + 2) * WINDOW]

    def scores():
        for j in range(nblk):
            for h in range(ATTN_HEADS):
                p, par = h // 2, h % 2
                qp = q[j * WINDOW:(j + 1) * WINDOW, p * LANES:(p + 1) * LANES]
                s = _dot_nt(qp, keys(kvar[p // 2][par], j)) + bias_ref[first if j == 0 else 0, h]
                s_scr[idx(j, h)] = s
                m_scr[idx(j, h)] = jnp.maximum(jnp.max(s, axis=-1, keepdims=True), sinks_ref[h] * LOG2E)

    def exps():
        for j in range(nblk):
            for h in range(ATTN_HEADS):
                m = m_scr[idx(j, h)]
                pe = jnp.exp2(s_scr[idx(j, h)] - m)
                p_scr[idx(j, h)] = pe.astype(BF16)
                m_scr[idx(j, h)] = 1.0 / (jnp.sum(pe, axis=-1, keepdims=True) + jnp.exp2(sinks_ref[h] * LOG2E - m))

    def values():
        blocks = []
        for j in range(nblk):
            pairs = []
            for p in range(ATTN_HEADS // 2):
                acc = None
                for par in range(2):
                    h = 2 * p + par
                    o = _dot(p_scr[idx(j, h)], keys(vvar[p // 2][par], j)) * m_scr[idx(j, h)]
                    acc = o if acc is None else acc + o
                pairs.append(acc)
            blocks.append(jnp.concatenate(pairs, axis=1))
        return jnp.concatenate(blocks, axis=0)

    return scores, exps, values


def _hgrn2(proj, lb, st_ref, hnorm, u_scr, stb_scr):
    tq = proj.shape[0]
    nc = tq // HG_CHUNK
    qr = proj[:, OFF_HQ:OFF_HQ + HG_W]
    fr = proj[:, OFF_HF:OFF_HF + HG_W]
    iv = proj[:, OFF_HI:OFF_HI + HG_W]
    gr = proj[:, OFF_HG:OFF_HG + HG_W]
    qh = qr * _sigmoid(qr)
    f = lb + (1.0 - lb) * _sigmoid(fr)
    kk = 1.0 - f
    logf = jnp.log(f)

    rmod = lax.broadcasted_iota(I32, (tq, HG_W), 0) & (HG_CHUNK - 1)
    bc = logf
    s = 1
    while s < HG_CHUNK:
        bc = bc + jnp.where(rmod >= s, pltpu.roll(bc, s, axis=0), 0.0)
        s *= 2

    b3 = bc.reshape(nc, HG_CHUNK, HG_W)
    blast = b3[:, HG_CHUNK - 1:HG_CHUNK, :]
    kend = (kk.reshape(nc, HG_CHUNK, HG_W) * jnp.exp(blast - b3)).reshape(tq, HG_W)
    decay = jnp.exp(blast).reshape(nc, HG_W)
    qdec = (qh * jnp.exp(bc)).astype(BF16)
    kdec = (kk * jnp.exp(-bc)).astype(BF16)
    kend = kend.astype(BF16)
    ivb = iv.astype(BF16)

    ri = lax.broadcasted_iota(I32, (tq, tq), 0)
    ci = lax.broadcasted_iota(I32, (tq, tq), 1)
    cmask = ((ri // HG_CHUNK) == (ci // HG_CHUNK)) & (ri >= ci)

    heads = [slice(hh * HG_DIM, (hh + 1) * HG_DIM) for hh in range(HG_HEADS)]
    chunks = [slice(n * HG_CHUNK, (n + 1) * HG_CHUNK) for n in range(nc)]

    lane_head = lax.broadcasted_iota(I32, (HG_CHUNK, HG_W), 1) // HG_DIM
    for n, rs in enumerate(chunks):
        vstack = jnp.concatenate([ivb[rs, sl] for sl in heads], axis=0)
        kblk = jnp.concatenate([jnp.where(lane_head == hh, kend[rs], 0.0).astype(BF16)
                                for hh in range(HG_HEADS)], axis=0)
        u_scr[n] = _dot_tn(vstack, kblk)

    st = st_ref[...]
    for n in range(nc):
        stb_scr[n] = st.astype(BF16)
        st = st * decay[n:n + 1] + u_scr[n]
    st_ref[...] = st

    outs = []
    for hh, sl in enumerate(heads):
        a = _dot_nt(qdec[:, sl], kdec[:, sl])
        a = jnp.where(cmask, a, 0.0).astype(BF16)
        o_intra = _dot(a, ivb[:, sl])
        inter = [_dot_nt(qdec[rs, sl], stb_scr[n, :, sl]) for n, rs in enumerate(chunks)]
        o = o_intra + jnp.concatenate(inter, axis=0)
        o = _rms(o) * hnorm[:, sl]
        g = gr[:, sl]
        outs.append(o * (g * _sigmoid(g)))
    return jnp.concatenate(outs, axis=1)


def _route_topk(h2, wr_hi, wr_lo, br):
    tq = h2.shape[0]
    h_hi, h_lo = _split(h2)
    logits = _dot(h_hi, wr_hi) + _dot(h_lo, wr_hi) + _dot(h_hi, wr_lo)
    lt = logits.T[0:ROUTER_ROWS] + br
    sub = lax.broadcasted_iota(I32, (SUBLANES, tq), 0).astype(F32)
    none = float(SUBLANES)

    gl = lt[0:GROUP_ROWS]
    gm = jnp.max(gl, axis=0, keepdims=True)
    gidx = jnp.min(jnp.where(gl == gm, sub, none), axis=0, keepdims=True)
    g_w = 1.0 / jnp.sum(jnp.exp(gl - gm), axis=0, keepdims=True)

    group_rows = lambda g: lt[GROUP_ROWS + EPG * g:GROUP_ROWS + EPG * (g + 1)]
    es = group_rows(0)
    for g in range(1, N_GROUPS):
        es = jnp.where(gidx == float(g), group_rows(g), es)
    m1 = jnp.max(es, axis=0, keepdims=True)
    i1 = jnp.min(jnp.where(es == m1, sub, none), axis=0, keepdims=True)
    e2 = jnp.where(sub == i1, NEG, es)
    m2 = jnp.max(e2, axis=0, keepdims=True)
    i2 = jnp.min(jnp.where(e2 == m2, sub, none), axis=0, keepdims=True)
    dd = jnp.exp(m2 - m1)
    w1 = g_w / (1.0 + dd)
    w2 = g_w * dd / (1.0 + dd)
    first_low = i1 < i2
    ea = jnp.minimum(i1, i2)
    eb = jnp.maximum(i1, i2)
    w_lo = jnp.where(first_low, w1, w2)
    w_hi = jnp.where(first_low, w2, w1)
    pair = ea * (2.0 * EPG - 1.0 - ea) * 0.5 + (eb - ea - 1.0)
    bucket = gidx * float(N_PAIRS) + pair
    return bucket, w_lo, w_hi


def _route_rank(bucket, w_lo, w_hi, carry_ref, bidx, live):
    tq = bucket.shape[1]
    brow = lax.broadcasted_iota(I32, (LANES, tq), 0).astype(F32)
    onehot = brow == bucket
    oh = jnp.where(onehot, live, 0.0)
    ti = lax.broadcasted_iota(I32, (tq, tq), 0)
    tj = lax.broadcasted_iota(I32, (tq, tq), 1)
    upper = jnp.where(ti < tj, 1.0, 0.0).astype(BF16)
    before = _dot(oh.astype(BF16), upper) + carry_ref[...]
    rank = jnp.sum(jnp.where(onehot, before, 0.0), axis=0, keepdims=True)
    carry_ref[...] = carry_ref[...] + jnp.sum(oh, axis=1, keepdims=True)

    lane_row = lax.broadcasted_iota(I32, (LANES - EXT_BATCH0, tq), 0)
    onehot_b = jnp.where(lane_row == bidx, 1.0, 0.0)
    info = jnp.concatenate([w_lo, w_hi, jnp.zeros((EXT_BATCH0 - 2, tq), F32), onehot_b], axis=0)
    return rank, info.T


def _mixer_kernel(sinks_ref, x_ref, mod_ref, ln1pre_ref, ln1post_ref, ln2pre_ref, anorm_ref, hnorm_ref,
                  lb_ref, win_ref, wout_ref, wrhi_ref, wrlo_ref, br_ref, bias_ref,
                  rows_ref, info_ref, cnt_ref,
                  kprev_ref, vprev_ref, st_ref, carry_ref, s_scr, m_scr, p_scr, u_scr, stb_scr,
                  h2_scr, proj_scr, keep_scr, *, tiles_per_seq):
    s = pl.program_id(0)
    n_tiles = pl.num_programs(0) - 1
    tq = x_ref.shape[0]
    t = lax.rem(jnp.minimum(s, n_tiles - 1), tiles_per_seq)
    bits = lambda a: pltpu.bitcast(a, U32)

    @pl.when(s == 0)
    def _():
        carry_ref[...] = jnp.zeros_like(carry_ref)
        h2_scr[...] = jnp.zeros_like(h2_scr)
        keep_scr[...] = jnp.zeros_like(keep_scr)

    @pl.when(t == 0)
    def _():
        st_ref[...] = jnp.zeros_like(st_ref)
        kprev_ref[...] = jnp.zeros_like(kprev_ref)
        vprev_ref[...] = jnp.zeros_like(vprev_ref)

    x = x_ref[...]
    mod = mod_ref[0]
    sh1, sc1, ga1, sh2, sc2 = mod[0:1], mod[1:2], mod[2:3], mod[3:4], mod[4:5]
    prev = jnp.maximum(s - 1, 0)

    @pl.when(s <= n_tiles)
    def _():
        bucket, w_lo, w_hi = _route_topk(h2_scr[...], wrhi_ref[...], wrlo_ref[...], br_ref[...])
        h = _rms(x) * ln1pre_ref[...] * (1.0 + sc1) + sh1
        proj_scr[...] = _dot(h.astype(BF16), win_ref[...])
        live = jnp.where(s >= 1, 1.0, 0.0)
        rank, ext = _route_rank(bucket, w_lo, w_hi, carry_ref, prev // tiles_per_seq, live)
        rows_ref[:, :, 0:OFF_EXT] = keep_scr[...].reshape(tq, 1, OFF_EXT)
        rows_ref[:, :, OFF_EXT:ROW_W] = bits(ext).reshape(tq, 1, EXT_W)
        info_ref[0] = jnp.concatenate([bucket, rank, jnp.zeros((6, tq), F32)], axis=0).astype(I32)
        cnt_ref[...] = jnp.broadcast_to(carry_ref[...], cnt_ref.shape)

    proj = proj_scr[...]
    scores, exps, values = _attention(proj, kprev_ref, vprev_ref, sinks_ref, bias_ref, t, s_scr, m_scr, p_scr)
    scores()
    exps()
    attn = _rms(values()) * anorm_ref[...]

    lbr = lb_ref[...]
    le = jnp.exp(lbr - jnp.max(lbr, axis=0, keepdims=True))
    lb = le[0:1] / jnp.sum(le, axis=0, keepdims=True)
    hg = _hgrn2(proj, lb, st_ref, hnorm_ref[...], u_scr, stb_scr)

    mix = _dot(jnp.concatenate([attn, hg], axis=1).astype(BF16), wout_ref[...])
    x1 = x + ga1 * (_rms(mix) * ln1post_ref[...])

    h2 = _rms(x1) * ln2pre_ref[...] * (1.0 + sc2) + sh2
    h2r = h2.astype(BF16).astype(F32)

    keep_scr[:, 0:D_MODEL] = bits(x1)
    keep_scr[:, OFF_H2P:OFF_EXT] = ((bits(h2r[:, 0:H2P_W]) >> 16)
                                    | (bits(h2r[:, H2P_W:D_MODEL]) & jnp.uint32(0xFFFF0000)))
    h2_scr[...] = h2


def _attn_bias():
    qi = np.arange(WINDOW)[:, None]
    kj = np.arange(2 * WINDOW)[None, :]
    dist = qi + WINDOW - kj
    in_win = (dist >= 0) & (dist < WINDOW)
    slopes = 2.0 ** (-8.0 * (np.arange(ATTN_HEADS) + 1.0) / ATTN_HEADS)
    b = np.where(in_win[None], -slopes[:, None, None] * dist[None] * LOG2E, NEG)
    b_first = np.where((kj >= WINDOW)[None], b, NEG)
    return jnp.asarray(np.stack([b, b_first]).astype(np.float32))


def _mixer(x2, mod3, sinks, ln1pre, ln1post, ln2pre, anorm, hnorm, lb, win, wout, wr_hi, wr_lo, br,
           bsz, seq):
    bias = _attn_bias()
    n = bsz * seq
    nt = seq // TQ
    n_tiles = bsz * nt
    n_sc = (TQ // WINDOW) * ATTN_HEADS
    cur = lambda s: jnp.minimum(s, n_tiles - 1)
    const = lambda s: (0, 0)
    full = lambda a: pl.BlockSpec(a.shape, const)
    return pl.pallas_call(
        functools.partial(_mixer_kernel, tiles_per_seq=nt),
        grid=(n_tiles + 1,),
        in_specs=[pl.BlockSpec(memory_space=pltpu.SMEM),
                  pl.BlockSpec((TQ, D_MODEL), lambda s: (cur(s), 0)),
                  pl.BlockSpec((1, 6, D_MODEL), lambda s: (cur(s) // nt, 0, 0)),
                  full(ln1pre), full(ln1post), full(ln2pre), full(anorm), full(hnorm), full(lb),
                  full(win), full(wout), full(wr_hi), full(wr_lo), full(br),
                  pl.BlockSpec(bias.shape, lambda s: (0, 0, 0, 0))],
        out_specs=[pl.BlockSpec((TQ, 1, ROW_W), lambda s: (jnp.maximum(s - 1, 0), 0, 0)),
                   pl.BlockSpec((1, 8, TQ), lambda s: (jnp.maximum(s - 1, 0), 0, 0)),
                   pl.BlockSpec((LANES, LANES), const)],
        out_shape=[jax.ShapeDtypeStruct((n, 1, ROW_W), U32),
                   jax.ShapeDtypeStruct((n // TQ, 8, TQ), I32),
                   jax.ShapeDtypeStruct((LANES, LANES), F32)],
        scratch_shapes=[pltpu.VMEM((WINDOW, KV_W), F32),
                        pltpu.VMEM((WINDOW, KV_W), F32),
                        pltpu.VMEM((HG_DIM, HG_W), F32),
                        pltpu.VMEM((LANES, 1), F32),
                        pltpu.VMEM((n_sc, WINDOW, 2 * WINDOW), F32),
                        pltpu.VMEM((n_sc, WINDOW, 1), F32),
                        pltpu.VMEM((n_sc, WINDOW, 2 * WINDOW), BF16),
                        pltpu.VMEM((TQ // HG_CHUNK, HG_DIM, HG_W), F32),
                        pltpu.VMEM((TQ // HG_CHUNK, HG_DIM, HG_W), BF16),
                        pltpu.VMEM((TQ, D_MODEL), F32),
                        pltpu.VMEM((TQ, IN_W), F32),
                        pltpu.VMEM((TQ, OFF_EXT), U32)],
        compiler_params=pltpu.CompilerParams(dimension_semantics=("arbitrary",),
                                             vmem_limit_bytes=VMEM_LIMIT),
        name="mixer",
    )(sinks, x2, mod3, ln1pre, ln1post, ln2pre, anorm, hnorm, lb, win, wout, wr_hi, wr_lo, br, bias)


PERM_STEPS = 8
PERM_ROWS = 8
PERM_UNROLL = 16


def _perm_kernel(rs_ref, cnt_ref, bucket_ref, rank_ref, perm_ref, pos_vmem, pos_smem, sem):
    pid = pl.program_id(0)
    rows, cols = pos_vmem.shape

    b = bucket_ref[0]
    start = jnp.zeros_like(b)
    for k in range(N_BUCKETS):
        start = jnp.where(b == k, rs_ref[k], start)
    pos_vmem[...] = start + rank_ref[0]
    copies = [pltpu.make_async_copy(pos_vmem.at[r], pos_smem.at[pl.ds(r * cols, cols)], sem) for r in range(rows)]
    for cp in copies:
        cp.start()

    @pl.when(pid == 0)
    def _():
        def per_bucket(k, carry):
            first = rs_ref[k]
            cnt = cnt_ref[k]

            def pad(r, c2):
                perm_ref[first + r] = 0
                return c2

            lax.fori_loop(cnt, ((cnt + TM - 1) // TM) * TM, pad, 0)
            return carry

        lax.fori_loop(0, N_BUCKETS, per_bucket, 0)

        def tail(blk, carry):
            for u in range(PERM_UNROLL):
                perm_ref[blk * PERM_UNROLL + u] = 0
            return carry

        lax.fori_loop(rs_ref[N_BUCKETS] // PERM_UNROLL, perm_ref.shape[0] // PERM_UNROLL, tail, 0)

    for cp in copies:
        cp.wait()
    base = pid * (rows * cols)

    def body(j, carry):
        i0 = j * PERM_UNROLL
        positions = [pos_smem[i0 + u] for u in range(PERM_UNROLL)]
        for u in range(PERM_UNROLL):
            perm_ref[positions[u]] = base + i0 + u
        return carry

    lax.fori_loop(0, rows * cols // PERM_UNROLL, body, 0)


def _perm(row_start, counts, bucket, rank, n_rows):
    n = bucket.shape[0]
    cols = n // (PERM_STEPS * PERM_ROWS)
    assert n % (PERM_STEPS * PERM_ROWS * PERM_UNROLL) == 0
    chunked = lambda a: a.reshape(PERM_STEPS, PERM_ROWS, cols)
    chunk_spec = pl.BlockSpec((1, PERM_ROWS, cols), lambda i: (i, 0, 0))
    return pl.pallas_call(
        _perm_kernel,
        grid=(PERM_STEPS,),
        in_specs=[pl.BlockSpec(memory_space=pltpu.SMEM),
                  pl.BlockSpec(memory_space=pltpu.SMEM),
                  chunk_spec, chunk_spec],
        out_specs=pl.BlockSpec(memory_space=pltpu.SMEM),
        out_shape=jax.ShapeDtypeStruct((n_rows,), I32),
        scratch_shapes=[pltpu.VMEM((PERM_ROWS, cols), I32), pltpu.SMEM((PERM_ROWS * cols,), I32),
                        pltpu.SemaphoreType.DMA(())],
        compiler_params=pltpu.CompilerParams(dimension_semantics=("arbitrary",)),
        name="perm",
    )(row_start, counts, chunked(bucket), chunked(rank))


GATHER_DEPTH = 3


def _moe_kernel(nt_ref, nv_ref, ea_ref, eb_ref, perm_ref,
                rows_hbm, gtab_hi_ref, gtab_lo_ref, ln2post_ref, wgu_a, wd_a, wgu_b, wd_b,
                out_hbm, xbuf, x2d, obuf, gsem, ssem):
    i = pl.program_id(0)
    nt = nt_ref[0]
    last_tile = pl.num_programs(0) - 1
    nbuf = xbuf.shape[0] // TM

    def start_gather(tile, pred):
        sl = lax.rem(tile, nbuf)
        base = jnp.minimum(tile, last_tile) * TM
        for r in range(TM):
            @pl.when(pred)
            def _():
                tok = perm_ref[base + r]
                pltpu.make_async_copy(rows_hbm.at[tok], xbuf.at[sl * TM + r], gsem.at[sl]).start()

    def wait_gather(tile):
        sl = lax.rem(tile, nbuf)
        pltpu.make_async_copy(rows_hbm.at[pl.ds(0, TM)], xbuf.at[pl.ds(sl * TM, TM)], gsem.at[sl]).wait()

    def wait_scatter(sl, nv):
        @pl.when(nv == TM)
        def _():
            pltpu.make_async_copy(obuf.at[sl], out_hbm.at[pl.ds(0, TM)], ssem.at[sl]).wait()

        @pl.when(nv < TM)
        def _():
            def one(r, carry):
                pltpu.make_async_copy(obuf.at[sl, pl.ds(0, 1)], out_hbm.at[pl.ds(0, 1)], ssem.at[sl]).wait()
                return carry

            lax.fori_loop(0, nv, one, 0)

    def compute(xb):
        x1 = pltpu.bitcast(xb[:, 0:D_MODEL], F32)
        hp = xb[:, OFF_H2P:OFF_EXT]
        h2a = pltpu.bitcast(hp << 16, F32).astype(BF16)
        h2b = pltpu.bitcast(hp & jnp.uint32(0xFFFF0000), F32).astype(BF16)
        ext = pltpu.bitcast(xb[:, OFF_EXT:ROW_W], F32)
        w_lo, w_hi = ext[:, 0:1], ext[:, 1:2]
        sel = ext.astype(BF16)
        ga2 = _dot(sel, gtab_hi_ref[...]) + _dot(sel, gtab_lo_ref[...])

        def expert(wgu_ref, wd_ref):
            gu = _dot(h2a, wgu_ref[0, 0:H2P_W]) + _dot(h2b, wgu_ref[0, H2P_W:D_MODEL])
            hg, hu = gu[:, 0:FF], gu[:, FF:2 * FF]
            act = (hg * _sigmoid(hg)) * hu
            return _dot(act.astype(BF16), wd_ref[0])

        y = w_lo * expert(wgu_a, wd_a) + w_hi * expert(wgu_b, wd_b)
        return x1 + ga2 * (_rms(y) * ln2post_ref[...])

    @pl.when(i == 0)
    def _():
        for d in range(GATHER_DEPTH):
            start_gather(d, d < nt)

    @pl.when(i < nt)
    def _():
        nv = nv_ref[i]
        osl = lax.rem(i, 2)
        wait_gather(i)

        @pl.when(i >= 2)
        def _():
            wait_scatter(osl, nv_ref[jnp.maximum(i - 2, 0)])

        start_gather(i + GATHER_DEPTH, i + GATHER_DEPTH < nt)
        x2d[...] = xbuf[pl.ds(lax.rem(i, nbuf) * TM, TM)].reshape(TM, ROW_W)
        result = compute(x2d[...])

        def scatter_row(k, r):
            tok = perm_ref[i * TM + r]
            pltpu.make_async_copy(obuf.at[k, pl.ds(r, 1)], out_hbm.at[pl.ds(tok, 1)], ssem.at[k]).start()

        for k in range(2):
            @pl.when((osl == k) & (nv == TM))
            def _():
                obuf[k] = result
                for r in range(TM):
                    scatter_row(k, r)

            @pl.when((osl == k) & (nv < TM))
            def _():
                obuf[k] = result
                for r in range(TM):
                    pl.when(r < nv)(functools.partial(scatter_row, k, r))

        @pl.when(i == nt - 1)
        def _():
            wait_scatter(osl, nv)

            @pl.when(i >= 1)
            def _():
                wait_scatter(1 - osl, nv_ref[jnp.maximum(i - 1, 0)])


def _moe(rows, gtab_hi, gtab_lo, ln2post, wgu, wd, nt, nv, ea, eb, perm, n_tiles):
    n = rows.shape[0]
    const2 = lambda i, *_: (0, 0)
    grid_spec = pltpu.PrefetchScalarGridSpec(
        num_scalar_prefetch=5,
        grid=(n_tiles,),
        in_specs=[pl.BlockSpec(memory_space=pl.ANY),
                  pl.BlockSpec(gtab_hi.shape, const2),
                  pl.BlockSpec(gtab_lo.shape, const2),
                  pl.BlockSpec(ln2post.shape, const2),
                  pl.BlockSpec((1, D_MODEL, 2 * FF), lambda i, nt, nv, ea, eb, perm: (ea[i], 0, 0)),
                  pl.BlockSpec((1, FF, D_MODEL), lambda i, nt, nv, ea, eb, perm: (ea[i], 0, 0)),
                  pl.BlockSpec((1, D_MODEL, 2 * FF), lambda i, nt, nv, ea, eb, perm: (eb[i], 0, 0)),
                  pl.BlockSpec((1, FF, D_MODEL), lambda i, nt, nv, ea, eb, perm: (eb[i], 0, 0))],
        out_specs=pl.BlockSpec(memory_space=pl.ANY),
        scratch_shapes=[pltpu.VMEM(((GATHER_DEPTH + 1) * TM, 1, ROW_W), U32),
                        pltpu.VMEM((TM, ROW_W), U32),
                        pltpu.VMEM((2, TM, D_MODEL), F32),
                        pltpu.SemaphoreType.DMA((GATHER_DEPTH + 1,)),
                        pltpu.SemaphoreType.DMA((2,))],
    )
    return pl.pallas_call(
        _moe_kernel,
        grid_spec=grid_spec,
        out_shape=jax.ShapeDtypeStruct((n, D_MODEL), F32),
        compiler_params=pltpu.CompilerParams(dimension_semantics=("arbitrary",),
                                             vmem_limit_bytes=VMEM_LIMIT),
        name="moe",
    )(nt, nv, ea, eb, perm, rows, gtab_hi, gtab_lo, ln2post, wgu, wd, wgu, wd)


def kernel(x, c, ln1_pre, ln1_post, ln2_pre, ln2_post, w_ada, b_ada, w_in, attn_sinks, attn_out_norm,
           hgrn_lb, hgrn_out_norm, w_out, w_router_group, b_router_group, w_router_expert,
           b_router_expert, w_exp_gate, w_exp_up, w_exp_down):
    bsz, seq, d = x.shape
    assert d == D_MODEL and seq % TQ == 0 and w_ada.shape[0] == 1 and hgrn_lb.shape[0] == 2
    assert bsz <= LANES - EXT_BATCH0 and (bsz * seq) % TM == 0
    n = bsz * seq

    mod = _ada(c, w_ada[0], b_ada[0])
    mod3 = mod.reshape(bsz, 6, d)

    wr = jnp.concatenate([w_router_group[0], jnp.zeros((d, GROUP_ROWS - N_GROUPS), F32), w_router_expert[0],
                          jnp.zeros((d, LANES - ROUTER_ROWS), F32)], axis=1)
    br = jnp.concatenate([b_router_group[0], jnp.full((GROUP_ROWS - N_GROUPS,), NEG, F32), b_router_expert[0]])
    wr_hi = wr.astype(BF16)
    wr_lo = (wr - wr_hi.astype(F32)).astype(BF16)

    x1ext, info, cnt = _mixer(
        x.reshape(n, d), mod3, attn_sinks[0], ln1_pre, ln1_post, ln2_pre, attn_out_norm, hgrn_out_norm,
        hgrn_lb, w_in[0].astype(BF16), w_out[0].astype(BF16), wr_hi, wr_lo, br.reshape(ROUTER_ROWS, 1),
        bsz, seq)

    n_tiles = n // TM + N_BUCKETS
    counts = cnt[:N_BUCKETS, 0].astype(I32)
    tiles_per = (counts + TM - 1) // TM
    tile_end = jnp.cumsum(tiles_per)
    tile_start = tile_end - tiles_per
    nt = tile_end[-1]
    bucket = info[:, 0, :].reshape(n)
    rank = info[:, 1, :].reshape(n)
    tid = jnp.arange(n_tiles, dtype=I32)[None, :]
    member = (tid >= tile_start[:, None]) & (tid < tile_end[:, None])
    pick = lambda per_bucket: jnp.sum(jnp.where(member, per_bucket, 0), axis=0).astype(I32)
    bidx = np.arange(N_BUCKETS, dtype=np.int32)
    ea_of = jnp.asarray((bidx // N_PAIRS) * EPG + _PAIR_A[bidx % N_PAIRS])[:, None]
    eb_of = jnp.asarray((bidx // N_PAIRS) * EPG + _PAIR_B[bidx % N_PAIRS])[:, None]
    last_used = jnp.arange(N_BUCKETS)[:, None] == jnp.max(jnp.where(tiles_per > 0, jnp.arange(N_BUCKETS), 0))
    unused = tid[0] >= nt
    nv = pick(jnp.clip(counts[:, None] - (tid - tile_start[:, None]) * TM, 0, TM))
    ea = jnp.where(unused, jnp.sum(jnp.where(last_used, ea_of, 0)), pick(ea_of)).astype(I32)
    eb = jnp.where(unused, jnp.sum(jnp.where(last_used, eb_of, 0)), pick(eb_of)).astype(I32)

    pad128 = lambda a: jnp.concatenate([a, jnp.zeros((LANES - a.shape[0],), I32)])
    row_start = jnp.concatenate([tile_start, nt.reshape(1)]) * TM
    perm = _perm(pad128(row_start), pad128(counts), bucket, rank, n_tiles * TM)

    wgu = jnp.concatenate([w_exp_gate[0], w_exp_up[0]], axis=-1).astype(BF16)
    wd = w_exp_down[0].astype(BF16)
    gtab = jnp.zeros((LANES, d), F32).at[EXT_BATCH0:EXT_BATCH0 + bsz].set(mod3[:, 5, :])
    gtab_hi = gtab.astype(BF16)
    gtab_lo = (gtab - gtab_hi.astype(F32)).astype(BF16)
    out = _moe(x1ext, gtab_hi, gtab_lo, ln2_post, wgu, wd, nt.reshape(1), nv, ea, eb, perm, n_tiles)
    return out.reshape(bsz, seq, d)
```

```python
import functools

import numpy as np
import jax
import jax.numpy as jnp
from jax import lax
from jax.experimental import pallas as pl
from jax.experimental.pallas import tpu as pltpu

F32 = jnp.float32
BF16 = jnp.bfloat16
I32 = jnp.int32

D_MODEL = 1024
ATTN_HEADS = 8
HEAD_DIM = 64
WINDOW = 128
ATTN_W = 512
KV_W = 128
HG_HEADS = 4
HG_DIM = 128
HG_W = 512
HG_CHUNK = 32
IN_W = 2816
N_GROUPS = 4
EPG = 8
N_EXPERTS = 32
FF = 256
N_PAIRS = EPG * (EPG - 1) // 2
N_BUCKETS = N_GROUPS * N_PAIRS
EPS = 1e-6
NEG = -1e30
LOG2E = 1.4426950408889634

LANES = 128
H2P_W = D_MODEL // 2
EXT_W = LANES
OFF_H2P = D_MODEL
OFF_EXT = D_MODEL + H2P_W
ROW_W = OFF_EXT + EXT_W
EXT_BATCH0 = 8
U32 = jnp.uint32
SUBLANES = 8
GROUP_ROWS = SUBLANES
ROUTER_ROWS = GROUP_ROWS + N_EXPERTS

TQ = 256
TM = 128
VMEM_LIMIT = 56 * 1024 * 1024

OFF_Q, OFF_K, OFF_V, OFF_HQ, OFF_HF, OFF_HI, OFF_HG = 0, 512, 640, 768, 1280, 1792, 2304

_PAIR_A = np.array([a for a in range(EPG) for b in range(a + 1, EPG)], np.int32)
_PAIR_B = np.array([b for a in range(EPG) for b in range(a + 1, EPG)], np.int32)


def _dot(a, b):
    return jnp.dot(a, b, preferred_element_type=F32)


def _dot_nt(a, b):
    return lax.dot_general(a, b, (((1,), (1,)), ((), ())), preferred_element_type=F32)


def _dot_tn(a, b):
    return lax.dot_general(a, b, (((0,), (0,)), ((), ())), preferred_element_type=F32)


def _split(a):
    hi = a.astype(BF16)
    lo = (a - hi.astype(F32)).astype(BF16)
    return hi, lo


def _rms(x):
    return x * lax.rsqrt(jnp.mean(x * x, axis=-1, keepdims=True) + EPS)


def _sigmoid(x):
    return 0.5 * jnp.tanh(0.5 * x) + 0.5


def _ada_kernel(c_ref, w_ref, b_ref, o_ref):
    c = c_ref[...]
    ca = c * _sigmoid(c)
    c_hi, c_lo = _split(ca)
    w_hi, w_lo = _split(w_ref[...])
    o_ref[...] = _dot(c_hi, w_hi) + _dot(c_lo, w_hi) + _dot(c_hi, w_lo) + b_ref[...]


def _ada(c, w, b):
    bsz, d = c.shape
    n_out = w.shape[1]
    return pl.pallas_call(
        _ada_kernel,
        grid=(n_out // d,),
        in_specs=[pl.BlockSpec((bsz, d), lambda j: (0, 0)),
                  pl.BlockSpec((d, d), lambda j: (0, j)),
                  pl.BlockSpec((1, d), lambda j: (0, j))],
        out_specs=pl.BlockSpec((bsz, d), lambda j: (0, j)),
        out_shape=jax.ShapeDtypeStruct((bsz, n_out), F32),
        compiler_params=pltpu.CompilerParams(dimension_semantics=("arbitrary",),
                                             vmem_limit_bytes=VMEM_LIMIT),
        name="adaln",
    )(c, w, b.reshape(1, n_out))


def _attention(proj, kprev_ref, vprev_ref, sinks_ref, bias_ref, t, s_scr, m_scr, p_scr):
    tq = proj.shape[0]
    q = (proj[:, OFF_Q:OFF_Q + ATTN_W] * (HEAD_DIM ** -0.5 * LOG2E)).astype(BF16)
    kf = jnp.concatenate([kprev_ref[...], proj[:, OFF_K:OFF_K + KV_W]], axis=0)
    vf = jnp.concatenate([vprev_ref[...], proj[:, OFF_V:OFF_V + KV_W]], axis=0)
    kprev_ref[...] = proj[tq - WINDOW:, OFF_K:OFF_K + KV_W]
    vprev_ref[...] = proj[tq - WINDOW:, OFF_V:OFF_V + KV_W]

    lo = lax.broadcasted_iota(I32, kf.shape, 1) < HEAD_DIM
    kr = pltpu.roll(kf, HEAD_DIM, axis=1)
    vr = pltpu.roll(vf, HEAD_DIM, axis=1)

    def variants(a, ar):
        return [[jnp.where(lo, a, 0.0).astype(BF16), jnp.where(lo, 0.0, ar).astype(BF16)],
                [jnp.where(lo, ar, 0.0).astype(BF16), jnp.where(lo, 0.0, a).astype(BF16)]]

    kvar = variants(kf, kr)
    vvar = variants(vf, vr)

    first = jnp.where(t > 0, 0, 1)

    nblk = tq // WINDOW
    idx = lambda j, h: j * ATTN_HEADS + h
    keys = lambda a, j: a[j * WINDOW:(j + 2) * WINDOW]

    def scores():
        for j in range(nblk):
            for h in range(ATTN_HEADS):
                p, par = h // 2, h % 2
                qp = q[j * WINDOW:(j + 1) * WINDOW, p * LANES:(p + 1) * LANES]
                s = _dot_nt(qp, keys(kvar[p // 2][par], j)) + bias_ref[first if j == 0 else 0, h]
                s_scr[idx(j, h)] = s
                m_scr[idx(j, h)] = jnp.maximum(jnp.max(s, axis=-1, keepdims=True), sinks_ref[h] * LOG2E)

    def exps():
        for j in range(nblk):
            for h in range(ATTN_HEADS):
                m = m_scr[idx(j, h)]
                pe = jnp.exp2(s_scr[idx(j, h)] - m)
                p_scr[idx(j, h)] = pe.astype(BF16)
                m_scr[idx(j, h)] = 1.0 / (jnp.sum(pe, axis=-1, keepdims=True) + jnp.exp2(sinks_ref[h] * LOG2E - m))

    def values():
        blocks = []
        for j in range(nblk):
            pairs = []
            for p in range(ATTN_HEADS // 2):
                acc = None
                for par in range(2):
                    h = 2 * p + par
                    o = _dot(p_scr[idx(j, h)], keys(vvar[p // 2][par], j)) * m_scr[idx(j, h)]
                    acc = o if acc is None else acc + o
                pairs.append(acc)
            blocks.append(jnp.concatenate(pairs, axis=1))
        return jnp.concatenate(blocks, axis=0)

    return scores, exps, values


def _hgrn2(proj, lb, st_ref, hnorm, u_scr, stb_scr):
    tq = proj.shape[0]
    nc = tq // HG_CHUNK
    qr = proj[:, OFF_HQ:OFF_HQ + HG_W]
    fr = proj[:, OFF_HF:OFF_HF + HG_W]
    iv = proj[:, OFF_HI:OFF_HI + HG_W]
    gr = proj[:, OFF_HG:OFF_HG + HG_W]
    qh = qr * _sigmoid(qr)
    f = lb + (1.0 - lb) * _sigmoid(fr)
    kk = 1.0 - f
    logf = jnp.log(f)

    rmod = lax.broadcasted_iota(I32, (tq, HG_W), 0) & (HG_CHUNK - 1)
    bc = logf
    s = 1
    while s < HG_CHUNK:
        bc = bc + jnp.where(rmod >= s, pltpu.roll(bc, s, axis=0), 0.0)
        s *= 2

    b3 = bc.reshape(nc, HG_CHUNK, HG_W)
    blast = b3[:, HG_CHUNK - 1:HG_CHUNK, :]
    kend = (kk.reshape(nc, HG_CHUNK, HG_W) * jnp.exp(blast - b3)).reshape(tq, HG_W)
    decay = jnp.exp(blast).reshape(nc, HG_W)
    qdec = (qh * jnp.exp(bc)).astype(BF16)
    kdec = (kk * jnp.exp(-bc)).astype(BF16)
    kend = kend.astype(BF16)
    ivb = iv.astype(BF16)

    ri = lax.broadcasted_iota(I32, (tq, tq), 0)
    ci = lax.broadcasted_iota(I32, (tq, tq), 1)
    cmask = ((ri // HG_CHUNK) == (ci // HG_CHUNK)) & (ri >= ci)

    heads = [slice(hh * HG_DIM, (hh + 1) * HG_DIM) for hh in range(HG_HEADS)]
    chunks = [slice(n * HG_CHUNK, (n + 1) * HG_CHUNK) for n in range(nc)]

    lane_head = lax.broadcasted_iota(I32, (HG_CHUNK, HG_W), 1) // HG_DIM
    for n, rs in enumerate(chunks):
        vstack = jnp.concatenate([ivb[rs, sl] for sl in heads], axis=0)
        kblk = jnp.concatenate([jnp.where(lane_head == hh, kend[rs], 0.0).astype(BF16)
                                for hh in range(HG_HEADS)], axis=0)
        u_scr[n] = _dot_tn(vstack, kblk)

    st = st_ref[...]
    for n in range(nc):
        stb_scr[n] = st.astype(BF16)
        st = st * decay[n:n + 1] + u_scr[n]
    st_ref[...] = st

    outs = []
    for hh, sl in enumerate(heads):
        a = _dot_nt(qdec[:, sl], kdec[:, sl])
        a = jnp.where(cmask, a, 0.0).astype(BF16)
        o_intra = _dot(a, ivb[:, sl])
        inter = [_dot_nt(qdec[rs, sl], stb_scr[n, :, sl]) for n, rs in enumerate(chunks)]
        o = o_intra + jnp.concatenate(inter, axis=0)
        o = _rms(o) * hnorm[:, sl]
        g = gr[:, sl]
        outs.append(o * (g * _sigmoid(g)))
    return jnp.concatenate(outs, axis=1)


def _route_topk(h2, wr_hi, wr_lo, br):
    tq = h2.shape[0]
    h_hi, h_lo = _split(h2)
    logits = _dot(h_hi, wr_hi) + _dot(h_lo, wr_hi) + _dot(h_hi, wr_lo)
    lt = logits.T[0:ROUTER_ROWS] + br
    sub = lax.broadcasted_iota(I32, (SUBLANES, tq), 0).astype(F32)
    none = float(SUBLANES)

    gl = lt[0:GROUP_ROWS]
    gm = jnp.max(gl, axis=0, keepdims=True)
    gidx = jnp.min(jnp.where(gl == gm, sub, none), axis=0, keepdims=True)
    g_w = 1.0 / jnp.sum(jnp.exp(gl - gm), axis=0, keepdims=True)

    group_rows = lambda g: lt[GROUP_ROWS + EPG * g:GROUP_ROWS + EPG * (g + 1)]
    es = group_rows(0)
    for g in range(1, N_GROUPS):
        es = jnp.where(gidx == float(g), group_rows(g), es)
    m1 = jnp.max(es, axis=0, keepdims=True)
    i1 = jnp.min(jnp.where(es == m1, sub, none), axis=0, keepdims=True)
    e2 = jnp.where(sub == i1, NEG, es)
    m2 = jnp.max(e2, axis=0, keepdims=True)
    i2 = jnp.min(jnp.where(e2 == m2, sub, none), axis=0, keepdims=True)
    dd = jnp.exp(m2 - m1)
    w1 = g_w / (1.0 + dd)
    w2 = g_w * dd / (1.0 + dd)
    first_low = i1 < i2
    ea = jnp.minimum(i1, i2)
    eb = jnp.maximum(i1, i2)
    w_lo = jnp.where(first_low, w1, w2)
    w_hi = jnp.where(first_low, w2, w1)
    pair = ea * (2.0 * EPG - 1.0 - ea) * 0.5 + (eb - ea - 1.0)
    bucket = gidx * float(N_PAIRS) + pair
    return bucket, w_lo, w_hi


def _route_rank(bucket, w_lo, w_hi, carry_ref, bidx, live):
    tq = bucket.shape[1]
    brow = lax.broadcasted_iota(I32, (LANES, tq), 0).astype(F32)
    onehot = brow == bucket
    oh = jnp.where(onehot, live, 0.0)
    ti = lax.broadcasted_iota(I32, (tq, tq), 0)
    tj = lax.broadcasted_iota(I32, (tq, tq), 1)
    upper = jnp.where(ti < tj, 1.0, 0.0).astype(BF16)
    before = _dot(oh.astype(BF16), upper) + carry_ref[...]
    rank = jnp.sum(jnp.where(onehot, before, 0.0), axis=0, keepdims=True)
    carry_ref[...] = carry_ref[...] + jnp.sum(oh, axis=1, keepdims=True)

    lane_row = lax.broadcasted_iota(I32, (LANES - EXT_BATCH0, tq), 0)
    onehot_b = jnp.where(lane_row == bidx, 1.0, 0.0)
    info = jnp.concatenate([w_lo, w_hi, jnp.zeros((EXT_BATCH0 - 2, tq), F32), onehot_b], axis=0)
    return rank, info.T


def _mixer_kernel(sinks_ref, x_ref, mod_ref, ln1pre_ref, ln1post_ref, ln2pre_ref, anorm_ref, hnorm_ref,
                  lb_ref, win_ref, wout_ref, wrhi_ref, wrlo_ref, br_ref, bias_ref,
                  rows_ref, info_ref, cnt_ref,
                  kprev_ref, vprev_ref, st_ref, carry_ref, s_scr, m_scr, p_scr, u_scr, stb_scr,
                  h2_scr, proj_scr, keep_scr, *, tiles_per_seq):
    s = pl.program_id(0)
    n_tiles = pl.num_programs(0) - 1
    tq = x_ref.shape[0]
    t = lax.rem(jnp.minimum(s, n_tiles - 1), tiles_per_seq)
    bits = lambda a: pltpu.bitcast(a, U32)

    @pl.when(s == 0)
    def _():
        carry_ref[...] = jnp.zeros_like(carry_ref)
        h2_scr[...] = jnp.zeros_like(h2_scr)
        keep_scr[...] = jnp.zeros_like(keep_scr)

    @pl.when(t == 0)
    def _():
        st_ref[...] = jnp.zeros_like(st_ref)
        kprev_ref[...] = jnp.zeros_like(kprev_ref)
        vprev_ref[...] = jnp.zeros_like(vprev_ref)

    x = x_ref[...]
    mod = mod_ref[0]
    sh1, sc1, ga1, sh2, sc2 = mod[0:1], mod[1:2], mod[2:3], mod[3:4], mod[4:5]
    prev = jnp.maximum(s - 1, 0)

    @pl.when(s <= n_tiles)
    def _():
        bucket, w_lo, w_hi = _route_topk(h2_scr[...], wrhi_ref[...], wrlo_ref[...], br_ref[...])
        h = _rms(x) * ln1pre_ref[...] * (1.0 + sc1) + sh1
        proj_scr[...] = _dot(h.astype(BF16), win_ref[...])
        live = jnp.where(s >= 1, 1.0, 0.0)
        rank, ext = _route_rank(bucket, w_lo, w_hi, carry_ref, prev // tiles_per_seq, live)
        rows_ref[:, :, 0:OFF_EXT] = keep_scr[...].reshape(tq, 1, OFF_EXT)
        rows_ref[:, :, OFF_EXT:ROW_W] = bits(ext).reshape(tq, 1, EXT_W)
        info_ref[0] = jnp.concatenate([bucket, rank, jnp.zeros((6, tq), F32)], axis=0).astype(I32)
        cnt_ref[...] = jnp.broadcast_to(carry_ref[...], cnt_ref.shape)

    proj = proj_scr[...]
    scores, exps, values = _attention(proj, kprev_ref, vprev_ref, sinks_ref, bias_ref, t, s_scr, m_scr, p_scr)
    scores()
    exps()
    attn = _rms(values()) * anorm_ref[...]

    lbr = lb_ref[...]
    le = jnp.exp(lbr - jnp.max(lbr, axis=0, keepdims=True))
    lb = le[0:1] / jnp.sum(le, axis=0, keepdims=True)
    hg = _hgrn2(proj, lb, st_ref, hnorm_ref[...], u_scr, stb_scr)

    mix = _dot(jnp.concatenate([attn, hg], axis=1).astype(BF16), wout_ref[...])
    x1 = x + ga1 * (_rms(mix) * ln1post_ref[...])

    h2 = _rms(x1) * ln2pre_ref[...] * (1.0 + sc2) + sh2
    h2r = h2.astype(BF16).astype(F32)

    keep_scr[:, 0:D_MODEL] = bits(x1)
    keep_scr[:, OFF_H2P:OFF_EXT] = ((bits(h2r[:, 0:H2P_W]) >> 16)
                                    | (bits(h2r[:, H2P_W:D_MODEL]) & jnp.uint32(0xFFFF0000)))
    h2_scr[...] = h2


def _attn_bias():
    qi = np.arange(WINDOW)[:, None]
    kj = np.arange(2 * WINDOW)[None, :]
    dist = qi + WINDOW - kj
    in_win = (dist >= 0) & (dist < WINDOW)
    slopes = 2.0 ** (-8.0 * (np.arange(ATTN_HEADS) + 1.0) / ATTN_HEADS)
    b = np.where(in_win[None], -slopes[:, None, None] * dist[None] * LOG2E, NEG)
    b_first = np.where((kj >= WINDOW)[None], b, NEG)
    return jnp.asarray(np.stack([b, b_first]).astype(np.float32))


def _mixer(x2, mod3, sinks, ln1pre, ln1post, ln2pre, anorm, hnorm, lb, win, wout, wr_hi, wr_lo, br,
           bsz, seq):
    bias = _attn_bias()
    n = bsz * seq
    nt = seq // TQ
    n_tiles = bsz * nt
    n_sc = (TQ // WINDOW) * ATTN_HEADS
    cur = lambda s: jnp.minimum(s, n_tiles - 1)
    const = lambda s: (0, 0)
    full = lambda a: pl.BlockSpec(a.shape, const)
    return pl.pallas_call(
        functools.partial(_mixer_kernel, tiles_per_seq=nt),
        grid=(n_tiles + 1,),
        in_specs=[pl.BlockSpec(memory_space=pltpu.SMEM),
                  pl.BlockSpec((TQ, D_MODEL), lambda s: (cur(s), 0)),
                  pl.BlockSpec((1, 6, D_MODEL), lambda s: (cur(s) // nt, 0, 0)),
                  full(ln1pre), full(ln1post), full(ln2pre), full(anorm), full(hnorm), full(lb),
                  full(win), full(wout), full(wr_hi), full(wr_lo), full(br),
                  pl.BlockSpec(bias.shape, lambda s: (0, 0, 0, 0))],
        out_specs=[pl.BlockSpec((TQ, 1, ROW_W), lambda s: (jnp.maximum(s - 1, 0), 0, 0)),
                   pl.BlockSpec((1, 8, TQ), lambda s: (jnp.maximum(s - 1, 0), 0, 0)),
                   pl.BlockSpec((LANES, LANES), const)],
        out_shape=[jax.ShapeDtypeStruct((n, 1, ROW_W), U32),
                   jax.ShapeDtypeStruct((n // TQ, 8, TQ), I32),
                   jax.ShapeDtypeStruct((LANES, LANES), F32)],
        scratch_shapes=[pltpu.VMEM((WINDOW, KV_W), F32),
                        pltpu.VMEM((WINDOW, KV_W), F32),
                        pltpu.VMEM((HG_DIM, HG_W), F32),
                        pltpu.VMEM((LANES, 1), F32),
                        pltpu.VMEM((n_sc, WINDOW, 2 * WINDOW), F32),
                        pltpu.VMEM((n_sc, WINDOW, 1), F32),
                        pltpu.VMEM((n_sc, WINDOW, 2 * WINDOW), BF16),
                        pltpu.VMEM((TQ // HG_CHUNK, HG_DIM, HG_W), F32),
                        pltpu.VMEM((TQ // HG_CHUNK, HG_DIM, HG_W), BF16),
                        pltpu.VMEM((TQ, D_MODEL), F32),
                        pltpu.VMEM((TQ, IN_W), F32),
                        pltpu.VMEM((TQ, OFF_EXT), U32)],
        compiler_params=pltpu.CompilerParams(dimension_semantics=("arbitrary",),
                                             vmem_limit_bytes=VMEM_LIMIT),
        name="mixer",
    )(sinks, x2, mod3, ln1pre, ln1post, ln2pre, anorm, hnorm, lb, win, wout, wr_hi, wr_lo, br, bias)


PERM_STEPS = 8
PERM_ROWS = 8
PERM_UNROLL = 16


def _perm_kernel(rs_ref, cnt_ref, bucket_ref, rank_ref, perm_ref, pos_vmem, pos_smem, sem):
    pid = pl.program_id(0)
    rows, cols = pos_vmem.shape

    b = bucket_ref[0]
    start = jnp.zeros_like(b)
    for k in range(N_BUCKETS):
        start = jnp.where(b == k, rs_ref[k], start)
    pos_vmem[...] = start + rank_ref[0]
    copies = [pltpu.make_async_copy(pos_vmem.at[r], pos_smem.at[pl.ds(r * cols, cols)], sem) for r in range(rows)]
    for cp in copies:
        cp.start()

    @pl.when(pid == 0)
    def _():
        def per_bucket(k, carry):
            first = rs_ref[k]
            cnt = cnt_ref[k]

            def pad(r, c2):
                perm_ref[first + r] = 0
                return c2

            lax.fori_loop(cnt, ((cnt + TM - 1) // TM) * TM, pad, 0)
            return carry

        lax.fori_loop(0, N_BUCKETS, per_bucket, 0)

        def tail(blk, carry):
            for u in range(PERM_UNROLL):
                perm_ref[blk * PERM_UNROLL + u] = 0
            return carry

        lax.fori_loop(rs_ref[N_BUCKETS] // PERM_UNROLL, perm_ref.shape[0] // PERM_UNROLL, tail, 0)

    for cp in copies:
        cp.wait()
    base = pid * (rows * cols)

    def body(j, carry):
        i0 = j * PERM_UNROLL
        positions = [pos_smem[i0 + u] for u in range(PERM_UNROLL)]
        for u in range(PERM_UNROLL):
            perm_ref[positions[u]] = base + i0 + u
        return carry

    lax.fori_loop(0, rows * cols // PERM_UNROLL, body, 0)


def _perm(row_start, counts, bucket, rank, n_rows):
    n = bucket.shape[0]
    cols = n // (PERM_STEPS * PERM_ROWS)
    assert n % (PERM_STEPS * PERM_ROWS * PERM_UNROLL) == 0
    chunked = lambda a: a.reshape(PERM_STEPS, PERM_ROWS, cols)
    chunk_spec = pl.BlockSpec((1, PERM_ROWS, cols), lambda i: (i, 0, 0))
    return pl.pallas_call(
        _perm_kernel,
        grid=(PERM_STEPS,),
        in_specs=[pl.BlockSpec(memory_space=pltpu.SMEM),
                  pl.BlockSpec(memory_space=pltpu.SMEM),
                  chunk_spec, chunk_spec],
        out_specs=pl.BlockSpec(memory_space=pltpu.SMEM),
        out_shape=jax.ShapeDtypeStruct((n_rows,), I32),
        scratch_shapes=[pltpu.VMEM((PERM_ROWS, cols), I32), pltpu.SMEM((PERM_ROWS * cols,), I32),
                        pltpu.SemaphoreType.DMA(())],
        compiler_params=pltpu.CompilerParams(dimension_semantics=("arbitrary",)),
        name="perm",
    )(row_start, counts, chunked(bucket), chunked(rank))


GATHER_DEPTH = 3


SCHED_EXPERT, SCHED_RUN_START, SCHED_SLOT, SCHED_NEXT_EXPERT = range(4)


def _moe_kernel(nt_ref, nv_ref, sched_ref, perm_ref,
                rows_hbm, gtab_hi_ref, gtab_lo_ref, ln2post_ref, wgu_hbm, wd_hbm,
                out_hbm, xbuf, x2d, obuf, wgu_buf, wd_buf, gsem, ssem, wsem):
    i = pl.program_id(0)
    nt = nt_ref[0]
    n_steps = pl.num_programs(0)
    last_tile = n_steps - 1
    sched = lambda field, side: sched_ref[(field * 2 + side) * n_steps + i]

    def weight_copies(side, expert, slot):
        return (pltpu.make_async_copy(wgu_hbm.at[expert], wgu_buf.at[side, slot], wsem.at[side, slot]),
                pltpu.make_async_copy(wd_hbm.at[expert], wd_buf.at[side, slot], wsem.at[side, slot]))
    nbuf = xbuf.shape[0] // TM

    def start_gather(tile, pred):
        sl = lax.rem(tile, nbuf)
        base = jnp.minimum(tile, last_tile) * TM
        for r in range(TM):
            @pl.when(pred)
            def _():
                tok = perm_ref[base + r]
                pltpu.make_async_copy(rows_hbm.at[tok], xbuf.at[sl * TM + r], gsem.at[sl]).start()

    def wait_gather(tile):
        sl = lax.rem(tile, nbuf)
        pltpu.make_async_copy(rows_hbm.at[pl.ds(0, TM)], xbuf.at[pl.ds(sl * TM, TM)], gsem.at[sl]).wait()

    def wait_scatter(sl, nv):
        @pl.when(nv == TM)
        def _():
            pltpu.make_async_copy(obuf.at[sl], out_hbm.at[pl.ds(0, TM)], ssem.at[sl]).wait()

        @pl.when(nv < TM)
        def _():
            def one(r, carry):
                pltpu.make_async_copy(obuf.at[sl, pl.ds(0, 1)], out_hbm.at[pl.ds(0, 1)], ssem.at[sl]).wait()
                return carry

            lax.fori_loop(0, nv, one, 0)

    def compute(xb, slot_a, slot_b):
        x1 = pltpu.bitcast(xb[:, 0:D_MODEL], F32)
        hp = xb[:, OFF_H2P:OFF_EXT]
        h2a = pltpu.bitcast(hp << 16, F32).astype(BF16)
        h2b = pltpu.bitcast(hp & jnp.uint32(0xFFFF0000), F32).astype(BF16)
        ext = pltpu.bitcast(xb[:, OFF_EXT:ROW_W], F32)
        w_lo, w_hi = ext[:, 0:1], ext[:, 1:2]
        sel = ext.astype(BF16)
        ga2 = _dot(sel, gtab_hi_ref[...]) + _dot(sel, gtab_lo_ref[...])

        def expert(side, slot):
            gu = (_dot(h2a, wgu_buf[side, slot, 0:H2P_W])
                  + _dot(h2b, wgu_buf[side, slot, H2P_W:D_MODEL]))
            hg, hu = gu[:, 0:FF], gu[:, FF:2 * FF]
            act = (hg * _sigmoid(hg)) * hu
            return _dot(act.astype(BF16), wd_buf[side, slot])

        y = w_lo * expert(0, slot_a) + w_hi * expert(1, slot_b)
        return x1 + ga2 * (_rms(y) * ln2post_ref[...])

    @pl.when(i == 0)
    def _():
        for side in range(2):
            for cp in weight_copies(side, sched(SCHED_EXPERT, side), 0):
                cp.start()
        for d in range(GATHER_DEPTH):
            start_gather(d, d < nt)

    @pl.when(i < nt)
    def _():
        nv = nv_ref[i]
        osl = lax.rem(i, 2)
        slots = []
        for side in range(2):
            slot = sched(SCHED_SLOT, side)
            slots.append(slot)

            @pl.when(sched(SCHED_RUN_START, side) == 1)
            def _():
                for cp in weight_copies(side, sched(SCHED_EXPERT, side), slot):
                    cp.wait()
                nxt = sched(SCHED_NEXT_EXPERT, side)

                @pl.when(nxt >= 0)
                def _():
                    for cp in weight_copies(side, nxt, 1 - slot):
                        cp.start()

        wait_gather(i)

        @pl.when(i >= 2)
        def _():
            wait_scatter(osl, nv_ref[jnp.maximum(i - 2, 0)])

        start_gather(i + GATHER_DEPTH, i + GATHER_DEPTH < nt)
        x2d[...] = xbuf[pl.ds(lax.rem(i, nbuf) * TM, TM)].reshape(TM, ROW_W)
        result = compute(x2d[...], *slots)

        def scatter_row(k, r):
            tok = perm_ref[i * TM + r]
            pltpu.make_async_copy(obuf.at[k, pl.ds(r, 1)], out_hbm.at[pl.ds(tok, 1)], ssem.at[k]).start()

        for k in range(2):
            @pl.when((osl == k) & (nv == TM))
            def _():
                obuf[k] = result
                for r in range(TM):
                    scatter_row(k, r)

            @pl.when((osl == k) & (nv < TM))
            def _():
                obuf[k] = result
                for r in range(TM):
                    pl.when(r < nv)(functools.partial(scatter_row, k, r))

        @pl.when(i == nt - 1)
        def _():
            wait_scatter(osl, nv)

            @pl.when(i >= 1)
            def _():
                wait_scatter(1 - osl, nv_ref[jnp.maximum(i - 1, 0)])


def _moe(rows, gtab_hi, gtab_lo, ln2post, wgu, wd, nt, nv, sched, perm, n_tiles):
    n = rows.shape[0]
    const2 = lambda i, *_: (0, 0)
    grid_spec = pltpu.PrefetchScalarGridSpec(
        num_scalar_prefetch=4,
        grid=(n_tiles,),
        in_specs=[pl.BlockSpec(memory_space=pl.ANY),
                  pl.BlockSpec(gtab_hi.shape, const2),
                  pl.BlockSpec(gtab_lo.shape, const2),
                  pl.BlockSpec(ln2post.shape, const2),
                  pl.BlockSpec(memory_space=pl.ANY),
                  pl.BlockSpec(memory_space=pl.ANY)],
        out_specs=pl.BlockSpec(memory_space=pl.ANY),
        scratch_shapes=[pltpu.VMEM(((GATHER_DEPTH + 1) * TM, 1, ROW_W), U32),
                        pltpu.VMEM((TM, ROW_W), U32),
                        pltpu.VMEM((2, TM, D_MODEL), F32),
                        pltpu.VMEM((2, 2, D_MODEL, 2 * FF), BF16),
                        pltpu.VMEM((2, 2, FF, D_MODEL), BF16),
                        pltpu.SemaphoreType.DMA((GATHER_DEPTH + 1,)),
                        pltpu.SemaphoreType.DMA((2,)),
                        pltpu.SemaphoreType.DMA((2, 2))],
    )
    return pl.pallas_call(
        _moe_kernel,
        grid_spec=grid_spec,
        out_shape=jax.ShapeDtypeStruct((n, D_MODEL), F32),
        compiler_params=pltpu.CompilerParams(dimension_semantics=("arbitrary",),
                                             vmem_limit_bytes=VMEM_LIMIT),
        name="moe",
    )(nt, nv, sched, perm, rows, gtab_hi, gtab_lo, ln2post, wgu, wd)


def _weight_schedule(ea, eb):
    n_tiles = ea.shape[0]
    tid = jnp.arange(n_tiles, dtype=I32)
    fields = [[], [], [], []]
    for e in (ea, eb):
        start = jnp.concatenate([jnp.ones((1,), I32), (e[1:] != e[:-1]).astype(I32)])
        slot = (jnp.cumsum(start) - 1) % 2
        start_idx = jnp.where(start == 1, tid, n_tiles)
        next_start = jnp.concatenate([lax.cummin(start_idx, reverse=True)[1:], jnp.full((1,), n_tiles, I32)])
        nxt = jnp.where(next_start < n_tiles, e[jnp.minimum(next_start, n_tiles - 1)], -1)
        for f, v in zip(fields, (e, start, slot, nxt)):
            f.append(v.astype(I32))
    return jnp.concatenate([v for f in fields for v in f])


def kernel(x, c, ln1_pre, ln1_post, ln2_pre, ln2_post, w_ada, b_ada, w_in, attn_sinks, attn_out_norm,
           hgrn_lb, hgrn_out_norm, w_out, w_router_group, b_router_group, w_router_expert,
           b_router_expert, w_exp_gate, w_exp_up, w_exp_down):
    bsz, seq, d = x.shape
    assert d == D_MODEL and seq % TQ == 0 and w_ada.shape[0] == 1 and hgrn_lb.shape[0] == 2
    assert bsz <= LANES - EXT_BATCH0 and (bsz * seq) % TM == 0
    n = bsz * seq

    mod = _ada(c, w_ada[0], b_ada[0])
    mod3 = mod.reshape(bsz, 6, d)

    wr = jnp.concatenate([w_router_group[0], jnp.zeros((d, GROUP_ROWS - N_GROUPS), F32), w_router_expert[0],
                          jnp.zeros((d, LANES - ROUTER_ROWS), F32)], axis=1)
    br = jnp.concatenate([b_router_group[0], jnp.full((GROUP_ROWS - N_GROUPS,), NEG, F32), b_router_expert[0]])
    wr_hi = wr.astype(BF16)
    wr_lo = (wr - wr_hi.astype(F32)).astype(BF16)

    x1ext, info, cnt = _mixer(
        x.reshape(n, d), mod3, attn_sinks[0], ln1_pre, ln1_post, ln2_pre, attn_out_norm, hgrn_out_norm,
        hgrn_lb, w_in[0].astype(BF16), w_out[0].astype(BF16), wr_hi, wr_lo, br.reshape(ROUTER_ROWS, 1),
        bsz, seq)

    n_tiles = n // TM + N_BUCKETS
    counts = cnt[:N_BUCKETS, 0].astype(I32)
    tiles_per = (counts + TM - 1) // TM
    tile_end = jnp.cumsum(tiles_per)
    tile_start = tile_end - tiles_per
    nt = tile_end[-1]
    bucket = info[:, 0, :].reshape(n)
    rank = info[:, 1, :].reshape(n)
    tid = jnp.arange(n_tiles, dtype=I32)[None, :]
    member = (tid >= tile_start[:, None]) & (tid < tile_end[:, None])
    pick = lambda per_bucket: jnp.sum(jnp.where(member, per_bucket, 0), axis=0).astype(I32)
    bidx = np.arange(N_BUCKETS, dtype=np.int32)
    ea_of = jnp.asarray((bidx // N_PAIRS) * EPG + _PAIR_A[bidx % N_PAIRS])[:, None]
    eb_of = jnp.asarray((bidx // N_PAIRS) * EPG + _PAIR_B[bidx % N_PAIRS])[:, None]
    last_used = jnp.arange(N_BUCKETS)[:, None] == jnp.max(jnp.where(tiles_per > 0, jnp.arange(N_BUCKETS), 0))
    unused = tid[0] >= nt
    nv = pick(jnp.clip(counts[:, None] - (tid - tile_start[:, None]) * TM, 0, TM))
    ea = jnp.where(unused, jnp.sum(jnp.where(last_used, ea_of, 0)), pick(ea_of)).astype(I32)
    eb = jnp.where(unused, jnp.sum(jnp.where(last_used, eb_of, 0)), pick(eb_of)).astype(I32)

    pad128 = lambda a: jnp.concatenate([a, jnp.zeros((LANES - a.shape[0],), I32)])
    row_start = jnp.concatenate([tile_start, nt.reshape(1)]) * TM
    perm = _perm(pad128(row_start), pad128(counts), bucket, rank, n_tiles * TM)

    wgu = jnp.concatenate([w_exp_gate[0], w_exp_up[0]], axis=-1).astype(BF16)
    wd = w_exp_down[0].astype(BF16)
    gtab = jnp.zeros((LANES, d), F32).at[EXT_BATCH0:EXT_BATCH0 + bsz].set(mod3[:, 5, :])
    gtab_hi = gtab.astype(BF16)
    gtab_lo = (gtab - gtab_hi.astype(F32)).astype(BF16)
    out = _moe(x1ext, gtab_hi, gtab_lo, ln2_post, wgu, wd, nt.reshape(1), nv, _weight_schedule(ea, eb), perm, n_tiles)
    return out.reshape(bsz, seq, d)
```

```python
import functools

import numpy as np
import jax
import jax.numpy as jnp
from jax import lax
from jax.experimental import pallas as pl
from jax.experimental.pallas import tpu as pltpu

F32 = jnp.float32
BF16 = jnp.bfloat16
I32 = jnp.int32

D_MODEL = 1024
ATTN_HEADS = 8
HEAD_DIM = 64
WINDOW = 128
ATTN_W = 512
KV_W = 128
HG_HEADS = 4
HG_DIM = 128
HG_W = 512
HG_CHUNK = 32
IN_W = 2816
N_GROUPS = 4
EPG = 8
N_EXPERTS = 32
FF = 256
N_PAIRS = EPG * (EPG - 1) // 2
N_BUCKETS = N_GROUPS * N_PAIRS
EPS = 1e-6
NEG = -1e30
LOG2E = 1.4426950408889634

LANES = 128
H2P_W = D_MODEL // 2
EXT_W = LANES
OFF_H2P = D_MODEL
OFF_EXT = D_MODEL + H2P_W
ROW_W = OFF_EXT + EXT_W
EXT_BATCH0 = 8
U32 = jnp.uint32
SUBLANES = 8
GROUP_ROWS = SUBLANES
ROUTER_ROWS = GROUP_ROWS + N_EXPERTS

TQ = 256
TM = 128
VMEM_LIMIT = 56 * 1024 * 1024

OFF_Q, OFF_K, OFF_V, OFF_HQ, OFF_HF, OFF_HI, OFF_HG = 0, 512, 640, 768, 1280, 1792, 2304

_PAIR_A = np.array([a for a in range(EPG) for b in range(a + 1, EPG)], np.int32)
_PAIR_B = np.array([b for a in range(EPG) for b in range(a + 1, EPG)], np.int32)


def _dot(a, b):
    return jnp.dot(a, b, preferred_element_type=F32)


def _dot_nt(a, b):
    return lax.dot_general(a, b, (((1,), (1,)), ((), ())), preferred_element_type=F32)


def _dot_tn(a, b):
    return lax.dot_general(a, b, (((0,), (0,)), ((), ())), preferred_element_type=F32)


def _split(a):
    hi = a.astype(BF16)
    lo = (a - hi.astype(F32)).astype(BF16)
    return hi, lo


def _rms(x):
    return x * lax.rsqrt(jnp.mean(x * x, axis=-1, keepdims=True) + EPS)


def _sigmoid(x):
    return 0.5 * jnp.tanh(0.5 * x) + 0.5


def _ada_kernel(c_ref, w_ref, b_ref, o_ref):
    c = c_ref[...]
    ca = c * _sigmoid(c)
    c_hi, c_lo = _split(ca)
    w_hi, w_lo = _split(w_ref[...])
    o_ref[...] = _dot(c_hi, w_hi) + _dot(c_lo, w_hi) + _dot(c_hi, w_lo) + b_ref[...]


def _ada(c, w, b):
    bsz, d = c.shape
    n_out = w.shape[1]
    return pl.pallas_call(
        _ada_kernel,
        grid=(n_out // d,),
        in_specs=[pl.BlockSpec((bsz, d), lambda j: (0, 0)),
                  pl.BlockSpec((d, d), lambda j: (0, j)),
                  pl.BlockSpec((1, d), lambda j: (0, j))],
        out_specs=pl.BlockSpec((bsz, d), lambda j: (0, j)),
        out_shape=jax.ShapeDtypeStruct((bsz, n_out), F32),
        compiler_params=pltpu.CompilerParams(dimension_semantics=("arbitrary",),
                                             vmem_limit_bytes=VMEM_LIMIT),
        name="adaln",
    )(c, w, b.reshape(1, n_out))


def _attention(proj, kprev_ref, vprev_ref, sinks_ref, bias_ref, t, s_scr, m_scr, p_scr):
    tq = proj.shape[0]
    q = (proj[:, OFF_Q:OFF_Q + ATTN_W] * (HEAD_DIM ** -0.5 * LOG2E)).astype(BF16)
    kf = jnp.concatenate([kprev_ref[...], proj[:, OFF_K:OFF_K + KV_W]], axis=0)
    vf = jnp.concatenate([vprev_ref[...], proj[:, OFF_V:OFF_V + KV_W]], axis=0)
    kprev_ref[...] = proj[tq - WINDOW:, OFF_K:OFF_K + KV_W]
    vprev_ref[...] = proj[tq - WINDOW:, OFF_V:OFF_V + KV_W]

    lo = lax.broadcasted_iota(I32, kf.shape, 1) < HEAD_DIM
    kr = pltpu.roll(kf, HEAD_DIM, axis=1)
    vr = pltpu.roll(vf, HEAD_DIM, axis=1)

    def variants(a, ar):
        return [[jnp.where(lo, a, 0.0).astype(BF16), jnp.where(lo, 0.0, ar).astype(BF16)],
                [jnp.where(lo, ar, 0.0).astype(BF16), jnp.where(lo, 0.0, a).astype(BF16)]]

    kvar = variants(kf, kr)
    vvar = variants(vf, vr)

    first = jnp.where(t > 0, 0, 1)

    nblk = tq // WINDOW
    idx = lambda j, h: j * ATTN_HEADS + h
    keys = lambda a, j: a[j * WINDOW:(j + 2) * WINDOW]

    def scores():
        for j in range(nblk):
            for h in range(ATTN_HEADS):
                p, par = h // 2, h % 2
                qp = q[j * WINDOW:(j + 1) * WINDOW, p * LANES:(p + 1) * LANES]
                s = _dot_nt(qp, keys(kvar[p // 2][par], j)) + bias_ref[first if j == 0 else 0, h]
                s_scr[idx(j, h)] = s
                m_scr[idx(j, h)] = jnp.maximum(jnp.max(s, axis=-1, keepdims=True), sinks_ref[h] * LOG2E)

    def exps():
        for j in range(nblk):
            for h in range(ATTN_HEADS):
                m = m_scr[idx(j, h)]
                pe = jnp.exp2(s_scr[idx(j, h)] - m)
                p_scr[idx(j, h)] = pe.astype(BF16)
                m_scr[idx(j, h)] = 1.0 / (jnp.sum(pe, axis=-1, keepdims=True) + jnp.exp2(sinks_ref[h] * LOG2E - m))

    def values():
        blocks = []
        for j in range(nblk):
            pairs = []
            for p in range(ATTN_HEADS // 2):
                acc = None
                for par in range(2):
                    h = 2 * p + par
                    o = _dot(p_scr[idx(j, h)], keys(vvar[p // 2][par], j)) * m_scr[idx(j, h)]
                    acc = o if acc is None else acc + o
                pairs.append(acc)
            blocks.append(jnp.concatenate(pairs, axis=1))
        return jnp.concatenate(blocks, axis=0)

    return scores, exps, values


def _hgrn2(proj, lb, st_ref, hnorm, u_scr, stb_scr):
    tq = proj.shape[0]
    nc = tq // HG_CHUNK
    qr = proj[:, OFF_HQ:OFF_HQ + HG_W]
    fr = proj[:, OFF_HF:OFF_HF + HG_W]
    iv = proj[:, OFF_HI:OFF_HI + HG_W]
    gr = proj[:, OFF_HG:OFF_HG + HG_W]
    qh = qr * _sigmoid(qr)
    f = lb + (1.0 - lb) * _sigmoid(fr)
    kk = 1.0 - f
    logf = jnp.log(f)

    rmod = lax.broadcasted_iota(I32, (tq, HG_W), 0) & (HG_CHUNK - 1)
    bc = logf
    s = 1
    while s < HG_CHUNK:
        bc = bc + jnp.where(rmod >= s, pltpu.roll(bc, s, axis=0), 0.0)
        s *= 2

    b3 = bc.reshape(nc, HG_CHUNK, HG_W)
    blast = b3[:, HG_CHUNK - 1:HG_CHUNK, :]
    kend = (kk.reshape(nc, HG_CHUNK, HG_W) * jnp.exp(blast - b3)).reshape(tq, HG_W)
    decay = jnp.exp(blast).reshape(nc, HG_W)
    qdec = (qh * jnp.exp(bc)).astype(BF16)
    kdec = (kk * jnp.exp(-bc)).astype(BF16)
    kend = kend.astype(BF16)
    ivb = iv.astype(BF16)

    ri = lax.broadcasted_iota(I32, (tq, tq), 0)
    ci = lax.broadcasted_iota(I32, (tq, tq), 1)
    cmask = ((ri // HG_CHUNK) == (ci // HG_CHUNK)) & (ri >= ci)

    heads = [slice(hh * HG_DIM, (hh + 1) * HG_DIM) for hh in range(HG_HEADS)]
    chunks = [slice(n * HG_CHUNK, (n + 1) * HG_CHUNK) for n in range(nc)]

    lane_head = lax.broadcasted_iota(I32, (HG_CHUNK, HG_W), 1) // HG_DIM
    for n, rs in enumerate(chunks):
        vstack = jnp.concatenate([ivb[rs, sl] for sl in heads], axis=0)
        kblk = jnp.concatenate([jnp.where(lane_head == hh, kend[rs], 0.0).astype(BF16)
                                for hh in range(HG_HEADS)], axis=0)
        u_scr[n] = _dot_tn(vstack, kblk)

    st = st_ref[...]
    for n in range(nc):
        stb_scr[n] = st.astype(BF16)
        st = st * decay[n:n + 1] + u_scr[n]
    st_ref[...] = st

    outs = []
    for hh, sl in enumerate(heads):
        a = _dot_nt(qdec[:, sl], kdec[:, sl])
        a = jnp.where(cmask, a, 0.0).astype(BF16)
        o_intra = _dot(a, ivb[:, sl])
        inter = [_dot_nt(qdec[rs, sl], stb_scr[n, :, sl]) for n, rs in enumerate(chunks)]
        o = o_intra + jnp.concatenate(inter, axis=0)
        o = _rms(o) * hnorm[:, sl]
        g = gr[:, sl]
        outs.append(o * (g * _sigmoid(g)))
    return jnp.concatenate(outs, axis=1)


def _route_topk(h2, wr_hi, wr_lo, br):
    tq = h2.shape[0]
    h_hi, h_lo = _split(h2)
    logits = _dot(h_hi, wr_hi) + _dot(h_lo, wr_hi) + _dot(h_hi, wr_lo)
    lt = logits.T[0:ROUTER_ROWS] + br
    sub = lax.broadcasted_iota(I32, (SUBLANES, tq), 0).astype(F32)
    none = float(SUBLANES)

    gl = lt[0:GROUP_ROWS]
    gm = jnp.max(gl, axis=0, keepdims=True)
    gidx = jnp.min(jnp.where(gl == gm, sub, none), axis=0, keepdims=True)
    g_w = 1.0 / jnp.sum(jnp.exp(gl - gm), axis=0, keepdims=True)

    group_rows = lambda g: lt[GROUP_ROWS + EPG * g:GROUP_ROWS + EPG * (g + 1)]
    es = group_rows(0)
    for g in range(1, N_GROUPS):
        es = jnp.where(gidx == float(g), group_rows(g), es)
    m1 = jnp.max(es, axis=0, keepdims=True)
    i1 = jnp.min(jnp.where(es == m1, sub, none), axis=0, keepdims=True)
    e2 = jnp.where(sub == i1, NEG, es)
    m2 = jnp.max(e2, axis=0, keepdims=True)
    i2 = jnp.min(jnp.where(e2 == m2, sub, none), axis=0, keepdims=True)
    dd = jnp.exp(m2 - m1)
    w1 = g_w / (1.0 + dd)
    w2 = g_w * dd / (1.0 + dd)
    first_low = i1 < i2
    ea = jnp.minimum(i1, i2)
    eb = jnp.maximum(i1, i2)
    w_lo = jnp.where(first_low, w1, w2)
    w_hi = jnp.where(first_low, w2, w1)
    pair = ea * (2.0 * EPG - 1.0 - ea) * 0.5 + (eb - ea - 1.0)
    bucket = gidx * float(N_PAIRS) + pair
    return bucket, w_lo, w_hi


def _route_rank(bucket, w_lo, w_hi, carry_ref, bidx, live):
    tq = bucket.shape[1]
    brow = lax.broadcasted_iota(I32, (LANES, tq), 0).astype(F32)
    onehot = brow == bucket
    oh = jnp.where(onehot, live, 0.0)
    ti = lax.broadcasted_iota(I32, (tq, tq), 0)
    tj = lax.broadcasted_iota(I32, (tq, tq), 1)
    upper = jnp.where(ti < tj, 1.0, 0.0).astype(BF16)
    before = _dot(oh.astype(BF16), upper) + carry_ref[...]
    rank = jnp.sum(jnp.where(onehot, before, 0.0), axis=0, keepdims=True)
    carry_ref[...] = carry_ref[...] + jnp.sum(oh, axis=1, keepdims=True)

    lane_row = lax.broadcasted_iota(I32, (LANES - EXT_BATCH0, tq), 0)
    onehot_b = jnp.where(lane_row == bidx, 1.0, 0.0)
    info = jnp.concatenate([w_lo, w_hi, jnp.zeros((EXT_BATCH0 - 2, tq), F32), onehot_b], axis=0)
    return rank, info.T


def _mixer_kernel(sinks_ref, x_ref, mod_ref, ln1pre_ref, ln1post_ref, ln2pre_ref, anorm_ref, hnorm_ref,
                  lb_ref, win_ref, wout_ref, wrhi_ref, wrlo_ref, br_ref, bias_ref,
                  rows_ref, info_ref, cnt_ref,
                  kprev_ref, vprev_ref, st_ref, carry_ref, s_scr, m_scr, p_scr, u_scr, stb_scr,
                  h2_scr, proj_scr, keep_scr, *, tiles_per_seq):
    s = pl.program_id(0)
    n_tiles = pl.num_programs(0) - 1
    tq = x_ref.shape[0]
    t = lax.rem(jnp.minimum(s, n_tiles - 1), tiles_per_seq)
    bits = lambda a: pltpu.bitcast(a, U32)

    @pl.when(s == 0)
    def _():
        carry_ref[...] = jnp.zeros_like(carry_ref)
        h2_scr[...] = jnp.zeros_like(h2_scr)
        keep_scr[...] = jnp.zeros_like(keep_scr)

    @pl.when(t == 0)
    def _():
        st_ref[...] = jnp.zeros_like(st_ref)
        kprev_ref[...] = jnp.zeros_like(kprev_ref)
        vprev_ref[...] = jnp.zeros_like(vprev_ref)

    x = x_ref[...]
    mod = mod_ref[0]
    sh1, sc1, ga1, sh2, sc2 = mod[0:1], mod[1:2], mod[2:3], mod[3:4], mod[4:5]
    prev = jnp.maximum(s - 1, 0)

    @pl.when(s <= n_tiles)
    def _():
        bucket, w_lo, w_hi = _route_topk(h2_scr[...], wrhi_ref[...], wrlo_ref[...], br_ref[...])
        h = _rms(x) * ln1pre_ref[...] * (1.0 + sc1) + sh1
        proj_scr[...] = _dot(h.astype(BF16), win_ref[...])
        live = jnp.where(s >= 1, 1.0, 0.0)
        rank, ext = _route_rank(bucket, w_lo, w_hi, carry_ref, prev // tiles_per_seq, live)
        rows_ref[:, :, 0:OFF_EXT] = keep_scr[...].reshape(tq, 1, OFF_EXT)
        rows_ref[:, :, OFF_EXT:ROW_W] = bits(ext).reshape(tq, 1, EXT_W)
        info_ref[0] = jnp.concatenate([bucket, rank, jnp.zeros((6, tq), F32)], axis=0).astype(I32)
        cnt_ref[...] = jnp.broadcast_to(carry_ref[...], cnt_ref.shape)

    proj = proj_scr[...]
    scores, exps, values = _attention(proj, kprev_ref, vprev_ref, sinks_ref, bias_ref, t, s_scr, m_scr, p_scr)
    scores()
    exps()
    attn = _rms(values()) * anorm_ref[...]

    lbr = lb_ref[...]
    le = jnp.exp(lbr - jnp.max(lbr, axis=0, keepdims=True))
    lb = le[0:1] / jnp.sum(le, axis=0, keepdims=True)
    hg = _hgrn2(proj, lb, st_ref, hnorm_ref[...], u_scr, stb_scr)

    mix = _dot(jnp.concatenate([attn, hg], axis=1).astype(BF16), wout_ref[...])
    x1 = x + ga1 * (_rms(mix) * ln1post_ref[...])

    h2 = _rms(x1) * ln2pre_ref[...] * (1.0 + sc2) + sh2
    h2r = h2.astype(BF16).astype(F32)

    keep_scr[:, 0:D_MODEL] = bits(x1)
    keep_scr[:, OFF_H2P:OFF_EXT] = ((bits(h2r[:, 0:H2P_W]) >> 16)
                                    | (bits(h2r[:, H2P_W:D_MODEL]) & jnp.uint32(0xFFFF0000)))
    h2_scr[...] = h2


def _attn_bias():
    qi = np.arange(WINDOW)[:, None]
    kj = np.arange(2 * WINDOW)[None, :]
    dist = qi + WINDOW - kj
    in_win = (dist >= 0) & (dist < WINDOW)
    slopes = 2.0 ** (-8.0 * (np.arange(ATTN_HEADS) + 1.0) / ATTN_HEADS)
    b = np.where(in_win[None], -slopes[:, None, None] * dist[None] * LOG2E, NEG)
    b_first = np.where((kj >= WINDOW)[None], b, NEG)
    return jnp.asarray(np.stack([b, b_first]).astype(np.float32))


def _mixer(x2, mod3, sinks, ln1pre, ln1post, ln2pre, anorm, hnorm, lb, win, wout, wr_hi, wr_lo, br,
           bsz, seq):
    bias = _attn_bias()
    n = bsz * seq
    nt = seq // TQ
    n_tiles = bsz * nt
    n_sc = (TQ // WINDOW) * ATTN_HEADS
    cur = lambda s: jnp.minimum(s, n_tiles - 1)
    const = lambda s: (0, 0)
    full = lambda a: pl.BlockSpec(a.shape, const)
    return pl.pallas_call(
        functools.partial(_mixer_kernel, tiles_per_seq=nt),
        grid=(n_tiles + 1,),
        in_specs=[pl.BlockSpec(memory_space=pltpu.SMEM),
                  pl.BlockSpec((TQ, D_MODEL), lambda s: (cur(s), 0)),
                  pl.BlockSpec((1, 6, D_MODEL), lambda s: (cur(s) // nt, 0, 0)),
                  full(ln1pre), full(ln1post), full(ln2pre), full(anorm), full(hnorm), full(lb),
                  full(win), full(wout), full(wr_hi), full(wr_lo), full(br),
                  pl.BlockSpec(bias.shape, lambda s: (0, 0, 0, 0))],
        out_specs=[pl.BlockSpec((TQ, 1, ROW_W), lambda s: (jnp.maximum(s - 1, 0), 0, 0)),
                   pl.BlockSpec((1, 8, TQ), lambda s: (jnp.maximum(s - 1, 0), 0, 0)),
                   pl.BlockSpec((LANES, LANES), const)],
        out_shape=[jax.ShapeDtypeStruct((n, 1, ROW_W), U32),
                   jax.ShapeDtypeStruct((n // TQ, 8, TQ), I32),
                   jax.ShapeDtypeStruct((LANES, LANES), F32)],
        scratch_shapes=[pltpu.VMEM((WINDOW, KV_W), F32),
                        pltpu.VMEM((WINDOW, KV_W), F32),
                        pltpu.VMEM((HG_DIM, HG_W), F32),
                        pltpu.VMEM((LANES, 1), F32),
                        pltpu.VMEM((n_sc, WINDOW, 2 * WINDOW), F32),
                        pltpu.VMEM((n_sc, WINDOW, 1), F32),
                        pltpu.VMEM((n_sc, WINDOW, 2 * WINDOW), BF16),
                        pltpu.VMEM((TQ // HG_CHUNK, HG_DIM, HG_W), F32),
                        pltpu.VMEM((TQ // HG_CHUNK, HG_DIM, HG_W), BF16),
                        pltpu.VMEM((TQ, D_MODEL), F32),
                        pltpu.VMEM((TQ, IN_W), F32),
                        pltpu.VMEM((TQ, OFF_EXT), U32)],
        compiler_params=pltpu.CompilerParams(dimension_semantics=("arbitrary",),
                                             vmem_limit_bytes=VMEM_LIMIT),
        name="mixer",
    )(sinks, x2, mod3, ln1pre, ln1post, ln2pre, anorm, hnorm, lb, win, wout, wr_hi, wr_lo, br, bias)


PERM_STEPS = 8
PERM_ROWS = 8
PERM_UNROLL = 16


def _perm_kernel(rs_ref, cnt_ref, bucket_ref, rank_ref, perm_ref, pos_vmem, pos_smem, sem):
    pid = pl.program_id(0)
    rows, cols = pos_vmem.shape

    b = bucket_ref[0]
    start = jnp.zeros_like(b)
    for k in range(N_BUCKETS):
        start = jnp.where(b == k, rs_ref[k], start)
    pos_vmem[...] = start + rank_ref[0]
    copies = [pltpu.make_async_copy(pos_vmem.at[r], pos_smem.at[pl.ds(r * cols, cols)], sem) for r in range(rows)]
    for cp in copies:
        cp.start()

    @pl.when(pid == 0)
    def _():
        def per_bucket(k, carry):
            first = rs_ref[k]
            cnt = cnt_ref[k]

            def pad(r, c2):
                perm_ref[first + r] = 0
                return c2

            lax.fori_loop(cnt, ((cnt + TM - 1) // TM) * TM, pad, 0)
            return carry

        lax.fori_loop(0, N_BUCKETS, per_bucket, 0)

        def tail(blk, carry):
            for u in range(PERM_UNROLL):
                perm_ref[blk * PERM_UNROLL + u] = 0
            return carry

        lax.fori_loop(rs_ref[N_BUCKETS] // PERM_UNROLL, perm_ref.shape[0] // PERM_UNROLL, tail, 0)

    for cp in copies:
        cp.wait()
    base = pid * (rows * cols)

    def body(j, carry):
        i0 = j * PERM_UNROLL
        positions = [pos_smem[i0 + u] for u in range(PERM_UNROLL)]
        for u in range(PERM_UNROLL):
            perm_ref[positions[u]] = base + i0 + u
        return carry

    lax.fori_loop(0, rows * cols // PERM_UNROLL, body, 0)


def _perm(row_start, counts, bucket, rank, n_rows):
    n = bucket.shape[0]
    cols = n // (PERM_STEPS * PERM_ROWS)
    assert n % (PERM_STEPS * PERM_ROWS * PERM_UNROLL) == 0
    chunked = lambda a: a.reshape(PERM_STEPS, PERM_ROWS, cols)
    chunk_spec = pl.BlockSpec((1, PERM_ROWS, cols), lambda i: (i, 0, 0))
    return pl.pallas_call(
        _perm_kernel,
        grid=(PERM_STEPS,),
        in_specs=[pl.BlockSpec(memory_space=pltpu.SMEM),
                  pl.BlockSpec(memory_space=pltpu.SMEM),
                  chunk_spec, chunk_spec],
        out_specs=pl.BlockSpec(memory_space=pltpu.SMEM),
        out_shape=jax.ShapeDtypeStruct((n_rows,), I32),
        scratch_shapes=[pltpu.VMEM((PERM_ROWS, cols), I32), pltpu.SMEM((PERM_ROWS * cols,), I32),
                        pltpu.SemaphoreType.DMA(())],
        compiler_params=pltpu.CompilerParams(dimension_semantics=("arbitrary",)),
        name="perm",
    )(row_start, counts, chunked(bucket), chunked(rank))


GATHER_DEPTH = 3


SCHED_EXPERT, SCHED_RUN_START, SCHED_SLOT, SCHED_NEXT_EXPERT = range(4)


def _moe_kernel(nt_ref, nv_ref, sched_ref, perm_ref,
                rows_hbm, gtab_hi_ref, gtab_lo_ref, ln2post_ref, wgu_hbm, wd_hbm,
                out_hbm, xbuf, x2d, obuf, wgu_buf, wd_buf, gsem, ssem, wsem):
    i = pl.program_id(0)
    nt = nt_ref[0]
    n_steps = pl.num_programs(0)
    last_tile = n_steps - 1
    sched = lambda field, side: sched_ref[(field * 2 + side) * n_steps + i]

    def weight_copies(side, expert, slot):
        return (pltpu.make_async_copy(wgu_hbm.at[expert], wgu_buf.at[side, slot], wsem.at[side, slot]),
                pltpu.make_async_copy(wd_hbm.at[expert], wd_buf.at[side, slot], wsem.at[side, slot]))
    nbuf = xbuf.shape[0] // TM

    def start_gather(tile, pred):
        sl = lax.rem(tile, nbuf)
        base = jnp.minimum(tile, last_tile) * TM
        for r in range(TM):
            @pl.when(pred)
            def _():
                tok = perm_ref[base + r]
                pltpu.make_async_copy(rows_hbm.at[tok], xbuf.at[sl * TM + r], gsem.at[sl]).start()

    def wait_gather(tile):
        sl = lax.rem(tile, nbuf)
        pltpu.make_async_copy(rows_hbm.at[pl.ds(0, TM)], xbuf.at[pl.ds(sl * TM, TM)], gsem.at[sl]).wait()

    def wait_scatter(sl, nv):
        @pl.when(nv == TM)
        def _():
            pltpu.make_async_copy(obuf.at[sl], out_hbm.at[pl.ds(0, TM)], ssem.at[sl]).wait()

        @pl.when(nv < TM)
        def _():
            def one(r, carry):
                pltpu.make_async_copy(obuf.at[sl, pl.ds(0, 1)], out_hbm.at[pl.ds(0, 1)], ssem.at[sl]).wait()
                return carry

            lax.fori_loop(0, nv, one, 0)

    def compute(xb, slot_a, slot_b):
        x1 = pltpu.bitcast(xb[:, 0:D_MODEL], F32)
        hp = xb[:, OFF_H2P:OFF_EXT]
        h2a = pltpu.bitcast(hp << 16, F32).astype(BF16)
        h2b = pltpu.bitcast(hp & jnp.uint32(0xFFFF0000), F32).astype(BF16)
        ext = pltpu.bitcast(xb[:, OFF_EXT:ROW_W], F32)
        w_lo, w_hi = ext[:, 0:1], ext[:, 1:2]
        sel = ext.astype(BF16)
        ga2 = _dot(sel, gtab_hi_ref[...]) + _dot(sel, gtab_lo_ref[...])

        def hidden(side, slot, w):
            gu = (_dot(h2a, wgu_buf[side, slot, 0:H2P_W])
                  + _dot(h2b, wgu_buf[side, slot, H2P_W:D_MODEL]))
            hg, hu = gu[:, 0:FF], gu[:, FF:2 * FF]
            return (w * ((hg * _sigmoid(hg)) * hu)).astype(BF16)

        act = jnp.concatenate([hidden(0, slot_a, w_lo), hidden(1, slot_b, w_hi)], axis=1)
        wd = jnp.concatenate([wd_buf[0, slot_a], wd_buf[1, slot_b]], axis=0)
        y = _dot(act, wd)
        return x1 + ga2 * (_rms(y) * ln2post_ref[...])

    @pl.when(i == 0)
    def _():
        for side in range(2):
            for cp in weight_copies(side, sched(SCHED_EXPERT, side), 0):
                cp.start()
        for d in range(GATHER_DEPTH):
            start_gather(d, d < nt)

    @pl.when(i < nt)
    def _():
        nv = nv_ref[i]
        osl = lax.rem(i, 2)
        slots = []
        for side in range(2):
            slot = sched(SCHED_SLOT, side)
            slots.append(slot)

            @pl.when(sched(SCHED_RUN_START, side) == 1)
            def _():
                for cp in weight_copies(side, sched(SCHED_EXPERT, side), slot):
                    cp.wait()
                nxt = sched(SCHED_NEXT_EXPERT, side)

                @pl.when(nxt >= 0)
                def _():
                    for cp in weight_copies(side, nxt, 1 - slot):
                        cp.start()

        wait_gather(i)

        @pl.when(i >= 2)
        def _():
            wait_scatter(osl, nv_ref[jnp.maximum(i - 2, 0)])

        start_gather(i + GATHER_DEPTH, i + GATHER_DEPTH < nt)
        x2d[...] = xbuf[pl.ds(lax.rem(i, nbuf) * TM, TM)].reshape(TM, ROW_W)
        result = compute(x2d[...], *slots)

        def scatter_row(k, r):
            tok = perm_ref[i * TM + r]
            pltpu.make_async_copy(obuf.at[k, pl.ds(r, 1)], out_hbm.at[pl.ds(tok, 1)], ssem.at[k]).start()

        for k in range(2):
            @pl.when((osl == k) & (nv == TM))
            def _():
                obuf[k] = result
                for r in range(TM):
                    scatter_row(k, r)

            @pl.when((osl == k) & (nv < TM))
            def _():
                obuf[k] = result
                for r in range(TM):
                    pl.when(r < nv)(functools.partial(scatter_row, k, r))

        @pl.when(i == nt - 1)
        def _():
            wait_scatter(osl, nv)

            @pl.when(i >= 1)
            def _():
                wait_scatter(1 - osl, nv_ref[jnp.maximum(i - 1, 0)])


def _moe(rows, gtab_hi, gtab_lo, ln2post, wgu, wd, nt, nv, sched, perm, n_tiles):
    n = rows.shape[0]
    const2 = lambda i, *_: (0, 0)
    grid_spec = pltpu.PrefetchScalarGridSpec(
        num_scalar_prefetch=4,
        grid=(n_tiles,),
        in_specs=[pl.BlockSpec(memory_space=pl.ANY),
                  pl.BlockSpec(gtab_hi.shape, const2),
                  pl.BlockSpec(gtab_lo.shape, const2),
                  pl.BlockSpec(ln2post.shape, const2),
                  pl.BlockSpec(memory_space=pl.ANY),
                  pl.BlockSpec(memory_space=pl.ANY)],
        out_specs=pl.BlockSpec(memory_space=pl.ANY),
        scratch_shapes=[pltpu.VMEM(((GATHER_DEPTH + 1) * TM, 1, ROW_W), U32),
                        pltpu.VMEM((TM, ROW_W), U32),
                        pltpu.VMEM((2, TM, D_MODEL), F32),
                        pltpu.VMEM((2, 2, D_MODEL, 2 * FF), BF16),
                        pltpu.VMEM((2, 2, FF, D_MODEL), BF16),
                        pltpu.SemaphoreType.DMA((GATHER_DEPTH + 1,)),
                        pltpu.SemaphoreType.DMA((2,)),
                        pltpu.SemaphoreType.DMA((2, 2))],
    )
    return pl.pallas_call(
        _moe_kernel,
        grid_spec=grid_spec,
        out_shape=jax.ShapeDtypeStruct((n, D_MODEL), F32),
        compiler_params=pltpu.CompilerParams(dimension_semantics=("arbitrary",),
                                             vmem_limit_bytes=VMEM_LIMIT),
        name="moe",
    )(nt, nv, sched, perm, rows, gtab_hi, gtab_lo, ln2post, wgu, wd)


def _weight_schedule(ea, eb):
    n_tiles = ea.shape[0]
    tid = jnp.arange(n_tiles, dtype=I32)
    fields = [[], [], [], []]
    for e in (ea, eb):
        start = jnp.concatenate([jnp.ones((1,), I32), (e[1:] != e[:-1]).astype(I32)])
        slot = (jnp.cumsum(start) - 1) % 2
        start_idx = jnp.where(start == 1, tid, n_tiles)
        next_start = jnp.concatenate([lax.cummin(start_idx, reverse=True)[1:], jnp.full((1,), n_tiles, I32)])
        nxt = jnp.where(next_start < n_tiles, e[jnp.minimum(next_start, n_tiles - 1)], -1)
        for f, v in zip(fields, (e, start, slot, nxt)):
            f.append(v.astype(I32))
    return jnp.concatenate([v for f in fields for v in f])


def kernel(x, c, ln1_pre, ln1_post, ln2_pre, ln2_post, w_ada, b_ada, w_in, attn_sinks, attn_out_norm,
           hgrn_lb, hgrn_out_norm, w_out, w_router_group, b_router_group, w_router_expert,
           b_router_expert, w_exp_gate, w_exp_up, w_exp_down):
    bsz, seq, d = x.shape
    assert d == D_MODEL and seq % TQ == 0 and w_ada.shape[0] == 1 and hgrn_lb.shape[0] == 2
    assert bsz <= LANES - EXT_BATCH0 and (bsz * seq) % TM == 0
    n = bsz * seq

    mod = _ada(c, w_ada[0], b_ada[0])
    mod3 = mod.reshape(bsz, 6, d)

    wr = jnp.concatenate([w_router_group[0], jnp.zeros((d, GROUP_ROWS - N_GROUPS), F32), w_router_expert[0],
                          jnp.zeros((d, LANES - ROUTER_ROWS), F32)], axis=1)
    br = jnp.concatenate([b_router_group[0], jnp.full((GROUP_ROWS - N_GROUPS,), NEG, F32), b_router_expert[0]])
    wr_hi = wr.astype(BF16)
    wr_lo = (wr - wr_hi.astype(F32)).astype(BF16)

    x1ext, info, cnt = _mixer(
        x.reshape(n, d), mod3, attn_sinks[0], ln1_pre, ln1_post, ln2_pre, attn_out_norm, hgrn_out_norm,
        hgrn_lb, w_in[0].astype(BF16), w_out[0].astype(BF16), wr_hi, wr_lo, br.reshape(ROUTER_ROWS, 1),
        bsz, seq)

    n_tiles = n // TM + N_BUCKETS
    counts = cnt[:N_BUCKETS, 0].astype(I32)
    tiles_per = (counts + TM - 1) // TM
    tile_end = jnp.cumsum(tiles_per)
    tile_start = tile_end - tiles_per
    nt = tile_end[-1]
    bucket = info[:, 0, :].reshape(n)
    rank = info[:, 1, :].reshape(n)
    tid = jnp.arange(n_tiles, dtype=I32)[None, :]
    member = (tid >= tile_start[:, None]) & (tid < tile_end[:, None])
    pick = lambda per_bucket: jnp.sum(jnp.where(member, per_bucket, 0), axis=0).astype(I32)
    bidx = np.arange(N_BUCKETS, dtype=np.int32)
    ea_of = jnp.asarray((bidx // N_PAIRS) * EPG + _PAIR_A[bidx % N_PAIRS])[:, None]
    eb_of = jnp.asarray((bidx // N_PAIRS) * EPG + _PAIR_B[bidx % N_PAIRS])[:, None]
    last_used = jnp.arange(N_BUCKETS)[:, None] == jnp.max(jnp.where(tiles_per > 0, jnp.arange(N_BUCKETS), 0))
    unused = tid[0] >= nt
    nv = pick(jnp.clip(counts[:, None] - (tid - tile_start[:, None]) * TM, 0, TM))
    ea = jnp.where(unused, jnp.sum(jnp.where(last_used, ea_of, 0)), pick(ea_of)).astype(I32)
    eb = jnp.where(unused, jnp.sum(jnp.where(last_used, eb_of, 0)), pick(eb_of)).astype(I32)

    pad128 = lambda a: jnp.concatenate([a, jnp.zeros((LANES - a.shape[0],), I32)])
    row_start = jnp.concatenate([tile_start, nt.reshape(1)]) * TM
    perm = _perm(pad128(row_start), pad128(counts), bucket, rank, n_tiles * TM)

    wgu = jnp.concatenate([w_exp_gate[0], w_exp_up[0]], axis=-1).astype(BF16)
    wd = w_exp_down[0].astype(BF16)
    gtab = jnp.zeros((LANES, d), F32).at[EXT_BATCH0:EXT_BATCH0 + bsz].set(mod3[:, 5, :])
    gtab_hi = gtab.astype(BF16)
    gtab_lo = (gtab - gtab_hi.astype(F32)).astype(BF16)
    out = _moe(x1ext, gtab_hi, gtab_lo, ln2_post, wgu, wd, nt.reshape(1), nv, _weight_schedule(ea, eb), perm, n_tiles)
    return out.reshape(bsz, seq, d)
```

```python
import functools

import numpy as np
import jax
import jax.numpy as jnp
from jax import lax
from jax.experimental import pallas as pl
from jax.experimental.pallas import tpu as pltpu

F32 = jnp.float32
BF16 = jnp.bfloat16
I32 = jnp.int32

D_MODEL = 1024
ATTN_HEADS = 8
HEAD_DIM = 64
WINDOW = 128
ATTN_W = 512
KV_W = 128
HG_HEADS = 4
HG_DIM = 128
HG_W = 512
HG_CHUNK = 32
IN_W = 2816
N_GROUPS = 4
EPG = 8
N_EXPERTS = 32
FF = 256
N_PAIRS = EPG * (EPG - 1) // 2
N_BUCKETS = N_GROUPS * N_PAIRS
EPS = 1e-6
NEG = -1e30
LOG2E = 1.4426950408889634

LANES = 128
H2P_W = D_MODEL // 2
EXT_W = LANES
OFF_H2P = D_MODEL
OFF_EXT = D_MODEL + H2P_W
ROW_W = OFF_EXT + EXT_W
EXT_BATCH0 = 8
U32 = jnp.uint32
SUBLANES = 8
GROUP_ROWS = SUBLANES
ROUTER_ROWS = GROUP_ROWS + N_EXPERTS

TQ = 256
TM = 128
VMEM_LIMIT = 56 * 1024 * 1024

OFF_Q, OFF_K, OFF_V, OFF_HQ, OFF_HF, OFF_HI, OFF_HG = 0, 512, 640, 768, 1280, 1792, 2304

_PAIR_A = np.array([a for a in range(EPG) for b in range(a + 1, EPG)], np.int32)
_PAIR_B = np.array([b for a in range(EPG) for b in range(a + 1, EPG)], np.int32)


def _dot(a, b):
    return jnp.dot(a, b, preferred_element_type=F32)


def _dot_nt(a, b):
    return lax.dot_general(a, b, (((1,), (1,)), ((), ())), preferred_element_type=F32)


def _dot_tn(a, b):
    return lax.dot_general(a, b, (((0,), (0,)), ((), ())), preferred_element_type=F32)


def _split(a):
    hi = a.astype(BF16)
    lo = (a - hi.astype(F32)).astype(BF16)
    return hi, lo


def _rms(x):
    return x * lax.rsqrt(jnp.mean(x * x, axis=-1, keepdims=True) + EPS)


def _sigmoid(x):
    return 0.5 * jnp.tanh(0.5 * x) + 0.5


def _ada_kernel(c_ref, w_ref, b_ref, o_ref):
    c = c_ref[...]
    ca = c * _sigmoid(c)
    c_hi, c_lo = _split(ca)
    w_hi, w_lo = _split(w_ref[...])
    o_ref[...] = _dot(c_hi, w_hi) + _dot(c_lo, w_hi) + _dot(c_hi, w_lo) + b_ref[...]


def _ada(c, w, b):
    bsz, d = c.shape
    n_out = w.shape[1]
    return pl.pallas_call(
        _ada_kernel,
        grid=(n_out // d,),
        in_specs=[pl.BlockSpec((bsz, d), lambda j: (0, 0)),
                  pl.BlockSpec((d, d), lambda j: (0, j)),
                  pl.BlockSpec((1, d), lambda j: (0, j))],
        out_specs=pl.BlockSpec((bsz, d), lambda j: (0, j)),
        out_shape=jax.ShapeDtypeStruct((bsz, n_out), F32),
        compiler_params=pltpu.CompilerParams(dimension_semantics=("arbitrary",),
                                             vmem_limit_bytes=VMEM_LIMIT),
        name="adaln",
    )(c, w, b.reshape(1, n_out))


def _attention(proj, kprev_ref, vprev_ref, sinks_ref, bias_ref, t, s_scr, m_scr, p_scr):
    tq = proj.shape[0]
    q = (proj[:, OFF_Q:OFF_Q + ATTN_W] * (HEAD_DIM ** -0.5 * LOG2E)).astype(BF16)
    kf = jnp.concatenate([kprev_ref[...], proj[:, OFF_K:OFF_K + KV_W]], axis=0)
    vf = jnp.concatenate([vprev_ref[...], proj[:, OFF_V:OFF_V + KV_W]], axis=0)
    kprev_ref[...] = proj[tq - WINDOW:, OFF_K:OFF_K + KV_W]
    vprev_ref[...] = proj[tq - WINDOW:, OFF_V:OFF_V + KV_W]

    lo = lax.broadcasted_iota(I32, kf.shape, 1) < HEAD_DIM
    kr = pltpu.roll(kf, HEAD_DIM, axis=1)
    vr = pltpu.roll(vf, HEAD_DIM, axis=1)

    def variants(a, ar):
        return [[jnp.where(lo, a, 0.0).astype(BF16), jnp.where(lo, 0.0, ar).astype(BF16)],
                [jnp.where(lo, ar, 0.0).astype(BF16), jnp.where(lo, 0.0, a).astype(BF16)]]

    kvar = variants(kf, kr)
    vvar = variants(vf, vr)

    first = jnp.where(t > 0, 0, 1)

    nblk = tq // WINDOW
    idx = lambda j, h: j * ATTN_HEADS + h
    keys = lambda a, j: a[j * WINDOW:(j + 2) * WINDOW]

    def scores():
        for j in range(nblk):
            for h in range(ATTN_HEADS):
                p, par = h // 2, h % 2
                qp = q[j * WINDOW:(j + 1) * WINDOW, p * LANES:(p + 1) * LANES]
                s = _dot_nt(qp, keys(kvar[p // 2][par], j)) + bias_ref[first if j == 0 else 0, h]
                s_scr[idx(j, h)] = s
                m_scr[idx(j, h)] = jnp.maximum(jnp.max(s, axis=-1, keepdims=True), sinks_ref[h] * LOG2E)

    def exps():
        for j in range(nblk):
            for h in range(ATTN_HEADS):
                m = m_scr[idx(j, h)]
                pe = jnp.exp2(s_scr[idx(j, h)] - m)
                p_scr[idx(j, h)] = pe.astype(BF16)
                m_scr[idx(j, h)] = 1.0 / (jnp.sum(pe, axis=-1, keepdims=True) + jnp.exp2(sinks_ref[h] * LOG2E - m))

    def values():
        blocks = []
        for j in range(nblk):
            pairs = []
            for p in range(ATTN_HEADS // 2):
                acc = None
                for par in range(2):
                    h = 2 * p + par
                    o = _dot(p_scr[idx(j, h)], keys(vvar[p // 2][par], j)) * m_scr[idx(j, h)]
                    acc = o if acc is None else acc + o
                pairs.append(acc)
            blocks.append(jnp.concatenate(pairs, axis=1))
        return jnp.concatenate(blocks, axis=0)

    return scores, exps, values


def _hgrn2(proj, lb, st_ref, hnorm, u_scr, stb_scr):
    tq = proj.shape[0]
    nc = tq // HG_CHUNK
    qr = proj[:, OFF_HQ:OFF_HQ + HG_W]
    fr = proj[:, OFF_HF:OFF_HF + HG_W]
    iv = proj[:, OFF_HI:OFF_HI + HG_W]
    gr = proj[:, OFF_HG:OFF_HG + HG_W]
    qh = qr * _sigmoid(qr)
    f = lb + (1.0 - lb) * _sigmoid(fr)
    kk = 1.0 - f
    logf = jnp.log(f)

    rmod = lax.broadcasted_iota(I32, (tq, HG_W), 0) & (HG_CHUNK - 1)
    bc = logf
    s = 1
    while s < HG_CHUNK:
        bc = bc + jnp.where(rmod >= s, pltpu.roll(bc, s, axis=0), 0.0)
        s *= 2

    b3 = bc.reshape(nc, HG_CHUNK, HG_W)
    blast = b3[:, HG_CHUNK - 1:HG_CHUNK, :]
    kend = (kk.reshape(nc, HG_CHUNK, HG_W) * jnp.exp(blast - b3)).reshape(tq, HG_W)
    decay = jnp.exp(blast).reshape(nc, HG_W)
    qdec = (qh * jnp.exp(bc)).astype(BF16)
    kdec = (kk * jnp.exp(-bc)).astype(BF16)
    kend = kend.astype(BF16)
    ivb = iv.astype(BF16)

    ri = lax.broadcasted_iota(I32, (tq, tq), 0)
    ci = lax.broadcasted_iota(I32, (tq, tq), 1)
    cmask = ((ri // HG_CHUNK) == (ci // HG_CHUNK)) & (ri >= ci)

    heads = [slice(hh * HG_DIM, (hh + 1) * HG_DIM) for hh in range(HG_HEADS)]
    chunks = [slice(n * HG_CHUNK, (n + 1) * HG_CHUNK) for n in range(nc)]

    lane_head = lax.broadcasted_iota(I32, (HG_CHUNK, HG_W), 1) // HG_DIM
    for n, rs in enumerate(chunks):
        vstack = jnp.concatenate([ivb[rs, sl] for sl in heads], axis=0)
        kblk = jnp.concatenate([jnp.where(lane_head == hh, kend[rs], 0.0).astype(BF16)
                                for hh in range(HG_HEADS)], axis=0)
        u_scr[n] = _dot_tn(vstack, kblk)

    st = st_ref[...]
    for n in range(nc):
        stb_scr[n] = st.astype(BF16)
        st = st * decay[n:n + 1] + u_scr[n]
    st_ref[...] = st

    outs = []
    for hh, sl in enumerate(heads):
        a = _dot_nt(qdec[:, sl], kdec[:, sl])
        a = jnp.where(cmask, a, 0.0).astype(BF16)
        o_intra = _dot(a, ivb[:, sl])
        inter = [_dot_nt(qdec[rs, sl], stb_scr[n, :, sl]) for n, rs in enumerate(chunks)]
        o = o_intra + jnp.concatenate(inter, axis=0)
        o = _rms(o) * hnorm[:, sl]
        g = gr[:, sl]
        outs.append(o * (g * _sigmoid(g)))
    return jnp.concatenate(outs, axis=1)


def _route_topk(h2, wr_hi, wr_lo, br):
    tq = h2.shape[0]
    h_hi, h_lo = _split(h2)
    logits = _dot(h_hi, wr_hi) + _dot(h_lo, wr_hi) + _dot(h_hi, wr_lo)
    lt = logits.T[0:ROUTER_ROWS] + br
    sub = lax.broadcasted_iota(I32, (SUBLANES, tq), 0).astype(F32)
    none = float(SUBLANES)

    gl = lt[0:GROUP_ROWS]
    gm = jnp.max(gl, axis=0, keepdims=True)
    gidx = jnp.min(jnp.where(gl == gm, sub, none), axis=0, keepdims=True)
    g_w = 1.0 / jnp.sum(jnp.exp(gl - gm), axis=0, keepdims=True)

    group_rows = lambda g: lt[GROUP_ROWS + EPG * g:GROUP_ROWS + EPG * (g + 1)]
    es = group_rows(0)
    for g in range(1, N_GROUPS):
        es = jnp.where(gidx == float(g), group_rows(g), es)
    m1 = jnp.max(es, axis=0, keepdims=True)
    i1 = jnp.min(jnp.where(es == m1, sub, none), axis=0, keepdims=True)
    e2 = jnp.where(sub == i1, NEG, es)
    m2 = jnp.max(e2, axis=0, keepdims=True)
    i2 = jnp.min(jnp.where(e2 == m2, sub, none), axis=0, keepdims=True)
    dd = jnp.exp(m2 - m1)
    w1 = g_w / (1.0 + dd)
    w2 = g_w * dd / (1.0 + dd)
    first_low = i1 < i2
    ea = jnp.minimum(i1, i2)
    eb = jnp.maximum(i1, i2)
    w_lo = jnp.where(first_low, w1, w2)
    w_hi = jnp.where(first_low, w2, w1)
    pair = ea * (2.0 * EPG - 1.0 - ea) * 0.5 + (eb - ea - 1.0)
    bucket = gidx * float(N_PAIRS) + pair
    return bucket, w_lo, w_hi


def _route_rank(bucket, w_lo, w_hi, carry_ref, bidx, live):
    tq = bucket.shape[1]
    brow = lax.broadcasted_iota(I32, (LANES, tq), 0).astype(F32)
    onehot = brow == bucket
    oh = jnp.where(onehot, live, 0.0)
    ti = lax.broadcasted_iota(I32, (tq, tq), 0)
    tj = lax.broadcasted_iota(I32, (tq, tq), 1)
    upper = jnp.where(ti < tj, 1.0, 0.0).astype(BF16)
    before = _dot(oh.astype(BF16), upper) + carry_ref[...]
    rank = jnp.sum(jnp.where(onehot, before, 0.0), axis=0, keepdims=True)
    carry_ref[...] = carry_ref[...] + jnp.sum(oh, axis=1, keepdims=True)

    lane_row = lax.broadcasted_iota(I32, (LANES - EXT_BATCH0, tq), 0)
    onehot_b = jnp.where(lane_row == bidx, 1.0, 0.0)
    info = jnp.concatenate([w_lo, w_hi, jnp.zeros((EXT_BATCH0 - 2, tq), F32), onehot_b], axis=0)
    return rank, info.T


def _mixer_kernel(sinks_ref, x_ref, xp_ref, mod_ref, modp_ref, ln1pre_ref, ln1post_ref, ln2pre_ref, anorm_ref,
                  hnorm_ref, lb_ref, win_ref, wout_ref, wrhi_ref, wrlo_ref, br_ref, bias_ref,
                  rows_ref, info_ref, cnt_ref,
                  kprev_ref, vprev_ref, st_ref, carry_ref, s_scr, m_scr, p_scr, u_scr, stb_scr,
                  proj_scr, mix_scr, *, tiles_per_seq):
    s = pl.program_id(0)
    n_tiles = pl.num_programs(0) - 1
    tq = x_ref.shape[0]
    t = lax.rem(jnp.minimum(s, n_tiles - 1), tiles_per_seq)
    bits = lambda a: pltpu.bitcast(a, U32)

    @pl.when(s == 0)
    def _():
        carry_ref[...] = jnp.zeros_like(carry_ref)
        mix_scr[...] = jnp.zeros_like(mix_scr)

    @pl.when(t == 0)
    def _():
        st_ref[...] = jnp.zeros_like(st_ref)
        kprev_ref[...] = jnp.zeros_like(kprev_ref)
        vprev_ref[...] = jnp.zeros_like(vprev_ref)

    prev = jnp.maximum(s - 1, 0)

    @pl.when(s <= n_tiles)
    def _():
        mod = mod_ref[0]
        h = _rms(x_ref[...]) * ln1pre_ref[...] * (1.0 + mod[1:2]) + mod[0:1]
        proj_scr[...] = _dot(h.astype(BF16), win_ref[...])

        modp = modp_ref[0]
        ga1, sh2, sc2 = modp[2:3], modp[3:4], modp[4:5]
        x1 = xp_ref[...] + ga1 * (_rms(mix_scr[...]) * ln1post_ref[...])
        h2 = _rms(x1) * ln2pre_ref[...] * (1.0 + sc2) + sh2
        h2r = h2.astype(BF16).astype(F32)
        packed = (bits(h2r[:, 0:H2P_W]) >> 16) | (bits(h2r[:, H2P_W:D_MODEL]) & jnp.uint32(0xFFFF0000))
        bucket, w_lo, w_hi = _route_topk(h2, wrhi_ref[...], wrlo_ref[...], br_ref[...])
        live = jnp.where(s >= 1, 1.0, 0.0)
        rank, ext = _route_rank(bucket, w_lo, w_hi, carry_ref, prev // tiles_per_seq, live)
        rows_ref[:, :, 0:D_MODEL] = bits(x1).reshape(tq, 1, D_MODEL)
        rows_ref[:, :, OFF_H2P:OFF_EXT] = packed.reshape(tq, 1, H2P_W)
        rows_ref[:, :, OFF_EXT:ROW_W] = bits(ext).reshape(tq, 1, EXT_W)
        info_ref[0] = jnp.concatenate([bucket, rank, jnp.zeros((6, tq), F32)], axis=0).astype(I32)
        cnt_ref[...] = jnp.broadcast_to(carry_ref[...], cnt_ref.shape)

    proj = proj_scr[...]
    scores, exps, values = _attention(proj, kprev_ref, vprev_ref, sinks_ref, bias_ref, t, s_scr, m_scr, p_scr)
    scores()
    exps()
    attn = _rms(values()) * anorm_ref[...]

    lbr = lb_ref[...]
    le = jnp.exp(lbr - jnp.max(lbr, axis=0, keepdims=True))
    lb = le[0:1] / jnp.sum(le, axis=0, keepdims=True)
    hg = _hgrn2(proj, lb, st_ref, hnorm_ref[...], u_scr, stb_scr)

    mix_scr[...] = _dot(jnp.concatenate([attn, hg], axis=1).astype(BF16), wout_ref[...])


def _attn_bias():
    qi = np.arange(WINDOW)[:, None]
    kj = np.arange(2 * WINDOW)[None, :]
    dist = qi + WINDOW - kj
    in_win = (dist >= 0) & (dist < WINDOW)
    slopes = 2.0 ** (-8.0 * (np.arange(ATTN_HEADS) + 1.0) / ATTN_HEADS)
    b = np.where(in_win[None], -slopes[:, None, None] * dist[None] * LOG2E, NEG)
    b_first = np.where((kj >= WINDOW)[None], b, NEG)
    return jnp.asarray(np.stack([b, b_first]).astype(np.float32))


def _mixer(x2, mod3, sinks, ln1pre, ln1post, ln2pre, anorm, hnorm, lb, win, wout, wr_hi, wr_lo, br,
           bsz, seq):
    bias = _attn_bias()
    n = bsz * seq
    nt = seq // TQ
    n_tiles = bsz * nt
    n_sc = (TQ // WINDOW) * ATTN_HEADS
    cur = lambda s: jnp.minimum(s, n_tiles - 1)
    prv = lambda s: jnp.maximum(s - 1, 0)
    const = lambda s: (0, 0)
    full = lambda a: pl.BlockSpec(a.shape, const)
    return pl.pallas_call(
        functools.partial(_mixer_kernel, tiles_per_seq=nt),
        grid=(n_tiles + 1,),
        in_specs=[pl.BlockSpec(memory_space=pltpu.SMEM),
                  pl.BlockSpec((TQ, D_MODEL), lambda s: (cur(s), 0)),
                  pl.BlockSpec((TQ, D_MODEL), lambda s: (prv(s), 0)),
                  pl.BlockSpec((1, 6, D_MODEL), lambda s: (cur(s) // nt, 0, 0)),
                  pl.BlockSpec((1, 6, D_MODEL), lambda s: (prv(s) // nt, 0, 0)),
                  full(ln1pre), full(ln1post), full(ln2pre), full(anorm), full(hnorm), full(lb),
                  full(win), full(wout), full(wr_hi), full(wr_lo), full(br),
                  pl.BlockSpec(bias.shape, lambda s: (0, 0, 0, 0))],
        out_specs=[pl.BlockSpec((TQ, 1, ROW_W), lambda s: (prv(s), 0, 0)),
                   pl.BlockSpec((1, 8, TQ), lambda s: (prv(s), 0, 0)),
                   pl.BlockSpec((LANES, LANES), const)],
        out_shape=[jax.ShapeDtypeStruct((n, 1, ROW_W), U32),
                   jax.ShapeDtypeStruct((n // TQ, 8, TQ), I32),
                   jax.ShapeDtypeStruct((LANES, LANES), F32)],
        scratch_shapes=[pltpu.VMEM((WINDOW, KV_W), F32),
                        pltpu.VMEM((WINDOW, KV_W), F32),
                        pltpu.VMEM((HG_DIM, HG_W), F32),
                        pltpu.VMEM((LANES, 1), F32),
                        pltpu.VMEM((n_sc, WINDOW, 2 * WINDOW), F32),
                        pltpu.VMEM((n_sc, WINDOW, 1), F32),
                        pltpu.VMEM((n_sc, WINDOW, 2 * WINDOW), BF16),
                        pltpu.VMEM((TQ // HG_CHUNK, HG_DIM, HG_W), F32),
                        pltpu.VMEM((TQ // HG_CHUNK, HG_DIM, HG_W), BF16),
                        pltpu.VMEM((TQ, IN_W), F32),
                        pltpu.VMEM((TQ, D_MODEL), F32)],
        compiler_params=pltpu.CompilerParams(dimension_semantics=("arbitrary",),
                                             vmem_limit_bytes=VMEM_LIMIT),
        name="mixer",
    )(sinks, x2, x2, mod3, mod3, ln1pre, ln1post, ln2pre, anorm, hnorm, lb, win, wout, wr_hi, wr_lo, br, bias)


PERM_STEPS = 8
PERM_ROWS = 8
PERM_UNROLL = 16


def _perm_kernel(rs_ref, cnt_ref, bucket_ref, rank_ref, perm_ref, pos_vmem, pos_smem, sem):
    pid = pl.program_id(0)
    rows, cols = pos_vmem.shape

    b = bucket_ref[0]
    start = jnp.zeros_like(b)
    for k in range(N_BUCKETS):
        start = jnp.where(b == k, rs_ref[k], start)
    pos_vmem[...] = start + rank_ref[0]
    copies = [pltpu.make_async_copy(pos_vmem.at[r], pos_smem.at[pl.ds(r * cols, cols)], sem) for r in range(rows)]
    for cp in copies:
        cp.start()

    @pl.when(pid == 0)
    def _():
        def per_bucket(k, carry):
            first = rs_ref[k]
            cnt = cnt_ref[k]

            def pad(r, c2):
                perm_ref[first + r] = 0
                return c2

            lax.fori_loop(cnt, ((cnt + TM - 1) // TM) * TM, pad, 0)
            return carry

        lax.fori_loop(0, N_BUCKETS, per_bucket, 0)

        def tail(blk, carry):
            for u in range(PERM_UNROLL):
                perm_ref[blk * PERM_UNROLL + u] = 0
            return carry

        lax.fori_loop(rs_ref[N_BUCKETS] // PERM_UNROLL, perm_ref.shape[0] // PERM_UNROLL, tail, 0)

    for cp in copies:
        cp.wait()
    base = pid * (rows * cols)

    def body(j, carry):
        i0 = j * PERM_UNROLL
        positions = [pos_smem[i0 + u] for u in range(PERM_UNROLL)]
        for u in range(PERM_UNROLL):
            perm_ref[positions[u]] = base + i0 + u
        return carry

    lax.fori_loop(0, rows * cols // PERM_UNROLL, body, 0)


def _perm(row_start, counts, bucket, rank, n_rows):
    n = bucket.shape[0]
    cols = n // (PERM_STEPS * PERM_ROWS)
    assert n % (PERM_STEPS * PERM_ROWS * PERM_UNROLL) == 0
    chunked = lambda a: a.reshape(PERM_STEPS, PERM_ROWS, cols)
    chunk_spec = pl.BlockSpec((1, PERM_ROWS, cols), lambda i: (i, 0, 0))
    return pl.pallas_call(
        _perm_kernel,
        grid=(PERM_STEPS,),
        in_specs=[pl.BlockSpec(memory_space=pltpu.SMEM),
                  pl.BlockSpec(memory_space=pltpu.SMEM),
                  chunk_spec, chunk_spec],
        out_specs=pl.BlockSpec(memory_space=pltpu.SMEM),
        out_shape=jax.ShapeDtypeStruct((n_rows,), I32),
        scratch_shapes=[pltpu.VMEM((PERM_ROWS, cols), I32), pltpu.SMEM((PERM_ROWS * cols,), I32),
                        pltpu.SemaphoreType.DMA(())],
        compiler_params=pltpu.CompilerParams(dimension_semantics=("arbitrary",)),
        name="perm",
    )(row_start, counts, chunked(bucket), chunked(rank))


GATHER_DEPTH = 3


SCHED_EXPERT, SCHED_RUN_START, SCHED_SLOT, SCHED_NEXT_EXPERT = range(4)


def _moe_kernel(nt_ref, nv_ref, sched_ref, perm_ref,
                rows_hbm, gtab_hi_ref, gtab_lo_ref, ln2post_ref, wgu_hbm, wd_hbm,
                out_hbm, xbuf, x2d, obuf, wgu_buf, wd_buf, gsem, ssem, wsem):
    i = pl.program_id(0)
    nt = nt_ref[0]
    n_steps = pl.num_programs(0)
    last_tile = n_steps - 1
    sched = lambda field, side: sched_ref[(field * 2 + side) * n_steps + i]

    def weight_copies(side, expert, slot):
        return (pltpu.make_async_copy(wgu_hbm.at[expert], wgu_buf.at[side, slot], wsem.at[side, slot]),
                pltpu.make_async_copy(wd_hbm.at[expert], wd_buf.at[side, slot], wsem.at[side, slot]))
    nbuf = xbuf.shape[0] // TM

    def start_gather(tile, pred):
        sl = lax.rem(tile, nbuf)
        base = jnp.minimum(tile, last_tile) * TM
        for r in range(TM):
            @pl.when(pred)
            def _():
                tok = perm_ref[base + r]
                pltpu.make_async_copy(rows_hbm.at[tok], xbuf.at[sl * TM + r], gsem.at[sl]).start()

    def wait_gather(tile):
        sl = lax.rem(tile, nbuf)
        pltpu.make_async_copy(rows_hbm.at[pl.ds(0, TM)], xbuf.at[pl.ds(sl * TM, TM)], gsem.at[sl]).wait()

    def wait_scatter(sl, nv):
        @pl.when(nv == TM)
        def _():
            pltpu.make_async_copy(obuf.at[sl], out_hbm.at[pl.ds(0, TM)], ssem.at[sl]).wait()

        @pl.when(nv < TM)
        def _():
            def one(r, carry):
                pltpu.make_async_copy(obuf.at[sl, pl.ds(0, 1)], out_hbm.at[pl.ds(0, 1)], ssem.at[sl]).wait()
                return carry

            lax.fori_loop(0, nv, one, 0)

    def compute(xb, slot_a, slot_b):
        x1 = pltpu.bitcast(xb[:, 0:D_MODEL], F32)
        hp = xb[:, OFF_H2P:OFF_EXT]
        h2a = pltpu.bitcast(hp << 16, F32).astype(BF16)
        h2b = pltpu.bitcast(hp & jnp.uint32(0xFFFF0000), F32).astype(BF16)
        ext = pltpu.bitcast(xb[:, OFF_EXT:ROW_W], F32)
        w_lo, w_hi = ext[:, 0:1], ext[:, 1:2]
        sel = ext.astype(BF16)
        ga2 = _dot(sel, gtab_hi_ref[...]) + _dot(sel, gtab_lo_ref[...])

        def hidden(side, slot, w):
            gu = (_dot(h2a, wgu_buf[side, slot, 0:H2P_W])
                  + _dot(h2b, wgu_buf[side, slot, H2P_W:D_MODEL]))
            hg, hu = gu[:, 0:FF], gu[:, FF:2 * FF]
            return (w * ((hg * _sigmoid(hg)) * hu)).astype(BF16)

        act = jnp.concatenate([hidden(0, slot_a, w_lo), hidden(1, slot_b, w_hi)], axis=1)
        wd = jnp.concatenate([wd_buf[0, slot_a], wd_buf[1, slot_b]], axis=0)
        y = _dot(act, wd)
        return x1 + ga2 * (_rms(y) * ln2post_ref[...])

    @pl.when(i == 0)
    def _():
        for side in range(2):
            for cp in weight_copies(side, sched(SCHED_EXPERT, side), 0):
                cp.start()
        for d in range(GATHER_DEPTH):
            start_gather(d, d < nt)

    @pl.when(i < nt)
    def _():
        nv = nv_ref[i]
        osl = lax.rem(i, 2)
        slots = []
        for side in range(2):
            slot = sched(SCHED_SLOT, side)
            slots.append(slot)

            @pl.when(sched(SCHED_RUN_START, side) == 1)
            def _():
                for cp in weight_copies(side, sched(SCHED_EXPERT, side), slot):
                    cp.wait()
                nxt = sched(SCHED_NEXT_EXPERT, side)

                @pl.when(nxt >= 0)
                def _():
                    for cp in weight_copies(side, nxt, 1 - slot):
                        cp.start()

        wait_gather(i)

        @pl.when(i >= 2)
        def _():
            wait_scatter(osl, nv_ref[jnp.maximum(i - 2, 0)])

        start_gather(i + GATHER_DEPTH, i + GATHER_DEPTH < nt)
        x2d[...] = xbuf[pl.ds(lax.rem(i, nbuf) * TM, TM)].reshape(TM, ROW_W)
        result = compute(x2d[...], *slots)

        def scatter_row(k, r):
            tok = perm_ref[i * TM + r]
            pltpu.make_async_copy(obuf.at[k, pl.ds(r, 1)], out_hbm.at[pl.ds(tok, 1)], ssem.at[k]).start()

        for k in range(2):
            @pl.when((osl == k) & (nv == TM))
            def _():
                obuf[k] = result
                for r in range(TM):
                    scatter_row(k, r)

            @pl.when((osl == k) & (nv < TM))
            def _():
                obuf[k] = result
                for r in range(TM):
                    pl.when(r < nv)(functools.partial(scatter_row, k, r))

        @pl.when(i == nt - 1)
        def _():
            wait_scatter(osl, nv)

            @pl.when(i >= 1)
            def _():
                wait_scatter(1 - osl, nv_ref[jnp.maximum(i - 1, 0)])


def _moe(rows, gtab_hi, gtab_lo, ln2post, wgu, wd, nt, nv, sched, perm, n_tiles):
    n = rows.shape[0]
    const2 = lambda i, *_: (0, 0)
    grid_spec = pltpu.PrefetchScalarGridSpec(
        num_scalar_prefetch=4,
        grid=(n_tiles,),
        in_specs=[pl.BlockSpec(memory_space=pl.ANY),
                  pl.BlockSpec(gtab_hi.shape, const2),
                  pl.BlockSpec(gtab_lo.shape, const2),
                  pl.BlockSpec(ln2post.shape, const2),
                  pl.BlockSpec(memory_space=pl.ANY),
                  pl.BlockSpec(memory_space=pl.ANY)],
        out_specs=pl.BlockSpec(memory_space=pl.ANY),
        scratch_shapes=[pltpu.VMEM(((GATHER_DEPTH + 1) * TM, 1, ROW_W), U32),
                        pltpu.VMEM((TM, ROW_W), U32),
                        pltpu.VMEM((2, TM, D_MODEL), F32),
                        pltpu.VMEM((2, 2, D_MODEL, 2 * FF), BF16),
                        pltpu.VMEM((2, 2, FF, D_MODEL), BF16),
                        pltpu.SemaphoreType.DMA((GATHER_DEPTH + 1,)),
                        pltpu.SemaphoreType.DMA((2,)),
                        pltpu.SemaphoreType.DMA((2, 2))],
    )
    return pl.pallas_call(
        _moe_kernel,
        grid_spec=grid_spec,
        out_shape=jax.ShapeDtypeStruct((n, D_MODEL), F32),
        compiler_params=pltpu.CompilerParams(dimension_semantics=("arbitrary",),
                                             vmem_limit_bytes=VMEM_LIMIT),
        name="moe",
    )(nt, nv, sched, perm, rows, gtab_hi, gtab_lo, ln2post, wgu, wd)


def _weight_schedule(ea, eb):
    n_tiles = ea.shape[0]
    tid = jnp.arange(n_tiles, dtype=I32)
    fields = [[], [], [], []]
    for e in (ea, eb):
        start = jnp.concatenate([jnp.ones((1,), I32), (e[1:] != e[:-1]).astype(I32)])
        slot = (jnp.cumsum(start) - 1) % 2
        start_idx = jnp.where(start == 1, tid, n_tiles)
        next_start = jnp.concatenate([lax.cummin(start_idx, reverse=True)[1:], jnp.full((1,), n_tiles, I32)])
        nxt = jnp.where(next_start < n_tiles, e[jnp.minimum(next_start, n_tiles - 1)], -1)
        for f, v in zip(fields, (e, start, slot, nxt)):
            f.append(v.astype(I32))
    return jnp.concatenate([v for f in fields for v in f])


def kernel(x, c, ln1_pre, ln1_post, ln2_pre, ln2_post, w_ada, b_ada, w_in, attn_sinks, attn_out_norm,
           hgrn_lb, hgrn_out_norm, w_out, w_router_group, b_router_group, w_router_expert,
           b_router_expert, w_exp_gate, w_exp_up, w_exp_down):
    bsz, seq, d = x.shape
    assert d == D_MODEL and seq % TQ == 0 and w_ada.shape[0] == 1 and hgrn_lb.shape[0] == 2
    assert bsz <= LANES - EXT_BATCH0 and (bsz * seq) % TM == 0
    n = bsz * seq

    mod = _ada(c, w_ada[0], b_ada[0])
    mod3 = mod.reshape(bsz, 6, d)

    wr = jnp.concatenate([w_router_group[0], jnp.zeros((d, GROUP_ROWS - N_GROUPS), F32), w_router_expert[0],
                          jnp.zeros((d, LANES - ROUTER_ROWS), F32)], axis=1)
    br = jnp.concatenate([b_router_group[0], jnp.full((GROUP_ROWS - N_GROUPS,), NEG, F32), b_router_expert[0]])
    wr_hi = wr.astype(BF16)
    wr_lo = (wr - wr_hi.astype(F32)).astype(BF16)

    x1ext, info, cnt = _mixer(
        x.reshape(n, d), mod3, attn_sinks[0], ln1_pre, ln1_post, ln2_pre, attn_out_norm, hgrn_out_norm,
        hgrn_lb, w_in[0].astype(BF16), w_out[0].astype(BF16), wr_hi, wr_lo, br.reshape(ROUTER_ROWS, 1),
        bsz, seq)

    n_tiles = n // TM + N_BUCKETS
    counts = cnt[:N_BUCKETS, 0].astype(I32)
    tiles_per = (counts + TM - 1) // TM
    tile_end = jnp.cumsum(tiles_per)
    tile_start = tile_end - tiles_per
    nt = tile_end[-1]
    bucket = info[:, 0, :].reshape(n)
    rank = info[:, 1, :].reshape(n)
    tid = jnp.arange(n_tiles, dtype=I32)[None, :]
    member = (tid >= tile_start[:, None]) & (tid < tile_end[:, None])
    pick = lambda per_bucket: jnp.sum(jnp.where(member, per_bucket, 0), axis=0).astype(I32)
    bidx = np.arange(N_BUCKETS, dtype=np.int32)
    ea_of = jnp.asarray((bidx // N_PAIRS) * EPG + _PAIR_A[bidx % N_PAIRS])[:, None]
    eb_of = jnp.asarray((bidx // N_PAIRS) * EPG + _PAIR_B[bidx % N_PAIRS])[:, None]
    last_used = jnp.arange(N_BUCKETS)[:, None] == jnp.max(jnp.where(tiles_per > 0, jnp.arange(N_BUCKETS), 0))
    unused = tid[0] >= nt
    nv = pick(jnp.clip(counts[:, None] - (tid - tile_start[:, None]) * TM, 0, TM))
    ea = jnp.where(unused, jnp.sum(jnp.where(last_used, ea_of, 0)), pick(ea_of)).astype(I32)
    eb = jnp.where(unused, jnp.sum(jnp.where(last_used, eb_of, 0)), pick(eb_of)).astype(I32)

    pad128 = lambda a: jnp.concatenate([a, jnp.zeros((LANES - a.shape[0],), I32)])
    row_start = jnp.concatenate([tile_start, nt.reshape(1)]) * TM
    perm = _perm(pad128(row_start), pad128(counts), bucket, rank, n_tiles * TM)

    wgu = jnp.concatenate([w_exp_gate[0], w_exp_up[0]], axis=-1).astype(BF16)
    wd = w_exp_down[0].astype(BF16)
    gtab = jnp.zeros((LANES, d), F32).at[EXT_BATCH0:EXT_BATCH0 + bsz].set(mod3[:, 5, :])
    gtab_hi = gtab.astype(BF16)
    gtab_lo = (gtab - gtab_hi.astype(F32)).astype(BF16)
    out = _moe(x1ext, gtab_hi, gtab_lo, ln2_post, wgu, wd, nt.reshape(1), nv, _weight_schedule(ea, eb), perm, n_tiles)
    return out.reshape(bsz, seq, d)
```

```python
import functools

import numpy as np
import jax
import jax.numpy as jnp
from jax import lax
from jax.experimental import pallas as pl
from jax.experimental.pallas import tpu as pltpu

F32 = jnp.float32
BF16 = jnp.bfloat16
I32 = jnp.int32

D_MODEL = 1024
ATTN_HEADS = 8
HEAD_DIM = 64
WINDOW = 128
ATTN_W = 512
KV_W = 128
HG_HEADS = 4
HG_DIM = 128
HG_W = 512
HG_CHUNK = 32
IN_W = 2816
N_GROUPS = 4
EPG = 8
N_EXPERTS = 32
FF = 256
N_PAIRS = EPG * (EPG - 1) // 2
N_BUCKETS = N_GROUPS * N_PAIRS
EPS = 1e-6
NEG = -1e30
LOG2E = 1.4426950408889634

LANES = 128
H2P_W = D_MODEL // 2
EXT_W = LANES
OFF_H2P = D_MODEL
OFF_EXT = D_MODEL + H2P_W
ROW_W = OFF_EXT + EXT_W
EXT_BATCH0 = 8
U32 = jnp.uint32
SUBLANES = 8
GROUP_ROWS = SUBLANES
ROUTER_ROWS = GROUP_ROWS + N_EXPERTS

TQ = 256
TM = 128
VMEM_LIMIT = 56 * 1024 * 1024

OFF_Q, OFF_K, OFF_V, OFF_HQ, OFF_HF, OFF_HI, OFF_HG = 0, 512, 640, 768, 1280, 1792, 2304

_PAIR_A = np.array([a for a in range(EPG) for b in range(a + 1, EPG)], np.int32)
_PAIR_B = np.array([b for a in range(EPG) for b in range(a + 1, EPG)], np.int32)


def _dot(a, b):
    return jnp.dot(a, b, preferred_element_type=F32)


def _dot_nt(a, b):
    return lax.dot_general(a, b, (((1,), (1,)), ((), ())), preferred_element_type=F32)


def _dot_tn(a, b):
    return lax.dot_general(a, b, (((0,), (0,)), ((), ())), preferred_element_type=F32)


def _split(a):
    hi = a.astype(BF16)
    lo = (a - hi.astype(F32)).astype(BF16)
    return hi, lo


def _rms(x):
    return x * lax.rsqrt(jnp.mean(x * x, axis=-1, keepdims=True) + EPS)


def _sigmoid(x):
    return 0.5 * jnp.tanh(0.5 * x) + 0.5


def _ada_kernel(c_ref, w_ref, b_ref, o_ref):
    c = c_ref[...]
    ca = c * _sigmoid(c)
    c_hi, c_lo = _split(ca)
    w_hi, w_lo = _split(w_ref[...])
    o_ref[...] = _dot(c_hi, w_hi) + _dot(c_lo, w_hi) + _dot(c_hi, w_lo) + b_ref[...]


def _ada(c, w, b):
    bsz, d = c.shape
    n_out = w.shape[1]
    return pl.pallas_call(
        _ada_kernel,
        grid=(n_out // d,),
        in_specs=[pl.BlockSpec((bsz, d), lambda j: (0, 0)),
                  pl.BlockSpec((d, d), lambda j: (0, j)),
                  pl.BlockSpec((1, d), lambda j: (0, j))],
        out_specs=pl.BlockSpec((bsz, d), lambda j: (0, j)),
        out_shape=jax.ShapeDtypeStruct((bsz, n_out), F32),
        compiler_params=pltpu.CompilerParams(dimension_semantics=("arbitrary",),
                                             vmem_limit_bytes=VMEM_LIMIT),
        name="adaln",
    )(c, w, b.reshape(1, n_out))


def _attention(proj, kprev_ref, vprev_ref, sinks_ref, bias_ref, t, s_scr, m_scr, p_scr):
    tq = proj.shape[0]
    q = (proj[:, OFF_Q:OFF_Q + ATTN_W] * (HEAD_DIM ** -0.5 * LOG2E)).astype(BF16)
    kf = jnp.concatenate([kprev_ref[...], proj[:, OFF_K:OFF_K + KV_W]], axis=0)
    vf = jnp.concatenate([vprev_ref[...], proj[:, OFF_V:OFF_V + KV_W]], axis=0)
    kprev_ref[...] = proj[tq - WINDOW:, OFF_K:OFF_K + KV_W]
    vprev_ref[...] = proj[tq - WINDOW:, OFF_V:OFF_V + KV_W]

    lo = lax.broadcasted_iota(I32, kf.shape, 1) < HEAD_DIM
    kr = pltpu.roll(kf, HEAD_DIM, axis=1)
    vr = pltpu.roll(vf, HEAD_DIM, axis=1)

    def variants(a, ar):
        return [[jnp.where(lo, a, 0.0).astype(BF16), jnp.where(lo, 0.0, ar).astype(BF16)],
                [jnp.where(lo, ar, 0.0).astype(BF16), jnp.where(lo, 0.0, a).astype(BF16)]]

    kvar = variants(kf, kr)
    vvar = variants(vf, vr)

    first = jnp.where(t > 0, 0, 1)

    nblk = tq // WINDOW
    idx = lambda j, h: j * ATTN_HEADS + h
    keys = lambda a, j: a[j * WINDOW:(j + 2) * WINDOW]

    def scores():
        for j in range(nblk):
            for h in range(ATTN_HEADS):
                p, par = h // 2, h % 2
                qp = q[j * WINDOW:(j + 1) * WINDOW, p * LANES:(p + 1) * LANES]
                s = _dot_nt(qp, keys(kvar[p // 2][par], j)) + bias_ref[first if j == 0 else 0, h]
                s_scr[idx(j, h)] = s
                m_scr[idx(j, h)] = jnp.maximum(jnp.max(s, axis=-1, keepdims=True), sinks_ref[h] * LOG2E)

    def exps():
        for j in range(nblk):
            for h in range(ATTN_HEADS):
                m = m_scr[idx(j, h)]
                pe = jnp.exp2(s_scr[idx(j, h)] - m)
                p_scr[idx(j, h)] = pe.astype(BF16)
                m_scr[idx(j, h)] = 1.0 / (jnp.sum(pe, axis=-1, keepdims=True) + jnp.exp2(sinks_ref[h] * LOG2E - m))

    def values():
        blocks = []
        for j in range(nblk):
            pairs = []
            for p in range(ATTN_HEADS // 2):
                acc = None
                for par in range(2):
                    h = 2 * p + par
                    o = _dot(p_scr[idx(j, h)], keys(vvar[p // 2][par], j)) * m_scr[idx(j, h)]
                    acc = o if acc is None else acc + o
                pairs.append(acc)
            blocks.append(jnp.concatenate(pairs, axis=1))
        return jnp.concatenate(blocks, axis=0)

    return scores, exps, values


def _hgrn2(proj, lb, st_ref, hnorm, u_scr, stb_scr):
    tq = proj.shape[0]
    nc = tq // HG_CHUNK
    qr = proj[:, OFF_HQ:OFF_HQ + HG_W]
    fr = proj[:, OFF_HF:OFF_HF + HG_W]
    iv = proj[:, OFF_HI:OFF_HI + HG_W]
    gr = proj[:, OFF_HG:OFF_HG + HG_W]
    qh = qr * _sigmoid(qr)
    f = lb + (1.0 - lb) * _sigmoid(fr)
    kk = 1.0 - f
    logf = jnp.log(f)

    rmod = lax.broadcasted_iota(I32, (tq, HG_W), 0) & (HG_CHUNK - 1)
    bc = logf
    s = 1
    while s < HG_CHUNK:
        bc = bc + jnp.where(rmod >= s, pltpu.roll(bc, s, axis=0), 0.0)
        s *= 2

    b3 = bc.reshape(nc, HG_CHUNK, HG_W)
    blast = b3[:, HG_CHUNK - 1:HG_CHUNK, :]
    kend = (kk.reshape(nc, HG_CHUNK, HG_W) * jnp.exp(blast - b3)).reshape(tq, HG_W)
    decay = jnp.exp(blast).reshape(nc, HG_W)
    qdec = (qh * jnp.exp(bc)).astype(BF16)
    kdec = (kk * jnp.exp(-bc)).astype(BF16)
    kend = kend.astype(BF16)
    ivb = iv.astype(BF16)

    ri = lax.broadcasted_iota(I32, (tq, tq), 0)
    ci = lax.broadcasted_iota(I32, (tq, tq), 1)
    cmask = ((ri // HG_CHUNK) == (ci // HG_CHUNK)) & (ri >= ci)

    heads = [slice(hh * HG_DIM, (hh + 1) * HG_DIM) for hh in range(HG_HEADS)]
    chunks = [slice(n * HG_CHUNK, (n + 1) * HG_CHUNK) for n in range(nc)]

    lane_head = lax.broadcasted_iota(I32, (HG_CHUNK, HG_W), 1) // HG_DIM
    for n, rs in enumerate(chunks):
        vstack = jnp.concatenate([ivb[rs, sl] for sl in heads], axis=0)
        kblk = jnp.concatenate([jnp.where(lane_head == hh, kend[rs], 0.0).astype(BF16)
                                for hh in range(HG_HEADS)], axis=0)
        u_scr[n] = _dot_tn(vstack, kblk)

    st = st_ref[...]
    for n in range(nc):
        stb_scr[n] = st.astype(BF16)
        st = st * decay[n:n + 1] + u_scr[n]
    st_ref[...] = st

    outs = []
    for hh, sl in enumerate(heads):
        a = _dot_nt(qdec[:, sl], kdec[:, sl])
        a = jnp.where(cmask, a, 0.0).astype(BF16)
        o_intra = _dot(a, ivb[:, sl])
        inter = [_dot_nt(qdec[rs, sl], stb_scr[n, :, sl]) for n, rs in enumerate(chunks)]
        o = o_intra + jnp.concatenate(inter, axis=0)
        o = _rms(o) * hnorm[:, sl]
        g = gr[:, sl]
        outs.append(o * (g * _sigmoid(g)))
    return jnp.concatenate(outs, axis=1)


def _route_topk(h2, wr_hi, wr_lo, br):
    tq = h2.shape[0]
    h_hi, h_lo = _split(h2)
    logits = _dot(h_hi, wr_hi) + _dot(h_lo, wr_hi) + _dot(h_hi, wr_lo)
    lt = logits.T[0:ROUTER_ROWS] + br
    sub = lax.broadcasted_iota(I32, (SUBLANES, tq), 0).astype(F32)
    none = float(SUBLANES)

    gl = lt[0:GROUP_ROWS]
    gm = jnp.max(gl, axis=0, keepdims=True)
    gidx = jnp.min(jnp.where(gl == gm, sub, none), axis=0, keepdims=True)
    g_w = 1.0 / jnp.sum(jnp.exp(gl - gm), axis=0, keepdims=True)

    group_rows = lambda g: lt[GROUP_ROWS + EPG * g:GROUP_ROWS + EPG * (g + 1)]
    es = group_rows(0)
    for g in range(1, N_GROUPS):
        es = jnp.where(gidx == float(g), group_rows(g), es)
    m1 = jnp.max(es, axis=0, keepdims=True)
    i1 = jnp.min(jnp.where(es == m1, sub, none), axis=0, keepdims=True)
    e2 = jnp.where(sub == i1, NEG, es)
    m2 = jnp.max(e2, axis=0, keepdims=True)
    i2 = jnp.min(jnp.where(e2 == m2, sub, none), axis=0, keepdims=True)
    dd = jnp.exp(m2 - m1)
    w1 = g_w / (1.0 + dd)
    w2 = g_w * dd / (1.0 + dd)
    first_low = i1 < i2
    ea = jnp.minimum(i1, i2)
    eb = jnp.maximum(i1, i2)
    w_lo = jnp.where(first_low, w1, w2)
    w_hi = jnp.where(first_low, w2, w1)
    pair = ea * (2.0 * EPG - 1.0 - ea) * 0.5 + (eb - ea - 1.0)
    bucket = gidx * float(N_PAIRS) + pair
    return bucket, w_lo, w_hi


def _route_rank(bucket, w_lo, w_hi, carry_ref, bidx, live):
    tq = bucket.shape[1]
    brow = lax.broadcasted_iota(I32, (LANES, tq), 0).astype(F32)
    onehot = brow == bucket
    oh = jnp.where(onehot, live, 0.0)
    ti = lax.broadcasted_iota(I32, (tq, tq), 0)
    tj = lax.broadcasted_iota(I32, (tq, tq), 1)
    upper = jnp.where(ti < tj, 1.0, 0.0).astype(BF16)
    before = _dot(oh.astype(BF16), upper) + carry_ref[...]
    rank = jnp.sum(jnp.where(onehot, before, 0.0), axis=0, keepdims=True)
    carry_ref[...] = carry_ref[...] + jnp.sum(oh, axis=1, keepdims=True)

    lane_row = lax.broadcasted_iota(I32, (LANES - EXT_BATCH0, tq), 0)
    onehot_b = jnp.where(lane_row == bidx, 1.0, 0.0)
    info = jnp.concatenate([w_lo, w_hi, jnp.zeros((EXT_BATCH0 - 2, tq), F32), onehot_b], axis=0)
    return rank, info.T


def _mixer_kernel(sinks_ref, x_ref, mod_ref, ln1pre_ref, ln1post_ref, ln2pre_ref, anorm_ref, hnorm_ref,
                  lb_ref, win_ref, wout_ref, wrhi_ref, wrlo_ref, br_ref, bias_ref,
                  rows_ref, info_ref, cnt_ref,
                  kprev_ref, vprev_ref, st_ref, carry_ref, s_scr, m_scr, p_scr, u_scr, stb_scr,
                  h2_scr, proj_scr, keep_scr, *, tiles_per_seq):
    s = pl.program_id(0)
    n_tiles = pl.num_programs(0) - 1
    tq = x_ref.shape[0]
    t = lax.rem(jnp.minimum(s, n_tiles - 1), tiles_per_seq)
    bits = lambda a: pltpu.bitcast(a, U32)

    @pl.when(s == 0)
    def _():
        carry_ref[...] = jnp.zeros_like(carry_ref)
        h2_scr[...] = jnp.zeros_like(h2_scr)
        keep_scr[...] = jnp.zeros_like(keep_scr)

    @pl.when(t == 0)
    def _():
        st_ref[...] = jnp.zeros_like(st_ref)
        kprev_ref[...] = jnp.zeros_like(kprev_ref)
        vprev_ref[...] = jnp.zeros_like(vprev_ref)

    x = x_ref[...]
    mod = mod_ref[0]
    sh1, sc1, ga1, sh2, sc2 = mod[0:1], mod[1:2], mod[2:3], mod[3:4], mod[4:5]
    prev = jnp.maximum(s - 1, 0)

    @pl.when(s <= n_tiles)
    def _():
        bucket, w_lo, w_hi = _route_topk(h2_scr[...], wrhi_ref[...], wrlo_ref[...], br_ref[...])
        h = _rms(x) * ln1pre_ref[...] * (1.0 + sc1) + sh1
        proj_scr[...] = _dot(h.astype(BF16), win_ref[...])
        live = jnp.where(s >= 1, 1.0, 0.0)
        rank, ext = _route_rank(bucket, w_lo, w_hi, carry_ref, prev // tiles_per_seq, live)
        rows_ref[:, :, 0:OFF_EXT] = keep_scr[...].reshape(tq, 1, OFF_EXT)
        rows_ref[:, :, OFF_EXT:ROW_W] = bits(ext).reshape(tq, 1, EXT_W)
        info_ref[0] = jnp.concatenate([bucket, rank, jnp.zeros((6, tq), F32)], axis=0).astype(I32)
        cnt_ref[...] = jnp.broadcast_to(carry_ref[...], cnt_ref.shape)

    proj = proj_scr[...]
    scores, exps, values = _attention(proj, kprev_ref, vprev_ref, sinks_ref, bias_ref, t, s_scr, m_scr, p_scr)
    scores()
    exps()
    attn = _rms(values()) * anorm_ref[...]

    lbr = lb_ref[...]
    le = jnp.exp(lbr - jnp.max(lbr, axis=0, keepdims=True))
    lb = le[0:1] / jnp.sum(le, axis=0, keepdims=True)
    hg = _hgrn2(proj, lb, st_ref, hnorm_ref[...], u_scr, stb_scr)

    mix = _dot(jnp.concatenate([attn, hg], axis=1).astype(BF16), wout_ref[...])
    x1 = x + ga1 * (_rms(mix) * ln1post_ref[...])

    h2 = _rms(x1) * ln2pre_ref[...] * (1.0 + sc2) + sh2
    h2r = h2.astype(BF16).astype(F32)

    keep_scr[:, 0:D_MODEL] = bits(x1)
    keep_scr[:, OFF_H2P:OFF_EXT] = ((bits(h2r[:, 0:H2P_W]) >> 16)
                                    | (bits(h2r[:, H2P_W:D_MODEL]) & jnp.uint32(0xFFFF0000)))
    h2_scr[...] = h2


def _attn_bias():
    qi = np.arange(WINDOW)[:, None]
    kj = np.arange(2 * WINDOW)[None, :]
    dist = qi + WINDOW - kj
    in_win = (dist >= 0) & (dist < WINDOW)
    slopes = 2.0 ** (-8.0 * (np.arange(ATTN_HEADS) + 1.0) / ATTN_HEADS)
    b = np.where(in_win[None], -slopes[:, None, None] * dist[None] * LOG2E, NEG)
    b_first = np.where((kj >= WINDOW)[None], b, NEG)
    return jnp.asarray(np.stack([b, b_first]).astype(np.float32))


def _mixer(x2, mod3, sinks, ln1pre, ln1post, ln2pre, anorm, hnorm, lb, win, wout, wr_hi, wr_lo, br,
           bsz, seq):
    bias = _attn_bias()
    n = bsz * seq
    nt = seq // TQ
    n_tiles = bsz * nt
    n_sc = (TQ // WINDOW) * ATTN_HEADS
    cur = lambda s: jnp.minimum(s, n_tiles - 1)
    const = lambda s: (0, 0)
    full = lambda a: pl.BlockSpec(a.shape, const)
    return pl.pallas_call(
        functools.partial(_mixer_kernel, tiles_per_seq=nt),
        grid=(n_tiles + 1,),
        in_specs=[pl.BlockSpec(memory_space=pltpu.SMEM),
                  pl.BlockSpec((TQ, D_MODEL), lambda s: (cur(s), 0)),
                  pl.BlockSpec((1, 6, D_MODEL), lambda s: (cur(s) // nt, 0, 0)),
                  full(ln1pre), full(ln1post), full(ln2pre), full(anorm), full(hnorm), full(lb),
                  full(win), full(wout), full(wr_hi), full(wr_lo), full(br),
                  pl.BlockSpec(bias.shape, lambda s: (0, 0, 0, 0))],
        out_specs=[pl.BlockSpec((TQ, 1, ROW_W), lambda s: (jnp.maximum(s - 1, 0), 0, 0)),
                   pl.BlockSpec((1, 8, TQ), lambda s: (jnp.maximum(s - 1, 0), 0, 0)),
                   pl.BlockSpec((LANES, LANES), const)],
        out_shape=[jax.ShapeDtypeStruct((n, 1, ROW_W), U32),
                   jax.ShapeDtypeStruct((n // TQ, 8, TQ), I32),
                   jax.ShapeDtypeStruct((LANES, LANES), F32)],
        scratch_shapes=[pltpu.VMEM((WINDOW, KV_W), F32),
                        pltpu.VMEM((WINDOW, KV_W), F32),
                        pltpu.VMEM((HG_DIM, HG_W), F32),
                        pltpu.VMEM((LANES, 1), F32),
                        pltpu.VMEM((n_sc, WINDOW, 2 * WINDOW), F32),
                        pltpu.VMEM((n_sc, WINDOW, 1), F32),
                        pltpu.VMEM((n_sc, WINDOW, 2 * WINDOW), BF16),
                        pltpu.VMEM((TQ // HG_CHUNK, HG_DIM, HG_W), F32),
                        pltpu.VMEM((TQ // HG_CHUNK, HG_DIM, HG_W), BF16),
                        pltpu.VMEM((TQ, D_MODEL), F32),
                        pltpu.VMEM((TQ, IN_W), F32),
                        pltpu.VMEM((TQ, OFF_EXT), U32)],
        compiler_params=pltpu.CompilerParams(dimension_semantics=("arbitrary",),
                                             vmem_limit_bytes=VMEM_LIMIT),
        name="mixer",
    )(sinks, x2, mod3, ln1pre, ln1post, ln2pre, anorm, hnorm, lb, win, wout, wr_hi, wr_lo, br, bias)


PERM_STEPS = 8
PERM_ROWS = 8
PERM_UNROLL = 16


def _perm_kernel(rs_ref, cnt_ref, bucket_ref, rank_ref, perm_ref, pos_vmem, pos_smem, sem):
    pid = pl.program_id(0)
    rows, cols = pos_vmem.shape

    b = bucket_ref[0]
    start = jnp.zeros_like(b)
    for k in range(N_BUCKETS):
        start = jnp.where(b == k, rs_ref[k], start)
    pos_vmem[...] = start + rank_ref[0]
    copies = [pltpu.make_async_copy(pos_vmem.at[r], pos_smem.at[pl.ds(r * cols, cols)], sem) for r in range(rows)]
    for cp in copies:
        cp.start()

    @pl.when(pid == 0)
    def _():
        def per_bucket(k, carry):
            first = rs_ref[k]
            cnt = cnt_ref[k]

            def pad(r, c2):
                perm_ref[first + r] = 0
                return c2

            lax.fori_loop(cnt, ((cnt + TM - 1) // TM) * TM, pad, 0)
            return carry

        lax.fori_loop(0, N_BUCKETS, per_bucket, 0)

        def tail(blk, carry):
            for u in range(PERM_UNROLL):
                perm_ref[blk * PERM_UNROLL + u] = 0
            return carry

        lax.fori_loop(rs_ref[N_BUCKETS] // PERM_UNROLL, perm_ref.shape[0] // PERM_UNROLL, tail, 0)

    for cp in copies:
        cp.wait()
    base = pid * (rows * cols)

    def body(j, carry):
        i0 = j * PERM_UNROLL
        positions = [pos_smem[i0 + u] for u in range(PERM_UNROLL)]
        for u in range(PERM_UNROLL):
            perm_ref[positions[u]] = base + i0 + u
        return carry

    lax.fori_loop(0, rows * cols // PERM_UNROLL, body, 0)


def _perm(row_start, counts, bucket, rank, n_rows):
    n = bucket.shape[0]
    cols = n // (PERM_STEPS * PERM_ROWS)
    assert n % (PERM_STEPS * PERM_ROWS * PERM_UNROLL) == 0
    chunked = lambda a: a.reshape(PERM_STEPS, PERM_ROWS, cols)
    chunk_spec = pl.BlockSpec((1, PERM_ROWS, cols), lambda i: (i, 0, 0))
    return pl.pallas_call(
        _perm_kernel,
        grid=(PERM_STEPS,),
        in_specs=[pl.BlockSpec(memory_space=pltpu.SMEM),
                  pl.BlockSpec(memory_space=pltpu.SMEM),
                  chunk_spec, chunk_spec],
        out_specs=pl.BlockSpec(memory_space=pltpu.SMEM),
        out_shape=jax.ShapeDtypeStruct((n_rows,), I32),
        scratch_shapes=[pltpu.VMEM((PERM_ROWS, cols), I32), pltpu.SMEM((PERM_ROWS * cols,), I32),
                        pltpu.SemaphoreType.DMA(())],
        compiler_params=pltpu.CompilerParams(dimension_semantics=("arbitrary",)),
        name="perm",
    )(row_start, counts, chunked(bucket), chunked(rank))


GATHER_DEPTH = 3


SCHED_EXPERT, SCHED_RUN_START, SCHED_SLOT, SCHED_NEXT_EXPERT = range(4)


def _moe_kernel(nt_ref, nv_ref, sched_ref, perm_ref,
                rows_hbm, gtab_hi_ref, gtab_lo_ref, wg_hbm, wu_hbm, wd_hbm,
                out_hbm, xbuf, x2d, obuf, wgu_buf, wd_buf, gsem, ssem, wsem):
    i = pl.program_id(0)
    nt = nt_ref[0]
    n_steps = pl.num_programs(0)
    last_tile = n_steps - 1
    sched = lambda field, side: sched_ref[(field * 2 + side) * n_steps + i]

    def weight_copies(side, expert, slot):
        return (pltpu.make_async_copy(wg_hbm.at[expert], wgu_buf.at[side, slot, :, 0:FF], wsem.at[side, slot]),
                pltpu.make_async_copy(wu_hbm.at[expert], wgu_buf.at[side, slot, :, FF:2 * FF], wsem.at[side, slot]),
                pltpu.make_async_copy(wd_hbm.at[expert], wd_buf.at[side, slot], wsem.at[side, slot]))
    nbuf = xbuf.shape[0] // TM

    def start_gather(tile, pred):
        sl = lax.rem(tile, nbuf)
        base = jnp.minimum(tile, last_tile) * TM
        for r in range(TM):
            @pl.when(pred)
            def _():
                tok = perm_ref[base + r]
                pltpu.make_async_copy(rows_hbm.at[tok], xbuf.at[sl * TM + r], gsem.at[sl]).start()

    def wait_gather(tile):
        sl = lax.rem(tile, nbuf)
        pltpu.make_async_copy(rows_hbm.at[pl.ds(0, TM)], xbuf.at[pl.ds(sl * TM, TM)], gsem.at[sl]).wait()

    def wait_scatter(sl, nv):
        @pl.when(nv == TM)
        def _():
            pltpu.make_async_copy(obuf.at[sl], out_hbm.at[pl.ds(0, TM)], ssem.at[sl]).wait()

        @pl.when(nv < TM)
        def _():
            def one(r, carry):
                pltpu.make_async_copy(obuf.at[sl, pl.ds(0, 1)], out_hbm.at[pl.ds(0, 1)], ssem.at[sl]).wait()
                return carry

            lax.fori_loop(0, nv, one, 0)

    def compute(xb, slot_a, slot_b):
        x1 = pltpu.bitcast(xb[:, 0:D_MODEL], F32)
        hp = xb[:, OFF_H2P:OFF_EXT]
        h2a = pltpu.bitcast(hp << 16, F32).astype(BF16)
        h2b = pltpu.bitcast(hp & jnp.uint32(0xFFFF0000), F32).astype(BF16)
        ext = pltpu.bitcast(xb[:, OFF_EXT:ROW_W], F32)
        w_lo, w_hi = ext[:, 0:1], ext[:, 1:2]
        sel = ext.astype(BF16)
        ga2 = _dot(sel, gtab_hi_ref[...]) + _dot(sel, gtab_lo_ref[...])

        def hidden(side, slot, w):
            gu = (_dot(h2a, wgu_buf[side, slot, 0:H2P_W])
                  + _dot(h2b, wgu_buf[side, slot, H2P_W:D_MODEL]))
            hg, hu = gu[:, 0:FF], gu[:, FF:2 * FF]
            return (w * ((hg * _sigmoid(hg)) * hu)).astype(BF16)

        act = jnp.concatenate([hidden(0, slot_a, w_lo), hidden(1, slot_b, w_hi)], axis=1)
        wd = jnp.concatenate([wd_buf[0, slot_a], wd_buf[1, slot_b]], axis=0)
        y = _dot(act, wd)
        return x1 + ga2 * _rms(y)

    @pl.when(i == 0)
    def _():
        for side in range(2):
            for cp in weight_copies(side, sched(SCHED_EXPERT, side), 0):
                cp.start()
        for d in range(GATHER_DEPTH):
            start_gather(d, d < nt)

    @pl.when(i < nt)
    def _():
        nv = nv_ref[i]
        osl = lax.rem(i, 2)
        slots = []
        for side in range(2):
            slot = sched(SCHED_SLOT, side)
            slots.append(slot)

            @pl.when(sched(SCHED_RUN_START, side) == 1)
            def _():
                for cp in weight_copies(side, sched(SCHED_EXPERT, side), slot):
                    cp.wait()
                nxt = sched(SCHED_NEXT_EXPERT, side)

                @pl.when(nxt >= 0)
                def _():
                    for cp in weight_copies(side, nxt, 1 - slot):
                        cp.start()

        wait_gather(i)

        @pl.when(i >= 2)
        def _():
            wait_scatter(osl, nv_ref[jnp.maximum(i - 2, 0)])

        start_gather(i + GATHER_DEPTH, i + GATHER_DEPTH < nt)
        x2d[...] = xbuf[pl.ds(lax.rem(i, nbuf) * TM, TM)].reshape(TM, ROW_W)
        result = compute(x2d[...], *slots)

        def scatter_row(k, r):
            tok = perm_ref[i * TM + r]
            pltpu.make_async_copy(obuf.at[k, pl.ds(r, 1)], out_hbm.at[pl.ds(tok, 1)], ssem.at[k]).start()

        for k in range(2):
            @pl.when((osl == k) & (nv == TM))
            def _():
                obuf[k] = result
                for r in range(TM):
                    scatter_row(k, r)

            @pl.when((osl == k) & (nv < TM))
            def _():
                obuf[k] = result
                for r in range(TM):
                    pl.when(r < nv)(functools.partial(scatter_row, k, r))

        @pl.when(i == nt - 1)
        def _():
            wait_scatter(osl, nv)

            @pl.when(i >= 1)
            def _():
                wait_scatter(1 - osl, nv_ref[jnp.maximum(i - 1, 0)])


def _moe(rows, gtab_hi, gtab_lo, wg, wu, wd, nt, nv, sched, perm, n_tiles):
    n = rows.shape[0]
    const2 = lambda i, *_: (0, 0)
    grid_spec = pltpu.PrefetchScalarGridSpec(
        num_scalar_prefetch=4,
        grid=(n_tiles,),
        in_specs=[pl.BlockSpec(memory_space=pl.ANY),
                  pl.BlockSpec(gtab_hi.shape, const2),
                  pl.BlockSpec(gtab_lo.shape, const2),
                  pl.BlockSpec(memory_space=pl.ANY),
                  pl.BlockSpec(memory_space=pl.ANY),
                  pl.BlockSpec(memory_space=pl.ANY)],
        out_specs=pl.BlockSpec(memory_space=pl.ANY),
        scratch_shapes=[pltpu.VMEM(((GATHER_DEPTH + 1) * TM, 1, ROW_W), U32),
                        pltpu.VMEM((TM, ROW_W), U32),
                        pltpu.VMEM((2, TM, D_MODEL), F32),
                        pltpu.VMEM((2, 2, D_MODEL, 2 * FF), BF16),
                        pltpu.VMEM((2, 2, FF, D_MODEL), BF16),
                        pltpu.SemaphoreType.DMA((GATHER_DEPTH + 1,)),
                        pltpu.SemaphoreType.DMA((2,)),
                        pltpu.SemaphoreType.DMA((2, 2))],
    )
    return pl.pallas_call(
        _moe_kernel,
        grid_spec=grid_spec,
        out_shape=jax.ShapeDtypeStruct((n, D_MODEL), F32),
        compiler_params=pltpu.CompilerParams(dimension_semantics=("arbitrary",),
                                             vmem_limit_bytes=VMEM_LIMIT),
        name="moe",
    )(nt, nv, sched, perm, rows, gtab_hi, gtab_lo, wg, wu, wd)


def _weight_schedule(ea, eb):
    n_tiles = ea.shape[0]
    tid = jnp.arange(n_tiles, dtype=I32)
    fields = [[], [], [], []]
    for e in (ea, eb):
        start = jnp.concatenate([jnp.ones((1,), I32), (e[1:] != e[:-1]).astype(I32)])
        slot = (jnp.cumsum(start) - 1) % 2
        start_idx = jnp.where(start == 1, tid, n_tiles)
        next_start = jnp.concatenate([lax.cummin(start_idx, reverse=True)[1:], jnp.full((1,), n_tiles, I32)])
        nxt = jnp.where(next_start < n_tiles, e[jnp.minimum(next_start, n_tiles - 1)], -1)
        for f, v in zip(fields, (e, start, slot, nxt)):
            f.append(v.astype(I32))
    return jnp.concatenate([v for f in fields for v in f])


def kernel(x, c, ln1_pre, ln1_post, ln2_pre, ln2_post, w_ada, b_ada, w_in, attn_sinks, attn_out_norm,
           hgrn_lb, hgrn_out_norm, w_out, w_router_group, b_router_group, w_router_expert,
           b_router_expert, w_exp_gate, w_exp_up, w_exp_down):
    bsz, seq, d = x.shape
    assert d == D_MODEL and seq % TQ == 0 and w_ada.shape[0] == 1 and hgrn_lb.shape[0] == 2
    assert bsz <= LANES - EXT_BATCH0 and (bsz * seq) % TM == 0
    n = bsz * seq

    mod = _ada(c, w_ada[0], b_ada[0])
    mod3 = mod.reshape(bsz, 6, d)

    wr = jnp.concatenate([w_router_group[0], jnp.zeros((d, GROUP_ROWS - N_GROUPS), F32), w_router_expert[0],
                          jnp.zeros((d, LANES - ROUTER_ROWS), F32)], axis=1)
    br = jnp.concatenate([b_router_group[0], jnp.full((GROUP_ROWS - N_GROUPS,), NEG, F32), b_router_expert[0]])
    wr_hi = wr.astype(BF16)
    wr_lo = (wr - wr_hi.astype(F32)).astype(BF16)

    x1ext, info, cnt = _mixer(
        x.reshape(n, d), mod3, attn_sinks[0], ln1_pre, ln1_post, ln2_pre, attn_out_norm, hgrn_out_norm,
        hgrn_lb, w_in[0].astype(BF16), w_out[0].astype(BF16), wr_hi, wr_lo, br.reshape(ROUTER_ROWS, 1),
        bsz, seq)

    n_tiles = n // TM + N_BUCKETS
    counts = cnt[:N_BUCKETS, 0].astype(I32)
    tiles_per = (counts + TM - 1) // TM
    tile_end = jnp.cumsum(tiles_per)
    tile_start = tile_end - tiles_per
    nt = tile_end[-1]
    bucket = info[:, 0, :].reshape(n)
    rank = info[:, 1, :].reshape(n)
    tid = jnp.arange(n_tiles, dtype=I32)[None, :]
    member = (tid >= tile_start[:, None]) & (tid < tile_end[:, None])
    pick = lambda per_bucket: jnp.sum(jnp.where(member, per_bucket, 0), axis=0).astype(I32)
    bidx = np.arange(N_BUCKETS, dtype=np.int32)
    ea_of = jnp.asarray((bidx // N_PAIRS) * EPG + _PAIR_A[bidx % N_PAIRS])[:, None]
    eb_of = jnp.asarray((bidx // N_PAIRS) * EPG + _PAIR_B[bidx % N_PAIRS])[:, None]
    last_used = jnp.arange(N_BUCKETS)[:, None] == jnp.max(jnp.where(tiles_per > 0, jnp.arange(N_BUCKETS), 0))
    unused = tid[0] >= nt
    nv = pick(jnp.clip(counts[:, None] - (tid - tile_start[:, None]) * TM, 0, TM))
    ea = jnp.where(unused, jnp.sum(jnp.where(last_used, ea_of, 0)), pick(ea_of)).astype(I32)
    eb = jnp.where(unused, jnp.sum(jnp.where(last_used, eb_of, 0)), pick(eb_of)).astype(I32)

    pad128 = lambda a: jnp.concatenate([a, jnp.zeros((LANES - a.shape[0],), I32)])
    row_start = jnp.concatenate([tile_start, nt.reshape(1)]) * TM
    perm = _perm(pad128(row_start), pad128(counts), bucket, rank, n_tiles * TM)

    wg, wu, wd = w_exp_gate[0].astype(BF16), w_exp_up[0].astype(BF16), w_exp_down[0].astype(BF16)
    gtab = jnp.zeros((LANES, d), F32).at[EXT_BATCH0:EXT_BATCH0 + bsz].set(mod3[:, 5, :] * ln2_post)
    gtab_hi = gtab.astype(BF16)
    gtab_lo = (gtab - gtab_hi.astype(F32)).astype(BF16)
    out = _moe(x1ext, gtab_hi, gtab_lo, wg, wu, wd, nt.reshape(1), nv, _weight_schedule(ea, eb), perm, n_tiles)
    return out.reshape(bsz, seq, d)
```

```python
import functools

import numpy as np
import jax
import jax.numpy as jnp
from jax import lax
from jax.experimental import pallas as pl
from jax.experimental.pallas import tpu as pltpu

F32 = jnp.float32
BF16 = jnp.bfloat16
I32 = jnp.int32

D_MODEL = 1024
ATTN_HEADS = 8
HEAD_DIM = 64
WINDOW = 128
ATTN_W = 512
KV_W = 128
HG_HEADS = 4
HG_DIM = 128
HG_W = 512
HG_CHUNK = 32
IN_W = 2816
N_GROUPS = 4
EPG = 8
N_EXPERTS = 32
FF = 256
N_PAIRS = EPG * (EPG - 1) // 2
N_BUCKETS = N_GROUPS * N_PAIRS
EPS = 1e-6
NEG = -1e30
LOG2E = 1.4426950408889634

LANES = 128
H2P_W = D_MODEL // 2
EXT_W = LANES
OFF_H2P = D_MODEL
OFF_EXT = D_MODEL + H2P_W
ROW_W = OFF_EXT + EXT_W
EXT_BATCH0 = 8
U32 = jnp.uint32
SUBLANES = 8
GROUP_ROWS = SUBLANES
ROUTER_ROWS = GROUP_ROWS + N_EXPERTS

TQ = 256
TM = 128
VMEM_LIMIT = 56 * 1024 * 1024

OFF_Q, OFF_K, OFF_V, OFF_HQ, OFF_HF, OFF_HI, OFF_HG = 0, 512, 640, 768, 1280, 1792, 2304

_PAIR_A = np.array([a for a in range(EPG) for b in range(a + 1, EPG)], np.int32)
_PAIR_B = np.array([b for a in range(EPG) for b in range(a + 1, EPG)], np.int32)


def _dot(a, b):
    return jnp.dot(a, b, preferred_element_type=F32)


def _dot_nt(a, b):
    return lax.dot_general(a, b, (((1,), (1,)), ((), ())), preferred_element_type=F32)


def _dot_tn(a, b):
    return lax.dot_general(a, b, (((0,), (0,)), ((), ())), preferred_element_type=F32)


def _split(a):
    hi = a.astype(BF16)
    lo = (a - hi.astype(F32)).astype(BF16)
    return hi, lo


def _rms(x):
    return x * lax.rsqrt(jnp.mean(x * x, axis=-1, keepdims=True) + EPS)


def _sigmoid(x):
    return 0.5 * jnp.tanh(0.5 * x) + 0.5


def _ada_kernel(c_ref, w_ref, b_ref, o_ref):
    c = c_ref[...]
    ca = c * _sigmoid(c)
    c_hi, c_lo = _split(ca)
    w_hi, w_lo = _split(w_ref[...])
    o_ref[...] = _dot(c_hi, w_hi) + _dot(c_lo, w_hi) + _dot(c_hi, w_lo) + b_ref[...]


def _ada(c, w, b):
    bsz, d = c.shape
    n_out = w.shape[1]
    return pl.pallas_call(
        _ada_kernel,
        grid=(n_out // d,),
        in_specs=[pl.BlockSpec((bsz, d), lambda j: (0, 0)),
                  pl.BlockSpec((d, d), lambda j: (0, j)),
                  pl.BlockSpec((1, d), lambda j: (0, j))],
        out_specs=pl.BlockSpec((bsz, d), lambda j: (0, j)),
        out_shape=jax.ShapeDtypeStruct((bsz, n_out), F32),
        compiler_params=pltpu.CompilerParams(dimension_semantics=("arbitrary",),
                                             vmem_limit_bytes=VMEM_LIMIT),
        name="adaln",
    )(c, w, b.reshape(1, n_out))


def _attention(proj, kprev_ref, vprev_ref, sinks_ref, bias_ref, t, s_scr, m_scr, p_scr):
    tq = proj.shape[0]
    q = (proj[:, OFF_Q:OFF_Q + ATTN_W] * (HEAD_DIM ** -0.5 * LOG2E)).astype(BF16)
    kf = jnp.concatenate([kprev_ref[...], proj[:, OFF_K:OFF_K + KV_W]], axis=0)
    vf = jnp.concatenate([vprev_ref[...], proj[:, OFF_V:OFF_V + KV_W]], axis=0)
    kprev_ref[...] = proj[tq - WINDOW:, OFF_K:OFF_K + KV_W]
    vprev_ref[...] = proj[tq - WINDOW:, OFF_V:OFF_V + KV_W]

    lo = lax.broadcasted_iota(I32, kf.shape, 1) < HEAD_DIM
    kr = pltpu.roll(kf, HEAD_DIM, axis=1)
    vr = pltpu.roll(vf, HEAD_DIM, axis=1)

    def variants(a, ar):
        return [[jnp.where(lo, a, 0.0).astype(BF16), jnp.where(lo, 0.0, ar).astype(BF16)],
                [jnp.where(lo, ar, 0.0).astype(BF16), jnp.where(lo, 0.0, a).astype(BF16)]]

    kvar = variants(kf, kr)
    vvar = variants(vf, vr)

    first = jnp.where(t > 0, 0, 1)

    nblk = tq // WINDOW
    idx = lambda j, h: j * ATTN_HEADS + h
    keys = lambda a, j: a[j * WINDOW:(j + 2) * WINDOW]

    def scores():
        for j in range(nblk):
            for h in range(ATTN_HEADS):
                p, par = h // 2, h % 2
                qp = q[j * WINDOW:(j + 1) * WINDOW, p * LANES:(p + 1) * LANES]
                s = _dot_nt(qp, keys(kvar[p // 2][par], j)) + bias_ref[first if j == 0 else 0, h]
                s_scr[idx(j, h)] = s
                m_scr[idx(j, h)] = jnp.maximum(jnp.max(s, axis=-1, keepdims=True), sinks_ref[h] * LOG2E)

    def exps():
        for j in range(nblk):
            for h in range(ATTN_HEADS):
                m = m_scr[idx(j, h)]
                pe = jnp.exp2(s_scr[idx(j, h)] - m)
                p_scr[idx(j, h)] = pe.astype(BF16)
                m_scr[idx(j, h)] = 1.0 / (jnp.sum(pe, axis=-1, keepdims=True) + jnp.exp2(sinks_ref[h] * LOG2E - m))

    def values():
        blocks = []
        for j in range(nblk):
            pairs = []
            for p in range(ATTN_HEADS // 2):
                acc = None
                for par in range(2):
                    h = 2 * p + par
                    o = _dot(p_scr[idx(j, h)], keys(vvar[p // 2][par], j)) * m_scr[idx(j, h)]
                    acc = o if acc is None else acc + o
                pairs.append(acc)
            blocks.append(jnp.concatenate(pairs, axis=1))
        return jnp.concatenate(blocks, axis=0)

    return scores, exps, values


def _hgrn2(proj, lb, st_ref, hnorm, u_scr, stb_scr):
    tq = proj.shape[0]
    nc = tq // HG_CHUNK
    qr = proj[:, OFF_HQ:OFF_HQ + HG_W]
    fr = proj[:, OFF_HF:OFF_HF + HG_W]
    iv = proj[:, OFF_HI:OFF_HI + HG_W]
    gr = proj[:, OFF_HG:OFF_HG + HG_W]
    qh = qr * _sigmoid(qr)
    f = lb + (1.0 - lb) * _sigmoid(fr)
    kk = 1.0 - f
    logf = jnp.log(f)

    rmod = lax.broadcasted_iota(I32, (tq, HG_W), 0) & (HG_CHUNK - 1)
    bc = logf
    s = 1
    while s < HG_CHUNK:
        bc = bc + jnp.where(rmod >= s, pltpu.roll(bc, s, axis=0), 0.0)
        s *= 2

    b3 = bc.reshape(nc, HG_CHUNK, HG_W)
    blast = b3[:, HG_CHUNK - 1:HG_CHUNK, :]
    kend = (kk.reshape(nc, HG_CHUNK, HG_W) * jnp.exp(blast - b3)).reshape(tq, HG_W)
    decay = jnp.exp(blast).reshape(nc, HG_W)
    qdec = (qh * jnp.exp(bc)).astype(BF16)
    kdec = (kk * jnp.exp(-bc)).astype(BF16)
    kend = kend.astype(BF16)
    ivb = iv.astype(BF16)

    ri = lax.broadcasted_iota(I32, (tq, tq), 0)
    ci = lax.broadcasted_iota(I32, (tq, tq), 1)
    cmask = ((ri // HG_CHUNK) == (ci // HG_CHUNK)) & (ri >= ci)

    heads = [slice(hh * HG_DIM, (hh + 1) * HG_DIM) for hh in range(HG_HEADS)]
    chunks = [slice(n * HG_CHUNK, (n + 1) * HG_CHUNK) for n in range(nc)]

    lane_head = lax.broadcasted_iota(I32, (HG_CHUNK, HG_W), 1) // HG_DIM
    for n, rs in enumerate(chunks):
        vstack = jnp.concatenate([ivb[rs, sl] for sl in heads], axis=0)
        kblk = jnp.concatenate([jnp.where(lane_head == hh, kend[rs], 0.0).astype(BF16)
                                for hh in range(HG_HEADS)], axis=0)
        u_scr[n] = _dot_tn(vstack, kblk)

    st = st_ref[...]
    for n in range(nc):
        stb_scr[n] = st.astype(BF16)
        st = st * decay[n:n + 1] + u_scr[n]
    st_ref[...] = st

    outs = []
    for hh, sl in enumerate(heads):
        a = _dot_nt(qdec[:, sl], kdec[:, sl])
        a = jnp.where(cmask, a, 0.0).astype(BF16)
        o_intra = _dot(a, ivb[:, sl])
        inter = [_dot_nt(qdec[rs, sl], stb_scr[n, :, sl]) for n, rs in enumerate(chunks)]
        o = o_intra + jnp.concatenate(inter, axis=0)
        o = _rms(o) * hnorm[:, sl]
        g = gr[:, sl]
        outs.append(o * (g * _sigmoid(g)))
    return jnp.concatenate(outs, axis=1)


def _route_topk(h2, wr_hi, wr_lo, br):
    tq = h2.shape[0]
    h_hi, h_lo = _split(h2)
    logits = _dot(h_hi, wr_hi) + _dot(h_lo, wr_hi) + _dot(h_hi, wr_lo)
    lt = logits.T[0:ROUTER_ROWS] + br
    sub = lax.broadcasted_iota(I32, (SUBLANES, tq), 0).astype(F32)
    none = float(SUBLANES)

    gl = lt[0:GROUP_ROWS]
    gm = jnp.max(gl, axis=0, keepdims=True)
    gidx = jnp.min(jnp.where(gl == gm, sub, none), axis=0, keepdims=True)
    g_w = 1.0 / jnp.sum(jnp.exp(gl - gm), axis=0, keepdims=True)

    group_rows = lambda g: lt[GROUP_ROWS + EPG * g:GROUP_ROWS + EPG * (g + 1)]
    es = group_rows(0)
    for g in range(1, N_GROUPS):
        es = jnp.where(gidx == float(g), group_rows(g), es)
    m1 = jnp.max(es, axis=0, keepdims=True)
    i1 = jnp.min(jnp.where(es == m1, sub, none), axis=0, keepdims=True)
    e2 = jnp.where(sub == i1, NEG, es)
    m2 = jnp.max(e2, axis=0, keepdims=True)
    i2 = jnp.min(jnp.where(e2 == m2, sub, none), axis=0, keepdims=True)
    dd = jnp.exp(m2 - m1)
    w1 = g_w / (1.0 + dd)
    w2 = g_w * dd / (1.0 + dd)
    first_low = i1 < i2
    ea = jnp.minimum(i1, i2)
    eb = jnp.maximum(i1, i2)
    w_lo = jnp.where(first_low, w1, w2)
    w_hi = jnp.where(first_low, w2, w1)
    pair = ea * (2.0 * EPG - 1.0 - ea) * 0.5 + (eb - ea - 1.0)
    bucket = gidx * float(N_PAIRS) + pair
    return bucket, w_lo, w_hi


def _route_rank(bucket, w_lo, w_hi, carry_ref, bidx, live):
    tq = bucket.shape[1]
    brow = lax.broadcasted_iota(I32, (LANES, tq), 0).astype(F32)
    onehot = brow == bucket
    oh = jnp.where(onehot, live, 0.0)
    ti = lax.broadcasted_iota(I32, (tq, tq), 0)
    tj = lax.broadcasted_iota(I32, (tq, tq), 1)
    upper = jnp.where(ti < tj, 1.0, 0.0).astype(BF16)
    before = _dot(oh.astype(BF16), upper) + carry_ref[...]
    rank = jnp.sum(jnp.where(onehot, before, 0.0), axis=0, keepdims=True)
    carry_ref[...] = carry_ref[...] + jnp.sum(oh, axis=1, keepdims=True)

    lane_row = lax.broadcasted_iota(I32, (LANES - EXT_BATCH0, tq), 0)
    onehot_b = jnp.where(lane_row == bidx, 1.0, 0.0)
    info = jnp.concatenate([w_lo, w_hi, jnp.zeros((EXT_BATCH0 - 2, tq), F32), onehot_b], axis=0)
    return rank, info.T


def _mixer_kernel(sinks_ref, x_ref, mod_ref, ln1pre_ref, ln1post_ref, ln2pre_ref, anorm_ref, hnorm_ref,
                  lb_ref, win_ref, wout_ref, wrhi_ref, wrlo_ref, br_ref, bias_ref,
                  rows_ref, info_ref, cnt_ref,
                  kprev_ref, vprev_ref, st_ref, carry_ref, s_scr, m_scr, p_scr, u_scr, stb_scr,
                  h2_scr, proj_scr, keep_scr, *, tiles_per_seq):
    s = pl.program_id(0)
    n_tiles = pl.num_programs(0) - 1
    tq = x_ref.shape[0]
    t = lax.rem(jnp.minimum(s, n_tiles - 1), tiles_per_seq)
    bits = lambda a: pltpu.bitcast(a, U32)

    @pl.when(s == 0)
    def _():
        carry_ref[...] = jnp.zeros_like(carry_ref)
        h2_scr[...] = jnp.zeros_like(h2_scr)
        keep_scr[...] = jnp.zeros_like(keep_scr)

    @pl.when(t == 0)
    def _():
        st_ref[...] = jnp.zeros_like(st_ref)
        kprev_ref[...] = jnp.zeros_like(kprev_ref)
        vprev_ref[...] = jnp.zeros_like(vprev_ref)

    x = x_ref[...]
    mod = mod_ref[0]
    sh1, sc1, ga1, sh2, sc2 = mod[0:1], mod[1:2], mod[2:3], mod[3:4], mod[4:5]
    prev = jnp.maximum(s - 1, 0)

    @pl.when(s <= n_tiles)
    def _():
        bucket, w_lo, w_hi = _route_topk(h2_scr[...], wrhi_ref[...], wrlo_ref[...], br_ref[...])
        h = _rms(x) * ln1pre_ref[...] * (1.0 + sc1) + sh1
        proj_scr[...] = _dot(h.astype(BF16), win_ref[...])
        live = jnp.where(s >= 1, 1.0, 0.0)
        rank, ext = _route_rank(bucket, w_lo, w_hi, carry_ref, prev // tiles_per_seq, live)
        rows_ref[:, :, 0:OFF_EXT] = keep_scr[...].reshape(tq, 1, OFF_EXT)
        rows_ref[:, :, OFF_EXT:ROW_W] = bits(ext).reshape(tq, 1, EXT_W)
        info_ref[0] = jnp.concatenate([bucket, rank, jnp.zeros((6, tq), F32)], axis=0).astype(I32)
        cnt_ref[...] = jnp.broadcast_to(carry_ref[...], cnt_ref.shape)

    proj = proj_scr[...]
    scores, exps, values = _attention(proj, kprev_ref, vprev_ref, sinks_ref, bias_ref, t, s_scr, m_scr, p_scr)
    scores()
    exps()
    attn = _rms(values()) * anorm_ref[...]

    lbr = lb_ref[...]
    le = jnp.exp(lbr - jnp.max(lbr, axis=0, keepdims=True))
    lb = le[0:1] / jnp.sum(le, axis=0, keepdims=True)
    hg = _hgrn2(proj, lb, st_ref, hnorm_ref[...], u_scr, stb_scr)

    mix = _dot(jnp.concatenate([attn, hg], axis=1).astype(BF16), wout_ref[...])
    x1 = x + ga1 * (_rms(mix) * ln1post_ref[...])

    h2 = _rms(x1) * ln2pre_ref[...] * (1.0 + sc2) + sh2
    h2r = h2.astype(BF16).astype(F32)

    keep_scr[:, 0:D_MODEL] = bits(x1)
    keep_scr[:, OFF_H2P:OFF_EXT] = ((bits(h2r[:, 0:H2P_W]) >> 16)
                                    | (bits(h2r[:, H2P_W:D_MODEL]) & jnp.uint32(0xFFFF0000)))
    h2_scr[...] = h2


def _attn_bias():
    qi = np.arange(WINDOW)[:, None]
    kj = np.arange(2 * WINDOW)[None, :]
    dist = qi + WINDOW - kj
    in_win = (dist >= 0) & (dist < WINDOW)
    slopes = 2.0 ** (-8.0 * (np.arange(ATTN_HEADS) + 1.0) / ATTN_HEADS)
    b = np.where(in_win[None], -slopes[:, None, None] * dist[None] * LOG2E, NEG)
    b_first = np.where((kj >= WINDOW)[None], b, NEG)
    return jnp.asarray(np.stack([b, b_first]).astype(np.float32))


def _mixer(x2, mod3, sinks, ln1pre, ln1post, ln2pre, anorm, hnorm, lb, win, wout, wr_hi, wr_lo, br,
           bsz, seq):
    bias = _attn_bias()
    n = bsz * seq
    nt = seq // TQ
    n_tiles = bsz * nt
    n_sc = (TQ // WINDOW) * ATTN_HEADS
    cur = lambda s: jnp.minimum(s, n_tiles - 1)
    const = lambda s: (0, 0)
    full = lambda a: pl.BlockSpec(a.shape, const)
    return pl.pallas_call(
        functools.partial(_mixer_kernel, tiles_per_seq=nt),
        grid=(n_tiles + 1,),
        in_specs=[pl.BlockSpec(memory_space=pltpu.SMEM),
                  pl.BlockSpec((TQ, D_MODEL), lambda s: (cur(s), 0)),
                  pl.BlockSpec((1, 6, D_MODEL), lambda s: (cur(s) // nt, 0, 0)),
                  full(ln1pre), full(ln1post), full(ln2pre), full(anorm), full(hnorm), full(lb),
                  full(win), full(wout), full(wr_hi), full(wr_lo), full(br),
                  pl.BlockSpec(bias.shape, lambda s: (0, 0, 0, 0))],
        out_specs=[pl.BlockSpec((TQ, 1, ROW_W), lambda s: (jnp.maximum(s - 1, 0), 0, 0)),
                   pl.BlockSpec((1, 8, TQ), lambda s: (jnp.maximum(s - 1, 0), 0, 0)),
                   pl.BlockSpec((LANES, LANES), const)],
        out_shape=[jax.ShapeDtypeStruct((n, 1, ROW_W), U32),
                   jax.ShapeDtypeStruct((n // TQ, 8, TQ), I32),
                   jax.ShapeDtypeStruct((LANES, LANES), F32)],
        scratch_shapes=[pltpu.VMEM((WINDOW, KV_W), F32),
                        pltpu.VMEM((WINDOW, KV_W), F32),
                        pltpu.VMEM((HG_DIM, HG_W), F32),
                        pltpu.VMEM((LANES, 1), F32),
                        pltpu.VMEM((n_sc, WINDOW, 2 * WINDOW), F32),
                        pltpu.VMEM((n_sc, WINDOW, 1), F32),
                        pltpu.VMEM((n_sc, WINDOW, 2 * WINDOW), BF16),
                        pltpu.VMEM((TQ // HG_CHUNK, HG_DIM, HG_W), F32),
                        pltpu.VMEM((TQ // HG_CHUNK, HG_DIM, HG_W), BF16),
                        pltpu.VMEM((TQ, D_MODEL), F32),
                        pltpu.VMEM((TQ, IN_W), F32),
                        pltpu.VMEM((TQ, OFF_EXT), U32)],
        compiler_params=pltpu.CompilerParams(dimension_semantics=("arbitrary",),
                                             vmem_limit_bytes=VMEM_LIMIT),
        name="mixer",
    )(sinks, x2, mod3, ln1pre, ln1post, ln2pre, anorm, hnorm, lb, win, wout, wr_hi, wr_lo, br, bias)


PERM_STEPS = 8
PERM_ROWS = 8
PERM_UNROLL = 16


def _perm_kernel(rs_ref, cnt_ref, bucket_ref, rank_ref, perm_ref, pos_vmem, pos_smem, sem):
    pid = pl.program_id(0)
    rows, cols = pos_vmem.shape

    b = bucket_ref[0]
    start = jnp.zeros_like(b)
    for k in range(N_BUCKETS):
        start = jnp.where(b == k, rs_ref[k], start)
    pos_vmem[...] = start + rank_ref[0]
    copies = [pltpu.make_async_copy(pos_vmem.at[r], pos_smem.at[pl.ds(r * cols, cols)], sem) for r in range(rows)]
    for cp in copies:
        cp.start()

    @pl.when(pid == 0)
    def _():
        def per_bucket(k, carry):
            first = rs_ref[k]
            cnt = cnt_ref[k]

            def pad(r, c2):
                perm_ref[first + r] = 0
                return c2

            lax.fori_loop(cnt, ((cnt + TM - 1) // TM) * TM, pad, 0)
            return carry

        lax.fori_loop(0, N_BUCKETS, per_bucket, 0)

        def tail(blk, carry):
            for u in range(PERM_UNROLL):
                perm_ref[blk * PERM_UNROLL + u] = 0
            return carry

        lax.fori_loop(rs_ref[N_BUCKETS] // PERM_UNROLL, perm_ref.shape[0] // PERM_UNROLL, tail, 0)

    for cp in copies:
        cp.wait()
    base = pid * (rows * cols)

    def body(j, carry):
        i0 = j * PERM_UNROLL
        positions = [pos_smem[i0 + u] for u in range(PERM_UNROLL)]
        for u in range(PERM_UNROLL):
            perm_ref[positions[u]] = base + i0 + u
        return carry

    lax.fori_loop(0, rows * cols // PERM_UNROLL, body, 0)


def _perm(row_start, counts, bucket, rank, n_rows):
    n = bucket.shape[0]
    cols = n // (PERM_STEPS * PERM_ROWS)
    assert n % (PERM_STEPS * PERM_ROWS * PERM_UNROLL) == 0
    chunked = lambda a: a.reshape(PERM_STEPS, PERM_ROWS, cols)
    chunk_spec = pl.BlockSpec((1, PERM_ROWS, cols), lambda i: (i, 0, 0))
    return pl.pallas_call(
        _perm_kernel,
        grid=(PERM_STEPS,),
        in_specs=[pl.BlockSpec(memory_space=pltpu.SMEM),
                  pl.BlockSpec(memory_space=pltpu.SMEM),
                  chunk_spec, chunk_spec],
        out_specs=pl.BlockSpec(memory_space=pltpu.SMEM),
        out_shape=jax.ShapeDtypeStruct((n_rows,), I32),
        scratch_shapes=[pltpu.VMEM((PERM_ROWS, cols), I32), pltpu.SMEM((PERM_ROWS * cols,), I32),
                        pltpu.SemaphoreType.DMA(())],
        compiler_params=pltpu.CompilerParams(dimension_semantics=("arbitrary",)),
        name="perm",
    )(row_start, counts, chunked(bucket), chunked(rank))


GATHER_DEPTH = 3
OUT_SLOTS = 3


SCHED_EXPERT, SCHED_RUN_START, SCHED_SLOT, SCHED_NEXT_EXPERT = range(4)


def _moe_kernel(nt_ref, nv_ref, sched_ref, perm_ref,
                rows_hbm, gtab_hi_ref, gtab_lo_ref, wg_hbm, wu_hbm, wd_hbm,
                out_hbm, xbuf, x2d, obuf, wgu_buf, wd_buf, gsem, ssem, wsem):
    i = pl.program_id(0)
    nt = nt_ref[0]
    n_steps = pl.num_programs(0)
    last_tile = n_steps - 1
    sched = lambda field, side: sched_ref[(field * 2 + side) * n_steps + i]

    def weight_copies(side, expert, slot):
        return (pltpu.make_async_copy(wg_hbm.at[expert], wgu_buf.at[side, slot, :, 0:FF], wsem.at[side, slot]),
                pltpu.make_async_copy(wu_hbm.at[expert], wgu_buf.at[side, slot, :, FF:2 * FF], wsem.at[side, slot]),
                pltpu.make_async_copy(wd_hbm.at[expert], wd_buf.at[side, slot], wsem.at[side, slot]))
    nbuf = xbuf.shape[0] // TM

    def start_gather(tile, pred):
        sl = lax.rem(tile, nbuf)
        base = jnp.minimum(tile, last_tile) * TM
        for r in range(TM):
            @pl.when(pred)
            def _():
                tok = perm_ref[base + r]
                pltpu.make_async_copy(rows_hbm.at[tok], xbuf.at[sl * TM + r], gsem.at[sl]).start()

    def wait_gather(tile):
        sl = lax.rem(tile, nbuf)
        pltpu.make_async_copy(rows_hbm.at[pl.ds(0, TM)], xbuf.at[pl.ds(sl * TM, TM)], gsem.at[sl]).wait()

    def wait_scatter(sl, nv):
        @pl.when(nv == TM)
        def _():
            pltpu.make_async_copy(obuf.at[sl], out_hbm.at[pl.ds(0, TM)], ssem.at[sl]).wait()

        @pl.when(nv < TM)
        def _():
            def one(r, carry):
                pltpu.make_async_copy(obuf.at[sl, pl.ds(0, 1)], out_hbm.at[pl.ds(0, 1)], ssem.at[sl]).wait()
                return carry

            lax.fori_loop(0, nv, one, 0)

    def compute(xb, slot_a, slot_b):
        x1 = pltpu.bitcast(xb[:, 0:D_MODEL], F32)
        hp = xb[:, OFF_H2P:OFF_EXT]
        h2a = pltpu.bitcast(hp << 16, F32).astype(BF16)
        h2b = pltpu.bitcast(hp & jnp.uint32(0xFFFF0000), F32).astype(BF16)
        ext = pltpu.bitcast(xb[:, OFF_EXT:ROW_W], F32)
        w_lo, w_hi = ext[:, 0:1], ext[:, 1:2]
        sel = ext.astype(BF16)
        ga2 = _dot(sel, gtab_hi_ref[...]) + _dot(sel, gtab_lo_ref[...])

        def hidden(side, slot, w):
            gu = (_dot(h2a, wgu_buf[side, slot, 0:H2P_W])
                  + _dot(h2b, wgu_buf[side, slot, H2P_W:D_MODEL]))
            hg, hu = gu[:, 0:FF], gu[:, FF:2 * FF]
            return (w * ((hg * _sigmoid(hg)) * hu)).astype(BF16)

        act = jnp.concatenate([hidden(0, slot_a, w_lo), hidden(1, slot_b, w_hi)], axis=1)
        wd = jnp.concatenate([wd_buf[0, slot_a], wd_buf[1, slot_b]], axis=0)
        y = _dot(act, wd)
        return x1 + ga2 * _rms(y)

    @pl.when(i == 0)
    def _():
        for side in range(2):
            for cp in weight_copies(side, sched(SCHED_EXPERT, side), 0):
                cp.start()
        for d in range(GATHER_DEPTH):
            start_gather(d, d < nt)

    @pl.when(i < nt)
    def _():
        nv = nv_ref[i]
        osl = lax.rem(i, OUT_SLOTS)
        slots = []
        for side in range(2):
            slot = sched(SCHED_SLOT, side)
            slots.append(slot)

            @pl.when(sched(SCHED_RUN_START, side) == 1)
            def _():
                for cp in weight_copies(side, sched(SCHED_EXPERT, side), slot):
                    cp.wait()
                nxt = sched(SCHED_NEXT_EXPERT, side)

                @pl.when(nxt >= 0)
                def _():
                    for cp in weight_copies(side, nxt, 1 - slot):
                        cp.start()

        wait_gather(i)

        @pl.when(i >= OUT_SLOTS)
        def _():
            wait_scatter(osl, nv_ref[jnp.maximum(i - OUT_SLOTS, 0)])

        start_gather(i + GATHER_DEPTH, i + GATHER_DEPTH < nt)
        x2d[...] = xbuf[pl.ds(lax.rem(i, nbuf) * TM, TM)].reshape(TM, ROW_W)
        result = compute(x2d[...], *slots)

        def scatter_row(k, r):
            tok = perm_ref[i * TM + r]
            pltpu.make_async_copy(obuf.at[k, pl.ds(r, 1)], out_hbm.at[pl.ds(tok, 1)], ssem.at[k]).start()

        for k in range(OUT_SLOTS):
            @pl.when((osl == k) & (nv == TM))
            def _():
                obuf[k] = result
                for r in range(TM):
                    scatter_row(k, r)

            @pl.when((osl == k) & (nv < TM))
            def _():
                obuf[k] = result
                for r in range(TM):
                    pl.when(r < nv)(functools.partial(scatter_row, k, r))

        @pl.when(i == nt - 1)
        def _():
            wait_scatter(osl, nv)
            for back in range(1, OUT_SLOTS):
                @pl.when(i >= back)
                def _():
                    wait_scatter(lax.rem(i - back + OUT_SLOTS, OUT_SLOTS), nv_ref[jnp.maximum(i - back, 0)])


def _moe(rows, gtab_hi, gtab_lo, wg, wu, wd, nt, nv, sched, perm, n_tiles):
    n = rows.shape[0]
    const2 = lambda i, *_: (0, 0)
    grid_spec = pltpu.PrefetchScalarGridSpec(
        num_scalar_prefetch=4,
        grid=(n_tiles,),
        in_specs=[pl.BlockSpec(memory_space=pl.ANY),
                  pl.BlockSpec(gtab_hi.shape, const2),
                  pl.BlockSpec(gtab_lo.shape, const2),
                  pl.BlockSpec(memory_space=pl.ANY),
                  pl.BlockSpec(memory_space=pl.ANY),
                  pl.BlockSpec(memory_space=pl.ANY)],
        out_specs=pl.BlockSpec(memory_space=pl.ANY),
        scratch_shapes=[pltpu.VMEM(((GATHER_DEPTH + 1) * TM, 1, ROW_W), U32),
                        pltpu.VMEM((TM, ROW_W), U32),
                        pltpu.VMEM((OUT_SLOTS, TM, D_MODEL), F32),
                        pltpu.VMEM((2, 2, D_MODEL, 2 * FF), BF16),
                        pltpu.VMEM((2, 2, FF, D_MODEL), BF16),
                        pltpu.SemaphoreType.DMA((GATHER_DEPTH + 1,)),
                        pltpu.SemaphoreType.DMA((OUT_SLOTS,)),
                        pltpu.SemaphoreType.DMA((2, 2))],
    )
    return pl.pallas_call(
        _moe_kernel,
        grid_spec=grid_spec,
        out_shape=jax.ShapeDtypeStruct((n, D_MODEL), F32),
        compiler_params=pltpu.CompilerParams(dimension_semantics=("arbitrary",),
                                             vmem_limit_bytes=VMEM_LIMIT),
        name="moe",
    )(nt, nv, sched, perm, rows, gtab_hi, gtab_lo, wg, wu, wd)


def _weight_schedule(ea, eb):
    n_tiles = ea.shape[0]
    tid = jnp.arange(n_tiles, dtype=I32)
    fields = [[], [], [], []]
    for e in (ea, eb):
        start = jnp.concatenate([jnp.ones((1,), I32), (e[1:] != e[:-1]).astype(I32)])
        slot = (jnp.cumsum(start) - 1) % 2
        start_idx = jnp.where(start == 1, tid, n_tiles)
        next_start = jnp.concatenate([lax.cummin(start_idx, reverse=True)[1:], jnp.full((1,), n_tiles, I32)])
        nxt = jnp.where(next_start < n_tiles, e[jnp.minimum(next_start, n_tiles - 1)], -1)
        for f, v in zip(fields, (e, start, slot, nxt)):
            f.append(v.astype(I32))
    return jnp.concatenate([v for f in fields for v in f])


def kernel(x, c, ln1_pre, ln1_post, ln2_pre, ln2_post, w_ada, b_ada, w_in, attn_sinks, attn_out_norm,
           hgrn_lb, hgrn_out_norm, w_out, w_router_group, b_router_group, w_router_expert,
           b_router_expert, w_exp_gate, w_exp_up, w_exp_down):
    bsz, seq, d = x.shape
    assert d == D_MODEL and seq % TQ == 0 and w_ada.shape[0] == 1 and hgrn_lb.shape[0] == 2
    assert bsz <= LANES - EXT_BATCH0 and (bsz * seq) % TM == 0
    n = bsz * seq

    mod = _ada(c, w_ada[0], b_ada[0])
    mod3 = mod.reshape(bsz, 6, d)

    wr = jnp.concatenate([w_router_group[0], jnp.zeros((d, GROUP_ROWS - N_GROUPS), F32), w_router_expert[0],
                          jnp.zeros((d, LANES - ROUTER_ROWS), F32)], axis=1)
    br = jnp.concatenate([b_router_group[0], jnp.full((GROUP_ROWS - N_GROUPS,), NEG, F32), b_router_expert[0]])
    wr_hi = wr.astype(BF16)
    wr_lo = (wr - wr_hi.astype(F32)).astype(BF16)

    x1ext, info, cnt = _mixer(
        x.reshape(n, d), mod3, attn_sinks[0], ln1_pre, ln1_post, ln2_pre, attn_out_norm, hgrn_out_norm,
        hgrn_lb, w_in[0].astype(BF16), w_out[0].astype(BF16), wr_hi, wr_lo, br.reshape(ROUTER_ROWS, 1),
        bsz, seq)

    n_tiles = n // TM + N_BUCKETS
    counts = cnt[:N_BUCKETS, 0].astype(I32)
    tiles_per = (counts + TM - 1) // TM
    tile_end = jnp.cumsum(tiles_per)
    tile_start = tile_end - tiles_per
    nt = tile_end[-1]
    bucket = info[:, 0, :].reshape(n)
    rank = info[:, 1, :].reshape(n)
    tid = jnp.arange(n_tiles, dtype=I32)[None, :]
    member = (tid >= tile_start[:, None]) & (tid < tile_end[:, None])
    pick = lambda per_bucket: jnp.sum(jnp.where(member, per_bucket, 0), axis=0).astype(I32)
    bidx = np.arange(N_BUCKETS, dtype=np.int32)
    ea_of = jnp.asarray((bidx // N_PAIRS) * EPG + _PAIR_A[bidx % N_PAIRS])[:, None]
    eb_of = jnp.asarray((bidx // N_PAIRS) * EPG + _PAIR_B[bidx % N_PAIRS])[:, None]
    last_used = jnp.arange(N_BUCKETS)[:, None] == jnp.max(jnp.where(tiles_per > 0, jnp.arange(N_BUCKETS), 0))
    unused = tid[0] >= nt
    nv = pick(jnp.clip(counts[:, None] - (tid - tile_start[:, None]) * TM, 0, TM))
    ea = jnp.where(unused, jnp.sum(jnp.where(last_used, ea_of, 0)), pick(ea_of)).astype(I32)
    eb = jnp.where(unused, jnp.sum(jnp.where(last_used, eb_of, 0)), pick(eb_of)).astype(I32)

    pad128 = lambda a: jnp.concatenate([a, jnp.zeros((LANES - a.shape[0],), I32)])
    row_start = jnp.concatenate([tile_start, nt.reshape(1)]) * TM
    perm = _perm(pad128(row_start), pad128(counts), bucket, rank, n_tiles * TM)

    wg, wu, wd = w_exp_gate[0].astype(BF16), w_exp_up[0].astype(BF16), w_exp_down[0].astype(BF16)
    gtab = jnp.zeros((LANES, d), F32).at[EXT_BATCH0:EXT_BATCH0 + bsz].set(mod3[:, 5, :] * ln2_post)
    gtab_hi = gtab.astype(BF16)
    gtab_lo = (gtab - gtab_hi.astype(F32)).astype(BF16)
    out = _moe(x1ext, gtab_hi, gtab_lo, wg, wu, wd, nt.reshape(1), nv, _weight_schedule(ea, eb), perm, n_tiles)
    return out.reshape(bsz, seq, d)
```

```python
import functools

import numpy as np
import jax
import jax.numpy as jnp
from jax import lax
from jax.experimental import pallas as pl
from jax.experimental.pallas import tpu as pltpu

F32 = jnp.float32
BF16 = jnp.bfloat16
I32 = jnp.int32

D_MODEL = 1024
ATTN_HEADS = 8
HEAD_DIM = 64
WINDOW = 128
ATTN_W = 512
KV_W = 128
HG_HEADS = 4
HG_DIM = 128
HG_W = 512
HG_CHUNK = 32
IN_W = 2816
N_GROUPS = 4
EPG = 8
N_EXPERTS = 32
FF = 256
N_PAIRS = EPG * (EPG - 1) // 2
N_BUCKETS = N_GROUPS * N_PAIRS
EPS = 1e-6
NEG = -1e30
LOG2E = 1.4426950408889634

LANES = 128
H2P_W = D_MODEL // 2
EXT_W = LANES
OFF_H2P = D_MODEL
OFF_EXT = D_MODEL + H2P_W
ROW_W = OFF_EXT + EXT_W
EXT_BATCH0 = 8
U32 = jnp.uint32
SUBLANES = 8
GROUP_ROWS = SUBLANES
ROUTER_ROWS = GROUP_ROWS + N_EXPERTS

TQ = 256
TM = 128
VMEM_LIMIT = 56 * 1024 * 1024

OFF_Q, OFF_K, OFF_V, OFF_HQ, OFF_HF, OFF_HI, OFF_HG = 0, 512, 640, 768, 1280, 1792, 2304

_PAIR_A = np.array([a for a in range(EPG) for b in range(a + 1, EPG)], np.int32)
_PAIR_B = np.array([b for a in range(EPG) for b in range(a + 1, EPG)], np.int32)


def _dot(a, b):
    return jnp.dot(a, b, preferred_element_type=F32)


def _dot_nt(a, b):
    return lax.dot_general(a, b, (((1,), (1,)), ((), ())), preferred_element_type=F32)


def _dot_tn(a, b):
    return lax.dot_general(a, b, (((0,), (0,)), ((), ())), preferred_element_type=F32)


def _split(a):
    hi = a.astype(BF16)
    lo = (a - hi.astype(F32)).astype(BF16)
    return hi, lo


def _rms(x):
    return x * lax.rsqrt(jnp.mean(x * x, axis=-1, keepdims=True) + EPS)


def _silu(x):
    half = 0.5 * x
    return half * jnp.tanh(half) + half


def _ada_kernel(c_ref, w_ref, b_ref, o_ref):
    c = c_ref[...]
    ca = _silu(c)
    c_hi, c_lo = _split(ca)
    w_hi, w_lo = _split(w_ref[...])
    o_ref[...] = _dot(c_hi, w_hi) + _dot(c_lo, w_hi) + _dot(c_hi, w_lo) + b_ref[...]


def _ada(c, w, b):
    bsz, d = c.shape
    n_out = w.shape[1]
    return pl.pallas_call(
        _ada_kernel,
        grid=(n_out // d,),
        in_specs=[pl.BlockSpec((bsz, d), lambda j: (0, 0)),
                  pl.BlockSpec((d, d), lambda j: (0, j)),
                  pl.BlockSpec((1, d), lambda j: (0, j))],
        out_specs=pl.BlockSpec((bsz, d), lambda j: (0, j)),
        out_shape=jax.ShapeDtypeStruct((bsz, n_out), F32),
        compiler_params=pltpu.CompilerParams(dimension_semantics=("arbitrary",),
                                             vmem_limit_bytes=VMEM_LIMIT),
        name="adaln",
    )(c, w, b.reshape(1, n_out))


def _attention(proj, kprev_ref, vprev_ref, sinks_ref, bias_ref, t, s_scr, m_scr, p_scr):
    tq = proj.shape[0]
    q = (proj[:, OFF_Q:OFF_Q + ATTN_W] * (HEAD_DIM ** -0.5 * LOG2E)).astype(BF16)
    kf = jnp.concatenate([kprev_ref[...], proj[:, OFF_K:OFF_K + KV_W]], axis=0)
    vf = jnp.concatenate([vprev_ref[...], proj[:, OFF_V:OFF_V + KV_W]], axis=0)
    kprev_ref[...] = proj[tq - WINDOW:, OFF_K:OFF_K + KV_W]
    vprev_ref[...] = proj[tq - WINDOW:, OFF_V:OFF_V + KV_W]

    lo = lax.broadcasted_iota(I32, kf.shape, 1) < HEAD_DIM
    kr = pltpu.roll(kf, HEAD_DIM, axis=1)
    vr = pltpu.roll(vf, HEAD_DIM, axis=1)

    def variants(a, ar):
        return [[jnp.where(lo, a, 0.0).astype(BF16), jnp.where(lo, 0.0, ar).astype(BF16)],
                [jnp.where(lo, ar, 0.0).astype(BF16), jnp.where(lo, 0.0, a).astype(BF16)]]

    kvar = variants(kf, kr)
    vvar = variants(vf, vr)

    first = jnp.where(t > 0, 0, 1)

    nblk = tq // WINDOW
    idx = lambda j, h: j * ATTN_HEADS + h
    keys = lambda a, j: a[j * WINDOW:(j + 2) * WINDOW]

    def scores():
        for j in range(nblk):
            for h in range(ATTN_HEADS):
                p, par = h // 2, h % 2
                qp = q[j * WINDOW:(j + 1) * WINDOW, p * LANES:(p + 1) * LANES]
                s = _dot_nt(qp, keys(kvar[p // 2][par], j)) + bias_ref[first if j == 0 else 0, h]
                s_scr[idx(j, h)] = s
                m_scr[idx(j, h)] = jnp.maximum(jnp.max(s, axis=-1, keepdims=True), sinks_ref[h] * LOG2E)

    def exps():
        for j in range(nblk):
            for h in range(ATTN_HEADS):
                m = m_scr[idx(j, h)]
                pe = jnp.exp2(s_scr[idx(j, h)] - m)
                p_scr[idx(j, h)] = pe.astype(BF16)
                m_scr[idx(j, h)] = 1.0 / (jnp.sum(pe, axis=-1, keepdims=True) + jnp.exp2(sinks_ref[h] * LOG2E - m))

    def values():
        blocks = []
        for j in range(nblk):
            pairs = []
            for p in range(ATTN_HEADS // 2):
                acc = None
                for par in range(2):
                    h = 2 * p + par
                    o = _dot(p_scr[idx(j, h)], keys(vvar[p // 2][par], j)) * m_scr[idx(j, h)]
                    acc = o if acc is None else acc + o
                pairs.append(acc)
            blocks.append(jnp.concatenate(pairs, axis=1))
        return jnp.concatenate(blocks, axis=0)

    return scores, exps, values


def _hgrn2(proj, lb, st_ref, hnorm, u_scr, stb_scr):
    tq = proj.shape[0]
    nc = tq // HG_CHUNK
    qr = proj[:, OFF_HQ:OFF_HQ + HG_W]
    fr = proj[:, OFF_HF:OFF_HF + HG_W]
    iv = proj[:, OFF_HI:OFF_HI + HG_W]
    gr = proj[:, OFF_HG:OFF_HG + HG_W]
    qh = _silu(qr)
    half_gap = 0.5 * (1.0 - lb)
    f = (lb + half_gap) + half_gap * jnp.tanh(0.5 * fr)
    kk = 1.0 - f
    logf = jnp.log(f)

    rmod = lax.broadcasted_iota(I32, (tq, HG_W), 0) & (HG_CHUNK - 1)
    bc = logf
    s = 1
    while s < HG_CHUNK:
        bc = bc + jnp.where(rmod >= s, pltpu.roll(bc, s, axis=0), 0.0)
        s *= 2

    b3 = bc.reshape(nc, HG_CHUNK, HG_W)
    blast = b3[:, HG_CHUNK - 1:HG_CHUNK, :]
    kend = (kk.reshape(nc, HG_CHUNK, HG_W) * jnp.exp(blast - b3)).reshape(tq, HG_W)
    decay = jnp.exp(blast).reshape(nc, HG_W)
    qdec = (qh * jnp.exp(bc)).astype(BF16)
    kdec = (kk * jnp.exp(-bc)).astype(BF16)
    kend = kend.astype(BF16)
    ivb = iv.astype(BF16)

    ri = lax.broadcasted_iota(I32, (tq, tq), 0)
    ci = lax.broadcasted_iota(I32, (tq, tq), 1)
    cmask = ((ri // HG_CHUNK) == (ci // HG_CHUNK)) & (ri >= ci)

    heads = [slice(hh * HG_DIM, (hh + 1) * HG_DIM) for hh in range(HG_HEADS)]
    chunks = [slice(n * HG_CHUNK, (n + 1) * HG_CHUNK) for n in range(nc)]

    lane_head = lax.broadcasted_iota(I32, (HG_CHUNK, HG_W), 1) // HG_DIM
    for n, rs in enumerate(chunks):
        vstack = jnp.concatenate([ivb[rs, sl] for sl in heads], axis=0)
        kblk = jnp.concatenate([jnp.where(lane_head == hh, kend[rs], 0.0).astype(BF16)
                                for hh in range(HG_HEADS)], axis=0)
        u_scr[n] = _dot_tn(vstack, kblk)

    st = st_ref[...]
    for n in range(nc):
        stb_scr[n] = st.astype(BF16)
        st = st * decay[n:n + 1] + u_scr[n]
    st_ref[...] = st

    outs = []
    for hh, sl in enumerate(heads):
        a = _dot_nt(qdec[:, sl], kdec[:, sl])
        a = jnp.where(cmask, a, 0.0).astype(BF16)
        o_intra = _dot(a, ivb[:, sl])
        inter = [_dot_nt(qdec[rs, sl], stb_scr[n, :, sl]) for n, rs in enumerate(chunks)]
        o = o_intra + jnp.concatenate(inter, axis=0)
        o = _rms(o) * hnorm[:, sl]
        g = gr[:, sl]
        outs.append(o * _silu(g))
    return jnp.concatenate(outs, axis=1)


def _route_topk(h2, wr_hi, wr_lo, br):
    tq = h2.shape[0]
    h_hi, h_lo = _split(h2)
    logits = _dot(h_hi, wr_hi) + _dot(h_lo, wr_hi) + _dot(h_hi, wr_lo)
    lt = logits.T[0:ROUTER_ROWS] + br
    sub = lax.broadcasted_iota(I32, (SUBLANES, tq), 0).astype(F32)
    none = float(SUBLANES)

    gl = lt[0:GROUP_ROWS]
    gm = jnp.max(gl, axis=0, keepdims=True)
    gidx = jnp.min(jnp.where(gl == gm, sub, none), axis=0, keepdims=True)
    g_w = 1.0 / jnp.sum(jnp.exp(gl - gm), axis=0, keepdims=True)

    group_rows = lambda g: lt[GROUP_ROWS + EPG * g:GROUP_ROWS + EPG * (g + 1)]
    es = group_rows(0)
    for g in range(1, N_GROUPS):
        es = jnp.where(gidx == float(g), group_rows(g), es)
    m1 = jnp.max(es, axis=0, keepdims=True)
    i1 = jnp.min(jnp.where(es == m1, sub, none), axis=0, keepdims=True)
    e2 = jnp.where(sub == i1, NEG, es)
    m2 = jnp.max(e2, axis=0, keepdims=True)
    i2 = jnp.min(jnp.where(e2 == m2, sub, none), axis=0, keepdims=True)
    dd = jnp.exp(m2 - m1)
    w1 = g_w / (1.0 + dd)
    w2 = g_w * dd / (1.0 + dd)
    first_low = i1 < i2
    ea = jnp.minimum(i1, i2)
    eb = jnp.maximum(i1, i2)
    w_lo = jnp.where(first_low, w1, w2)
    w_hi = jnp.where(first_low, w2, w1)
    pair = ea * (2.0 * EPG - 1.0 - ea) * 0.5 + (eb - ea - 1.0)
    bucket = gidx * float(N_PAIRS) + pair
    return bucket, w_lo, w_hi


def _route_rank(bucket, w_lo, w_hi, carry_ref, bidx, live):
    tq = bucket.shape[1]
    brow = lax.broadcasted_iota(I32, (LANES, tq), 0).astype(F32)
    onehot = brow == bucket
    oh = jnp.where(onehot, live, 0.0)
    ti = lax.broadcasted_iota(I32, (tq, tq), 0)
    tj = lax.broadcasted_iota(I32, (tq, tq), 1)
    upper = jnp.where(ti < tj, 1.0, 0.0).astype(BF16)
    before = _dot(oh.astype(BF16), upper) + carry_ref[...]
    rank = jnp.sum(jnp.where(onehot, before, 0.0), axis=0, keepdims=True)
    carry_ref[...] = carry_ref[...] + jnp.sum(oh, axis=1, keepdims=True)

    lane_row = lax.broadcasted_iota(I32, (LANES - EXT_BATCH0, tq), 0)
    onehot_b = jnp.where(lane_row == bidx, 1.0, 0.0)
    info = jnp.concatenate([w_lo, w_hi, jnp.zeros((EXT_BATCH0 - 2, tq), F32), onehot_b], axis=0)
    return rank, info.T


def _mixer_kernel(sinks_ref, x_ref, mod_ref, ln1pre_ref, ln1post_ref, ln2pre_ref, anorm_ref, hnorm_ref,
                  lb_ref, win_ref, wout_ref, wrhi_ref, wrlo_ref, br_ref, bias_ref,
                  rows_ref, info_ref, cnt_ref,
                  kprev_ref, vprev_ref, st_ref, carry_ref, s_scr, m_scr, p_scr, u_scr, stb_scr,
                  h2_scr, proj_scr, keep_scr, *, tiles_per_seq):
    s = pl.program_id(0)
    n_tiles = pl.num_programs(0) - 1
    tq = x_ref.shape[0]
    t = lax.rem(jnp.minimum(s, n_tiles - 1), tiles_per_seq)
    bits = lambda a: pltpu.bitcast(a, U32)

    @pl.when(s == 0)
    def _():
        carry_ref[...] = jnp.zeros_like(carry_ref)
        h2_scr[...] = jnp.zeros_like(h2_scr)
        keep_scr[...] = jnp.zeros_like(keep_scr)

    @pl.when(t == 0)
    def _():
        st_ref[...] = jnp.zeros_like(st_ref)
        kprev_ref[...] = jnp.zeros_like(kprev_ref)
        vprev_ref[...] = jnp.zeros_like(vprev_ref)

    x = x_ref[...]
    mod = mod_ref[0]
    sh1, sc1, ga1, sh2, sc2 = mod[0:1], mod[1:2], mod[2:3], mod[3:4], mod[4:5]
    prev = jnp.maximum(s - 1, 0)

    @pl.when(s <= n_tiles)
    def _():
        bucket, w_lo, w_hi = _route_topk(h2_scr[...], wrhi_ref[...], wrlo_ref[...], br_ref[...])
        h = _rms(x) * (ln1pre_ref[...] * (1.0 + sc1)) + sh1
        proj_scr[...] = _dot(h.astype(BF16), win_ref[...])
        live = jnp.where(s >= 1, 1.0, 0.0)
        rank, ext = _route_rank(bucket, w_lo, w_hi, carry_ref, prev // tiles_per_seq, live)
        rows_ref[:, :, 0:OFF_EXT] = keep_scr[...].reshape(tq, 1, OFF_EXT)
        rows_ref[:, :, OFF_EXT:ROW_W] = bits(ext).reshape(tq, 1, EXT_W)
        info_ref[0] = jnp.concatenate([bucket, rank, jnp.zeros((6, tq), F32)], axis=0).astype(I32)
        cnt_ref[...] = jnp.broadcast_to(carry_ref[...], cnt_ref.shape)

    proj = proj_scr[...]
    scores, exps, values = _attention(proj, kprev_ref, vprev_ref, sinks_ref, bias_ref, t, s_scr, m_scr, p_scr)
    scores()
    exps()
    attn = _rms(values()) * anorm_ref[...]

    lbr = lb_ref[...]
    le = jnp.exp(lbr - jnp.max(lbr, axis=0, keepdims=True))
    lb = le[0:1] / jnp.sum(le, axis=0, keepdims=True)
    hg = _hgrn2(proj, lb, st_ref, hnorm_ref[...], u_scr, stb_scr)

    mix = _dot(jnp.concatenate([attn, hg], axis=1).astype(BF16), wout_ref[...])
    x1 = x + _rms(mix) * (ga1 * ln1post_ref[...])

    h2 = _rms(x1) * (ln2pre_ref[...] * (1.0 + sc2)) + sh2
    h2r = h2.astype(BF16).astype(F32)

    keep_scr[:, 0:D_MODEL] = bits(x1)
    keep_scr[:, OFF_H2P:OFF_EXT] = ((bits(h2r[:, 0:H2P_W]) >> 16)
                                    | (bits(h2r[:, H2P_W:D_MODEL]) & jnp.uint32(0xFFFF0000)))
    h2_scr[...] = h2


def _attn_bias():
    qi = np.arange(WINDOW)[:, None]
    kj = np.arange(2 * WINDOW)[None, :]
    dist = qi + WINDOW - kj
    in_win = (dist >= 0) & (dist < WINDOW)
    slopes = 2.0 ** (-8.0 * (np.arange(ATTN_HEADS) + 1.0) / ATTN_HEADS)
    b = np.where(in_win[None], -slopes[:, None, None] * dist[None] * LOG2E, NEG)
    b_first = np.where((kj >= WINDOW)[None], b, NEG)
    return jnp.asarray(np.stack([b, b_first]).astype(np.float32))


def _mixer(x2, mod3, sinks, ln1pre, ln1post, ln2pre, anorm, hnorm, lb, win, wout, wr_hi, wr_lo, br,
           bsz, seq):
    bias = _attn_bias()
    n = bsz * seq
    nt = seq // TQ
    n_tiles = bsz * nt
    n_sc = (TQ // WINDOW) * ATTN_HEADS
    cur = lambda s: jnp.minimum(s, n_tiles - 1)
    const = lambda s: (0, 0)
    full = lambda a: pl.BlockSpec(a.shape, const)
    return pl.pallas_call(
        functools.partial(_mixer_kernel, tiles_per_seq=nt),
        grid=(n_tiles + 1,),
        in_specs=[pl.BlockSpec(memory_space=pltpu.SMEM),
                  pl.BlockSpec((TQ, D_MODEL), lambda s: (cur(s), 0)),
                  pl.BlockSpec((1, 6, D_MODEL), lambda s: (cur(s) // nt, 0, 0)),
                  full(ln1pre), full(ln1post), full(ln2pre), full(anorm), full(hnorm), full(lb),
                  full(win), full(wout), full(wr_hi), full(wr_lo), full(br),
                  pl.BlockSpec(bias.shape, lambda s: (0, 0, 0, 0))],
        out_specs=[pl.BlockSpec((TQ, 1, ROW_W), lambda s: (jnp.maximum(s - 1, 0), 0, 0)),
                   pl.BlockSpec((1, 8, TQ), lambda s: (jnp.maximum(s - 1, 0), 0, 0)),
                   pl.BlockSpec((LANES, LANES), const)],
        out_shape=[jax.ShapeDtypeStruct((n, 1, ROW_W), U32),
                   jax.ShapeDtypeStruct((n // TQ, 8, TQ), I32),
                   jax.ShapeDtypeStruct((LANES, LANES), F32)],
        scratch_shapes=[pltpu.VMEM((WINDOW, KV_W), F32),
                        pltpu.VMEM((WINDOW, KV_W), F32),
                        pltpu.VMEM((HG_DIM, HG_W), F32),
                        pltpu.VMEM((LANES, 1), F32),
                        pltpu.VMEM((n_sc, WINDOW, 2 * WINDOW), F32),
                        pltpu.VMEM((n_sc, WINDOW, 1), F32),
                        pltpu.VMEM((n_sc, WINDOW, 2 * WINDOW), BF16),
                        pltpu.VMEM((TQ // HG_CHUNK, HG_DIM, HG_W), F32),
                        pltpu.VMEM((TQ // HG_CHUNK, HG_DIM, HG_W), BF16),
                        pltpu.VMEM((TQ, D_MODEL), F32),
                        pltpu.VMEM((TQ, IN_W), F32),
                        pltpu.VMEM((TQ, OFF_EXT), U32)],
        compiler_params=pltpu.CompilerParams(dimension_semantics=("arbitrary",),
                                             vmem_limit_bytes=VMEM_LIMIT),
        name="mixer",
    )(sinks, x2, mod3, ln1pre, ln1post, ln2pre, anorm, hnorm, lb, win, wout, wr_hi, wr_lo, br, bias)


PERM_STEPS = 8
PERM_ROWS = 8
PERM_UNROLL = 16


def _perm_kernel(rs_ref, cnt_ref, bucket_ref, rank_ref, perm_ref, pos_vmem, pos_smem, sem):
    pid = pl.program_id(0)
    rows, cols = pos_vmem.shape

    b = bucket_ref[0]
    start = jnp.zeros_like(b)
    for k in range(N_BUCKETS):
        start = jnp.where(b == k, rs_ref[k], start)
    pos_vmem[...] = start + rank_ref[0]
    copies = [pltpu.make_async_copy(pos_vmem.at[r], pos_smem.at[pl.ds(r * cols, cols)], sem) for r in range(rows)]
    for cp in copies:
        cp.start()

    @pl.when(pid == 0)
    def _():
        def per_bucket(k, carry):
            first = rs_ref[k]
            cnt = cnt_ref[k]

            def pad(r, c2):
                perm_ref[first + r] = 0
                return c2

            lax.fori_loop(cnt, ((cnt + TM - 1) // TM) * TM, pad, 0)
            return carry

        lax.fori_loop(0, N_BUCKETS, per_bucket, 0)

        def tail(blk, carry):
            for u in range(PERM_UNROLL):
                perm_ref[blk * PERM_UNROLL + u] = 0
            return carry

        lax.fori_loop(rs_ref[N_BUCKETS] // PERM_UNROLL, perm_ref.shape[0] // PERM_UNROLL, tail, 0)

    for cp in copies:
        cp.wait()
    base = pid * (rows * cols)

    def body(j, carry):
        i0 = j * PERM_UNROLL
        positions = [pos_smem[i0 + u] for u in range(PERM_UNROLL)]
        for u in range(PERM_UNROLL):
            perm_ref[positions[u]] = base + i0 + u
        return carry

    lax.fori_loop(0, rows * cols // PERM_UNROLL, body, 0)


def _perm(row_start, counts, bucket, rank, n_rows):
    n = bucket.shape[0]
    cols = n // (PERM_STEPS * PERM_ROWS)
    assert n % (PERM_STEPS * PERM_ROWS * PERM_UNROLL) == 0
    chunked = lambda a: a.reshape(PERM_STEPS, PERM_ROWS, cols)
    chunk_spec = pl.BlockSpec((1, PERM_ROWS, cols), lambda i: (i, 0, 0))
    return pl.pallas_call(
        _perm_kernel,
        grid=(PERM_STEPS,),
        in_specs=[pl.BlockSpec(memory_space=pltpu.SMEM),
                  pl.BlockSpec(memory_space=pltpu.SMEM),
                  chunk_spec, chunk_spec],
        out_specs=pl.BlockSpec(memory_space=pltpu.SMEM),
        out_shape=jax.ShapeDtypeStruct((n_rows,), I32),
        scratch_shapes=[pltpu.VMEM((PERM_ROWS, cols), I32), pltpu.SMEM((PERM_ROWS * cols,), I32),
                        pltpu.SemaphoreType.DMA(())],
        compiler_params=pltpu.CompilerParams(dimension_semantics=("arbitrary",)),
        name="perm",
    )(row_start, counts, chunked(bucket), chunked(rank))


GATHER_DEPTH = 3
OUT_SLOTS = 3


SCHED_EXPERT, SCHED_RUN_START, SCHED_SLOT, SCHED_NEXT_EXPERT = range(4)


def _moe_kernel(nt_ref, nv_ref, sched_ref, perm_ref,
                rows_hbm, gtab_hi_ref, gtab_lo_ref, wg_hbm, wu_hbm, wd_hbm,
                out_hbm, xbuf, x2d, obuf, wgu_buf, wd_buf, gsem, ssem, wsem):
    i = pl.program_id(0)
    nt = nt_ref[0]
    n_steps = pl.num_programs(0)
    last_tile = n_steps - 1
    sched = lambda field, side: sched_ref[(field * 2 + side) * n_steps + i]

    def weight_copies(side, expert, slot):
        return (pltpu.make_async_copy(wg_hbm.at[expert], wgu_buf.at[side, slot, :, 0:FF], wsem.at[side, slot]),
                pltpu.make_async_copy(wu_hbm.at[expert], wgu_buf.at[side, slot, :, FF:2 * FF], wsem.at[side, slot]),
                pltpu.make_async_copy(wd_hbm.at[expert], wd_buf.at[side, slot], wsem.at[side, slot]))
    nbuf = xbuf.shape[0] // TM

    def start_gather(tile, pred):
        sl = lax.rem(tile, nbuf)
        base = jnp.minimum(tile, last_tile) * TM
        for r in range(TM):
            @pl.when(pred)
            def _():
                tok = perm_ref[base + r]
                pltpu.make_async_copy(rows_hbm.at[tok], xbuf.at[sl * TM + r], gsem.at[sl]).start()

    def wait_gather(tile):
        sl = lax.rem(tile, nbuf)
        pltpu.make_async_copy(rows_hbm.at[pl.ds(0, TM)], xbuf.at[pl.ds(sl * TM, TM)], gsem.at[sl]).wait()

    def wait_scatter(sl, nv):
        @pl.when(nv == TM)
        def _():
            pltpu.make_async_copy(obuf.at[sl], out_hbm.at[pl.ds(0, TM)], ssem.at[sl]).wait()

        @pl.when(nv < TM)
        def _():
            def one(r, carry):
                pltpu.make_async_copy(obuf.at[sl, pl.ds(0, 1)], out_hbm.at[pl.ds(0, 1)], ssem.at[sl]).wait()
                return carry

            lax.fori_loop(0, nv, one, 0)

    def compute(xb, slot_a, slot_b):
        x1 = pltpu.bitcast(xb[:, 0:D_MODEL], F32)
        hp = xb[:, OFF_H2P:OFF_EXT]
        h2a = pltpu.bitcast(hp << 16, F32).astype(BF16)
        h2b = pltpu.bitcast(hp & jnp.uint32(0xFFFF0000), F32).astype(BF16)
        ext = pltpu.bitcast(xb[:, OFF_EXT:ROW_W], F32)
        w_lo, w_hi = ext[:, 0:1], ext[:, 1:2]
        sel = ext.astype(BF16)
        ga2 = _dot(sel, gtab_hi_ref[...]) + _dot(sel, gtab_lo_ref[...])

        def hidden(side, slot, w):
            gu = (_dot(h2a, wgu_buf[side, slot, 0:H2P_W])
                  + _dot(h2b, wgu_buf[side, slot, H2P_W:D_MODEL]))
            hg, hu = gu[:, 0:FF], gu[:, FF:2 * FF]
            return (w * (_silu(hg) * hu)).astype(BF16)

        act = jnp.concatenate([hidden(0, slot_a, w_lo), hidden(1, slot_b, w_hi)], axis=1)
        wd = jnp.concatenate([wd_buf[0, slot_a], wd_buf[1, slot_b]], axis=0)
        y = _dot(act, wd)
        return x1 + ga2 * _rms(y)

    @pl.when(i == 0)
    def _():
        for side in range(2):
            for cp in weight_copies(side, sched(SCHED_EXPERT, side), 0):
                cp.start()
        for d in range(GATHER_DEPTH):
            start_gather(d, d < nt)

    @pl.when(i < nt)
    def _():
        nv = nv_ref[i]
        osl = lax.rem(i, OUT_SLOTS)
        slots = []
        for side in range(2):
            slot = sched(SCHED_SLOT, side)
            slots.append(slot)

            @pl.when(sched(SCHED_RUN_START, side) == 1)
            def _():
                for cp in weight_copies(side, sched(SCHED_EXPERT, side), slot):
                    cp.wait()
                nxt = sched(SCHED_NEXT_EXPERT, side)

                @pl.when(nxt >= 0)
                def _():
                    for cp in weight_copies(side, nxt, 1 - slot):
                        cp.start()

        wait_gather(i)

        @pl.when(i >= OUT_SLOTS)
        def _():
            wait_scatter(osl, nv_ref[jnp.maximum(i - OUT_SLOTS, 0)])

        start_gather(i + GATHER_DEPTH, i + GATHER_DEPTH < nt)
        x2d[...] = xbuf[pl.ds(lax.rem(i, nbuf) * TM, TM)].reshape(TM, ROW_W)
        result = compute(x2d[...], *slots)

        def scatter_row(k, r):
            tok = perm_ref[i * TM + r]
            pltpu.make_async_copy(obuf.at[k, pl.ds(r, 1)], out_hbm.at[pl.ds(tok, 1)], ssem.at[k]).start()

        for k in range(OUT_SLOTS):
            @pl.when((osl == k) & (nv == TM))
            def _():
                obuf[k] = result
                for r in range(TM):
                    scatter_row(k, r)

            @pl.when((osl == k) & (nv < TM))
            def _():
                obuf[k] = result
                for r in range(TM):
                    pl.when(r < nv)(functools.partial(scatter_row, k, r))

        @pl.when(i == nt - 1)
        def _():
            wait_scatter(osl, nv)
            for back in range(1, OUT_SLOTS):
                @pl.when(i >= back)
                def _():
                    wait_scatter(lax.rem(i - back + OUT_SLOTS, OUT_SLOTS), nv_ref[jnp.maximum(i - back, 0)])


def _moe(rows, gtab_hi, gtab_lo, wg, wu, wd, nt, nv, sched, perm, n_tiles):
    n = rows.shape[0]
    const2 = lambda i, *_: (0, 0)
    grid_spec = pltpu.PrefetchScalarGridSpec(
        num_scalar_prefetch=4,
        grid=(n_tiles,),
        in_specs=[pl.BlockSpec(memory_space=pl.ANY),
                  pl.BlockSpec(gtab_hi.shape, const2),
                  pl.BlockSpec(gtab_lo.shape, const2),
                  pl.BlockSpec(memory_space=pl.ANY),
                  pl.BlockSpec(memory_space=pl.ANY),
                  pl.BlockSpec(memory_space=pl.ANY)],
        out_specs=pl.BlockSpec(memory_space=pl.ANY),
        scratch_shapes=[pltpu.VMEM(((GATHER_DEPTH + 1) * TM, 1, ROW_W), U32),
                        pltpu.VMEM((TM, ROW_W), U32),
                        pltpu.VMEM((OUT_SLOTS, TM, D_MODEL), F32),
                        pltpu.VMEM((2, 2, D_MODEL, 2 * FF), BF16),
                        pltpu.VMEM((2, 2, FF, D_MODEL), BF16),
                        pltpu.SemaphoreType.DMA((GATHER_DEPTH + 1,)),
                        pltpu.SemaphoreType.DMA((OUT_SLOTS,)),
                        pltpu.SemaphoreType.DMA((2, 2))],
    )
    return pl.pallas_call(
        _moe_kernel,
        grid_spec=grid_spec,
        out_shape=jax.ShapeDtypeStruct((n, D_MODEL), F32),
        compiler_params=pltpu.CompilerParams(dimension_semantics=("arbitrary",),
                                             vmem_limit_bytes=VMEM_LIMIT),
        name="moe",
    )(nt, nv, sched, perm, rows, gtab_hi, gtab_lo, wg, wu, wd)


def _weight_schedule(ea, eb):
    n_tiles = ea.shape[0]
    tid = jnp.arange(n_tiles, dtype=I32)
    fields = [[], [], [], []]
    for e in (ea, eb):
        start = jnp.concatenate([jnp.ones((1,), I32), (e[1:] != e[:-1]).astype(I32)])
        slot = (jnp.cumsum(start) - 1) % 2
        start_idx = jnp.where(start == 1, tid, n_tiles)
        next_start = jnp.concatenate([lax.cummin(start_idx, reverse=True)[1:], jnp.full((1,), n_tiles, I32)])
        nxt = jnp.where(next_start < n_tiles, e[jnp.minimum(next_start, n_tiles - 1)], -1)
        for f, v in zip(fields, (e, start, slot, nxt)):
            f.append(v.astype(I32))
    return jnp.concatenate([v for f in fields for v in f])


def kernel(x, c, ln1_pre, ln1_post, ln2_pre, ln2_post, w_ada, b_ada, w_in, attn_sinks, attn_out_norm,
           hgrn_lb, hgrn_out_norm, w_out, w_router_group, b_router_group, w_router_expert,
           b_router_expert, w_exp_gate, w_exp_up, w_exp_down):
    bsz, seq, d = x.shape
    assert d == D_MODEL and seq % TQ == 0 and w_ada.shape[0] == 1 and hgrn_lb.shape[0] == 2
    assert bsz <= LANES - EXT_BATCH0 and (bsz * seq) % TM == 0
    n = bsz * seq

    mod = _ada(c, w_ada[0], b_ada[0])
    mod3 = mod.reshape(bsz, 6, d)

    wr = jnp.concatenate([w_router_group[0], jnp.zeros((d, GROUP_ROWS - N_GROUPS), F32), w_router_expert[0],
                          jnp.zeros((d, LANES - ROUTER_ROWS), F32)], axis=1)
    br = jnp.concatenate([b_router_group[0], jnp.full((GROUP_ROWS - N_GROUPS,), NEG, F32), b_router_expert[0]])
    wr_hi = wr.astype(BF16)
    wr_lo = (wr - wr_hi.astype(F32)).astype(BF16)

    x1ext, info, cnt = _mixer(
        x.reshape(n, d), mod3, attn_sinks[0], ln1_pre, ln1_post, ln2_pre, attn_out_norm, hgrn_out_norm,
        hgrn_lb, w_in[0].astype(BF16), w_out[0].astype(BF16), wr_hi, wr_lo, br.reshape(ROUTER_ROWS, 1),
        bsz, seq)

    n_tiles = n // TM + N_BUCKETS
    counts = cnt[:N_BUCKETS, 0].astype(I32)
    tiles_per = (counts + TM - 1) // TM
    tile_end = jnp.cumsum(tiles_per)
    tile_start = tile_end - tiles_per
    nt = tile_end[-1]
    bucket = info[:, 0, :].reshape(n)
    rank = info[:, 1, :].reshape(n)
    tid = jnp.arange(n_tiles, dtype=I32)[None, :]
    member = (tid >= tile_start[:, None]) & (tid < tile_end[:, None])
    pick = lambda per_bucket: jnp.sum(jnp.where(member, per_bucket, 0), axis=0).astype(I32)
    bidx = np.arange(N_BUCKETS, dtype=np.int32)
    ea_of = jnp.asarray((bidx // N_PAIRS) * EPG + _PAIR_A[bidx % N_PAIRS])[:, None]
    eb_of = jnp.asarray((bidx // N_PAIRS) * EPG + _PAIR_B[bidx % N_PAIRS])[:, None]
    last_used = jnp.arange(N_BUCKETS)[:, None] == jnp.max(jnp.where(tiles_per > 0, jnp.arange(N_BUCKETS), 0))
    unused = tid[0] >= nt
    nv = pick(jnp.clip(counts[:, None] - (tid - tile_start[:, None]) * TM, 0, TM))
    ea = jnp.where(unused, jnp.sum(jnp.where(last_used, ea_of, 0)), pick(ea_of)).astype(I32)
    eb = jnp.where(unused, jnp.sum(jnp.where(last_used, eb_of, 0)), pick(eb_of)).astype(I32)

    pad128 = lambda a: jnp.concatenate([a, jnp.zeros((LANES - a.shape[0],), I32)])
    row_start = jnp.concatenate([tile_start, nt.reshape(1)]) * TM
    perm = _perm(pad128(row_start), pad128(counts), bucket, rank, n_tiles * TM)

    wg, wu, wd = w_exp_gate[0].astype(BF16), w_exp_up[0].astype(BF16), w_exp_down[0].astype(BF16)
    gtab = jnp.zeros((LANES, d), F32).at[EXT_BATCH0:EXT_BATCH0 + bsz].set(mod3[:, 5, :] * ln2_post)
    gtab_hi = gtab.astype(BF16)
    gtab_lo = (gtab - gtab_hi.astype(F32)).astype(BF16)
    out = _moe(x1ext, gtab_hi, gtab_lo, wg, wu, wd, nt.reshape(1), nv, _weight_schedule(ea, eb), perm, n_tiles)
    return out.reshape(bsz, seq, d)
```

```python
import functools

import numpy as np
import jax
import jax.numpy as jnp
from jax import lax
from jax.experimental import pallas as pl
from jax.experimental.pallas import tpu as pltpu

F32 = jnp.float32
BF16 = jnp.bfloat16
I32 = jnp.int32

D_MODEL = 1024
ATTN_HEADS = 8
HEAD_DIM = 64
WINDOW = 128
ATTN_W = 512
KV_W = 128
HG_HEADS = 4
HG_DIM = 128
HG_W = 512
HG_CHUNK = 32
IN_W = 2816
N_GROUPS = 4
EPG = 8
N_EXPERTS = 32
FF = 256
N_PAIRS = EPG * (EPG - 1) // 2
N_BUCKETS = N_GROUPS * N_PAIRS
EPS = 1e-6
NEG = -1e30
LOG2E = 1.4426950408889634

LANES = 128
H2P_W = D_MODEL // 2
EXT_W = LANES
OFF_H2P = D_MODEL
OFF_EXT = D_MODEL + H2P_W
ROW_W = OFF_EXT + EXT_W
EXT_BATCH0 = 8
U32 = jnp.uint32
SUBLANES = 8
GROUP_ROWS = SUBLANES
ROUTER_ROWS = GROUP_ROWS + N_EXPERTS

TQ = 256
TM = 128
VMEM_LIMIT = 56 * 1024 * 1024

OFF_Q, OFF_K, OFF_V, OFF_HQ, OFF_HF, OFF_HI, OFF_HG = 0, 512, 640, 768, 1280, 1792, 2304

_PAIR_A = np.array([a for a in range(EPG) for b in range(a + 1, EPG)], np.int32)
_PAIR_B = np.array([b for a in range(EPG) for b in range(a + 1, EPG)], np.int32)


def _dot(a, b):
    return jnp.dot(a, b, preferred_element_type=F32)


def _dot_nt(a, b):
    return lax.dot_general(a, b, (((1,), (1,)), ((), ())), preferred_element_type=F32)


def _dot_tn(a, b):
    return lax.dot_general(a, b, (((0,), (0,)), ((), ())), preferred_element_type=F32)


def _split(a):
    hi = a.astype(BF16)
    lo = (a - hi.astype(F32)).astype(BF16)
    return hi, lo


def _rms(x):
    return x * lax.rsqrt(jnp.mean(x * x, axis=-1, keepdims=True) + EPS)


def _silu(x):
    half = 0.5 * x
    return half * jnp.tanh(half) + half


def _ada_kernel(c_ref, w_ref, b_ref, o_ref):
    c = c_ref[...]
    ca = _silu(c)
    c_hi, c_lo = _split(ca)
    w_hi, w_lo = _split(w_ref[...])
    o_ref[...] = _dot(c_hi, w_hi) + _dot(c_lo, w_hi) + _dot(c_hi, w_lo) + b_ref[...]


def _ada(c, w, b):
    bsz, d = c.shape
    n_out = w.shape[1]
    return pl.pallas_call(
        _ada_kernel,
        grid=(n_out // d,),
        in_specs=[pl.BlockSpec((bsz, d), lambda j: (0, 0)),
                  pl.BlockSpec((d, d), lambda j: (0, j)),
                  pl.BlockSpec((1, d), lambda j: (0, j))],
        out_specs=pl.BlockSpec((bsz, d), lambda j: (0, j)),
        out_shape=jax.ShapeDtypeStruct((bsz, n_out), F32),
        compiler_params=pltpu.CompilerParams(dimension_semantics=("arbitrary",),
                                             vmem_limit_bytes=VMEM_LIMIT),
        name="adaln",
    )(c, w, b.reshape(1, n_out))


def _attention(proj, kprev_ref, vprev_ref, sinks_ref, bias_ref, t, s_scr, m_scr, p_scr):
    tq = proj.shape[0]
    q = (proj[:, OFF_Q:OFF_Q + ATTN_W] * (HEAD_DIM ** -0.5 * LOG2E)).astype(BF16)
    kf = jnp.concatenate([kprev_ref[...], proj[:, OFF_K:OFF_K + KV_W]], axis=0)
    vf = jnp.concatenate([vprev_ref[...], proj[:, OFF_V:OFF_V + KV_W]], axis=0)
    kprev_ref[...] = proj[tq - WINDOW:, OFF_K:OFF_K + KV_W]
    vprev_ref[...] = proj[tq - WINDOW:, OFF_V:OFF_V + KV_W]

    lo = lax.broadcasted_iota(I32, kf.shape, 1) < HEAD_DIM
    kr = pltpu.roll(kf, HEAD_DIM, axis=1)
    vr = pltpu.roll(vf, HEAD_DIM, axis=1)

    def variants(a, ar):
        return [[jnp.where(lo, a, 0.0).astype(BF16), jnp.where(lo, 0.0, ar).astype(BF16)],
                [jnp.where(lo, ar, 0.0).astype(BF16), jnp.where(lo, 0.0, a).astype(BF16)]]

    kvar = variants(kf, kr)
    vvar = variants(vf, vr)

    first = jnp.where(t > 0, 0, 1)

    nblk = tq // WINDOW
    idx = lambda j, h: j * ATTN_HEADS + h
    keys = lambda a, j: a[j * WINDOW:(j + 2) * WINDOW]

    def scores():
        for j in range(nblk):
            for h in range(ATTN_HEADS):
                p, par = h // 2, h % 2
                qp = q[j * WINDOW:(j + 1) * WINDOW, p * LANES:(p + 1) * LANES]
                s = _dot_nt(qp, keys(kvar[p // 2][par], j)) + bias_ref[first if j == 0 else 0, h]
                s_scr[idx(j, h)] = s
                m_scr[idx(j, h)] = jnp.maximum(jnp.max(s, axis=-1, keepdims=True), sinks_ref[h] * LOG2E)

    def exps():
        for j in range(nblk):
            for h in range(ATTN_HEADS):
                m = m_scr[idx(j, h)]
                pe = jnp.exp2(s_scr[idx(j, h)] - m)
                p_scr[idx(j, h)] = pe.astype(BF16)
                m_scr[idx(j, h)] = 1.0 / (jnp.sum(pe, axis=-1, keepdims=True) + jnp.exp2(sinks_ref[h] * LOG2E - m))

    def values():
        blocks = []
        for j in range(nblk):
            pairs = []
            for p in range(ATTN_HEADS // 2):
                acc = None
                for par in range(2):
                    h = 2 * p + par
                    o = _dot(p_scr[idx(j, h)], keys(vvar[p // 2][par], j)) * m_scr[idx(j, h)]
                    acc = o if acc is None else acc + o
                pairs.append(acc)
            blocks.append(jnp.concatenate(pairs, axis=1))
        return jnp.concatenate(blocks, axis=0)

    return scores, exps, values


def _hgrn2(proj, lb, st_ref, hnorm, u_scr, stb_scr):
    tq = proj.shape[0]
    nc = tq // HG_CHUNK
    qr = proj[:, OFF_HQ:OFF_HQ + HG_W]
    fr = proj[:, OFF_HF:OFF_HF + HG_W]
    iv = proj[:, OFF_HI:OFF_HI + HG_W]
    gr = proj[:, OFF_HG:OFF_HG + HG_W]
    qh = _silu(qr)
    half_gap = 0.5 * (1.0 - lb)
    f = (lb + half_gap) + half_gap * jnp.tanh(0.5 * fr)
    kk = 1.0 - f

    rmod = lax.broadcasted_iota(I32, (tq, HG_W), 0) & (HG_CHUNK - 1)
    bc = jnp.log2(f)
    s = 1
    while s < HG_CHUNK:
        bc = bc + jnp.where(rmod >= s, pltpu.roll(bc, s, axis=0), 0.0)
        s *= 2

    b3 = bc.reshape(nc, HG_CHUNK, HG_W)
    blast = b3[:, HG_CHUNK - 1:HG_CHUNK, :]
    kend = (kk.reshape(nc, HG_CHUNK, HG_W) * jnp.exp2(blast - b3)).reshape(tq, HG_W)
    decay = jnp.exp2(blast).reshape(nc, HG_W)
    qdec = (qh * jnp.exp2(bc)).astype(BF16)
    kdec = (kk * jnp.exp2(-bc)).astype(BF16)
    kend = kend.astype(BF16)
    ivb = iv.astype(BF16)

    ri = lax.broadcasted_iota(I32, (tq, tq), 0)
    ci = lax.broadcasted_iota(I32, (tq, tq), 1)
    cmask = ((ri // HG_CHUNK) == (ci // HG_CHUNK)) & (ri >= ci)

    heads = [slice(hh * HG_DIM, (hh + 1) * HG_DIM) for hh in range(HG_HEADS)]
    chunks = [slice(n * HG_CHUNK, (n + 1) * HG_CHUNK) for n in range(nc)]

    lane_head = lax.broadcasted_iota(I32, (HG_CHUNK, HG_W), 1) // HG_DIM
    for n, rs in enumerate(chunks):
        vstack = jnp.concatenate([ivb[rs, sl] for sl in heads], axis=0)
        kblk = jnp.concatenate([jnp.where(lane_head == hh, kend[rs], 0.0).astype(BF16)
                                for hh in range(HG_HEADS)], axis=0)
        u_scr[n] = _dot_tn(vstack, kblk)

    st = st_ref[...]
    for n in range(nc):
        stb_scr[n] = st.astype(BF16)
        st = st * decay[n:n + 1] + u_scr[n]
    st_ref[...] = st

    outs = []
    for hh, sl in enumerate(heads):
        a = _dot_nt(qdec[:, sl], kdec[:, sl])
        a = jnp.where(cmask, a, 0.0).astype(BF16)
        o_intra = _dot(a, ivb[:, sl])
        inter = [_dot_nt(qdec[rs, sl], stb_scr[n, :, sl]) for n, rs in enumerate(chunks)]
        o = o_intra + jnp.concatenate(inter, axis=0)
        o = _rms(o) * hnorm[:, sl]
        g = gr[:, sl]
        outs.append(o * _silu(g))
    return jnp.concatenate(outs, axis=1)


def _route_topk(h2, wr_hi, wr_lo, br):
    tq = h2.shape[0]
    h_hi, h_lo = _split(h2)
    logits = _dot(h_hi, wr_hi) + _dot(h_lo, wr_hi) + _dot(h_hi, wr_lo)
    lt = logits.T[0:ROUTER_ROWS] + br
    sub = lax.broadcasted_iota(I32, (SUBLANES, tq), 0).astype(F32)
    none = float(SUBLANES)

    gl = lt[0:GROUP_ROWS]
    gm = jnp.max(gl, axis=0, keepdims=True)
    gidx = jnp.min(jnp.where(gl == gm, sub, none), axis=0, keepdims=True)
    g_w = 1.0 / jnp.sum(jnp.exp(gl - gm), axis=0, keepdims=True)

    group_rows = lambda g: lt[GROUP_ROWS + EPG * g:GROUP_ROWS + EPG * (g + 1)]
    es = group_rows(0)
    for g in range(1, N_GROUPS):
        es = jnp.where(gidx == float(g), group_rows(g), es)
    m1 = jnp.max(es, axis=0, keepdims=True)
    i1 = jnp.min(jnp.where(es == m1, sub, none), axis=0, keepdims=True)
    e2 = jnp.where(sub == i1, NEG, es)
    m2 = jnp.max(e2, axis=0, keepdims=True)
    i2 = jnp.min(jnp.where(e2 == m2, sub, none), axis=0, keepdims=True)
    dd = jnp.exp(m2 - m1)
    w1 = g_w / (1.0 + dd)
    w2 = g_w * dd / (1.0 + dd)
    first_low = i1 < i2
    ea = jnp.minimum(i1, i2)
    eb = jnp.maximum(i1, i2)
    w_lo = jnp.where(first_low, w1, w2)
    w_hi = jnp.where(first_low, w2, w1)
    pair = ea * (2.0 * EPG - 1.0 - ea) * 0.5 + (eb - ea - 1.0)
    bucket = gidx * float(N_PAIRS) + pair
    return bucket, w_lo, w_hi


def _route_rank(bucket, w_lo, w_hi, carry_ref, bidx, live):
    tq = bucket.shape[1]
    brow = lax.broadcasted_iota(I32, (LANES, tq), 0).astype(F32)
    onehot = brow == bucket
    oh = jnp.where(onehot, live, 0.0)
    ti = lax.broadcasted_iota(I32, (tq, tq), 0)
    tj = lax.broadcasted_iota(I32, (tq, tq), 1)
    upper = jnp.where(ti < tj, 1.0, 0.0).astype(BF16)
    before = _dot(oh.astype(BF16), upper) + carry_ref[...]
    rank = jnp.sum(jnp.where(onehot, before, 0.0), axis=0, keepdims=True)
    carry_ref[...] = carry_ref[...] + jnp.sum(oh, axis=1, keepdims=True)

    lane_row = lax.broadcasted_iota(I32, (LANES - EXT_BATCH0, tq), 0)
    onehot_b = jnp.where(lane_row == bidx, 1.0, 0.0)
    info = jnp.concatenate([w_lo, w_hi, jnp.zeros((EXT_BATCH0 - 2, tq), F32), onehot_b], axis=0)
    return rank, info.T


def _mixer_kernel(sinks_ref, x_ref, mod_ref, ln1pre_ref, ln1post_ref, ln2pre_ref, anorm_ref, hnorm_ref,
                  lb_ref, win_ref, wout_ref, wrhi_ref, wrlo_ref, br_ref, bias_ref,
                  rows_ref, info_ref, cnt_ref,
                  kprev_ref, vprev_ref, st_ref, carry_ref, s_scr, m_scr, p_scr, u_scr, stb_scr,
                  h2_scr, proj_scr, keep_scr, *, tiles_per_seq):
    s = pl.program_id(0)
    n_tiles = pl.num_programs(0) - 1
    tq = x_ref.shape[0]
    t = lax.rem(jnp.minimum(s, n_tiles - 1), tiles_per_seq)
    bits = lambda a: pltpu.bitcast(a, U32)

    @pl.when(s == 0)
    def _():
        carry_ref[...] = jnp.zeros_like(carry_ref)
        h2_scr[...] = jnp.zeros_like(h2_scr)
        keep_scr[...] = jnp.zeros_like(keep_scr)

    @pl.when(t == 0)
    def _():
        st_ref[...] = jnp.zeros_like(st_ref)
        kprev_ref[...] = jnp.zeros_like(kprev_ref)
        vprev_ref[...] = jnp.zeros_like(vprev_ref)

    x = x_ref[...]
    mod = mod_ref[0]
    sh1, sc1, ga1, sh2, sc2 = mod[0:1], mod[1:2], mod[2:3], mod[3:4], mod[4:5]
    prev = jnp.maximum(s - 1, 0)

    @pl.when(s <= n_tiles)
    def _():
        bucket, w_lo, w_hi = _route_topk(h2_scr[...], wrhi_ref[...], wrlo_ref[...], br_ref[...])
        h = _rms(x) * (ln1pre_ref[...] * (1.0 + sc1)) + sh1
        proj_scr[...] = _dot(h.astype(BF16), win_ref[...])
        live = jnp.where(s >= 1, 1.0, 0.0)
        rank, ext = _route_rank(bucket, w_lo, w_hi, carry_ref, prev // tiles_per_seq, live)
        rows_ref[:, :, 0:OFF_EXT] = keep_scr[...].reshape(tq, 1, OFF_EXT)
        rows_ref[:, :, OFF_EXT:ROW_W] = bits(ext).reshape(tq, 1, EXT_W)
        info_ref[0] = jnp.concatenate([bucket, rank, jnp.zeros((6, tq), F32)], axis=0).astype(I32)
        cnt_ref[...] = jnp.broadcast_to(carry_ref[...], cnt_ref.shape)

    proj = proj_scr[...]
    scores, exps, values = _attention(proj, kprev_ref, vprev_ref, sinks_ref, bias_ref, t, s_scr, m_scr, p_scr)
    scores()
    exps()
    attn = _rms(values()) * anorm_ref[...]

    lbr = lb_ref[...]
    le = jnp.exp(lbr - jnp.max(lbr, axis=0, keepdims=True))
    lb = le[0:1] / jnp.sum(le, axis=0, keepdims=True)
    hg = _hgrn2(proj, lb, st_ref, hnorm_ref[...], u_scr, stb_scr)

    mix = _dot(jnp.concatenate([attn, hg], axis=1).astype(BF16), wout_ref[...])
    x1 = x + _rms(mix) * (ga1 * ln1post_ref[...])

    h2 = _rms(x1) * (ln2pre_ref[...] * (1.0 + sc2)) + sh2
    h2r = h2.astype(BF16).astype(F32)

    keep_scr[:, 0:D_MODEL] = bits(x1)
    keep_scr[:, OFF_H2P:OFF_EXT] = ((bits(h2r[:, 0:H2P_W]) >> 16)
                                    | (bits(h2r[:, H2P_W:D_MODEL]) & jnp.uint32(0xFFFF0000)))
    h2_scr[...] = h2


def _attn_bias():
    qi = np.arange(WINDOW)[:, None]
    kj = np.arange(2 * WINDOW)[None, :]
    dist = qi + WINDOW - kj
    in_win = (dist >= 0) & (dist < WINDOW)
    slopes = 2.0 ** (-8.0 * (np.arange(ATTN_HEADS) + 1.0) / ATTN_HEADS)
    b = np.where(in_win[None], -slopes[:, None, None] * dist[None] * LOG2E, NEG)
    b_first = np.where((kj >= WINDOW)[None], b, NEG)
    return jnp.asarray(np.stack([b, b_first]).astype(np.float32))


def _mixer(x2, mod3, sinks, ln1pre, ln1post, ln2pre, anorm, hnorm, lb, win, wout, wr_hi, wr_lo, br,
           bsz, seq):
    bias = _attn_bias()
    n = bsz * seq
    nt = seq // TQ
    n_tiles = bsz * nt
    n_sc = (TQ // WINDOW) * ATTN_HEADS
    cur = lambda s: jnp.minimum(s, n_tiles - 1)
    const = lambda s: (0, 0)
    full = lambda a: pl.BlockSpec(a.shape, const)
    return pl.pallas_call(
        functools.partial(_mixer_kernel, tiles_per_seq=nt),
        grid=(n_tiles + 1,),
        in_specs=[pl.BlockSpec(memory_space=pltpu.SMEM),
                  pl.BlockSpec((TQ, D_MODEL), lambda s: (cur(s), 0)),
                  pl.BlockSpec((1, 6, D_MODEL), lambda s: (cur(s) // nt, 0, 0)),
                  full(ln1pre), full(ln1post), full(ln2pre), full(anorm), full(hnorm), full(lb),
                  full(win), full(wout), full(wr_hi), full(wr_lo), full(br),
                  pl.BlockSpec(bias.shape, lambda s: (0, 0, 0, 0))],
        out_specs=[pl.BlockSpec((TQ, 1, ROW_W), lambda s: (jnp.maximum(s - 1, 0), 0, 0)),
                   pl.BlockSpec((1, 8, TQ), lambda s: (jnp.maximum(s - 1, 0), 0, 0)),
                   pl.BlockSpec((LANES, LANES), const)],
        out_shape=[jax.ShapeDtypeStruct((n, 1, ROW_W), U32),
                   jax.ShapeDtypeStruct((n // TQ, 8, TQ), I32),
                   jax.ShapeDtypeStruct((LANES, LANES), F32)],
        scratch_shapes=[pltpu.VMEM((WINDOW, KV_W), F32),
                        pltpu.VMEM((WINDOW, KV_W), F32),
                        pltpu.VMEM((HG_DIM, HG_W), F32),
                        pltpu.VMEM((LANES, 1), F32),
                        pltpu.VMEM((n_sc, WINDOW, 2 * WINDOW), F32),
                        pltpu.VMEM((n_sc, WINDOW, 1), F32),
                        pltpu.VMEM((n_sc, WINDOW, 2 * WINDOW), BF16),
                        pltpu.VMEM((TQ // HG_CHUNK, HG_DIM, HG_W), F32),
                        pltpu.VMEM((TQ // HG_CHUNK, HG_DIM, HG_W), BF16),
                        pltpu.VMEM((TQ, D_MODEL), F32),
                        pltpu.VMEM((TQ, IN_W), F32),
                        pltpu.VMEM((TQ, OFF_EXT), U32)],
        compiler_params=pltpu.CompilerParams(dimension_semantics=("arbitrary",),
                                             vmem_limit_bytes=VMEM_LIMIT),
        name="mixer",
    )(sinks, x2, mod3, ln1pre, ln1post, ln2pre, anorm, hnorm, lb, win, wout, wr_hi, wr_lo, br, bias)


PERM_STEPS = 8
PERM_ROWS = 8
PERM_UNROLL = 16


def _perm_kernel(rs_ref, cnt_ref, bucket_ref, rank_ref, perm_ref, pos_vmem, pos_smem, sem):
    pid = pl.program_id(0)
    rows, cols = pos_vmem.shape

    b = bucket_ref[0]
    start = jnp.zeros_like(b)
    for k in range(N_BUCKETS):
        start = jnp.where(b == k, rs_ref[k], start)
    pos_vmem[...] = start + rank_ref[0]
    copies = [pltpu.make_async_copy(pos_vmem.at[r], pos_smem.at[pl.ds(r * cols, cols)], sem) for r in range(rows)]
    for cp in copies:
        cp.start()

    @pl.when(pid == 0)
    def _():
        def per_bucket(k, carry):
            first = rs_ref[k]
            cnt = cnt_ref[k]

            def pad(r, c2):
                perm_ref[first + r] = 0
                return c2

            lax.fori_loop(cnt, ((cnt + TM - 1) // TM) * TM, pad, 0)
            return carry

        lax.fori_loop(0, N_BUCKETS, per_bucket, 0)

        def tail(blk, carry):
            for u in range(PERM_UNROLL):
                perm_ref[blk * PERM_UNROLL + u] = 0
            return carry

        lax.fori_loop(rs_ref[N_BUCKETS] // PERM_UNROLL, perm_ref.shape[0] // PERM_UNROLL, tail, 0)

    for cp in copies:
        cp.wait()
    base = pid * (rows * cols)

    def body(j, carry):
        i0 = j * PERM_UNROLL
        positions = [pos_smem[i0 + u] for u in range(PERM_UNROLL)]
        for u in range(PERM_UNROLL):
            perm_ref[positions[u]] = base + i0 + u
        return carry

    lax.fori_loop(0, rows * cols // PERM_UNROLL, body, 0)


def _perm(row_start, counts, bucket, rank, n_rows):
    n = bucket.shape[0]
    cols = n // (PERM_STEPS * PERM_ROWS)
    assert n % (PERM_STEPS * PERM_ROWS * PERM_UNROLL) == 0
    chunked = lambda a: a.reshape(PERM_STEPS, PERM_ROWS, cols)
    chunk_spec = pl.BlockSpec((1, PERM_ROWS, cols), lambda i: (i, 0, 0))
    return pl.pallas_call(
        _perm_kernel,
        grid=(PERM_STEPS,),
        in_specs=[pl.BlockSpec(memory_space=pltpu.SMEM),
                  pl.BlockSpec(memory_space=pltpu.SMEM),
                  chunk_spec, chunk_spec],
        out_specs=pl.BlockSpec(memory_space=pltpu.SMEM),
        out_shape=jax.ShapeDtypeStruct((n_rows,), I32),
        scratch_shapes=[pltpu.VMEM((PERM_ROWS, cols), I32), pltpu.SMEM((PERM_ROWS * cols,), I32),
                        pltpu.SemaphoreType.DMA(())],
        compiler_params=pltpu.CompilerParams(dimension_semantics=("arbitrary",)),
        name="perm",
    )(row_start, counts, chunked(bucket), chunked(rank))


GATHER_DEPTH = 3
OUT_SLOTS = 3


SCHED_EXPERT, SCHED_RUN_START, SCHED_SLOT, SCHED_NEXT_EXPERT = range(4)


def _moe_kernel(nt_ref, nv_ref, sched_ref, perm_ref,
                rows_hbm, gtab_hi_ref, gtab_lo_ref, wg_hbm, wu_hbm, wd_hbm,
                out_hbm, xbuf, x2d, obuf, wgu_buf, wd_buf, gsem, ssem, wsem):
    i = pl.program_id(0)
    nt = nt_ref[0]
    n_steps = pl.num_programs(0)
    last_tile = n_steps - 1
    sched = lambda field, side: sched_ref[(field * 2 + side) * n_steps + i]

    def weight_copies(side, expert, slot):
        return (pltpu.make_async_copy(wg_hbm.at[expert], wgu_buf.at[side, slot, :, 0:FF], wsem.at[side, slot]),
                pltpu.make_async_copy(wu_hbm.at[expert], wgu_buf.at[side, slot, :, FF:2 * FF], wsem.at[side, slot]),
                pltpu.make_async_copy(wd_hbm.at[expert], wd_buf.at[side, slot], wsem.at[side, slot]))
    nbuf = xbuf.shape[0] // TM

    def start_gather(tile, pred):
        sl = lax.rem(tile, nbuf)
        base = jnp.minimum(tile, last_tile) * TM
        for r in range(TM):
            @pl.when(pred)
            def _():
                tok = perm_ref[base + r]
                pltpu.make_async_copy(rows_hbm.at[tok], xbuf.at[sl * TM + r], gsem.at[sl]).start()

    def wait_gather(tile):
        sl = lax.rem(tile, nbuf)
        pltpu.make_async_copy(rows_hbm.at[pl.ds(0, TM)], xbuf.at[pl.ds(sl * TM, TM)], gsem.at[sl]).wait()

    def wait_scatter(sl, nv):
        @pl.when(nv == TM)
        def _():
            pltpu.make_async_copy(obuf.at[sl], out_hbm.at[pl.ds(0, TM)], ssem.at[sl]).wait()

        @pl.when(nv < TM)
        def _():
            def one(r, carry):
                pltpu.make_async_copy(obuf.at[sl, pl.ds(0, 1)], out_hbm.at[pl.ds(0, 1)], ssem.at[sl]).wait()
                return carry

            lax.fori_loop(0, nv, one, 0)

    def compute(xb, slot_a, slot_b):
        x1 = pltpu.bitcast(xb[:, 0:D_MODEL], F32)
        hp = xb[:, OFF_H2P:OFF_EXT]
        h2a = pltpu.bitcast(hp << 16, F32).astype(BF16)
        h2b = pltpu.bitcast(hp & jnp.uint32(0xFFFF0000), F32).astype(BF16)
        ext = pltpu.bitcast(xb[:, OFF_EXT:ROW_W], F32)
        w_lo, w_hi = ext[:, 0:1], ext[:, 1:2]
        sel = ext.astype(BF16)
        ga2 = _dot(sel, gtab_hi_ref[...]) + _dot(sel, gtab_lo_ref[...])

        def hidden(side, slot, w):
            gu = (_dot(h2a, wgu_buf[side, slot, 0:H2P_W])
                  + _dot(h2b, wgu_buf[side, slot, H2P_W:D_MODEL]))
            hg, hu = gu[:, 0:FF], gu[:, FF:2 * FF]
            return (w * (_silu(hg) * hu)).astype(BF16)

        act = jnp.concatenate([hidden(0, slot_a, w_lo), hidden(1, slot_b, w_hi)], axis=1)
        wd = jnp.concatenate([wd_buf[0, slot_a], wd_buf[1, slot_b]], axis=0)
        y = _dot(act, wd)
        return x1 + ga2 * _rms(y)

    @pl.when(i == 0)
    def _():
        for side in range(2):
            for cp in weight_copies(side, sched(SCHED_EXPERT, side), 0):
                cp.start()
        for d in range(GATHER_DEPTH):
            start_gather(d, d < nt)

    @pl.when(i < nt)
    def _():
        nv = nv_ref[i]
        osl = lax.rem(i, OUT_SLOTS)
        slots = []
        for side in range(2):
            slot = sched(SCHED_SLOT, side)
            slots.append(slot)

            @pl.when(sched(SCHED_RUN_START, side) == 1)
            def _():
                for cp in weight_copies(side, sched(SCHED_EXPERT, side), slot):
                    cp.wait()
                nxt = sched(SCHED_NEXT_EXPERT, side)

                @pl.when(nxt >= 0)
                def _():
                    for cp in weight_copies(side, nxt, 1 - slot):
                        cp.start()

        wait_gather(i)

        @pl.when(i >= OUT_SLOTS)
        def _():
            wait_scatter(osl, nv_ref[jnp.maximum(i - OUT_SLOTS, 0)])

        start_gather(i + GATHER_DEPTH, i + GATHER_DEPTH < nt)
        x2d[...] = xbuf[pl.ds(lax.rem(i, nbuf) * TM, TM)].reshape(TM, ROW_W)
        result = compute(x2d[...], *slots)

        def scatter_row(k, r):
            tok = perm_ref[i * TM + r]
            pltpu.make_async_copy(obuf.at[k, pl.ds(r, 1)], out_hbm.at[pl.ds(tok, 1)], ssem.at[k]).start()

        for k in range(OUT_SLOTS):
            @pl.when((osl == k) & (nv == TM))
            def _():
                obuf[k] = result
                for r in range(TM):
                    scatter_row(k, r)

            @pl.when((osl == k) & (nv < TM))
            def _():
                obuf[k] = result
                for r in range(TM):
                    pl.when(r < nv)(functools.partial(scatter_row, k, r))

        @pl.when(i == nt - 1)
        def _():
            wait_scatter(osl, nv)
            for back in range(1, OUT_SLOTS):
                @pl.when(i >= back)
                def _():
                    wait_scatter(lax.rem(i - back + OUT_SLOTS, OUT_SLOTS), nv_ref[jnp.maximum(i - back, 0)])


def _moe(rows, gtab_hi, gtab_lo, wg, wu, wd, nt, nv, sched, perm, n_tiles):
    n = rows.shape[0]
    const2 = lambda i, *_: (0, 0)
    grid_spec = pltpu.PrefetchScalarGridSpec(
        num_scalar_prefetch=4,
        grid=(n_tiles,),
        in_specs=[pl.BlockSpec(memory_space=pl.ANY),
                  pl.BlockSpec(gtab_hi.shape, const2),
                  pl.BlockSpec(gtab_lo.shape, const2),
                  pl.BlockSpec(memory_space=pl.ANY),
                  pl.BlockSpec(memory_space=pl.ANY),
                  pl.BlockSpec(memory_space=pl.ANY)],
        out_specs=pl.BlockSpec(memory_space=pl.ANY),
        scratch_shapes=[pltpu.VMEM(((GATHER_DEPTH + 1) * TM, 1, ROW_W), U32),
                        pltpu.VMEM((TM, ROW_W), U32),
                        pltpu.VMEM((OUT_SLOTS, TM, D_MODEL), F32),
                        pltpu.VMEM((2, 2, D_MODEL, 2 * FF), BF16),
                        pltpu.VMEM((2, 2, FF, D_MODEL), BF16),
                        pltpu.SemaphoreType.DMA((GATHER_DEPTH + 1,)),
                        pltpu.SemaphoreType.DMA((OUT_SLOTS,)),
                        pltpu.SemaphoreType.DMA((2, 2))],
    )
    return pl.pallas_call(
        _moe_kernel,
        grid_spec=grid_spec,
        out_shape=jax.ShapeDtypeStruct((n, D_MODEL), F32),
        compiler_params=pltpu.CompilerParams(dimension_semantics=("arbitrary",),
                                             vmem_limit_bytes=VMEM_LIMIT),
        name="moe",
    )(nt, nv, sched, perm, rows, gtab_hi, gtab_lo, wg, wu, wd)


def _weight_schedule(ea, eb):
    n_tiles = ea.shape[0]
    tid = jnp.arange(n_tiles, dtype=I32)
    fields = [[], [], [], []]
    for e in (ea, eb):
        start = jnp.concatenate([jnp.ones((1,), I32), (e[1:] != e[:-1]).astype(I32)])
        slot = (jnp.cumsum(start) - 1) % 2
        start_idx = jnp.where(start == 1, tid, n_tiles)
        next_start = jnp.concatenate([lax.cummin(start_idx, reverse=True)[1:], jnp.full((1,), n_tiles, I32)])
        nxt = jnp.where(next_start < n_tiles, e[jnp.minimum(next_start, n_tiles - 1)], -1)
        for f, v in zip(fields, (e, start, slot, nxt)):
            f.append(v.astype(I32))
    return jnp.concatenate([v for f in fields for v in f])


def kernel(x, c, ln1_pre, ln1_post, ln2_pre, ln2_post, w_ada, b_ada, w_in, attn_sinks, attn_out_norm,
           hgrn_lb, hgrn_out_norm, w_out, w_router_group, b_router_group, w_router_expert,
           b_router_expert, w_exp_gate, w_exp_up, w_exp_down):
    bsz, seq, d = x.shape
    assert d == D_MODEL and seq % TQ == 0 and w_ada.shape[0] == 1 and hgrn_lb.shape[0] == 2
    assert bsz <= LANES - EXT_BATCH0 and (bsz * seq) % TM == 0
    n = bsz * seq

    mod = _ada(c, w_ada[0], b_ada[0])
    mod3 = mod.reshape(bsz, 6, d)

    wr = jnp.concatenate([w_router_group[0], jnp.zeros((d, GROUP_ROWS - N_GROUPS), F32), w_router_expert[0],
                          jnp.zeros((d, LANES - ROUTER_ROWS), F32)], axis=1)
    br = jnp.concatenate([b_router_group[0], jnp.full((GROUP_ROWS - N_GROUPS,), NEG, F32), b_router_expert[0]])
    wr_hi = wr.astype(BF16)
    wr_lo = (wr - wr_hi.astype(F32)).astype(BF16)

    x1ext, info, cnt = _mixer(
        x.reshape(n, d), mod3, attn_sinks[0], ln1_pre, ln1_post, ln2_pre, attn_out_norm, hgrn_out_norm,
        hgrn_lb, w_in[0].astype(BF16), w_out[0].astype(BF16), wr_hi, wr_lo, br.reshape(ROUTER_ROWS, 1),
        bsz, seq)

    n_tiles = n // TM + N_BUCKETS
    counts = cnt[:N_BUCKETS, 0].astype(I32)
    tiles_per = (counts + TM - 1) // TM
    tile_end = jnp.cumsum(tiles_per)
    tile_start = tile_end - tiles_per
    nt = tile_end[-1]
    bucket = info[:, 0, :].reshape(n)
    rank = info[:, 1, :].reshape(n)
    tid = jnp.arange(n_tiles, dtype=I32)[None, :]
    member = (tid >= tile_start[:, None]) & (tid < tile_end[:, None])
    pick = lambda per_bucket: jnp.sum(jnp.where(member, per_bucket, 0), axis=0).astype(I32)
    bidx = np.arange(N_BUCKETS, dtype=np.int32)
    ea_of = jnp.asarray((bidx // N_PAIRS) * EPG + _PAIR_A[bidx % N_PAIRS])[:, None]
    eb_of = jnp.asarray((bidx // N_PAIRS) * EPG + _PAIR_B[bidx % N_PAIRS])[:, None]
    last_used = jnp.arange(N_BUCKETS)[:, None] == jnp.max(jnp.where(tiles_per > 0, jnp.arange(N_BUCKETS), 0))
    unused = tid[0] >= nt
    nv = pick(jnp.clip(counts[:, None] - (tid - tile_start[:, None]) * TM, 0, TM))
    ea = jnp.where(unused, jnp.sum(jnp.where(last_used, ea_of, 0)), pick(ea_of)).astype(I32)
    eb = jnp.where(unused, jnp.sum(jnp.where(last_used, eb_of, 0)), pick(eb_of)).astype(I32)

    pad128 = lambda a: jnp.concatenate([a, jnp.zeros((LANES - a.shape[0],), I32)])
    row_start = jnp.concatenate([tile_start, nt.reshape(1)]) * TM
    perm = _perm(pad128(row_start), pad128(counts), bucket, rank, n_tiles * TM)

    wg, wu, wd = w_exp_gate[0].astype(BF16), w_exp_up[0].astype(BF16), w_exp_down[0].astype(BF16)
    gtab = jnp.zeros((LANES, d), F32).at[EXT_BATCH0:EXT_BATCH0 + bsz].set(mod3[:, 5, :] * ln2_post)
    gtab_hi = gtab.astype(BF16)
    gtab_lo = (gtab - gtab_hi.astype(F32)).astype(BF16)
    out = _moe(x1ext, gtab_hi, gtab_lo, wg, wu, wd, nt.reshape(1), nv, _weight_schedule(ea, eb), perm, n_tiles)
    return out.reshape(bsz, seq, d)
```

```python
import functools

import numpy as np
import jax
import jax.numpy as jnp
from jax import lax
from jax.experimental import pallas as pl
from jax.experimental.pallas import tpu as pltpu

F32 = jnp.float32
BF16 = jnp.bfloat16
I32 = jnp.int32

D_MODEL = 1024
ATTN_HEADS = 8
HEAD_DIM = 64
WINDOW = 128
ATTN_W = 512
KV_W = 128
HG_HEADS = 4
HG_DIM = 128
HG_W = 512
HG_CHUNK = 32
IN_W = 2816
N_GROUPS = 4
EPG = 8
N_EXPERTS = 32
FF = 256
N_PAIRS = EPG * (EPG - 1) // 2
N_BUCKETS = N_GROUPS * N_PAIRS
EPS = 1e-6
NEG = -1e30
LOG2E = 1.4426950408889634

LANES = 128
H2P_W = D_MODEL // 2
EXT_W = LANES
OFF_H2P = D_MODEL
OFF_EXT = D_MODEL + H2P_W
ROW_W = OFF_EXT + EXT_W
EXT_BATCH0 = 8
U32 = jnp.uint32
SUBLANES = 8
GROUP_ROWS = SUBLANES
ROUTER_ROWS = GROUP_ROWS + N_EXPERTS

TQ = 256
TM = 128
VMEM_LIMIT = 56 * 1024 * 1024

OFF_Q, OFF_K, OFF_V, OFF_HQ, OFF_HF, OFF_HI, OFF_HG = 0, 512, 640, 768, 1280, 1792, 2304

_PAIR_A = np.array([a for a in range(EPG) for b in range(a + 1, EPG)], np.int32)
_PAIR_B = np.array([b for a in range(EPG) for b in range(a + 1, EPG)], np.int32)


def _dot(a, b):
    return jnp.dot(a, b, preferred_element_type=F32)


def _dot_nt(a, b):
    return lax.dot_general(a, b, (((1,), (1,)), ((), ())), preferred_element_type=F32)


def _dot_tn(a, b):
    return lax.dot_general(a, b, (((0,), (0,)), ((), ())), preferred_element_type=F32)


def _split(a):
    hi = a.astype(BF16)
    lo = (a - hi.astype(F32)).astype(BF16)
    return hi, lo


def _rms(x):
    return x * lax.rsqrt(jnp.mean(x * x, axis=-1, keepdims=True) + EPS)


def _silu(x):
    half = 0.5 * x
    return half * jnp.tanh(half) + half


def _ada_kernel(c_ref, w_ref, b_ref, o_ref):
    c = c_ref[...]
    ca = _silu(c)
    c_hi, c_lo = _split(ca)
    w_hi, w_lo = _split(w_ref[...])
    o_ref[...] = _dot(c_hi, w_hi) + _dot(c_lo, w_hi) + _dot(c_hi, w_lo) + b_ref[...]


def _ada(c, w, b):
    bsz, d = c.shape
    n_out = w.shape[1]
    return pl.pallas_call(
        _ada_kernel,
        grid=(n_out // d,),
        in_specs=[pl.BlockSpec((bsz, d), lambda j: (0, 0)),
                  pl.BlockSpec((d, d), lambda j: (0, j)),
                  pl.BlockSpec((1, d), lambda j: (0, j))],
        out_specs=pl.BlockSpec((bsz, d), lambda j: (0, j)),
        out_shape=jax.ShapeDtypeStruct((bsz, n_out), F32),
        compiler_params=pltpu.CompilerParams(dimension_semantics=("arbitrary",),
                                             vmem_limit_bytes=VMEM_LIMIT),
        name="adaln",
    )(c, w, b.reshape(1, n_out))


def _attention(proj, kprev_ref, vprev_ref, sinks_ref, bias_ref, t, s_scr, m_scr, p_scr):
    tq = proj.shape[0]
    q = (proj[:, OFF_Q:OFF_Q + ATTN_W] * (HEAD_DIM ** -0.5 * LOG2E)).astype(BF16)
    kf = jnp.concatenate([kprev_ref[...], proj[:, OFF_K:OFF_K + KV_W]], axis=0)
    vf = jnp.concatenate([vprev_ref[...], proj[:, OFF_V:OFF_V + KV_W]], axis=0)
    kprev_ref[...] = proj[tq - WINDOW:, OFF_K:OFF_K + KV_W]
    vprev_ref[...] = proj[tq - WINDOW:, OFF_V:OFF_V + KV_W]

    lo = lax.broadcasted_iota(I32, kf.shape, 1) < HEAD_DIM
    kr = pltpu.roll(kf, HEAD_DIM, axis=1)
    vr = pltpu.roll(vf, HEAD_DIM, axis=1)

    def variants(a, ar):
        return [[jnp.where(lo, a, 0.0).astype(BF16), jnp.where(lo, 0.0, ar).astype(BF16)],
                [jnp.where(lo, ar, 0.0).astype(BF16), jnp.where(lo, 0.0, a).astype(BF16)]]

    kvar = variants(kf, kr)
    vvar = variants(vf, vr)

    first = jnp.where(t > 0, 0, 1)

    nblk = tq // WINDOW
    idx = lambda j, h: j * ATTN_HEADS + h
    keys = lambda a, j: a[j * WINDOW:(j + 2) * WINDOW]

    def scores():
        for j in range(nblk):
            for h in range(ATTN_HEADS):
                p, par = h // 2, h % 2
                qp = q[j * WINDOW:(j + 1) * WINDOW, p * LANES:(p + 1) * LANES]
                s = _dot_nt(qp, keys(kvar[p // 2][par], j)) + bias_ref[first if j == 0 else 0, h]
                s_scr[idx(j, h)] = s
                m_scr[idx(j, h)] = jnp.maximum(jnp.max(s, axis=-1, keepdims=True), sinks_ref[h] * LOG2E)

    def exps():
        for j in range(nblk):
            for h in range(ATTN_HEADS):
                m = m_scr[idx(j, h)]
                pe = jnp.exp2(s_scr[idx(j, h)] - m)
                p_scr[idx(j, h)] = pe.astype(BF16)
                m_scr[idx(j, h)] = 1.0 / (jnp.sum(pe, axis=-1, keepdims=True) + jnp.exp2(sinks_ref[h] * LOG2E - m))

    def values():
        blocks = []
        for j in range(nblk):
            pairs = []
            for p in range(ATTN_HEADS // 2):
                acc = None
                for par in range(2):
                    h = 2 * p + par
                    o = _dot(p_scr[idx(j, h)], keys(vvar[p // 2][par], j)) * m_scr[idx(j, h)]
                    acc = o if acc is None else acc + o
                pairs.append(acc)
            blocks.append(jnp.concatenate(pairs, axis=1))
        return jnp.concatenate(blocks, axis=0)

    return scores, exps, values


def _hgrn2(proj, lb, st_ref, hnorm, u_scr, stb_scr):
    tq = proj.shape[0]
    nc = tq // HG_CHUNK
    qr = proj[:, OFF_HQ:OFF_HQ + HG_W]
    fr = proj[:, OFF_HF:OFF_HF + HG_W]
    iv = proj[:, OFF_HI:OFF_HI + HG_W]
    gr = proj[:, OFF_HG:OFF_HG + HG_W]
    qh = _silu(qr)
    half_gap = 0.5 * (1.0 - lb)
    f = (lb + half_gap) + half_gap * jnp.tanh(0.5 * fr)
    kk = 1.0 - f
    logf = jnp.log(f)

    rmod = lax.broadcasted_iota(I32, (tq, HG_W), 0) & (HG_CHUNK - 1)
    bc = logf
    s = 1
    while s < HG_CHUNK:
        bc = bc + jnp.where(rmod >= s, pltpu.roll(bc, s, axis=0), 0.0)
        s *= 2

    b3 = bc.reshape(nc, HG_CHUNK, HG_W)
    blast = b3[:, HG_CHUNK - 1:HG_CHUNK, :]
    kend = (kk.reshape(nc, HG_CHUNK, HG_W) * jnp.exp(blast - b3)).reshape(tq, HG_W)
    decay = jnp.exp(blast).reshape(nc, HG_W)
    qdec = (qh * jnp.exp(bc)).astype(BF16)
    kdec = (kk * jnp.exp(-bc)).astype(BF16)
    kend = kend.astype(BF16)
    ivb = iv.astype(BF16)

    ri = lax.broadcasted_iota(I32, (tq, tq), 0)
    ci = lax.broadcasted_iota(I32, (tq, tq), 1)
    cmask = ((ri // HG_CHUNK) == (ci // HG_CHUNK)) & (ri >= ci)

    heads = [slice(hh * HG_DIM, (hh + 1) * HG_DIM) for hh in range(HG_HEADS)]
    chunks = [slice(n * HG_CHUNK, (n + 1) * HG_CHUNK) for n in range(nc)]

    lane_head = lax.broadcasted_iota(I32, (HG_CHUNK, HG_W), 1) // HG_DIM
    for n, rs in enumerate(chunks):
        vstack = jnp.concatenate([ivb[rs, sl] for sl in heads], axis=0)
        kblk = jnp.concatenate([jnp.where(lane_head == hh, kend[rs], 0.0).astype(BF16)
                                for hh in range(HG_HEADS)], axis=0)
        u_scr[n] = _dot_tn(vstack, kblk)

    st = st_ref[...]
    for n in range(nc):
        stb_scr[n] = st.astype(BF16)
        st = st * decay[n:n + 1] + u_scr[n]
    st_ref[...] = st

    outs = []
    for hh, sl in enumerate(heads):
        a = _dot_nt(qdec[:, sl], kdec[:, sl])
        a = jnp.where(cmask, a, 0.0).astype(BF16)
        o_intra = _dot(a, ivb[:, sl])
        inter = [_dot_nt(qdec[rs, sl], stb_scr[n, :, sl]) for n, rs in enumerate(chunks)]
        o = o_intra + jnp.concatenate(inter, axis=0)
        o = _rms(o) * hnorm[:, sl]
        g = gr[:, sl]
        outs.append(o * _silu(g))
    return jnp.concatenate(outs, axis=1)


def _route_topk(h2, wr_hi, wr_lo, br):
    tq = h2.shape[0]
    h_hi, h_lo = _split(h2)
    logits = _dot(h_hi, wr_hi) + _dot(h_lo, wr_hi) + _dot(h_hi, wr_lo)
    lt = logits.T[0:ROUTER_ROWS] + br
    sub = lax.broadcasted_iota(I32, (SUBLANES, tq), 0).astype(F32)
    none = float(SUBLANES)

    gl = lt[0:GROUP_ROWS]
    gm = jnp.max(gl, axis=0, keepdims=True)
    gidx = jnp.min(jnp.where(gl == gm, sub, none), axis=0, keepdims=True)
    g_w = 1.0 / jnp.sum(jnp.exp(gl - gm), axis=0, keepdims=True)

    group_rows = lambda g: lt[GROUP_ROWS + EPG * g:GROUP_ROWS + EPG * (g + 1)]
    es = group_rows(0)
    for g in range(1, N_GROUPS):
        es = jnp.where(gidx == float(g), group_rows(g), es)
    m1 = jnp.max(es, axis=0, keepdims=True)
    i1 = jnp.min(jnp.where(es == m1, sub, none), axis=0, keepdims=True)
    e2 = jnp.where(sub == i1, NEG, es)
    m2 = jnp.max(e2, axis=0, keepdims=True)
    i2 = jnp.min(jnp.where(e2 == m2, sub, none), axis=0, keepdims=True)
    dd = jnp.exp(m2 - m1)
    w1 = g_w / (1.0 + dd)
    w2 = g_w * dd / (1.0 + dd)
    first_low = i1 < i2
    ea = jnp.minimum(i1, i2)
    eb = jnp.maximum(i1, i2)
    w_lo = jnp.where(first_low, w1, w2)
    w_hi = jnp.where(first_low, w2, w1)
    pair = ea * (2.0 * EPG - 1.0 - ea) * 0.5 + (eb - ea - 1.0)
    bucket = gidx * float(N_PAIRS) + pair
    return bucket, w_lo, w_hi


def _route_rank(bucket, w_lo, w_hi, carry_ref, bidx, live):
    tq = bucket.shape[1]
    brow = lax.broadcasted_iota(I32, (LANES, tq), 0).astype(F32)
    onehot = brow == bucket
    oh = jnp.where(onehot, live, 0.0)
    ti = lax.broadcasted_iota(I32, (tq, tq), 0)
    tj = lax.broadcasted_iota(I32, (tq, tq), 1)
    upper = jnp.where(ti < tj, 1.0, 0.0).astype(BF16)
    before = _dot(oh.astype(BF16), upper) + carry_ref[...]
    rank = jnp.sum(jnp.where(onehot, before, 0.0), axis=0, keepdims=True)
    carry_ref[...] = carry_ref[...] + jnp.sum(oh, axis=1, keepdims=True)

    lane_row = lax.broadcasted_iota(I32, (LANES - EXT_BATCH0, tq), 0)
    onehot_b = jnp.where(lane_row == bidx, 1.0, 0.0)
    info = jnp.concatenate([w_lo, w_hi, jnp.zeros((EXT_BATCH0 - 2, tq), F32), onehot_b], axis=0)
    return rank, info.T


def _mixer_kernel(sinks_ref, x_ref, mod_ref, ln1pre_ref, ln1post_ref, ln2pre_ref, anorm_ref, hnorm_ref,
                  lb_ref, win_ref, wout_ref, wrhi_ref, wrlo_ref, br_ref, bias_ref,
                  rows_hbm, info_ref, cnt_ref,
                  kprev_ref, vprev_ref, st_ref, carry_ref, s_scr, m_scr, p_scr, u_scr, stb_scr,
                  h2_scr, proj_scr, rowbuf, rsem, *, tiles_per_seq):
    s = pl.program_id(0)
    n_tiles = pl.num_programs(0) - 1
    tq = x_ref.shape[0]
    t = lax.rem(jnp.minimum(s, n_tiles - 1), tiles_per_seq)
    slot = lax.rem(s, 2)
    bits = lambda a: pltpu.bitcast(a, U32)

    def row_copy(buf_slot, tile):
        return pltpu.make_async_copy(rowbuf.at[buf_slot], rows_hbm.at[pl.ds(pl.multiple_of(tile * tq, tq), tq), 0],
                                     rsem.at[buf_slot])

    @pl.when(s == 0)
    def _():
        carry_ref[...] = jnp.zeros_like(carry_ref)
        h2_scr[...] = jnp.zeros_like(h2_scr)

    @pl.when(t == 0)
    def _():
        st_ref[...] = jnp.zeros_like(st_ref)
        kprev_ref[...] = jnp.zeros_like(kprev_ref)
        vprev_ref[...] = jnp.zeros_like(vprev_ref)

    x = x_ref[...]
    mod = mod_ref[0]
    sh1, sc1, ga1, sh2, sc2 = mod[0:1], mod[1:2], mod[2:3], mod[3:4], mod[4:5]
    prev = jnp.maximum(s - 1, 0)

    @pl.when(s <= n_tiles)
    def _():
        bucket, w_lo, w_hi = _route_topk(h2_scr[...], wrhi_ref[...], wrlo_ref[...], br_ref[...])
        h = _rms(x) * (ln1pre_ref[...] * (1.0 + sc1)) + sh1
        proj_scr[...] = _dot(h.astype(BF16), win_ref[...])
        live = jnp.where(s >= 1, 1.0, 0.0)
        rank, ext = _route_rank(bucket, w_lo, w_hi, carry_ref, prev // tiles_per_seq, live)
        rowbuf[1 - slot, :, OFF_EXT:ROW_W] = bits(ext)
        info_ref[0] = jnp.concatenate([bucket, rank, jnp.zeros((6, tq), F32)], axis=0).astype(I32)
        cnt_ref[...] = jnp.broadcast_to(carry_ref[...], cnt_ref.shape)

    proj = proj_scr[...]
    scores, exps, values = _attention(proj, kprev_ref, vprev_ref, sinks_ref, bias_ref, t, s_scr, m_scr, p_scr)
    scores()
    exps()
    attn = _rms(values()) * anorm_ref[...]

    lbr = lb_ref[...]
    le = jnp.exp(lbr - jnp.max(lbr, axis=0, keepdims=True))
    lb = le[0:1] / jnp.sum(le, axis=0, keepdims=True)
    hg = _hgrn2(proj, lb, st_ref, hnorm_ref[...], u_scr, stb_scr)

    mix = _dot(jnp.concatenate([attn, hg], axis=1).astype(BF16), wout_ref[...])
    x1 = x + _rms(mix) * (ga1 * ln1post_ref[...])

    h2 = _rms(x1) * (ln2pre_ref[...] * (1.0 + sc2)) + sh2
    h2r = h2.astype(BF16).astype(F32)

    @pl.when(s >= 2)
    def _():
        row_copy(slot, s - 2).wait()

    rowbuf[slot, :, 0:D_MODEL] = bits(x1)
    rowbuf[slot, :, OFF_H2P:OFF_EXT] = ((bits(h2r[:, 0:H2P_W]) >> 16)
                                        | (bits(h2r[:, H2P_W:D_MODEL]) & jnp.uint32(0xFFFF0000)))
    h2_scr[...] = h2

    @pl.when(s >= 1)
    def _():
        row_copy(1 - slot, prev).start()

    @pl.when(s == n_tiles)
    def _():
        row_copy(1 - slot, prev).wait()


def _attn_bias():
    qi = np.arange(WINDOW)[:, None]
    kj = np.arange(2 * WINDOW)[None, :]
    dist = qi + WINDOW - kj
    in_win = (dist >= 0) & (dist < WINDOW)
    slopes = 2.0 ** (-8.0 * (np.arange(ATTN_HEADS) + 1.0) / ATTN_HEADS)
    b = np.where(in_win[None], -slopes[:, None, None] * dist[None] * LOG2E, NEG)
    b_first = np.where((kj >= WINDOW)[None], b, NEG)
    return jnp.asarray(np.stack([b, b_first]).astype(np.float32))


def _mixer(x2, mod3, sinks, ln1pre, ln1post, ln2pre, anorm, hnorm, lb, win, wout, wr_hi, wr_lo, br,
           bsz, seq):
    bias = _attn_bias()
    n = bsz * seq
    nt = seq // TQ
    n_tiles = bsz * nt
    n_sc = (TQ // WINDOW) * ATTN_HEADS
    cur = lambda s: jnp.minimum(s, n_tiles - 1)
    const = lambda s: (0, 0)
    full = lambda a: pl.BlockSpec(a.shape, const)
    return pl.pallas_call(
        functools.partial(_mixer_kernel, tiles_per_seq=nt),
        grid=(n_tiles + 1,),
        in_specs=[pl.BlockSpec(memory_space=pltpu.SMEM),
                  pl.BlockSpec((TQ, D_MODEL), lambda s: (cur(s), 0)),
                  pl.BlockSpec((1, 6, D_MODEL), lambda s: (cur(s) // nt, 0, 0)),
                  full(ln1pre), full(ln1post), full(ln2pre), full(anorm), full(hnorm), full(lb),
                  full(win), full(wout), full(wr_hi), full(wr_lo), full(br),
                  pl.BlockSpec(bias.shape, lambda s: (0, 0, 0, 0))],
        out_specs=[pl.BlockSpec(memory_space=pl.ANY),
                   pl.BlockSpec((1, 8, TQ), lambda s: (jnp.maximum(s - 1, 0), 0, 0)),
                   pl.BlockSpec((LANES, LANES), const)],
        out_shape=[jax.ShapeDtypeStruct((n, 1, ROW_W), U32),
                   jax.ShapeDtypeStruct((n // TQ, 8, TQ), I32),
                   jax.ShapeDtypeStruct((LANES, LANES), F32)],
        scratch_shapes=[pltpu.VMEM((WINDOW, KV_W), F32),
                        pltpu.VMEM((WINDOW, KV_W), F32),
                        pltpu.VMEM((HG_DIM, HG_W), F32),
                        pltpu.VMEM((LANES, 1), F32),
                        pltpu.VMEM((n_sc, WINDOW, 2 * WINDOW), F32),
                        pltpu.VMEM((n_sc, WINDOW, 1), F32),
                        pltpu.VMEM((n_sc, WINDOW, 2 * WINDOW), BF16),
                        pltpu.VMEM((TQ // HG_CHUNK, HG_DIM, HG_W), F32),
                        pltpu.VMEM((TQ // HG_CHUNK, HG_DIM, HG_W), BF16),
                        pltpu.VMEM((TQ, D_MODEL), F32),
                        pltpu.VMEM((TQ, IN_W), F32),
                        pltpu.VMEM((2, TQ, ROW_W), U32),
                        pltpu.SemaphoreType.DMA((2,))],
        compiler_params=pltpu.CompilerParams(dimension_semantics=("arbitrary",),
                                             vmem_limit_bytes=VMEM_LIMIT),
        name="mixer",
    )(sinks, x2, mod3, ln1pre, ln1post, ln2pre, anorm, hnorm, lb, win, wout, wr_hi, wr_lo, br, bias)


PERM_STEPS = 8
PERM_ROWS = 8
PERM_UNROLL = 16


def _perm_kernel(rs_ref, cnt_ref, bucket_ref, rank_ref, perm_ref, pos_vmem, pos_smem, sem):
    pid = pl.program_id(0)
    rows, cols = pos_vmem.shape

    b = bucket_ref[0]
    start = jnp.zeros_like(b)
    for k in range(N_BUCKETS):
        start = jnp.where(b == k, rs_ref[k], start)
    pos_vmem[...] = start + rank_ref[0]
    copies = [pltpu.make_async_copy(pos_vmem.at[r], pos_smem.at[pl.ds(r * cols, cols)], sem) for r in range(rows)]
    for cp in copies:
        cp.start()

    @pl.when(pid == 0)
    def _():
        def per_bucket(k, carry):
            first = rs_ref[k]
            cnt = cnt_ref[k]

            def pad(r, c2):
                perm_ref[first + r] = 0
                return c2

            lax.fori_loop(cnt, ((cnt + TM - 1) // TM) * TM, pad, 0)
            return carry

        lax.fori_loop(0, N_BUCKETS, per_bucket, 0)

        def tail(blk, carry):
            for u in range(PERM_UNROLL):
                perm_ref[blk * PERM_UNROLL + u] = 0
            return carry

        lax.fori_loop(rs_ref[N_BUCKETS] // PERM_UNROLL, perm_ref.shape[0] // PERM_UNROLL, tail, 0)

    for cp in copies:
        cp.wait()
    base = pid * (rows * cols)

    def body(j, carry):
        i0 = j * PERM_UNROLL
        positions = [pos_smem[i0 + u] for u in range(PERM_UNROLL)]
        for u in range(PERM_UNROLL):
            perm_ref[positions[u]] = base + i0 + u
        return carry

    lax.fori_loop(0, rows * cols // PERM_UNROLL, body, 0)


def _perm(row_start, counts, bucket, rank, n_rows):
    n = bucket.shape[0]
    cols = n // (PERM_STEPS * PERM_ROWS)
    assert n % (PERM_STEPS * PERM_ROWS * PERM_UNROLL) == 0
    chunked = lambda a: a.reshape(PERM_STEPS, PERM_ROWS, cols)
    chunk_spec = pl.BlockSpec((1, PERM_ROWS, cols), lambda i: (i, 0, 0))
    return pl.pallas_call(
        _perm_kernel,
        grid=(PERM_STEPS,),
        in_specs=[pl.BlockSpec(memory_space=pltpu.SMEM),
                  pl.BlockSpec(memory_space=pltpu.SMEM),
                  chunk_spec, chunk_spec],
        out_specs=pl.BlockSpec(memory_space=pltpu.SMEM),
        out_shape=jax.ShapeDtypeStruct((n_rows,), I32),
        scratch_shapes=[pltpu.VMEM((PERM_ROWS, cols), I32), pltpu.SMEM((PERM_ROWS * cols,), I32),
                        pltpu.SemaphoreType.DMA(())],
        compiler_params=pltpu.CompilerParams(dimension_semantics=("arbitrary",)),
        name="perm",
    )(row_start, counts, chunked(bucket), chunked(rank))


GATHER_DEPTH = 3
OUT_SLOTS = 3


SCHED_EXPERT, SCHED_RUN_START, SCHED_SLOT, SCHED_NEXT_EXPERT = range(4)


def _moe_kernel(nt_ref, nv_ref, sched_ref, perm_ref,
                rows_hbm, gtab_hi_ref, gtab_lo_ref, wg_hbm, wu_hbm, wd_hbm,
                out_hbm, xbuf, x2d, obuf, wgu_buf, wd_buf, gsem, ssem, wsem):
    i = pl.program_id(0)
    nt = nt_ref[0]
    n_steps = pl.num_programs(0)
    last_tile = n_steps - 1
    sched = lambda field, side: sched_ref[(field * 2 + side) * n_steps + i]

    def weight_copies(side, expert, slot):
        return (pltpu.make_async_copy(wg_hbm.at[expert], wgu_buf.at[side, slot, :, 0:FF], wsem.at[side, slot]),
                pltpu.make_async_copy(wu_hbm.at[expert], wgu_buf.at[side, slot, :, FF:2 * FF], wsem.at[side, slot]),
                pltpu.make_async_copy(wd_hbm.at[expert], wd_buf.at[side, slot], wsem.at[side, slot]))
    nbuf = xbuf.shape[0] // TM

    def start_gather(tile, pred):
        sl = lax.rem(tile, nbuf)
        base = jnp.minimum(tile, last_tile) * TM
        for r in range(TM):
            @pl.when(pred)
            def _():
                tok = perm_ref[base + r]
                pltpu.make_async_copy(rows_hbm.at[tok], xbuf.at[sl * TM + r], gsem.at[sl]).start()

    def wait_gather(tile):
        sl = lax.rem(tile, nbuf)
        pltpu.make_async_copy(rows_hbm.at[pl.ds(0, TM)], xbuf.at[pl.ds(sl * TM, TM)], gsem.at[sl]).wait()

    def wait_scatter(sl, nv):
        @pl.when(nv == TM)
        def _():
            pltpu.make_async_copy(obuf.at[sl], out_hbm.at[pl.ds(0, TM)], ssem.at[sl]).wait()

        @pl.when(nv < TM)
        def _():
            def one(r, carry):
                pltpu.make_async_copy(obuf.at[sl, pl.ds(0, 1)], out_hbm.at[pl.ds(0, 1)], ssem.at[sl]).wait()
                return carry

            lax.fori_loop(0, nv, one, 0)

    def compute(xb, slot_a, slot_b):
        x1 = pltpu.bitcast(xb[:, 0:D_MODEL], F32)
        hp = xb[:, OFF_H2P:OFF_EXT]
        h2a = pltpu.bitcast(hp << 16, F32).astype(BF16)
        h2b = pltpu.bitcast(hp & jnp.uint32(0xFFFF0000), F32).astype(BF16)
        ext = pltpu.bitcast(xb[:, OFF_EXT:ROW_W], F32)
        w_lo, w_hi = ext[:, 0:1], ext[:, 1:2]
        sel = ext.astype(BF16)
        ga2 = _dot(sel, gtab_hi_ref[...]) + _dot(sel, gtab_lo_ref[...])

        def hidden(side, slot, w):
            gu = (_dot(h2a, wgu_buf[side, slot, 0:H2P_W])
                  + _dot(h2b, wgu_buf[side, slot, H2P_W:D_MODEL]))
            hg, hu = gu[:, 0:FF], gu[:, FF:2 * FF]
            return (w * (_silu(hg) * hu)).astype(BF16)

        act = jnp.concatenate([hidden(0, slot_a, w_lo), hidden(1, slot_b, w_hi)], axis=1)
        wd = jnp.concatenate([wd_buf[0, slot_a], wd_buf[1, slot_b]], axis=0)
        y = _dot(act, wd)
        return x1 + ga2 * _rms(y)

    @pl.when(i == 0)
    def _():
        for side in range(2):
            for cp in weight_copies(side, sched(SCHED_EXPERT, side), 0):
                cp.start()
        for d in range(GATHER_DEPTH):
            start_gather(d, d < nt)

    @pl.when(i < nt)
    def _():
        nv = nv_ref[i]
        osl = lax.rem(i, OUT_SLOTS)
        slots = []
        for side in range(2):
            slot = sched(SCHED_SLOT, side)
            slots.append(slot)

            @pl.when(sched(SCHED_RUN_START, side) == 1)
            def _():
                for cp in weight_copies(side, sched(SCHED_EXPERT, side), slot):
                    cp.wait()
                nxt = sched(SCHED_NEXT_EXPERT, side)

                @pl.when(nxt >= 0)
                def _():
                    for cp in weight_copies(side, nxt, 1 - slot):
                        cp.start()

        wait_gather(i)

        @pl.when(i >= OUT_SLOTS)
        def _():
            wait_scatter(osl, nv_ref[jnp.maximum(i - OUT_SLOTS, 0)])

        start_gather(i + GATHER_DEPTH, i + GATHER_DEPTH < nt)
        x2d[...] = xbuf[pl.ds(lax.rem(i, nbuf) * TM, TM)].reshape(TM, ROW_W)
        result = compute(x2d[...], *slots)

        def scatter_row(k, r):
            tok = perm_ref[i * TM + r]
            pltpu.make_async_copy(obuf.at[k, pl.ds(r, 1)], out_hbm.at[pl.ds(tok, 1)], ssem.at[k]).start()

        for k in range(OUT_SLOTS):
            @pl.when((osl == k) & (nv == TM))
            def _():
                obuf[k] = result
                for r in range(TM):
                    scatter_row(k, r)

            @pl.when((osl == k) & (nv < TM))
            def _():
                obuf[k] = result
                for r in range(TM):
                    pl.when(r < nv)(functools.partial(scatter_row, k, r))

        @pl.when(i == nt - 1)
        def _():
            wait_scatter(osl, nv)
            for back in range(1, OUT_SLOTS):
                @pl.when(i >= back)
                def _():
                    wait_scatter(lax.rem(i - back + OUT_SLOTS, OUT_SLOTS), nv_ref[jnp.maximum(i - back, 0)])


def _moe(rows, gtab_hi, gtab_lo, wg, wu, wd, nt, nv, sched, perm, n_tiles):
    n = rows.shape[0]
    const2 = lambda i, *_: (0, 0)
    grid_spec = pltpu.PrefetchScalarGridSpec(
        num_scalar_prefetch=4,
        grid=(n_tiles,),
        in_specs=[pl.BlockSpec(memory_space=pl.ANY),
                  pl.BlockSpec(gtab_hi.shape, const2),
                  pl.BlockSpec(gtab_lo.shape, const2),
                  pl.BlockSpec(memory_space=pl.ANY),
                  pl.BlockSpec(memory_space=pl.ANY),
                  pl.BlockSpec(memory_space=pl.ANY)],
        out_specs=pl.BlockSpec(memory_space=pl.ANY),
        scratch_shapes=[pltpu.VMEM(((GATHER_DEPTH + 1) * TM, 1, ROW_W), U32),
                        pltpu.VMEM((TM, ROW_W), U32),
                        pltpu.VMEM((OUT_SLOTS, TM, D_MODEL), F32),
                        pltpu.VMEM((2, 2, D_MODEL, 2 * FF), BF16),
                        pltpu.VMEM((2, 2, FF, D_MODEL), BF16),
                        pltpu.SemaphoreType.DMA((GATHER_DEPTH + 1,)),
                        pltpu.SemaphoreType.DMA((OUT_SLOTS,)),
                        pltpu.SemaphoreType.DMA((2, 2))],
    )
    return pl.pallas_call(
        _moe_kernel,
        grid_spec=grid_spec,
        out_shape=jax.ShapeDtypeStruct((n, D_MODEL), F32),
        compiler_params=pltpu.CompilerParams(dimension_semantics=("arbitrary",),
                                             vmem_limit_bytes=VMEM_LIMIT),
        name="moe",
    )(nt, nv, sched, perm, rows, gtab_hi, gtab_lo, wg, wu, wd)


def _weight_schedule(ea, eb):
    n_tiles = ea.shape[0]
    tid = jnp.arange(n_tiles, dtype=I32)
    fields = [[], [], [], []]
    for e in (ea, eb):
        start = jnp.concatenate([jnp.ones((1,), I32), (e[1:] != e[:-1]).astype(I32)])
        slot = (jnp.cumsum(start) - 1) % 2
        start_idx = jnp.where(start == 1, tid, n_tiles)
        next_start = jnp.concatenate([lax.cummin(start_idx, reverse=True)[1:], jnp.full((1,), n_tiles, I32)])
        nxt = jnp.where(next_start < n_tiles, e[jnp.minimum(next_start, n_tiles - 1)], -1)
        for f, v in zip(fields, (e, start, slot, nxt)):
            f.append(v.astype(I32))
    return jnp.concatenate([v for f in fields for v in f])


def kernel(x, c, ln1_pre, ln1_post, ln2_pre, ln2_post, w_ada, b_ada, w_in, attn_sinks, attn_out_norm,
           hgrn_lb, hgrn_out_norm, w_out, w_router_group, b_router_group, w_router_expert,
           b_router_expert, w_exp_gate, w_exp_up, w_exp_down):
    bsz, seq, d = x.shape
    assert d == D_MODEL and seq % TQ == 0 and w_ada.shape[0] == 1 and hgrn_lb.shape[0] == 2
    assert bsz <= LANES - EXT_BATCH0 and (bsz * seq) % TM == 0
    n = bsz * seq

    mod = _ada(c, w_ada[0], b_ada[0])
    mod3 = mod.reshape(bsz, 6, d)

    wr = jnp.concatenate([w_router_group[0], jnp.zeros((d, GROUP_ROWS - N_GROUPS), F32), w_router_expert[0],
                          jnp.zeros((d, LANES - ROUTER_ROWS), F32)], axis=1)
    br = jnp.concatenate([b_router_group[0], jnp.full((GROUP_ROWS - N_GROUPS,), NEG, F32), b_router_expert[0]])
    wr_hi = wr.astype(BF16)
    wr_lo = (wr - wr_hi.astype(F32)).astype(BF16)

    x1ext, info, cnt = _mixer(
        x.reshape(n, d), mod3, attn_sinks[0], ln1_pre, ln1_post, ln2_pre, attn_out_norm, hgrn_out_norm,
        hgrn_lb, w_in[0].astype(BF16), w_out[0].astype(BF16), wr_hi, wr_lo, br.reshape(ROUTER_ROWS, 1),
        bsz, seq)

    n_tiles = n // TM + N_BUCKETS
    counts = cnt[:N_BUCKETS, 0].astype(I32)
    tiles_per = (counts + TM - 1) // TM
    tile_end = jnp.cumsum(tiles_per)
    tile_start = tile_end - tiles_per
    nt = tile_end[-1]
    bucket = info[:, 0, :].reshape(n)
    rank = info[:, 1, :].reshape(n)
    tid = jnp.arange(n_tiles, dtype=I32)[None, :]
    member = (tid >= tile_start[:, None]) & (tid < tile_end[:, None])
    pick = lambda per_bucket: jnp.sum(jnp.where(member, per_bucket, 0), axis=0).astype(I32)
    bidx = np.arange(N_BUCKETS, dtype=np.int32)
    ea_of = jnp.asarray((bidx // N_PAIRS) * EPG + _PAIR_A[bidx % N_PAIRS])[:, None]
    eb_of = jnp.asarray((bidx // N_PAIRS) * EPG + _PAIR_B[bidx % N_PAIRS])[:, None]
    last_used = jnp.arange(N_BUCKETS)[:, None] == jnp.max(jnp.where(tiles_per > 0, jnp.arange(N_BUCKETS), 0))
    unused = tid[0] >= nt
    nv = pick(jnp.clip(counts[:, None] - (tid - tile_start[:, None]) * TM, 0, TM))
    ea = jnp.where(unused, jnp.sum(jnp.where(last_used, ea_of, 0)), pick(ea_of)).astype(I32)
    eb = jnp.where(unused, jnp.sum(jnp.where(last_used, eb_of, 0)), pick(eb_of)).astype(I32)

    pad128 = lambda a: jnp.concatenate([a, jnp.zeros((LANES - a.shape[0],), I32)])
    row_start = jnp.concatenate([tile_start, nt.reshape(1)]) * TM
    perm = _perm(pad128(row_start), pad128(counts), bucket, rank, n_tiles * TM)

    wg, wu, wd = w_exp_gate[0].astype(BF16), w_exp_up[0].astype(BF16), w_exp_down[0].astype(BF16)
    gtab = jnp.zeros((LANES, d), F32).at[EXT_BATCH0:EXT_BATCH0 + bsz].set(mod3[:, 5, :] * ln2_post)
    gtab_hi = gtab.astype(BF16)
    gtab_lo = (gtab - gtab_hi.astype(F32)).astype(BF16)
    out = _moe(x1ext, gtab_hi, gtab_lo, wg, wu, wd, nt.reshape(1), nv, _weight_schedule(ea, eb), perm, n_tiles)
    return out.reshape(bsz, seq, d)
```

```python
import functools

import numpy as np
import jax
import jax.numpy as jnp
from jax import lax
from jax.experimental import pallas as pl
from jax.experimental.pallas import tpu as pltpu

F32 = jnp.float32
BF16 = jnp.bfloat16
I32 = jnp.int32

D_MODEL = 1024
ATTN_HEADS = 8
HEAD_DIM = 64
WINDOW = 128
ATTN_W = 512
KV_W = 128
HG_HEADS = 4
HG_DIM = 128
HG_W = 512
HG_CHUNK = 32
IN_W = 2816
N_GROUPS = 4
EPG = 8
N_EXPERTS = 32
FF = 256
N_PAIRS = EPG * (EPG - 1) // 2
N_BUCKETS = N_GROUPS * N_PAIRS
EPS = 1e-6
NEG = -1e30
LOG2E = 1.4426950408889634

LANES = 128
H2P_W = D_MODEL // 2
EXT_W = LANES
OFF_H2P = D_MODEL
OFF_EXT = D_MODEL + H2P_W
ROW_W = OFF_EXT + EXT_W
EXT_BATCH0 = 8
U32 = jnp.uint32
SUBLANES = 8
GROUP_ROWS = SUBLANES
ROUTER_ROWS = GROUP_ROWS + N_EXPERTS

TQ = 256
TM = 128
VMEM_LIMIT = 56 * 1024 * 1024

OFF_Q, OFF_K, OFF_V, OFF_HQ, OFF_HF, OFF_HI, OFF_HG = 0, 512, 640, 768, 1280, 1792, 2304

_PAIR_A = np.array([a for a in range(EPG) for b in range(a + 1, EPG)], np.int32)
_PAIR_B = np.array([b for a in range(EPG) for b in range(a + 1, EPG)], np.int32)


def _dot(a, b):
    return jnp.dot(a, b, preferred_element_type=F32)


def _dot_nt(a, b):
    return lax.dot_general(a, b, (((1,), (1,)), ((), ())), preferred_element_type=F32)


def _dot_tn(a, b):
    return lax.dot_general(a, b, (((0,), (0,)), ((), ())), preferred_element_type=F32)


def _split(a):
    hi = a.astype(BF16)
    lo = (a - hi.astype(F32)).astype(BF16)
    return hi, lo


def _rms(x):
    return x * lax.rsqrt(jnp.mean(x * x, axis=-1, keepdims=True) + EPS)


def _silu(x):
    half = 0.5 * x
    return half * jnp.tanh(half) + half


def _ada_kernel(c_ref, w_ref, b_ref, o_ref):
    c = c_ref[...]
    ca = _silu(c)
    c_hi, c_lo = _split(ca)
    w_hi, w_lo = _split(w_ref[...])
    o_ref[...] = _dot(c_hi, w_hi) + _dot(c_lo, w_hi) + _dot(c_hi, w_lo) + b_ref[...]


def _ada(c, w, b):
    bsz, d = c.shape
    n_out = w.shape[1]
    return pl.pallas_call(
        _ada_kernel,
        grid=(n_out // d,),
        in_specs=[pl.BlockSpec((bsz, d), lambda j: (0, 0)),
                  pl.BlockSpec((d, d), lambda j: (0, j)),
                  pl.BlockSpec((1, d), lambda j: (0, j))],
        out_specs=pl.BlockSpec((bsz, d), lambda j: (0, j)),
        out_shape=jax.ShapeDtypeStruct((bsz, n_out), F32),
        compiler_params=pltpu.CompilerParams(dimension_semantics=("arbitrary",),
                                             vmem_limit_bytes=VMEM_LIMIT),
        name="adaln",
    )(c, w, b.reshape(1, n_out))


def _attention(proj, kprev_ref, vprev_ref, sinks_ref, bias_ref, t, s_scr, m_scr, p_scr):
    tq = proj.shape[0]
    q = (proj[:, OFF_Q:OFF_Q + ATTN_W] * (HEAD_DIM ** -0.5 * LOG2E)).astype(BF16)
    kf = jnp.concatenate([kprev_ref[...], proj[:, OFF_K:OFF_K + KV_W]], axis=0)
    vf = jnp.concatenate([vprev_ref[...], proj[:, OFF_V:OFF_V + KV_W]], axis=0)
    kprev_ref[...] = proj[tq - WINDOW:, OFF_K:OFF_K + KV_W]
    vprev_ref[...] = proj[tq - WINDOW:, OFF_V:OFF_V + KV_W]

    lo = lax.broadcasted_iota(I32, kf.shape, 1) < HEAD_DIM
    kr = pltpu.roll(kf, HEAD_DIM, axis=1)
    vr = pltpu.roll(vf, HEAD_DIM, axis=1)

    def variants(a, ar):
        return [[jnp.where(lo, a, 0.0).astype(BF16), jnp.where(lo, 0.0, ar).astype(BF16)],
                [jnp.where(lo, ar, 0.0).astype(BF16), jnp.where(lo, 0.0, a).astype(BF16)]]

    kvar = variants(kf, kr)
    vvar = variants(vf, vr)

    first = jnp.where(t > 0, 0, 1)

    nblk = tq // WINDOW
    idx = lambda j, h: j * ATTN_HEADS + h
    keys = lambda a, j: a[j * WINDOW:(j + 2) * WINDOW]

    def scores():
        for j in range(nblk):
            for h in range(ATTN_HEADS):
                p, par = h // 2, h % 2
                qp = q[j * WINDOW:(j + 1) * WINDOW, p * LANES:(p + 1) * LANES]
                s = _dot_nt(qp, keys(kvar[p // 2][par], j)) + bias_ref[first if j == 0 else 0, h]
                s_scr[idx(j, h)] = s
                m_scr[idx(j, h)] = jnp.maximum(jnp.max(s, axis=-1, keepdims=True), sinks_ref[h] * LOG2E)

    def exps():
        for j in range(nblk):
            for h in range(ATTN_HEADS):
                m = m_scr[idx(j, h)]
                pe = jnp.exp2(s_scr[idx(j, h)] - m)
                p_scr[idx(j, h)] = pe.astype(BF16)
                m_scr[idx(j, h)] = 1.0 / (jnp.sum(pe, axis=-1, keepdims=True) + jnp.exp2(sinks_ref[h] * LOG2E - m))

    def values():
        blocks = []
        for j in range(nblk):
            pairs = []
            for p in range(ATTN_HEADS // 2):
                acc = None
                for par in range(2):
                    h = 2 * p + par
                    o = _dot(p_scr[idx(j, h)], keys(vvar[p // 2][par], j)) * m_scr[idx(j, h)]
                    acc = o if acc is None else acc + o
                pairs.append(acc)
            blocks.append(jnp.concatenate(pairs, axis=1))
        return jnp.concatenate(blocks, axis=0)

    return scores, exps, values


def _hgrn2(proj, lb, st_ref, hnorm, u_scr, stb_scr):
    tq = proj.shape[0]
    nc = tq // HG_CHUNK
    qr = proj[:, OFF_HQ:OFF_HQ + HG_W]
    fr = proj[:, OFF_HF:OFF_HF + HG_W]
    iv = proj[:, OFF_HI:OFF_HI + HG_W]
    gr = proj[:, OFF_HG:OFF_HG + HG_W]
    qh = _silu(qr)
    half_gap = 0.5 * (1.0 - lb)
    f = (lb + half_gap) + half_gap * jnp.tanh(0.5 * fr)
    kk = 1.0 - f
    logf = jnp.log(f)

    rmod = lax.broadcasted_iota(I32, (tq, HG_W), 0) & (HG_CHUNK - 1)
    bc = logf
    s = 1
    while s < HG_CHUNK:
        bc = bc + jnp.where(rmod >= s, pltpu.roll(bc, s, axis=0), 0.0)
        s *= 2

    b3 = bc.reshape(nc, HG_CHUNK, HG_W)
    blast = b3[:, HG_CHUNK - 1:HG_CHUNK, :]
    kend = (kk.reshape(nc, HG_CHUNK, HG_W) * jnp.exp(blast - b3)).reshape(tq, HG_W)
    decay = jnp.exp(blast).reshape(nc, HG_W)
    qdec = (qh * jnp.exp(bc)).astype(BF16)
    kdec = (kk * jnp.exp(-bc)).astype(BF16)
    kend = kend.astype(BF16)
    ivb = iv.astype(BF16)

    ri = lax.broadcasted_iota(I32, (tq, tq), 0)
    ci = lax.broadcasted_iota(I32, (tq, tq), 1)
    cmask = ((ri // HG_CHUNK) == (ci // HG_CHUNK)) & (ri >= ci)

    heads = [slice(hh * HG_DIM, (hh + 1) * HG_DIM) for hh in range(HG_HEADS)]
    chunks = [slice(n * HG_CHUNK, (n + 1) * HG_CHUNK) for n in range(nc)]

    lane_head = lax.broadcasted_iota(I32, (HG_CHUNK, HG_W), 1) // HG_DIM
    for n, rs in enumerate(chunks):
        vstack = jnp.concatenate([ivb[rs, sl] for sl in heads], axis=0)
        kblk = jnp.concatenate([jnp.where(lane_head == hh, kend[rs], 0.0).astype(BF16)
                                for hh in range(HG_HEADS)], axis=0)
        u_scr[n] = _dot_tn(vstack, kblk)

    st = st_ref[...]
    for n in range(nc):
        stb_scr[n] = st.astype(BF16)
        st = st * decay[n:n + 1] + u_scr[n]
    st_ref[...] = st

    outs = []
    for hh, sl in enumerate(heads):
        a = _dot_nt(qdec[:, sl], kdec[:, sl])
        a = jnp.where(cmask, a, 0.0).astype(BF16)
        o_intra = _dot(a, ivb[:, sl])
        inter = [_dot_nt(qdec[rs, sl], stb_scr[n, :, sl]) for n, rs in enumerate(chunks)]
        o = o_intra + jnp.concatenate(inter, axis=0)
        o = _rms(o) * hnorm[:, sl]
        g = gr[:, sl]
        outs.append(o * _silu(g))
    return jnp.concatenate(outs, axis=1)


def _route_topk(h2, wr_hi, wr_lo, br):
    tq = h2.shape[0]
    h_hi, h_lo = _split(h2)
    logits = _dot(h_hi, wr_hi) + _dot(h_lo, wr_hi) + _dot(h_hi, wr_lo)
    lt = logits.T[0:ROUTER_ROWS] + br
    sub = lax.broadcasted_iota(I32, (SUBLANES, tq), 0).astype(F32)
    none = float(SUBLANES)

    gl = lt[0:GROUP_ROWS]
    gm = jnp.max(gl, axis=0, keepdims=True)
    gidx = jnp.min(jnp.where(gl == gm, sub, none), axis=0, keepdims=True)
    g_w = 1.0 / jnp.sum(jnp.exp(gl - gm), axis=0, keepdims=True)

    group_rows = lambda g: lt[GROUP_ROWS + EPG * g:GROUP_ROWS + EPG * (g + 1)]
    es = group_rows(0)
    for g in range(1, N_GROUPS):
        es = jnp.where(gidx == float(g), group_rows(g), es)
    m1 = jnp.max(es, axis=0, keepdims=True)
    i1 = jnp.min(jnp.where(es == m1, sub, none), axis=0, keepdims=True)
    e2 = jnp.where(sub == i1, NEG, es)
    m2 = jnp.max(e2, axis=0, keepdims=True)
    i2 = jnp.min(jnp.where(e2 == m2, sub, none), axis=0, keepdims=True)
    dd = jnp.exp(m2 - m1)
    w1 = g_w / (1.0 + dd)
    w2 = g_w * dd / (1.0 + dd)
    first_low = i1 < i2
    ea = jnp.minimum(i1, i2)
    eb = jnp.maximum(i1, i2)
    w_lo = jnp.where(first_low, w1, w2)
    w_hi = jnp.where(first_low, w2, w1)
    pair = ea * (2.0 * EPG - 1.0 - ea) * 0.5 + (eb - ea - 1.0)
    bucket = gidx * float(N_PAIRS) + pair
    return bucket, w_lo, w_hi


def _route_rank(bucket, w_lo, w_hi, carry_ref, bidx, live):
    tq = bucket.shape[1]
    brow = lax.broadcasted_iota(I32, (LANES, tq), 0).astype(F32)
    onehot = brow == bucket
    oh = jnp.where(onehot, live, 0.0)
    ti = lax.broadcasted_iota(I32, (tq, tq), 0)
    tj = lax.broadcasted_iota(I32, (tq, tq), 1)
    upper = jnp.where(ti < tj, 1.0, 0.0).astype(BF16)
    before = _dot(oh.astype(BF16), upper) + carry_ref[...]
    rank = jnp.sum(jnp.where(onehot, before, 0.0), axis=0, keepdims=True)
    carry_ref[...] = carry_ref[...] + jnp.sum(oh, axis=1, keepdims=True)

    lane_row = lax.broadcasted_iota(I32, (LANES - EXT_BATCH0, tq), 0)
    onehot_b = jnp.where(lane_row == bidx, 1.0, 0.0)
    info = jnp.concatenate([w_lo, w_hi, jnp.zeros((EXT_BATCH0 - 2, tq), F32), onehot_b], axis=0)
    return rank, info.T


def _mixer_kernel(sinks_ref, x_ref, mod_ref, ln1pre_ref, ln1post_ref, ln2pre_ref, anorm_ref, hnorm_ref,
                  lb_ref, win_ref, wout_ref, wrhi_ref, wrlo_ref, br_ref, bias_ref,
                  rows_hbm, info_ref, cnt_ref,
                  kprev_ref, vprev_ref, st_ref, carry_ref, s_scr, m_scr, p_scr, u_scr, stb_scr,
                  h2_scr, proj_scr, rowbuf, rsem, *, tiles_per_seq):
    s = pl.program_id(0)
    n_tiles = pl.num_programs(0) - 1
    tq = x_ref.shape[0]
    t = lax.rem(jnp.minimum(s, n_tiles - 1), tiles_per_seq)
    slot = lax.rem(s, 2)
    bits = lambda a: pltpu.bitcast(a, U32)

    def row_copy(buf_slot, tile):
        return pltpu.make_async_copy(rowbuf.at[buf_slot], rows_hbm.at[pl.ds(pl.multiple_of(tile * tq, tq), tq), 0],
                                     rsem.at[buf_slot])

    @pl.when(s == 0)
    def _():
        carry_ref[...] = jnp.zeros_like(carry_ref)
        h2_scr[...] = jnp.zeros_like(h2_scr)

    @pl.when(t == 0)
    def _():
        st_ref[...] = jnp.zeros_like(st_ref)
        kprev_ref[...] = jnp.zeros_like(kprev_ref)
        vprev_ref[...] = jnp.zeros_like(vprev_ref)

    x = x_ref[...]
    mod = mod_ref[0]
    sh1, sc1, ga1, sh2, sc2 = mod[0:1], mod[1:2], mod[2:3], mod[3:4], mod[4:5]
    prev = jnp.maximum(s - 1, 0)

    @pl.when(s <= n_tiles)
    def _():
        bucket, w_lo, w_hi = _route_topk(h2_scr[...], wrhi_ref[...], wrlo_ref[...], br_ref[...])
        h = _rms(x) * (ln1pre_ref[...] * (1.0 + sc1)) + sh1
        proj_scr[...] = _dot(h.astype(BF16), win_ref[...])
        live = jnp.where(s >= 1, 1.0, 0.0)
        rank, ext = _route_rank(bucket, w_lo, w_hi, carry_ref, prev // tiles_per_seq, live)
        rowbuf[1 - slot, :, OFF_EXT:ROW_W] = bits(ext)
        info_ref[0] = jnp.concatenate([bucket, rank, jnp.zeros((6, tq), F32)], axis=0).astype(I32)
        cnt_ref[...] = jnp.broadcast_to(carry_ref[...], cnt_ref.shape)

    proj = proj_scr[...]
    scores, exps, values = _attention(proj, kprev_ref, vprev_ref, sinks_ref, bias_ref, t, s_scr, m_scr, p_scr)
    scores()
    exps()
    attn = _rms(values()) * anorm_ref[...]

    lbr = lb_ref[...]
    le = jnp.exp(lbr - jnp.max(lbr, axis=0, keepdims=True))
    lb = le[0:1] / jnp.sum(le, axis=0, keepdims=True)
    hg = _hgrn2(proj, lb, st_ref, hnorm_ref[...], u_scr, stb_scr)

    mix = _dot(jnp.concatenate([attn, hg], axis=1).astype(BF16), wout_ref[...])
    x1 = x + _rms(mix) * (ga1 * ln1post_ref[...])

    h2 = _rms(x1) * (ln2pre_ref[...] * (1.0 + sc2)) + sh2
    h2r = h2.astype(BF16).astype(F32)

    @pl.when(s >= 2)
    def _():
        row_copy(slot, s - 2).wait()

    rowbuf[slot, :, 0:D_MODEL] = bits(x1)
    rowbuf[slot, :, OFF_H2P:OFF_EXT] = ((bits(h2r[:, 0:H2P_W]) >> 16)
                                        | (bits(h2r[:, H2P_W:D_MODEL]) & jnp.uint32(0xFFFF0000)))
    h2_scr[...] = h2

    @pl.when(s >= 1)
    def _():
        row_copy(1 - slot, prev).start()

    @pl.when(s == n_tiles)
    def _():
        row_copy(1 - slot, prev).wait()


def _attn_bias():
    qi = np.arange(WINDOW)[:, None]
    kj = np.arange(2 * WINDOW)[None, :]
    dist = qi + WINDOW - kj
    in_win = (dist >= 0) & (dist < WINDOW)
    slopes = 2.0 ** (-8.0 * (np.arange(ATTN_HEADS) + 1.0) / ATTN_HEADS)
    b = np.where(in_win[None], -slopes[:, None, None] * dist[None] * LOG2E, NEG)
    b_first = np.where((kj >= WINDOW)[None], b, NEG)
    return jnp.asarray(np.stack([b, b_first]).astype(np.float32))


def _mixer(x2, mod3, sinks, ln1pre, ln1post, ln2pre, anorm, hnorm, lb, win, wout, wr_hi, wr_lo, br,
           bsz, seq):
    bias = _attn_bias()
    n = bsz * seq
    nt = seq // TQ
    n_tiles = bsz * nt
    n_sc = (TQ // WINDOW) * ATTN_HEADS
    cur = lambda s: jnp.minimum(s, n_tiles - 1)
    const = lambda s: (0, 0)
    full = lambda a: pl.BlockSpec(a.shape, const)
    return pl.pallas_call(
        functools.partial(_mixer_kernel, tiles_per_seq=nt),
        grid=(n_tiles + 1,),
        in_specs=[pl.BlockSpec(memory_space=pltpu.SMEM),
                  pl.BlockSpec((TQ, D_MODEL), lambda s: (cur(s), 0)),
                  pl.BlockSpec((1, 6, D_MODEL), lambda s: (cur(s) // nt, 0, 0)),
                  full(ln1pre), full(ln1post), full(ln2pre), full(anorm), full(hnorm), full(lb),
                  full(win), full(wout), full(wr_hi), full(wr_lo), full(br),
                  pl.BlockSpec(bias.shape, lambda s: (0, 0, 0, 0))],
        out_specs=[pl.BlockSpec(memory_space=pl.ANY),
                   pl.BlockSpec((1, 8, TQ), lambda s: (jnp.maximum(s - 1, 0), 0, 0)),
                   pl.BlockSpec((LANES, LANES), const)],
        out_shape=[jax.ShapeDtypeStruct((n, 1, ROW_W), U32),
                   jax.ShapeDtypeStruct((n // TQ, 8, TQ), I32),
                   jax.ShapeDtypeStruct((LANES, LANES), F32)],
        scratch_shapes=[pltpu.VMEM((WINDOW, KV_W), F32),
                        pltpu.VMEM((WINDOW, KV_W), F32),
                        pltpu.VMEM((HG_DIM, HG_W), F32),
                        pltpu.VMEM((LANES, 1), F32),
                        pltpu.VMEM((n_sc, WINDOW, 2 * WINDOW), F32),
                        pltpu.VMEM((n_sc, WINDOW, 1), F32),
                        pltpu.VMEM((n_sc, WINDOW, 2 * WINDOW), BF16),
                        pltpu.VMEM((TQ // HG_CHUNK, HG_DIM, HG_W), F32),
                        pltpu.VMEM((TQ // HG_CHUNK, HG_DIM, HG_W), BF16),
                        pltpu.VMEM((TQ, D_MODEL), F32),
                        pltpu.VMEM((TQ, IN_W), F32),
                        pltpu.VMEM((2, TQ, ROW_W), U32),
                        pltpu.SemaphoreType.DMA((2,))],
        compiler_params=pltpu.CompilerParams(dimension_semantics=("arbitrary",),
                                             vmem_limit_bytes=VMEM_LIMIT),
        name="mixer",
    )(sinks, x2, mod3, ln1pre, ln1post, ln2pre, anorm, hnorm, lb, win, wout, wr_hi, wr_lo, br, bias)


PERM_STEPS = 8
PERM_ROWS = 8
PERM_UNROLL = 16


def _perm_kernel(rs_ref, cnt_ref, bucket_ref, rank_ref, perm_ref, pos_vmem, pos_smem, sem):
    pid = pl.program_id(0)
    rows, cols = pos_vmem.shape

    b = bucket_ref[0]
    start = jnp.zeros_like(b)
    for k in range(N_BUCKETS):
        start = jnp.where(b == k, rs_ref[k], start)
    pos_vmem[...] = start + rank_ref[0]
    copies = [pltpu.make_async_copy(pos_vmem.at[r], pos_smem.at[pl.ds(r * cols, cols)], sem) for r in range(rows)]
    for cp in copies:
        cp.start()

    @pl.when(pid == 0)
    def _():
        def per_bucket(k, carry):
            first = rs_ref[k]
            cnt = cnt_ref[k]

            def pad(r, c2):
                perm_ref[first + r] = 0
                return c2

            lax.fori_loop(cnt, ((cnt + TM - 1) // TM) * TM, pad, 0)
            return carry

        lax.fori_loop(0, N_BUCKETS, per_bucket, 0)

        def tail(blk, carry):
            for u in range(PERM_UNROLL):
                perm_ref[blk * PERM_UNROLL + u] = 0
            return carry

        lax.fori_loop(rs_ref[N_BUCKETS] // PERM_UNROLL, perm_ref.shape[0] // PERM_UNROLL, tail, 0)

    for cp in copies:
        cp.wait()
    base = pid * (rows * cols)

    def body(j, carry):
        i0 = j * PERM_UNROLL
        positions = [pos_smem[i0 + u] for u in range(PERM_UNROLL)]
        for u in range(PERM_UNROLL):
            perm_ref[positions[u]] = base + i0 + u
        return carry

    lax.fori_loop(0, rows * cols // PERM_UNROLL, body, 0)


def _perm(row_start, counts, bucket, rank, n_rows):
    n = bucket.shape[0]
    cols = n // (PERM_STEPS * PERM_ROWS)
    assert n % (PERM_STEPS * PERM_ROWS * PERM_UNROLL) == 0
    chunked = lambda a: a.reshape(PERM_STEPS, PERM_ROWS, cols)
    chunk_spec = pl.BlockSpec((1, PERM_ROWS, cols), lambda i: (i, 0, 0))
    return pl.pallas_call(
        _perm_kernel,
        grid=(PERM_STEPS,),
        in_specs=[pl.BlockSpec(memory_space=pltpu.SMEM),
                  pl.BlockSpec(memory_space=pltpu.SMEM),
                  chunk_spec, chunk_spec],
        out_specs=pl.BlockSpec(memory_space=pltpu.SMEM),
        out_shape=jax.ShapeDtypeStruct((n_rows,), I32),
        scratch_shapes=[pltpu.VMEM((PERM_ROWS, cols), I32), pltpu.SMEM((PERM_ROWS * cols,), I32),
                        pltpu.SemaphoreType.DMA(())],
        compiler_params=pltpu.CompilerParams(dimension_semantics=("arbitrary",)),
        name="perm",
    )(row_start, counts, chunked(bucket), chunked(rank))


GATHER_DEPTH = 3
OUT_SLOTS = 3


SCHED_EXPERT, SCHED_RUN_START, SCHED_SLOT, SCHED_NEXT_EXPERT = range(4)


def _moe_kernel(nt_ref, nv_ref, sched_ref, perm_ref,
                rows_hbm, gain_ref, wg_hbm, wu_hbm, wd_hbm,
                out_hbm, xbuf, x2d, obuf, wgu_buf, wd_buf, gsem, ssem, wsem):
    i = pl.program_id(0)
    nt = nt_ref[0]
    n_steps = pl.num_programs(0)
    last_tile = n_steps - 1
    sched = lambda field, side: sched_ref[(field * 2 + side) * n_steps + i]

    def weight_copies(side, expert, slot):
        return (pltpu.make_async_copy(wg_hbm.at[expert], wgu_buf.at[side, slot, :, 0:FF], wsem.at[side, slot]),
                pltpu.make_async_copy(wu_hbm.at[expert], wgu_buf.at[side, slot, :, FF:2 * FF], wsem.at[side, slot]),
                pltpu.make_async_copy(wd_hbm.at[expert], wd_buf.at[side, slot], wsem.at[side, slot]))
    nbuf = xbuf.shape[0] // TM

    def start_gather(tile, pred):
        sl = lax.rem(tile, nbuf)
        base = jnp.minimum(tile, last_tile) * TM
        for r in range(TM):
            @pl.when(pred)
            def _():
                tok = perm_ref[base + r]
                pltpu.make_async_copy(rows_hbm.at[tok], xbuf.at[sl * TM + r], gsem.at[sl]).start()

    def wait_gather(tile):
        sl = lax.rem(tile, nbuf)
        pltpu.make_async_copy(rows_hbm.at[pl.ds(0, TM)], xbuf.at[pl.ds(sl * TM, TM)], gsem.at[sl]).wait()

    def wait_scatter(sl, nv):
        @pl.when(nv == TM)
        def _():
            pltpu.make_async_copy(obuf.at[sl], out_hbm.at[pl.ds(0, TM)], ssem.at[sl]).wait()

        @pl.when(nv < TM)
        def _():
            def one(r, carry):
                pltpu.make_async_copy(obuf.at[sl, pl.ds(0, 1)], out_hbm.at[pl.ds(0, 1)], ssem.at[sl]).wait()
                return carry

            lax.fori_loop(0, nv, one, 0)

    def compute(xb, slot_a, slot_b):
        x1 = pltpu.bitcast(xb[:, 0:D_MODEL], F32)
        hp = xb[:, OFF_H2P:OFF_EXT]
        h2a = pltpu.bitcast(hp << 16, F32).astype(BF16)
        h2b = pltpu.bitcast(hp & jnp.uint32(0xFFFF0000), F32).astype(BF16)
        ext = pltpu.bitcast(xb[:, OFF_EXT:ROW_W], F32)
        w_lo, w_hi = ext[:, 0:1], ext[:, 1:2]
        ga2 = jnp.zeros((TM, D_MODEL), F32)
        for k in range(gain_ref.shape[0]):
            ga2 = jnp.where(ext[:, EXT_BATCH0 + k:EXT_BATCH0 + k + 1] > 0.5, gain_ref[k:k + 1, :], ga2)

        def hidden(side, slot, w):
            gu = (_dot(h2a, wgu_buf[side, slot, 0:H2P_W])
                  + _dot(h2b, wgu_buf[side, slot, H2P_W:D_MODEL]))
            hg, hu = gu[:, 0:FF], gu[:, FF:2 * FF]
            return (w * (_silu(hg) * hu)).astype(BF16)

        act = jnp.concatenate([hidden(0, slot_a, w_lo), hidden(1, slot_b, w_hi)], axis=1)
        wd = jnp.concatenate([wd_buf[0, slot_a], wd_buf[1, slot_b]], axis=0)
        y = _dot(act, wd)
        return x1 + ga2 * _rms(y)

    @pl.when(i == 0)
    def _():
        for side in range(2):
            for cp in weight_copies(side, sched(SCHED_EXPERT, side), 0):
                cp.start()
        for d in range(GATHER_DEPTH):
            start_gather(d, d < nt)

    @pl.when(i < nt)
    def _():
        nv = nv_ref[i]
        osl = lax.rem(i, OUT_SLOTS)
        slots = []
        for side in range(2):
            slot = sched(SCHED_SLOT, side)
            slots.append(slot)

            @pl.when(sched(SCHED_RUN_START, side) == 1)
            def _():
                for cp in weight_copies(side, sched(SCHED_EXPERT, side), slot):
                    cp.wait()
                nxt = sched(SCHED_NEXT_EXPERT, side)

                @pl.when(nxt >= 0)
                def _():
                    for cp in weight_copies(side, nxt, 1 - slot):
                        cp.start()

        wait_gather(i)

        @pl.when(i >= OUT_SLOTS)
        def _():
            wait_scatter(osl, nv_ref[jnp.maximum(i - OUT_SLOTS, 0)])

        start_gather(i + GATHER_DEPTH, i + GATHER_DEPTH < nt)
        x2d[...] = xbuf[pl.ds(lax.rem(i, nbuf) * TM, TM)].reshape(TM, ROW_W)
        result = compute(x2d[...], *slots)

        def scatter_row(k, r):
            tok = perm_ref[i * TM + r]
            pltpu.make_async_copy(obuf.at[k, pl.ds(r, 1)], out_hbm.at[pl.ds(tok, 1)], ssem.at[k]).start()

        for k in range(OUT_SLOTS):
            @pl.when((osl == k) & (nv == TM))
            def _():
                obuf[k] = result
                for r in range(TM):
                    scatter_row(k, r)

            @pl.when((osl == k) & (nv < TM))
            def _():
                obuf[k] = result
                for r in range(TM):
                    pl.when(r < nv)(functools.partial(scatter_row, k, r))

        @pl.when(i == nt - 1)
        def _():
            wait_scatter(osl, nv)
            for back in range(1, OUT_SLOTS):
                @pl.when(i >= back)
                def _():
                    wait_scatter(lax.rem(i - back + OUT_SLOTS, OUT_SLOTS), nv_ref[jnp.maximum(i - back, 0)])


def _moe(rows, gain, wg, wu, wd, nt, nv, sched, perm, n_tiles):
    n = rows.shape[0]
    const2 = lambda i, *_: (0, 0)
    grid_spec = pltpu.PrefetchScalarGridSpec(
        num_scalar_prefetch=4,
        grid=(n_tiles,),
        in_specs=[pl.BlockSpec(memory_space=pl.ANY),
                  pl.BlockSpec(gain.shape, const2),
                  pl.BlockSpec(memory_space=pl.ANY),
                  pl.BlockSpec(memory_space=pl.ANY),
                  pl.BlockSpec(memory_space=pl.ANY)],
        out_specs=pl.BlockSpec(memory_space=pl.ANY),
        scratch_shapes=[pltpu.VMEM(((GATHER_DEPTH + 1) * TM, 1, ROW_W), U32),
                        pltpu.VMEM((TM, ROW_W), U32),
                        pltpu.VMEM((OUT_SLOTS, TM, D_MODEL), F32),
                        pltpu.VMEM((2, 2, D_MODEL, 2 * FF), BF16),
                        pltpu.VMEM((2, 2, FF, D_MODEL), BF16),
                        pltpu.SemaphoreType.DMA((GATHER_DEPTH + 1,)),
                        pltpu.SemaphoreType.DMA((OUT_SLOTS,)),
                        pltpu.SemaphoreType.DMA((2, 2))],
    )
    return pl.pallas_call(
        _moe_kernel,
        grid_spec=grid_spec,
        out_shape=jax.ShapeDtypeStruct((n, D_MODEL), F32),
        compiler_params=pltpu.CompilerParams(dimension_semantics=("arbitrary",),
                                             vmem_limit_bytes=VMEM_LIMIT),
        name="moe",
    )(nt, nv, sched, perm, rows, gain, wg, wu, wd)


def _weight_schedule(ea, eb):
    n_tiles = ea.shape[0]
    tid = jnp.arange(n_tiles, dtype=I32)
    fields = [[], [], [], []]
    for e in (ea, eb):
        start = jnp.concatenate([jnp.ones((1,), I32), (e[1:] != e[:-1]).astype(I32)])
        slot = (jnp.cumsum(start) - 1) % 2
        start_idx = jnp.where(start == 1, tid, n_tiles)
        next_start = jnp.concatenate([lax.cummin(start_idx, reverse=True)[1:], jnp.full((1,), n_tiles, I32)])
        nxt = jnp.where(next_start < n_tiles, e[jnp.minimum(next_start, n_tiles - 1)], -1)
        for f, v in zip(fields, (e, start, slot, nxt)):
            f.append(v.astype(I32))
    return jnp.concatenate([v for f in fields for v in f])


def kernel(x, c, ln1_pre, ln1_post, ln2_pre, ln2_post, w_ada, b_ada, w_in, attn_sinks, attn_out_norm,
           hgrn_lb, hgrn_out_norm, w_out, w_router_group, b_router_group, w_router_expert,
           b_router_expert, w_exp_gate, w_exp_up, w_exp_down):
    bsz, seq, d = x.shape
    assert d == D_MODEL and seq % TQ == 0 and w_ada.shape[0] == 1 and hgrn_lb.shape[0] == 2
    assert bsz <= LANES - EXT_BATCH0 and (bsz * seq) % TM == 0
    n = bsz * seq

    mod = _ada(c, w_ada[0], b_ada[0])
    mod3 = mod.reshape(bsz, 6, d)

    wr = jnp.concatenate([w_router_group[0], jnp.zeros((d, GROUP_ROWS - N_GROUPS), F32), w_router_expert[0],
                          jnp.zeros((d, LANES - ROUTER_ROWS), F32)], axis=1)
    br = jnp.concatenate([b_router_group[0], jnp.full((GROUP_ROWS - N_GROUPS,), NEG, F32), b_router_expert[0]])
    wr_hi = wr.astype(BF16)
    wr_lo = (wr - wr_hi.astype(F32)).astype(BF16)

    x1ext, info, cnt = _mixer(
        x.reshape(n, d), mod3, attn_sinks[0], ln1_pre, ln1_post, ln2_pre, attn_out_norm, hgrn_out_norm,
        hgrn_lb, w_in[0].astype(BF16), w_out[0].astype(BF16), wr_hi, wr_lo, br.reshape(ROUTER_ROWS, 1),
        bsz, seq)

    n_tiles = n // TM + N_BUCKETS
    counts = cnt[:N_BUCKETS, 0].astype(I32)
    tiles_per = (counts + TM - 1) // TM
    tile_end = jnp.cumsum(tiles_per)
    tile_start = tile_end - tiles_per
    nt = tile_end[-1]
    bucket = info[:, 0, :].reshape(n)
    rank = info[:, 1, :].reshape(n)
    tid = jnp.arange(n_tiles, dtype=I32)[None, :]
    member = (tid >= tile_start[:, None]) & (tid < tile_end[:, None])
    pick = lambda per_bucket: jnp.sum(jnp.where(member, per_bucket, 0), axis=0).astype(I32)
    bidx = np.arange(N_BUCKETS, dtype=np.int32)
    ea_of = jnp.asarray((bidx // N_PAIRS) * EPG + _PAIR_A[bidx % N_PAIRS])[:, None]
    eb_of = jnp.asarray((bidx // N_PAIRS) * EPG + _PAIR_B[bidx % N_PAIRS])[:, None]
    last_used = jnp.arange(N_BUCKETS)[:, None] == jnp.max(jnp.where(tiles_per > 0, jnp.arange(N_BUCKETS), 0))
    unused = tid[0] >= nt
    nv = pick(jnp.clip(counts[:, None] - (tid - tile_start[:, None]) * TM, 0, TM))
    ea = jnp.where(unused, jnp.sum(jnp.where(last_used, ea_of, 0)), pick(ea_of)).astype(I32)
    eb = jnp.where(unused, jnp.sum(jnp.where(last_used, eb_of, 0)), pick(eb_of)).astype(I32)

    pad128 = lambda a: jnp.concatenate([a, jnp.zeros((LANES - a.shape[0],), I32)])
    row_start = jnp.concatenate([tile_start, nt.reshape(1)]) * TM
    perm = _perm(pad128(row_start), pad128(counts), bucket, rank, n_tiles * TM)

    wg, wu, wd = w_exp_gate[0].astype(BF16), w_exp_up[0].astype(BF16), w_exp_down[0].astype(BF16)
    gain = mod3[:, 5, :] * ln2_post
    out = _moe(x1ext, gain, wg, wu, wd, nt.reshape(1), nv, _weight_schedule(ea, eb), perm, n_tiles)
    return out.reshape(bsz, seq, d)
```

```python
import functools

import numpy as np
import jax
import jax.numpy as jnp
from jax import lax
from jax.experimental import pallas as pl
from jax.experimental.pallas import tpu as pltpu

F32 = jnp.float32
BF16 = jnp.bfloat16
I32 = jnp.int32

D_MODEL = 1024
ATTN_HEADS = 8
HEAD_DIM = 64
WINDOW = 128
ATTN_W = 512
KV_W = 128
HG_HEADS = 4
HG_DIM = 128
HG_W = 512
HG_CHUNK = 32
IN_W = 2816
N_GROUPS = 4
EPG = 8
N_EXPERTS = 32
FF = 256
N_PAIRS = EPG * (EPG - 1) // 2
N_BUCKETS = N_GROUPS * N_PAIRS
EPS = 1e-6
NEG = -1e30
LOG2E = 1.4426950408889634

LANES = 128
H2P_W = D_MODEL // 2
EXT_W = LANES
OFF_H2P = D_MODEL
OFF_EXT = D_MODEL + H2P_W
ROW_W = OFF_EXT + EXT_W
EXT_BATCH0 = 8
U32 = jnp.uint32
SUBLANES = 8
GROUP_ROWS = SUBLANES
ROUTER_ROWS = GROUP_ROWS + N_EXPERTS

TQ = 256
TM = 128
VMEM_LIMIT = 56 * 1024 * 1024

OFF_Q, OFF_K, OFF_V, OFF_HQ, OFF_HF, OFF_HI, OFF_HG = 0, 512, 640, 768, 1280, 1792, 2304

_PAIR_A = np.array([a for a in range(EPG) for b in range(a + 1, EPG)], np.int32)
_PAIR_B = np.array([b for a in range(EPG) for b in range(a + 1, EPG)], np.int32)


def _dot(a, b):
    return jnp.dot(a, b, preferred_element_type=F32)


def _dot_nt(a, b):
    return lax.dot_general(a, b, (((1,), (1,)), ((), ())), preferred_element_type=F32)


def _dot_tn(a, b):
    return lax.dot_general(a, b, (((0,), (0,)), ((), ())), preferred_element_type=F32)


def _split(a):
    hi = a.astype(BF16)
    lo = (a - hi.astype(F32)).astype(BF16)
    return hi, lo


def _rms(x):
    return x * lax.rsqrt(jnp.mean(x * x, axis=-1, keepdims=True) + EPS)


def _silu(x):
    half = 0.5 * x
    return half * jnp.tanh(half) + half


def _ada_kernel(c_ref, w_ref, b_ref, o_ref):
    c = c_ref[...]
    ca = _silu(c)
    c_hi, c_lo = _split(ca)
    w_hi, w_lo = _split(w_ref[...])
    o_ref[...] = _dot(c_hi, w_hi) + _dot(c_lo, w_hi) + _dot(c_hi, w_lo) + b_ref[...]


def _ada(c, w, b):
    bsz, d = c.shape
    n_out = w.shape[1]
    return pl.pallas_call(
        _ada_kernel,
        grid=(n_out // d,),
        in_specs=[pl.BlockSpec((bsz, d), lambda j: (0, 0)),
                  pl.BlockSpec((d, d), lambda j: (0, j)),
                  pl.BlockSpec((1, d), lambda j: (0, j))],
        out_specs=pl.BlockSpec((bsz, d), lambda j: (0, j)),
        out_shape=jax.ShapeDtypeStruct((bsz, n_out), F32),
        compiler_params=pltpu.CompilerParams(dimension_semantics=("arbitrary",),
                                             vmem_limit_bytes=VMEM_LIMIT),
        name="adaln",
    )(c, w, b.reshape(1, n_out))


def _attention(proj, kprev_ref, vprev_ref, sinks_ref, bias_ref, t, s_scr, m_scr, p_scr):
    tq = proj.shape[0]
    q = (proj[:, OFF_Q:OFF_Q + ATTN_W] * (HEAD_DIM ** -0.5 * LOG2E)).astype(BF16)
    kf = jnp.concatenate([kprev_ref[...], proj[:, OFF_K:OFF_K + KV_W]], axis=0)
    vf = jnp.concatenate([vprev_ref[...], proj[:, OFF_V:OFF_V + KV_W]], axis=0)
    kprev_ref[...] = proj[tq - WINDOW:, OFF_K:OFF_K + KV_W]
    vprev_ref[...] = proj[tq - WINDOW:, OFF_V:OFF_V + KV_W]

    lo = lax.broadcasted_iota(I32, kf.shape, 1) < HEAD_DIM
    kr = pltpu.roll(kf, HEAD_DIM, axis=1)
    vr = pltpu.roll(vf, HEAD_DIM, axis=1)

    def variants(a, ar):
        return [[jnp.where(lo, a, 0.0).astype(BF16), jnp.where(lo, 0.0, ar).astype(BF16)],
                [jnp.where(lo, ar, 0.0).astype(BF16), jnp.where(lo, 0.0, a).astype(BF16)]]

    kvar = variants(kf, kr)
    vvar = variants(vf, vr)

    first = jnp.where(t > 0, 0, 1)

    nblk = tq // WINDOW
    idx = lambda j, h: j * ATTN_HEADS + h
    keys = lambda a, j: a[j * WINDOW:(j + 2) * WINDOW]

    def scores():
        for j in range(nblk):
            for h in range(ATTN_HEADS):
                p, par = h // 2, h % 2
                qp = q[j * WINDOW:(j + 1) * WINDOW, p * LANES:(p + 1) * LANES]
                s = _dot_nt(qp, keys(kvar[p // 2][par], j)) + bias_ref[first if j == 0 else 0, h]
                s_scr[idx(j, h)] = s
                m_scr[idx(j, h)] = jnp.maximum(jnp.max(s, axis=-1, keepdims=True), sinks_ref[h] * LOG2E)

    def exps():
        for j in range(nblk):
            for h in range(ATTN_HEADS):
                m = m_scr[idx(j, h)]
                pe = jnp.exp2(s_scr[idx(j, h)] - m)
                p_scr[idx(j, h)] = pe.astype(BF16)
                m_scr[idx(j, h)] = 1.0 / (jnp.sum(pe, axis=-1, keepdims=True) + jnp.exp2(sinks_ref[h] * LOG2E - m))

    def values():
        blocks = []
        for j in range(nblk):
            pairs = []
            for p in range(ATTN_HEADS // 2):
                acc = None
                for par in range(2):
                    h = 2 * p + par
                    o = _dot(p_scr[idx(j, h)], keys(vvar[p // 2][par], j)) * m_scr[idx(j, h)]
                    acc = o if acc is None else acc + o
                pairs.append(acc)
            blocks.append(jnp.concatenate(pairs, axis=1))
        return jnp.concatenate(blocks, axis=0)

    return scores, exps, values


def _hgrn2(proj, lb, st_ref, hnorm, u_scr, stb_scr):
    tq = proj.shape[0]
    nc = tq // HG_CHUNK
    qr = proj[:, OFF_HQ:OFF_HQ + HG_W]
    fr = proj[:, OFF_HF:OFF_HF + HG_W]
    iv = proj[:, OFF_HI:OFF_HI + HG_W]
    gr = proj[:, OFF_HG:OFF_HG + HG_W]
    qh = _silu(qr)
    half_gap = 0.5 * (1.0 - lb)
    f = (lb + half_gap) + half_gap * jnp.tanh(0.5 * fr)
    kk = 1.0 - f
    logf = jnp.log(f)

    rmod = lax.broadcasted_iota(I32, (tq, HG_W), 0) & (HG_CHUNK - 1)
    bc = logf
    s = 1
    while s < HG_CHUNK:
        bc = bc + jnp.where(rmod >= s, pltpu.roll(bc, s, axis=0), 0.0)
        s *= 2

    b3 = bc.reshape(nc, HG_CHUNK, HG_W)
    blast = b3[:, HG_CHUNK - 1:HG_CHUNK, :]
    kend = (kk.reshape(nc, HG_CHUNK, HG_W) * jnp.exp(blast - b3)).reshape(tq, HG_W)
    decay = jnp.exp(blast).reshape(nc, HG_W)
    qdec = (qh * jnp.exp(bc)).astype(BF16)
    kdec = (kk * jnp.exp(-bc)).astype(BF16)
    kend = kend.astype(BF16)
    ivb = iv.astype(BF16)

    ri = lax.broadcasted_iota(I32, (tq, tq), 0)
    ci = lax.broadcasted_iota(I32, (tq, tq), 1)
    cmask = ((ri // HG_CHUNK) == (ci // HG_CHUNK)) & (ri >= ci)

    heads = [slice(hh * HG_DIM, (hh + 1) * HG_DIM) for hh in range(HG_HEADS)]
    chunks = [slice(n * HG_CHUNK, (n + 1) * HG_CHUNK) for n in range(nc)]

    lane_head = lax.broadcasted_iota(I32, (HG_CHUNK, HG_W), 1) // HG_DIM
    for n, rs in enumerate(chunks):
        vstack = jnp.concatenate([ivb[rs, sl] for sl in heads], axis=0)
        kblk = jnp.concatenate([jnp.where(lane_head == hh, kend[rs], 0.0).astype(BF16)
                                for hh in range(HG_HEADS)], axis=0)
        u_scr[n] = _dot_tn(vstack, kblk)

    st = st_ref[...]
    for n in range(nc):
        stb_scr[n] = st.astype(BF16)
        st = st * decay[n:n + 1] + u_scr[n]
    st_ref[...] = st

    outs = []
    for hh, sl in enumerate(heads):
        a = _dot_nt(qdec[:, sl], kdec[:, sl])
        a = jnp.where(cmask, a, 0.0).astype(BF16)
        o_intra = _dot(a, ivb[:, sl])
        inter = [_dot_nt(qdec[rs, sl], stb_scr[n, :, sl]) for n, rs in enumerate(chunks)]
        o = o_intra + jnp.concatenate(inter, axis=0)
        o = _rms(o) * hnorm[:, sl]
        g = gr[:, sl]
        outs.append(o * _silu(g))
    return jnp.concatenate(outs, axis=1)


def _route_topk(h2, wr_hi, wr_lo, br):
    tq = h2.shape[0]
    h_hi, h_lo = _split(h2)
    logits = _dot(h_hi, wr_hi) + _dot(h_lo, wr_hi) + _dot(h_hi, wr_lo)
    lt = logits.T[0:ROUTER_ROWS] + br
    sub = lax.broadcasted_iota(I32, (SUBLANES, tq), 0).astype(F32)
    none = float(SUBLANES)

    gl = lt[0:GROUP_ROWS]
    gm = jnp.max(gl, axis=0, keepdims=True)
    gidx = jnp.min(jnp.where(gl == gm, sub, none), axis=0, keepdims=True)
    g_w = 1.0 / jnp.sum(jnp.exp(gl - gm), axis=0, keepdims=True)

    group_rows = lambda g: lt[GROUP_ROWS + EPG * g:GROUP_ROWS + EPG * (g + 1)]
    es = group_rows(0)
    for g in range(1, N_GROUPS):
        es = jnp.where(gidx == float(g), group_rows(g), es)
    m1 = jnp.max(es, axis=0, keepdims=True)
    i1 = jnp.min(jnp.where(es == m1, sub, none), axis=0, keepdims=True)
    e2 = jnp.where(sub == i1, NEG, es)
    m2 = jnp.max(e2, axis=0, keepdims=True)
    i2 = jnp.min(jnp.where(e2 == m2, sub, none), axis=0, keepdims=True)
    dd = jnp.exp(m2 - m1)
    w1 = g_w / (1.0 + dd)
    w2 = g_w * dd / (1.0 + dd)
    first_low = i1 < i2
    ea = jnp.minimum(i1, i2)
    eb = jnp.maximum(i1, i2)
    w_lo = jnp.where(first_low, w1, w2)
    w_hi = jnp.where(first_low, w2, w1)
    pair = ea * (2.0 * EPG - 1.0 - ea) * 0.5 + (eb - ea - 1.0)
    bucket = gidx * float(N_PAIRS) + pair
    return bucket, w_lo, w_hi


def _route_rank(bucket, w_lo, w_hi, carry_ref, bidx, live):
    tq = bucket.shape[1]
    brow = lax.broadcasted_iota(I32, (LANES, tq), 0).astype(F32)
    onehot = brow == bucket
    oh = jnp.where(onehot, live, 0.0)
    ti = lax.broadcasted_iota(I32, (tq, tq), 0)
    tj = lax.broadcasted_iota(I32, (tq, tq), 1)
    upper = jnp.where(ti < tj, 1.0, 0.0).astype(BF16)
    before = _dot(oh.astype(BF16), upper) + carry_ref[...]
    rank = jnp.sum(jnp.where(onehot, before, 0.0), axis=0, keepdims=True)
    carry_ref[...] = carry_ref[...] + jnp.sum(oh, axis=1, keepdims=True)

    lane_row = lax.broadcasted_iota(I32, (LANES - EXT_BATCH0, tq), 0)
    onehot_b = jnp.where(lane_row == bidx, 1.0, 0.0)
    info = jnp.concatenate([w_lo, w_hi, jnp.zeros((EXT_BATCH0 - 2, tq), F32), onehot_b], axis=0)
    return rank, info.T


def _mixer_kernel(sinks_ref, x_ref, mod_ref, ln1pre_ref, ln1post_ref, ln2pre_ref, anorm_ref, hnorm_ref,
                  lb_ref, win_ref, wout_ref, wrhi_ref, wrlo_ref, br_ref, bias_ref,
                  rows_hbm, info_ref, cnt_ref,
                  kprev_ref, vprev_ref, st_ref, carry_ref, s_scr, m_scr, p_scr, u_scr, stb_scr,
                  h2_scr, proj_scr, rowbuf, rsem, *, tiles_per_seq):
    s = pl.program_id(0)
    n_tiles = pl.num_programs(0) - 1
    tq = x_ref.shape[0]
    t = lax.rem(jnp.minimum(s, n_tiles - 1), tiles_per_seq)
    slot = lax.rem(s, 2)
    bits = lambda a: pltpu.bitcast(a, U32)

    def row_copy(buf_slot, tile):
        return pltpu.make_async_copy(rowbuf.at[buf_slot], rows_hbm.at[pl.ds(pl.multiple_of(tile * tq, tq), tq), 0],
                                     rsem.at[buf_slot])

    @pl.when(s == 0)
    def _():
        carry_ref[...] = jnp.zeros_like(carry_ref)
        h2_scr[...] = jnp.zeros_like(h2_scr)

    @pl.when(t == 0)
    def _():
        st_ref[...] = jnp.zeros_like(st_ref)
        kprev_ref[...] = jnp.zeros_like(kprev_ref)
        vprev_ref[...] = jnp.zeros_like(vprev_ref)

    x = x_ref[...]
    mod = mod_ref[0]
    sh1, sc1, ga1, sh2, sc2 = mod[0:1], mod[1:2], mod[2:3], mod[3:4], mod[4:5]
    prev = jnp.maximum(s - 1, 0)

    @pl.when(s <= n_tiles)
    def _():
        bucket, w_lo, w_hi = _route_topk(h2_scr[...], wrhi_ref[...], wrlo_ref[...], br_ref[...])
        h = _rms(x) * (ln1pre_ref[...] * (1.0 + sc1)) + sh1
        proj_scr[...] = _dot(h.astype(BF16), win_ref[...])
        live = jnp.where(s >= 1, 1.0, 0.0)
        rank, ext = _route_rank(bucket, w_lo, w_hi, carry_ref, prev // tiles_per_seq, live)
        rowbuf[1 - slot, :, OFF_EXT:ROW_W] = bits(ext)
        info_ref[0] = jnp.concatenate([bucket, rank, jnp.zeros((6, tq), F32)], axis=0).astype(I32)
        cnt_ref[...] = jnp.broadcast_to(carry_ref[...], cnt_ref.shape)

    proj = proj_scr[...]
    scores, exps, values = _attention(proj, kprev_ref, vprev_ref, sinks_ref, bias_ref, t, s_scr, m_scr, p_scr)
    scores()
    exps()
    attn = _rms(values()) * anorm_ref[...]

    lbr = lb_ref[...]
    le = jnp.exp(lbr - jnp.max(lbr, axis=0, keepdims=True))
    lb = le[0:1] / jnp.sum(le, axis=0, keepdims=True)
    hg = _hgrn2(proj, lb, st_ref, hnorm_ref[...], u_scr, stb_scr)

    mix = _dot(jnp.concatenate([attn, hg], axis=1).astype(BF16), wout_ref[...])
    x1 = x + _rms(mix) * (ga1 * ln1post_ref[...])

    h2 = _rms(x1) * (ln2pre_ref[...] * (1.0 + sc2)) + sh2
    h2r = h2.astype(BF16).astype(F32)

    @pl.when(s >= 2)
    def _():
        row_copy(slot, s - 2).wait()

    rowbuf[slot, :, 0:D_MODEL] = bits(x1)
    rowbuf[slot, :, OFF_H2P:OFF_EXT] = ((bits(h2r[:, 0:H2P_W]) >> 16)
                                        | (bits(h2r[:, H2P_W:D_MODEL]) & jnp.uint32(0xFFFF0000)))
    h2_scr[...] = h2

    @pl.when(s >= 1)
    def _():
        row_copy(1 - slot, prev).start()

    @pl.when(s == n_tiles)
    def _():
        row_copy(1 - slot, prev).wait()


def _attn_bias():
    qi = np.arange(WINDOW)[:, None]
    kj = np.arange(2 * WINDOW)[None, :]
    dist = qi + WINDOW - kj
    in_win = (dist >= 0) & (dist < WINDOW)
    slopes = 2.0 ** (-8.0 * (np.arange(ATTN_HEADS) + 1.0) / ATTN_HEADS)
    b = np.where(in_win[None], -slopes[:, None, None] * dist[None] * LOG2E, NEG)
    b_first = np.where((kj >= WINDOW)[None], b, NEG)
    return jnp.asarray(np.stack([b, b_first]).astype(np.float32))


def _mixer(x2, mod3, sinks, ln1pre, ln1post, ln2pre, anorm, hnorm, lb, win, wout, wr_hi, wr_lo, br,
           bsz, seq):
    bias = _attn_bias()
    n = bsz * seq
    nt = seq // TQ
    n_tiles = bsz * nt
    n_sc = (TQ // WINDOW) * ATTN_HEADS
    cur = lambda s: jnp.minimum(s, n_tiles - 1)
    const = lambda s: (0, 0)
    full = lambda a: pl.BlockSpec(a.shape, const)
    return pl.pallas_call(
        functools.partial(_mixer_kernel, tiles_per_seq=nt),
        grid=(n_tiles + 1,),
        in_specs=[pl.BlockSpec(memory_space=pltpu.SMEM),
                  pl.BlockSpec((TQ, D_MODEL), lambda s: (cur(s), 0)),
                  pl.BlockSpec((1, 6, D_MODEL), lambda s: (cur(s) // nt, 0, 0)),
                  full(ln1pre), full(ln1post), full(ln2pre), full(anorm), full(hnorm), full(lb),
                  full(win), full(wout), full(wr_hi), full(wr_lo), full(br),
                  pl.BlockSpec(bias.shape, lambda s: (0, 0, 0, 0))],
        out_specs=[pl.BlockSpec(memory_space=pl.ANY),
                   pl.BlockSpec((1, 8, TQ), lambda s: (jnp.maximum(s - 1, 0), 0, 0)),
                   pl.BlockSpec((LANES, LANES), const)],
        out_shape=[jax.ShapeDtypeStruct((n, 1, ROW_W), U32),
                   jax.ShapeDtypeStruct((n // TQ, 8, TQ), I32),
                   jax.ShapeDtypeStruct((LANES, LANES), F32)],
        scratch_shapes=[pltpu.VMEM((WINDOW, KV_W), F32),
                        pltpu.VMEM((WINDOW, KV_W), F32),
                        pltpu.VMEM((HG_DIM, HG_W), F32),
                        pltpu.VMEM((LANES, 1), F32),
                        pltpu.VMEM((n_sc, WINDOW, 2 * WINDOW), F32),
                        pltpu.VMEM((n_sc, WINDOW, 1), F32),
                        pltpu.VMEM((n_sc, WINDOW, 2 * WINDOW), BF16),
                        pltpu.VMEM((TQ // HG_CHUNK, HG_DIM, HG_W), F32),
                        pltpu.VMEM((TQ // HG_CHUNK, HG_DIM, HG_W), BF16),
                        pltpu.VMEM((TQ, D_MODEL), F32),
                        pltpu.VMEM((TQ, IN_W), F32),
                        pltpu.VMEM((2, TQ, ROW_W), U32),
                        pltpu.SemaphoreType.DMA((2,))],
        compiler_params=pltpu.CompilerParams(dimension_semantics=("arbitrary",),
                                             vmem_limit_bytes=VMEM_LIMIT),
        name="mixer",
    )(sinks, x2, mod3, ln1pre, ln1post, ln2pre, anorm, hnorm, lb, win, wout, wr_hi, wr_lo, br, bias)


PERM_STEPS = 8
PERM_ROWS = 8
PERM_UNROLL = 16


def _perm_kernel(rs_ref, cnt_ref, bucket_ref, rank_ref, perm_ref, pos_vmem, pos_smem, sem):
    pid = pl.program_id(0)
    rows, cols = pos_vmem.shape

    b = bucket_ref[0]
    start = jnp.zeros_like(b)
    for k in range(N_BUCKETS):
        start = jnp.where(b == k, rs_ref[k], start)
    pos_vmem[...] = start + rank_ref[0]
    copies = [pltpu.make_async_copy(pos_vmem.at[r], pos_smem.at[pl.ds(r * cols, cols)], sem) for r in range(rows)]
    for cp in copies:
        cp.start()

    @pl.when(pid == 0)
    def _():
        def per_bucket(k, carry):
            first = rs_ref[k]
            cnt = cnt_ref[k]

            def pad(r, c2):
                perm_ref[first + r] = 0
                return c2

            lax.fori_loop(cnt, ((cnt + TM - 1) // TM) * TM, pad, 0)
            return carry

        lax.fori_loop(0, N_BUCKETS, per_bucket, 0)

        def tail(blk, carry):
            for u in range(PERM_UNROLL):
                perm_ref[blk * PERM_UNROLL + u] = 0
            return carry

        lax.fori_loop(rs_ref[N_BUCKETS] // PERM_UNROLL, perm_ref.shape[0] // PERM_UNROLL, tail, 0)

    for cp in copies:
        cp.wait()
    base = pid * (rows * cols)

    def body(j, carry):
        i0 = j * PERM_UNROLL
        positions = [pos_smem[i0 + u] for u in range(PERM_UNROLL)]
        for u in range(PERM_UNROLL):
            perm_ref[positions[u]] = base + i0 + u
        return carry

    lax.fori_loop(0, rows * cols // PERM_UNROLL, body, 0)


def _perm(row_start, counts, bucket, rank, n_rows):
    n = bucket.shape[0]
    cols = n // (PERM_STEPS * PERM_ROWS)
    assert n % (PERM_STEPS * PERM_ROWS * PERM_UNROLL) == 0
    chunked = lambda a: a.reshape(PERM_STEPS, PERM_ROWS, cols)
    chunk_spec = pl.BlockSpec((1, PERM_ROWS, cols), lambda i: (i, 0, 0))
    return pl.pallas_call(
        _perm_kernel,
        grid=(PERM_STEPS,),
        in_specs=[pl.BlockSpec(memory_space=pltpu.SMEM),
                  pl.BlockSpec(memory_space=pltpu.SMEM),
                  chunk_spec, chunk_spec],
        out_specs=pl.BlockSpec(memory_space=pltpu.SMEM),
        out_shape=jax.ShapeDtypeStruct((n_rows,), I32),
        scratch_shapes=[pltpu.VMEM((PERM_ROWS, cols), I32), pltpu.SMEM((PERM_ROWS * cols,), I32),
                        pltpu.SemaphoreType.DMA(())],
        compiler_params=pltpu.CompilerParams(dimension_semantics=("arbitrary",)),
        name="perm",
    )(row_start, counts, chunked(bucket), chunked(rank))


GATHER_DEPTH = 3
OUT_SLOTS = 3
DMA_QUEUES = 2


SCHED_EXPERT, SCHED_RUN_START, SCHED_SLOT, SCHED_NEXT_EXPERT = range(4)


def _moe_kernel(nt_ref, nv_ref, sched_ref, perm_ref,
                rows_hbm, gain_ref, wg_hbm, wu_hbm, wd_hbm,
                out_hbm, xbuf, x2d, obuf, wgu_buf, wd_buf, gsem, ssem, wsem):
    i = pl.program_id(0)
    nt = nt_ref[0]
    n_steps = pl.num_programs(0)
    last_tile = n_steps - 1
    sched = lambda field, side: sched_ref[(field * 2 + side) * n_steps + i]

    def weight_copies(side, expert, slot):
        return (pltpu.make_async_copy(wg_hbm.at[expert], wgu_buf.at[side, slot, :, 0:FF], wsem.at[side, slot]),
                pltpu.make_async_copy(wu_hbm.at[expert], wgu_buf.at[side, slot, :, FF:2 * FF], wsem.at[side, slot]),
                pltpu.make_async_copy(wd_hbm.at[expert], wd_buf.at[side, slot], wsem.at[side, slot]))
    nbuf = xbuf.shape[0] // TM

    def start_gather(tile, pred):
        sl = lax.rem(tile, nbuf)
        base = jnp.minimum(tile, last_tile) * TM
        for r in range(TM):
            @pl.when(pred)
            def _():
                tok = perm_ref[base + r]
                pltpu.make_async_copy(rows_hbm.at[tok], xbuf.at[sl * TM + r],
                                      gsem.at[sl]).start(priority=r % DMA_QUEUES)

    def wait_gather(tile):
        sl = lax.rem(tile, nbuf)
        pltpu.make_async_copy(rows_hbm.at[pl.ds(0, TM)], xbuf.at[pl.ds(sl * TM, TM)], gsem.at[sl]).wait()

    def wait_scatter(sl, nv):
        @pl.when(nv == TM)
        def _():
            pltpu.make_async_copy(obuf.at[sl], out_hbm.at[pl.ds(0, TM)], ssem.at[sl]).wait()

        @pl.when(nv < TM)
        def _():
            def one(r, carry):
                pltpu.make_async_copy(obuf.at[sl, pl.ds(0, 1)], out_hbm.at[pl.ds(0, 1)], ssem.at[sl]).wait()
                return carry

            lax.fori_loop(0, nv, one, 0)

    def compute(xb, slot_a, slot_b):
        x1 = pltpu.bitcast(xb[:, 0:D_MODEL], F32)
        hp = xb[:, OFF_H2P:OFF_EXT]
        h2a = pltpu.bitcast(hp << 16, F32).astype(BF16)
        h2b = pltpu.bitcast(hp & jnp.uint32(0xFFFF0000), F32).astype(BF16)
        ext = pltpu.bitcast(xb[:, OFF_EXT:ROW_W], F32)
        w_lo, w_hi = ext[:, 0:1], ext[:, 1:2]
        ga2 = jnp.zeros((TM, D_MODEL), F32)
        for k in range(gain_ref.shape[0]):
            ga2 = jnp.where(ext[:, EXT_BATCH0 + k:EXT_BATCH0 + k + 1] > 0.5, gain_ref[k:k + 1, :], ga2)

        def hidden(side, slot, w):
            gu = (_dot(h2a, wgu_buf[side, slot, 0:H2P_W])
                  + _dot(h2b, wgu_buf[side, slot, H2P_W:D_MODEL]))
            hg, hu = gu[:, 0:FF], gu[:, FF:2 * FF]
            return (w * (_silu(hg) * hu)).astype(BF16)

        act = jnp.concatenate([hidden(0, slot_a, w_lo), hidden(1, slot_b, w_hi)], axis=1)
        wd = jnp.concatenate([wd_buf[0, slot_a], wd_buf[1, slot_b]], axis=0)
        y = _dot(act, wd)
        return x1 + ga2 * _rms(y)

    @pl.when(i == 0)
    def _():
        for side in range(2):
            for cp in weight_copies(side, sched(SCHED_EXPERT, side), 0):
                cp.start()
        for d in range(GATHER_DEPTH):
            start_gather(d, d < nt)

    @pl.when(i < nt)
    def _():
        nv = nv_ref[i]
        osl = lax.rem(i, OUT_SLOTS)
        slots = []
        for side in range(2):
            slot = sched(SCHED_SLOT, side)
            slots.append(slot)

            @pl.when(sched(SCHED_RUN_START, side) == 1)
            def _():
                for cp in weight_copies(side, sched(SCHED_EXPERT, side), slot):
                    cp.wait()
                nxt = sched(SCHED_NEXT_EXPERT, side)

                @pl.when(nxt >= 0)
                def _():
                    for cp in weight_copies(side, nxt, 1 - slot):
                        cp.start()

        wait_gather(i)

        @pl.when(i >= OUT_SLOTS)
        def _():
            wait_scatter(osl, nv_ref[jnp.maximum(i - OUT_SLOTS, 0)])

        start_gather(i + GATHER_DEPTH, i + GATHER_DEPTH < nt)
        x2d[...] = xbuf[pl.ds(lax.rem(i, nbuf) * TM, TM)].reshape(TM, ROW_W)
        result = compute(x2d[...], *slots)

        def scatter_row(k, r):
            tok = perm_ref[i * TM + r]
            pltpu.make_async_copy(obuf.at[k, pl.ds(r, 1)], out_hbm.at[pl.ds(tok, 1)],
                                  ssem.at[k]).start(priority=r % DMA_QUEUES)

        for k in range(OUT_SLOTS):
            @pl.when((osl == k) & (nv == TM))
            def _():
                obuf[k] = result
                for r in range(TM):
                    scatter_row(k, r)

            @pl.when((osl == k) & (nv < TM))
            def _():
                obuf[k] = result
                for r in range(TM):
                    pl.when(r < nv)(functools.partial(scatter_row, k, r))

        @pl.when(i == nt - 1)
        def _():
            wait_scatter(osl, nv)
            for back in range(1, OUT_SLOTS):
                @pl.when(i >= back)
                def _():
                    wait_scatter(lax.rem(i - back + OUT_SLOTS, OUT_SLOTS), nv_ref[jnp.maximum(i - back, 0)])


def _moe(rows, gain, wg, wu, wd, nt, nv, sched, perm, n_tiles):
    n = rows.shape[0]
    const2 = lambda i, *_: (0, 0)
    grid_spec = pltpu.PrefetchScalarGridSpec(
        num_scalar_prefetch=4,
        grid=(n_tiles,),
        in_specs=[pl.BlockSpec(memory_space=pl.ANY),
                  pl.BlockSpec(gain.shape, const2),
                  pl.BlockSpec(memory_space=pl.ANY),
                  pl.BlockSpec(memory_space=pl.ANY),
                  pl.BlockSpec(memory_space=pl.ANY)],
        out_specs=pl.BlockSpec(memory_space=pl.ANY),
        scratch_shapes=[pltpu.VMEM(((GATHER_DEPTH + 1) * TM, 1, ROW_W), U32),
                        pltpu.VMEM((TM, ROW_W), U32),
                        pltpu.VMEM((OUT_SLOTS, TM, D_MODEL), F32),
                        pltpu.VMEM((2, 2, D_MODEL, 2 * FF), BF16),
                        pltpu.VMEM((2, 2, FF, D_MODEL), BF16),
                        pltpu.SemaphoreType.DMA((GATHER_DEPTH + 1,)),
                        pltpu.SemaphoreType.DMA((OUT_SLOTS,)),
                        pltpu.SemaphoreType.DMA((2, 2))],
    )
    return pl.pallas_call(
        _moe_kernel,
        grid_spec=grid_spec,
        out_shape=jax.ShapeDtypeStruct((n, D_MODEL), F32),
        compiler_params=pltpu.CompilerParams(dimension_semantics=("arbitrary",),
                                             vmem_limit_bytes=VMEM_LIMIT),
        name="moe",
    )(nt, nv, sched, perm, rows, gain, wg, wu, wd)


def _weight_schedule(ea, eb):
    n_tiles = ea.shape[0]
    tid = jnp.arange(n_tiles, dtype=I32)
    fields = [[], [], [], []]
    for e in (ea, eb):
        start = jnp.concatenate([jnp.ones((1,), I32), (e[1:] != e[:-1]).astype(I32)])
        slot = (jnp.cumsum(start) - 1) % 2
        start_idx = jnp.where(start == 1, tid, n_tiles)
        next_start = jnp.concatenate([lax.cummin(start_idx, reverse=True)[1:], jnp.full((1,), n_tiles, I32)])
        nxt = jnp.where(next_start < n_tiles, e[jnp.minimum(next_start, n_tiles - 1)], -1)
        for f, v in zip(fields, (e, start, slot, nxt)):
            f.append(v.astype(I32))
    return jnp.concatenate([v for f in fields for v in f])


def kernel(x, c, ln1_pre, ln1_post, ln2_pre, ln2_post, w_ada, b_ada, w_in, attn_sinks, attn_out_norm,
           hgrn_lb, hgrn_out_norm, w_out, w_router_group, b_router_group, w_router_expert,
           b_router_expert, w_exp_gate, w_exp_up, w_exp_down):
    bsz, seq, d = x.shape
    assert d == D_MODEL and seq % TQ == 0 and w_ada.shape[0] == 1 and hgrn_lb.shape[0] == 2
    assert bsz <= LANES - EXT_BATCH0 and (bsz * seq) % TM == 0
    n = bsz * seq

    mod = _ada(c, w_ada[0], b_ada[0])
    mod3 = mod.reshape(bsz, 6, d)

    wr = jnp.concatenate([w_router_group[0], jnp.zeros((d, GROUP_ROWS - N_GROUPS), F32), w_router_expert[0],
                          jnp.zeros((d, LANES - ROUTER_ROWS), F32)], axis=1)
    br = jnp.concatenate([b_router_group[0], jnp.full((GROUP_ROWS - N_GROUPS,), NEG, F32), b_router_expert[0]])
    wr_hi = wr.astype(BF16)
    wr_lo = (wr - wr_hi.astype(F32)).astype(BF16)

    x1ext, info, cnt = _mixer(
        x.reshape(n, d), mod3, attn_sinks[0], ln1_pre, ln1_post, ln2_pre, attn_out_norm, hgrn_out_norm,
        hgrn_lb, w_in[0].astype(BF16), w_out[0].astype(BF16), wr_hi, wr_lo, br.reshape(ROUTER_ROWS, 1),
        bsz, seq)

    n_tiles = n // TM + N_BUCKETS
    counts = cnt[:N_BUCKETS, 0].astype(I32)
    tiles_per = (counts + TM - 1) // TM
    tile_end = jnp.cumsum(tiles_per)
    tile_start = tile_end - tiles_per
    nt = tile_end[-1]
    bucket = info[:, 0, :].reshape(n)
    rank = info[:, 1, :].reshape(n)
    tid = jnp.arange(n_tiles, dtype=I32)[None, :]
    member = (tid >= tile_start[:, None]) & (tid < tile_end[:, None])
    pick = lambda per_bucket: jnp.sum(jnp.where(member, per_bucket, 0), axis=0).astype(I32)
    bidx = np.arange(N_BUCKETS, dtype=np.int32)
    ea_of = jnp.asarray((bidx // N_PAIRS) * EPG + _PAIR_A[bidx % N_PAIRS])[:, None]
    eb_of = jnp.asarray((bidx // N_PAIRS) * EPG + _PAIR_B[bidx % N_PAIRS])[:, None]
    last_used = jnp.arange(N_BUCKETS)[:, None] == jnp.max(jnp.where(tiles_per > 0, jnp.arange(N_BUCKETS), 0))
    unused = tid[0] >= nt
    nv = pick(jnp.clip(counts[:, None] - (tid - tile_start[:, None]) * TM, 0, TM))
    ea = jnp.where(unused, jnp.sum(jnp.where(last_used, ea_of, 0)), pick(ea_of)).astype(I32)
    eb = jnp.where(unused, jnp.sum(jnp.where(last_used, eb_of, 0)), pick(eb_of)).astype(I32)

    pad128 = lambda a: jnp.concatenate([a, jnp.zeros((LANES - a.shape[0],), I32)])
    row_start = jnp.concatenate([tile_start, nt.reshape(1)]) * TM
    perm = _perm(pad128(row_start), pad128(counts), bucket, rank, n_tiles * TM)

    wg, wu, wd = w_exp_gate[0].astype(BF16), w_exp_up[0].astype(BF16), w_exp_down[0].astype(BF16)
    gain = mod3[:, 5, :] * ln2_post
    out = _moe(x1ext, gain, wg, wu, wd, nt.reshape(1), nv, _weight_schedule(ea, eb), perm, n_tiles)
    return out.reshape(bsz, seq, d)
```

```python
import functools

import numpy as np
import jax
import jax.numpy as jnp
from jax import lax
from jax.experimental import pallas as pl
from jax.experimental.pallas import tpu as pltpu

F32 = jnp.float32
BF16 = jnp.bfloat16
I32 = jnp.int32

D_MODEL = 1024
ATTN_HEADS = 8
HEAD_DIM = 64
WINDOW = 128
ATTN_W = 512
KV_W = 128
HG_HEADS = 4
HG_DIM = 128
HG_W = 512
HG_CHUNK = 32
IN_W = 2816
N_GROUPS = 4
EPG = 8
N_EXPERTS = 32
FF = 256
N_PAIRS = EPG * (EPG - 1) // 2
N_BUCKETS = N_GROUPS * N_PAIRS
EPS = 1e-6
NEG = -1e30
LOG2E = 1.4426950408889634

LANES = 128
H2P_W = D_MODEL // 2
EXT_W = 16
OFF_H2P = D_MODEL
OFF_EXT = D_MODEL + H2P_W
ROW_W = OFF_EXT + EXT_W
EXT_BATCH0 = 8
U32 = jnp.uint32
SUBLANES = 8
GROUP_ROWS = SUBLANES
ROUTER_ROWS = GROUP_ROWS + N_EXPERTS

TQ = 256
TM = 128
VMEM_LIMIT = 56 * 1024 * 1024

OFF_Q, OFF_K, OFF_V, OFF_HQ, OFF_HF, OFF_HI, OFF_HG = 0, 512, 640, 768, 1280, 1792, 2304

_PAIR_A = np.array([a for a in range(EPG) for b in range(a + 1, EPG)], np.int32)
_PAIR_B = np.array([b for a in range(EPG) for b in range(a + 1, EPG)], np.int32)


def _dot(a, b):
    return jnp.dot(a, b, preferred_element_type=F32)


def _dot_nt(a, b):
    return lax.dot_general(a, b, (((1,), (1,)), ((), ())), preferred_element_type=F32)


def _dot_tn(a, b):
    return lax.dot_general(a, b, (((0,), (0,)), ((), ())), preferred_element_type=F32)


def _split(a):
    hi = a.astype(BF16)
    lo = (a - hi.astype(F32)).astype(BF16)
    return hi, lo


def _rms(x):
    return x * lax.rsqrt(jnp.mean(x * x, axis=-1, keepdims=True) + EPS)


def _silu(x):
    half = 0.5 * x
    return half * jnp.tanh(half) + half


def _ada_kernel(c_ref, w_ref, b_ref, o_ref):
    c = c_ref[...]
    ca = _silu(c)
    c_hi, c_lo = _split(ca)
    w_hi, w_lo = _split(w_ref[...])
    o_ref[...] = _dot(c_hi, w_hi) + _dot(c_lo, w_hi) + _dot(c_hi, w_lo) + b_ref[...]


def _ada(c, w, b):
    bsz, d = c.shape
    n_out = w.shape[1]
    return pl.pallas_call(
        _ada_kernel,
        grid=(n_out // d,),
        in_specs=[pl.BlockSpec((bsz, d), lambda j: (0, 0)),
                  pl.BlockSpec((d, d), lambda j: (0, j)),
                  pl.BlockSpec((1, d), lambda j: (0, j))],
        out_specs=pl.BlockSpec((bsz, d), lambda j: (0, j)),
        out_shape=jax.ShapeDtypeStruct((bsz, n_out), F32),
        compiler_params=pltpu.CompilerParams(dimension_semantics=("arbitrary",),
                                             vmem_limit_bytes=VMEM_LIMIT),
        name="adaln",
    )(c, w, b.reshape(1, n_out))


def _attention(proj, kprev_ref, vprev_ref, sinks_ref, bias_ref, t, s_scr, m_scr, p_scr):
    tq = proj.shape[0]
    q = (proj[:, OFF_Q:OFF_Q + ATTN_W] * (HEAD_DIM ** -0.5 * LOG2E)).astype(BF16)
    kf = jnp.concatenate([kprev_ref[...], proj[:, OFF_K:OFF_K + KV_W]], axis=0)
    vf = jnp.concatenate([vprev_ref[...], proj[:, OFF_V:OFF_V + KV_W]], axis=0)
    kprev_ref[...] = proj[tq - WINDOW:, OFF_K:OFF_K + KV_W]
    vprev_ref[...] = proj[tq - WINDOW:, OFF_V:OFF_V + KV_W]

    lo = lax.broadcasted_iota(I32, kf.shape, 1) < HEAD_DIM
    kr = pltpu.roll(kf, HEAD_DIM, axis=1)
    vr = pltpu.roll(vf, HEAD_DIM, axis=1)

    def variants(a, ar):
        return [[jnp.where(lo, a, 0.0).astype(BF16), jnp.where(lo, 0.0, ar).astype(BF16)],
                [jnp.where(lo, ar, 0.0).astype(BF16), jnp.where(lo, 0.0, a).astype(BF16)]]

    kvar = variants(kf, kr)
    vvar = variants(vf, vr)

    first = jnp.where(t > 0, 0, 1)

    nblk = tq // WINDOW
    idx = lambda j, h: j * ATTN_HEADS + h
    keys = lambda a, j: a[j * WINDOW:(j + 2) * WINDOW]

    def scores():
        for j in range(nblk):
            for h in range(ATTN_HEADS):
                p, par = h // 2, h % 2
                qp = q[j * WINDOW:(j + 1) * WINDOW, p * LANES:(p + 1) * LANES]
                s = _dot_nt(qp, keys(kvar[p // 2][par], j)) + bias_ref[first if j == 0 else 0, h]
                s_scr[idx(j, h)] = s
                m_scr[idx(j, h)] = jnp.maximum(jnp.max(s, axis=-1, keepdims=True), sinks_ref[h] * LOG2E)

    def exps():
        for j in range(nblk):
            for h in range(ATTN_HEADS):
                m = m_scr[idx(j, h)]
                pe = jnp.exp2(s_scr[idx(j, h)] - m)
                p_scr[idx(j, h)] = pe.astype(BF16)
                m_scr[idx(j, h)] = 1.0 / (jnp.sum(pe, axis=-1, keepdims=True) + jnp.exp2(sinks_ref[h] * LOG2E - m))

    def values():
        blocks = []
        for j in range(nblk):
            pairs = []
            for p in range(ATTN_HEADS // 2):
                acc = None
                for par in range(2):
                    h = 2 * p + par
                    o = _dot(p_scr[idx(j, h)], keys(vvar[p // 2][par], j)) * m_scr[idx(j, h)]
                    acc = o if acc is None else acc + o
                pairs.append(acc)
            blocks.append(jnp.concatenate(pairs, axis=1))
        return jnp.concatenate(blocks, axis=0)

    return scores, exps, values


def _hgrn2(proj, lb, st_ref, hnorm, u_scr, stb_scr):
    tq = proj.shape[0]
    nc = tq // HG_CHUNK
    qr = proj[:, OFF_HQ:OFF_HQ + HG_W]
    fr = proj[:, OFF_HF:OFF_HF + HG_W]
    iv = proj[:, OFF_HI:OFF_HI + HG_W]
    gr = proj[:, OFF_HG:OFF_HG + HG_W]
    qh = _silu(qr)
    half_gap = 0.5 * (1.0 - lb)
    f = (lb + half_gap) + half_gap * jnp.tanh(0.5 * fr)
    kk = 1.0 - f
    logf = jnp.log(f)

    rmod = lax.broadcasted_iota(I32, (tq, HG_W), 0) & (HG_CHUNK - 1)
    bc = logf
    s = 1
    while s < HG_CHUNK:
        bc = bc + jnp.where(rmod >= s, pltpu.roll(bc, s, axis=0), 0.0)
        s *= 2

    b3 = bc.reshape(nc, HG_CHUNK, HG_W)
    blast = b3[:, HG_CHUNK - 1:HG_CHUNK, :]
    kend = (kk.reshape(nc, HG_CHUNK, HG_W) * jnp.exp(blast - b3)).reshape(tq, HG_W)
    decay = jnp.exp(blast).reshape(nc, HG_W)
    qdec = (qh * jnp.exp(bc)).astype(BF16)
    kdec = (kk * jnp.exp(-bc)).astype(BF16)
    kend = kend.astype(BF16)
    ivb = iv.astype(BF16)

    ri = lax.broadcasted_iota(I32, (tq, tq), 0)
    ci = lax.broadcasted_iota(I32, (tq, tq), 1)
    cmask = ((ri // HG_CHUNK) == (ci // HG_CHUNK)) & (ri >= ci)

    heads = [slice(hh * HG_DIM, (hh + 1) * HG_DIM) for hh in range(HG_HEADS)]
    chunks = [slice(n * HG_CHUNK, (n + 1) * HG_CHUNK) for n in range(nc)]

    lane_head = lax.broadcasted_iota(I32, (HG_CHUNK, HG_W), 1) // HG_DIM
    for n, rs in enumerate(chunks):
        vstack = jnp.concatenate([ivb[rs, sl] for sl in heads], axis=0)
        kblk = jnp.concatenate([jnp.where(lane_head == hh, kend[rs], 0.0).astype(BF16)
                                for hh in range(HG_HEADS)], axis=0)
        u_scr[n] = _dot_tn(vstack, kblk)

    st = st_ref[...]
    for n in range(nc):
        stb_scr[n] = st.astype(BF16)
        st = st * decay[n:n + 1] + u_scr[n]
    st_ref[...] = st

    outs = []
    for hh, sl in enumerate(heads):
        a = _dot_nt(qdec[:, sl], kdec[:, sl])
        a = jnp.where(cmask, a, 0.0).astype(BF16)
        o_intra = _dot(a, ivb[:, sl])
        inter = [_dot_nt(qdec[rs, sl], stb_scr[n, :, sl]) for n, rs in enumerate(chunks)]
        o = o_intra + jnp.concatenate(inter, axis=0)
        o = _rms(o) * hnorm[:, sl]
        g = gr[:, sl]
        outs.append(o * _silu(g))
    return jnp.concatenate(outs, axis=1)


def _route_topk(h2, wr_hi, wr_lo, br):
    tq = h2.shape[0]
    h_hi, h_lo = _split(h2)
    logits = _dot(h_hi, wr_hi) + _dot(h_lo, wr_hi) + _dot(h_hi, wr_lo)
    lt = logits.T[0:ROUTER_ROWS] + br
    sub = lax.broadcasted_iota(I32, (SUBLANES, tq), 0).astype(F32)
    none = float(SUBLANES)

    gl = lt[0:GROUP_ROWS]
    gm = jnp.max(gl, axis=0, keepdims=True)
    gidx = jnp.min(jnp.where(gl == gm, sub, none), axis=0, keepdims=True)
    g_w = 1.0 / jnp.sum(jnp.exp(gl - gm), axis=0, keepdims=True)

    group_rows = lambda g: lt[GROUP_ROWS + EPG * g:GROUP_ROWS + EPG * (g + 1)]
    es = group_rows(0)
    for g in range(1, N_GROUPS):
        es = jnp.where(gidx == float(g), group_rows(g), es)
    m1 = jnp.max(es, axis=0, keepdims=True)
    i1 = jnp.min(jnp.where(es == m1, sub, none), axis=0, keepdims=True)
    e2 = jnp.where(sub == i1, NEG, es)
    m2 = jnp.max(e2, axis=0, keepdims=True)
    i2 = jnp.min(jnp.where(e2 == m2, sub, none), axis=0, keepdims=True)
    dd = jnp.exp(m2 - m1)
    w1 = g_w / (1.0 + dd)
    w2 = g_w * dd / (1.0 + dd)
    first_low = i1 < i2
    ea = jnp.minimum(i1, i2)
    eb = jnp.maximum(i1, i2)
    w_lo = jnp.where(first_low, w1, w2)
    w_hi = jnp.where(first_low, w2, w1)
    pair = ea * (2.0 * EPG - 1.0 - ea) * 0.5 + (eb - ea - 1.0)
    bucket = gidx * float(N_PAIRS) + pair
    return bucket, w_lo, w_hi


def _route_rank(bucket, w_lo, w_hi, carry_ref, bidx, live):
    tq = bucket.shape[1]
    brow = lax.broadcasted_iota(I32, (LANES, tq), 0).astype(F32)
    onehot = brow == bucket
    oh = jnp.where(onehot, live, 0.0)
    ti = lax.broadcasted_iota(I32, (tq, tq), 0)
    tj = lax.broadcasted_iota(I32, (tq, tq), 1)
    upper = jnp.where(ti < tj, 1.0, 0.0).astype(BF16)
    before = _dot(oh.astype(BF16), upper) + carry_ref[...]
    rank = jnp.sum(jnp.where(onehot, before, 0.0), axis=0, keepdims=True)
    carry_ref[...] = carry_ref[...] + jnp.sum(oh, axis=1, keepdims=True)

    lane_row = lax.broadcasted_iota(I32, (LANES - EXT_BATCH0, tq), 0)
    onehot_b = jnp.where(lane_row == bidx, 1.0, 0.0)
    info = jnp.concatenate([w_lo, w_hi, jnp.zeros((EXT_BATCH0 - 2, tq), F32), onehot_b], axis=0)
    return rank, info.T[:, 0:EXT_W]


def _mixer_kernel(sinks_ref, x_ref, mod_ref, ln1pre_ref, ln1post_ref, ln2pre_ref, anorm_ref, hnorm_ref,
                  lb_ref, win_ref, wout_ref, wrhi_ref, wrlo_ref, br_ref, bias_ref,
                  rows_hbm, info_ref, cnt_ref,
                  kprev_ref, vprev_ref, st_ref, carry_ref, s_scr, m_scr, p_scr, u_scr, stb_scr,
                  h2_scr, proj_scr, rowbuf, rsem, *, tiles_per_seq):
    s = pl.program_id(0)
    n_tiles = pl.num_programs(0) - 1
    tq = x_ref.shape[0]
    t = lax.rem(jnp.minimum(s, n_tiles - 1), tiles_per_seq)
    slot = lax.rem(s, 2)
    bits = lambda a: pltpu.bitcast(a, U32)

    def row_copy(buf_slot, tile):
        return pltpu.make_async_copy(rowbuf.at[buf_slot], rows_hbm.at[pl.ds(pl.multiple_of(tile * tq, tq), tq), 0],
                                     rsem.at[buf_slot])

    @pl.when(s == 0)
    def _():
        carry_ref[...] = jnp.zeros_like(carry_ref)
        h2_scr[...] = jnp.zeros_like(h2_scr)

    @pl.when(t == 0)
    def _():
        st_ref[...] = jnp.zeros_like(st_ref)
        kprev_ref[...] = jnp.zeros_like(kprev_ref)
        vprev_ref[...] = jnp.zeros_like(vprev_ref)

    x = x_ref[...]
    mod = mod_ref[0]
    sh1, sc1, ga1, sh2, sc2 = mod[0:1], mod[1:2], mod[2:3], mod[3:4], mod[4:5]
    prev = jnp.maximum(s - 1, 0)

    @pl.when(s <= n_tiles)
    def _():
        bucket, w_lo, w_hi = _route_topk(h2_scr[...], wrhi_ref[...], wrlo_ref[...], br_ref[...])
        h = _rms(x) * (ln1pre_ref[...] * (1.0 + sc1)) + sh1
        proj_scr[...] = _dot(h.astype(BF16), win_ref[...])
        live = jnp.where(s >= 1, 1.0, 0.0)
        rank, ext = _route_rank(bucket, w_lo, w_hi, carry_ref, prev // tiles_per_seq, live)
        rowbuf[1 - slot, :, OFF_EXT:ROW_W] = bits(ext)
        info_ref[0] = jnp.concatenate([bucket, rank, jnp.zeros((6, tq), F32)], axis=0).astype(I32)
        cnt_ref[...] = jnp.broadcast_to(carry_ref[...], cnt_ref.shape)

    proj = proj_scr[...]
    scores, exps, values = _attention(proj, kprev_ref, vprev_ref, sinks_ref, bias_ref, t, s_scr, m_scr, p_scr)
    scores()
    exps()
    attn = _rms(values()) * anorm_ref[...]

    lbr = lb_ref[...]
    le = jnp.exp(lbr - jnp.max(lbr, axis=0, keepdims=True))
    lb = le[0:1] / jnp.sum(le, axis=0, keepdims=True)
    hg = _hgrn2(proj, lb, st_ref, hnorm_ref[...], u_scr, stb_scr)

    mix = _dot(jnp.concatenate([attn, hg], axis=1).astype(BF16), wout_ref[...])
    x1 = x + _rms(mix) * (ga1 * ln1post_ref[...])

    h2 = _rms(x1) * (ln2pre_ref[...] * (1.0 + sc2)) + sh2
    h2r = h2.astype(BF16).astype(F32)

    @pl.when(s >= 2)
    def _():
        row_copy(slot, s - 2).wait()

    rowbuf[slot, :, 0:D_MODEL] = bits(x1)
    rowbuf[slot, :, OFF_H2P:OFF_EXT] = ((bits(h2r[:, 0:H2P_W]) >> 16)
                                        | (bits(h2r[:, H2P_W:D_MODEL]) & jnp.uint32(0xFFFF0000)))
    h2_scr[...] = h2

    @pl.when(s >= 1)
    def _():
        row_copy(1 - slot, prev).start()

    @pl.when(s == n_tiles)
    def _():
        row_copy(1 - slot, prev).wait()


def _attn_bias():
    qi = np.arange(WINDOW)[:, None]
    kj = np.arange(2 * WINDOW)[None, :]
    dist = qi + WINDOW - kj
    in_win = (dist >= 0) & (dist < WINDOW)
    slopes = 2.0 ** (-8.0 * (np.arange(ATTN_HEADS) + 1.0) / ATTN_HEADS)
    b = np.where(in_win[None], -slopes[:, None, None] * dist[None] * LOG2E, NEG)
    b_first = np.where((kj >= WINDOW)[None], b, NEG)
    return jnp.asarray(np.stack([b, b_first]).astype(np.float32))


def _mixer(x2, mod3, sinks, ln1pre, ln1post, ln2pre, anorm, hnorm, lb, win, wout, wr_hi, wr_lo, br,
           bsz, seq):
    bias = _attn_bias()
    n = bsz * seq
    nt = seq // TQ
    n_tiles = bsz * nt
    n_sc = (TQ // WINDOW) * ATTN_HEADS
    cur = lambda s: jnp.minimum(s, n_tiles - 1)
    const = lambda s: (0, 0)
    full = lambda a: pl.BlockSpec(a.shape, const)
    return pl.pallas_call(
        functools.partial(_mixer_kernel, tiles_per_seq=nt),
        grid=(n_tiles + 1,),
        in_specs=[pl.BlockSpec(memory_space=pltpu.SMEM),
                  pl.BlockSpec((TQ, D_MODEL), lambda s: (cur(s), 0)),
                  pl.BlockSpec((1, 6, D_MODEL), lambda s: (cur(s) // nt, 0, 0)),
                  full(ln1pre), full(ln1post), full(ln2pre), full(anorm), full(hnorm), full(lb),
                  full(win), full(wout), full(wr_hi), full(wr_lo), full(br),
                  pl.BlockSpec(bias.shape, lambda s: (0, 0, 0, 0))],
        out_specs=[pl.BlockSpec(memory_space=pl.ANY),
                   pl.BlockSpec((1, 8, TQ), lambda s: (jnp.maximum(s - 1, 0), 0, 0)),
                   pl.BlockSpec((LANES, LANES), const)],
        out_shape=[jax.ShapeDtypeStruct((n, 1, ROW_W), U32),
                   jax.ShapeDtypeStruct((n // TQ, 8, TQ), I32),
                   jax.ShapeDtypeStruct((LANES, LANES), F32)],
        scratch_shapes=[pltpu.VMEM((WINDOW, KV_W), F32),
                        pltpu.VMEM((WINDOW, KV_W), F32),
                        pltpu.VMEM((HG_DIM, HG_W), F32),
                        pltpu.VMEM((LANES, 1), F32),
                        pltpu.VMEM((n_sc, WINDOW, 2 * WINDOW), F32),
                        pltpu.VMEM((n_sc, WINDOW, 1), F32),
                        pltpu.VMEM((n_sc, WINDOW, 2 * WINDOW), BF16),
                        pltpu.VMEM((TQ // HG_CHUNK, HG_DIM, HG_W), F32),
                        pltpu.VMEM((TQ // HG_CHUNK, HG_DIM, HG_W), BF16),
                        pltpu.VMEM((TQ, D_MODEL), F32),
                        pltpu.VMEM((TQ, IN_W), F32),
                        pltpu.VMEM((2, TQ, ROW_W), U32),
                        pltpu.SemaphoreType.DMA((2,))],
        compiler_params=pltpu.CompilerParams(dimension_semantics=("arbitrary",),
                                             vmem_limit_bytes=VMEM_LIMIT),
        name="mixer",
    )(sinks, x2, mod3, ln1pre, ln1post, ln2pre, anorm, hnorm, lb, win, wout, wr_hi, wr_lo, br, bias)


PERM_STEPS = 8
PERM_ROWS = 8
PERM_UNROLL = 16


def _perm_kernel(rs_ref, cnt_ref, bucket_ref, rank_ref, perm_ref, pos_vmem, pos_smem, sem):
    pid = pl.program_id(0)
    rows, cols = pos_vmem.shape

    b = bucket_ref[0]
    start = jnp.zeros_like(b)
    for k in range(N_BUCKETS):
        start = jnp.where(b == k, rs_ref[k], start)
    pos_vmem[...] = start + rank_ref[0]
    copies = [pltpu.make_async_copy(pos_vmem.at[r], pos_smem.at[pl.ds(r * cols, cols)], sem) for r in range(rows)]
    for cp in copies:
        cp.start()

    @pl.when(pid == 0)
    def _():
        def per_bucket(k, carry):
            first = rs_ref[k]
            cnt = cnt_ref[k]

            def pad(r, c2):
                perm_ref[first + r] = 0
                return c2

            lax.fori_loop(cnt, ((cnt + TM - 1) // TM) * TM, pad, 0)
            return carry

        lax.fori_loop(0, N_BUCKETS, per_bucket, 0)

        def tail(blk, carry):
            for u in range(PERM_UNROLL):
                perm_ref[blk * PERM_UNROLL + u] = 0
            return carry

        lax.fori_loop(rs_ref[N_BUCKETS] // PERM_UNROLL, perm_ref.shape[0] // PERM_UNROLL, tail, 0)

    for cp in copies:
        cp.wait()
    base = pid * (rows * cols)

    def body(j, carry):
        i0 = j * PERM_UNROLL
        positions = [pos_smem[i0 + u] for u in range(PERM_UNROLL)]
        for u in range(PERM_UNROLL):
            perm_ref[positions[u]] = base + i0 + u
        return carry

    lax.fori_loop(0, rows * cols // PERM_UNROLL, body, 0)


def _perm(row_start, counts, bucket, rank, n_rows):
    n = bucket.shape[0]
    cols = n // (PERM_STEPS * PERM_ROWS)
    assert n % (PERM_STEPS * PERM_ROWS * PERM_UNROLL) == 0
    chunked = lambda a: a.reshape(PERM_STEPS, PERM_ROWS, cols)
    chunk_spec = pl.BlockSpec((1, PERM_ROWS, cols), lambda i: (i, 0, 0))
    return pl.pallas_call(
        _perm_kernel,
        grid=(PERM_STEPS,),
        in_specs=[pl.BlockSpec(memory_space=pltpu.SMEM),
                  pl.BlockSpec(memory_space=pltpu.SMEM),
                  chunk_spec, chunk_spec],
        out_specs=pl.BlockSpec(memory_space=pltpu.SMEM),
        out_shape=jax.ShapeDtypeStruct((n_rows,), I32),
        scratch_shapes=[pltpu.VMEM((PERM_ROWS, cols), I32), pltpu.SMEM((PERM_ROWS * cols,), I32),
                        pltpu.SemaphoreType.DMA(())],
        compiler_params=pltpu.CompilerParams(dimension_semantics=("arbitrary",)),
        name="perm",
    )(row_start, counts, chunked(bucket), chunked(rank))


GATHER_DEPTH = 3
OUT_SLOTS = 3


SCHED_EXPERT, SCHED_RUN_START, SCHED_SLOT, SCHED_NEXT_EXPERT = range(4)


def _moe_kernel(nt_ref, nv_ref, sched_ref, perm_ref,
                rows_hbm, gain_ref, wg_hbm, wu_hbm, wd_hbm,
                out_hbm, xbuf, x2d, obuf, wgu_buf, wd_buf, gsem, ssem, wsem):
    i = pl.program_id(0)
    nt = nt_ref[0]
    n_steps = pl.num_programs(0)
    last_tile = n_steps - 1
    sched = lambda field, side: sched_ref[(field * 2 + side) * n_steps + i]

    def weight_copies(side, expert, slot):
        return (pltpu.make_async_copy(wg_hbm.at[expert], wgu_buf.at[side, slot, :, 0:FF], wsem.at[side, slot]),
                pltpu.make_async_copy(wu_hbm.at[expert], wgu_buf.at[side, slot, :, FF:2 * FF], wsem.at[side, slot]),
                pltpu.make_async_copy(wd_hbm.at[expert], wd_buf.at[side, slot], wsem.at[side, slot]))
    nbuf = xbuf.shape[0] // TM

    def start_gather(tile, pred):
        sl = lax.rem(tile, nbuf)
        base = jnp.minimum(tile, last_tile) * TM
        for r in range(TM):
            @pl.when(pred)
            def _():
                tok = perm_ref[base + r]
                pltpu.make_async_copy(rows_hbm.at[tok], xbuf.at[sl * TM + r], gsem.at[sl]).start()

    def wait_gather(tile):
        sl = lax.rem(tile, nbuf)
        pltpu.make_async_copy(rows_hbm.at[pl.ds(0, TM)], xbuf.at[pl.ds(sl * TM, TM)], gsem.at[sl]).wait()

    def wait_scatter(sl, nv):
        @pl.when(nv == TM)
        def _():
            pltpu.make_async_copy(obuf.at[sl], out_hbm.at[pl.ds(0, TM)], ssem.at[sl]).wait()

        @pl.when(nv < TM)
        def _():
            def one(r, carry):
                pltpu.make_async_copy(obuf.at[sl, pl.ds(0, 1)], out_hbm.at[pl.ds(0, 1)], ssem.at[sl]).wait()
                return carry

            lax.fori_loop(0, nv, one, 0)

    def compute(xb, slot_a, slot_b):
        x1 = pltpu.bitcast(xb[:, 0:D_MODEL], F32)
        hp = xb[:, OFF_H2P:OFF_EXT]
        h2a = pltpu.bitcast(hp << 16, F32).astype(BF16)
        h2b = pltpu.bitcast(hp & jnp.uint32(0xFFFF0000), F32).astype(BF16)
        ext = pltpu.bitcast(xb[:, OFF_EXT:ROW_W], F32)
        w_lo, w_hi = ext[:, 0:1], ext[:, 1:2]
        ga2 = jnp.zeros((TM, D_MODEL), F32)
        for k in range(gain_ref.shape[0]):
            ga2 = jnp.where(ext[:, EXT_BATCH0 + k:EXT_BATCH0 + k + 1] > 0.5, gain_ref[k:k + 1, :], ga2)

        def hidden(side, slot, w):
            gu = (_dot(h2a, wgu_buf[side, slot, 0:H2P_W])
                  + _dot(h2b, wgu_buf[side, slot, H2P_W:D_MODEL]))
            hg, hu = gu[:, 0:FF], gu[:, FF:2 * FF]
            return (w * (_silu(hg) * hu)).astype(BF16)

        act = jnp.concatenate([hidden(0, slot_a, w_lo), hidden(1, slot_b, w_hi)], axis=1)
        wd = jnp.concatenate([wd_buf[0, slot_a], wd_buf[1, slot_b]], axis=0)
        y = _dot(act, wd)
        return x1 + ga2 * _rms(y)

    @pl.when(i == 0)
    def _():
        for side in range(2):
            for cp in weight_copies(side, sched(SCHED_EXPERT, side), 0):
                cp.start()
        for d in range(GATHER_DEPTH):
            start_gather(d, d < nt)

    @pl.when(i < nt)
    def _():
        nv = nv_ref[i]
        osl = lax.rem(i, OUT_SLOTS)
        slots = []
        for side in range(2):
            slot = sched(SCHED_SLOT, side)
            slots.append(slot)

            @pl.when(sched(SCHED_RUN_START, side) == 1)
            def _():
                for cp in weight_copies(side, sched(SCHED_EXPERT, side), slot):
                    cp.wait()
                nxt = sched(SCHED_NEXT_EXPERT, side)

                @pl.when(nxt >= 0)
                def _():
                    for cp in weight_copies(side, nxt, 1 - slot):
                        cp.start()

        wait_gather(i)

        @pl.when(i >= OUT_SLOTS)
        def _():
            wait_scatter(osl, nv_ref[jnp.maximum(i - OUT_SLOTS, 0)])

        start_gather(i + GATHER_DEPTH, i + GATHER_DEPTH < nt)
        x2d[...] = xbuf[pl.ds(lax.rem(i, nbuf) * TM, TM)].reshape(TM, ROW_W)
        result = compute(x2d[...], *slots)

        def scatter_row(k, r):
            tok = perm_ref[i * TM + r]
            pltpu.make_async_copy(obuf.at[k, pl.ds(r, 1)], out_hbm.at[pl.ds(tok, 1)], ssem.at[k]).start()

        for k in range(OUT_SLOTS):
            @pl.when((osl == k) & (nv == TM))
            def _():
                obuf[k] = result
                for r in range(TM):
                    scatter_row(k, r)

            @pl.when((osl == k) & (nv < TM))
            def _():
                obuf[k] = result
                for r in range(TM):
                    pl.when(r < nv)(functools.partial(scatter_row, k, r))

        @pl.when(i == nt - 1)
        def _():
            wait_scatter(osl, nv)
            for back in range(1, OUT_SLOTS):
                @pl.when(i >= back)
                def _():
                    wait_scatter(lax.rem(i - back + OUT_SLOTS, OUT_SLOTS), nv_ref[jnp.maximum(i - back, 0)])


def _moe(rows, gain, wg, wu, wd, nt, nv, sched, perm, n_tiles):
    n = rows.shape[0]
    const2 = lambda i, *_: (0, 0)
    grid_spec = pltpu.PrefetchScalarGridSpec(
        num_scalar_prefetch=4,
        grid=(n_tiles,),
        in_specs=[pl.BlockSpec(memory_space=pl.ANY),
                  pl.BlockSpec(gain.shape, const2),
                  pl.BlockSpec(memory_space=pl.ANY),
                  pl.BlockSpec(memory_space=pl.ANY),
                  pl.BlockSpec(memory_space=pl.ANY)],
        out_specs=pl.BlockSpec(memory_space=pl.ANY),
        scratch_shapes=[pltpu.VMEM(((GATHER_DEPTH + 1) * TM, 1, ROW_W), U32),
                        pltpu.VMEM((TM, ROW_W), U32),
                        pltpu.VMEM((OUT_SLOTS, TM, D_MODEL), F32),
                        pltpu.VMEM((2, 2, D_MODEL, 2 * FF), BF16),
                        pltpu.VMEM((2, 2, FF, D_MODEL), BF16),
                        pltpu.SemaphoreType.DMA((GATHER_DEPTH + 1,)),
                        pltpu.SemaphoreType.DMA((OUT_SLOTS,)),
                        pltpu.SemaphoreType.DMA((2, 2))],
    )
    return pl.pallas_call(
        _moe_kernel,
        grid_spec=grid_spec,
        out_shape=jax.ShapeDtypeStruct((n, D_MODEL), F32),
        compiler_params=pltpu.CompilerParams(dimension_semantics=("arbitrary",),
                                             vmem_limit_bytes=VMEM_LIMIT),
        name="moe",
    )(nt, nv, sched, perm, rows, gain, wg, wu, wd)


def _weight_schedule(ea, eb):
    n_tiles = ea.shape[0]
    tid = jnp.arange(n_tiles, dtype=I32)
    fields = [[], [], [], []]
    for e in (ea, eb):
        start = jnp.concatenate([jnp.ones((1,), I32), (e[1:] != e[:-1]).astype(I32)])
        slot = (jnp.cumsum(start) - 1) % 2
        start_idx = jnp.where(start == 1, tid, n_tiles)
        next_start = jnp.concatenate([lax.cummin(start_idx, reverse=True)[1:], jnp.full((1,), n_tiles, I32)])
        nxt = jnp.where(next_start < n_tiles, e[jnp.minimum(next_start, n_tiles - 1)], -1)
        for f, v in zip(fields, (e, start, slot, nxt)):
            f.append(v.astype(I32))
    return jnp.concatenate([v for f in fields for v in f])


def kernel(x, c, ln1_pre, ln1_post, ln2_pre, ln2_post, w_ada, b_ada, w_in, attn_sinks, attn_out_norm,
           hgrn_lb, hgrn_out_norm, w_out, w_router_group, b_router_group, w_router_expert,
           b_router_expert, w_exp_gate, w_exp_up, w_exp_down):
    bsz, seq, d = x.shape
    assert d == D_MODEL and seq % TQ == 0 and w_ada.shape[0] == 1 and hgrn_lb.shape[0] == 2
    assert bsz <= EXT_W - EXT_BATCH0 and (bsz * seq) % TM == 0
    n = bsz * seq

    mod = _ada(c, w_ada[0], b_ada[0])
    mod3 = mod.reshape(bsz, 6, d)

    wr = jnp.concatenate([w_router_group[0], jnp.zeros((d, GROUP_ROWS - N_GROUPS), F32), w_router_expert[0],
                          jnp.zeros((d, LANES - ROUTER_ROWS), F32)], axis=1)
    br = jnp.concatenate([b_router_group[0], jnp.full((GROUP_ROWS - N_GROUPS,), NEG, F32), b_router_expert[0]])
    wr_hi = wr.astype(BF16)
    wr_lo = (wr - wr_hi.astype(F32)).astype(BF16)

    x1ext, info, cnt = _mixer(
        x.reshape(n, d), mod3, attn_sinks[0], ln1_pre, ln1_post, ln2_pre, attn_out_norm, hgrn_out_norm,
        hgrn_lb, w_in[0].astype(BF16), w_out[0].astype(BF16), wr_hi, wr_lo, br.reshape(ROUTER_ROWS, 1),
        bsz, seq)

    n_tiles = n // TM + N_BUCKETS
    counts = cnt[:N_BUCKETS, 0].astype(I32)
    tiles_per = (counts + TM - 1) // TM
    tile_end = jnp.cumsum(tiles_per)
    tile_start = tile_end - tiles_per
    nt = tile_end[-1]
    bucket = info[:, 0, :].reshape(n)
    rank = info[:, 1, :].reshape(n)
    tid = jnp.arange(n_tiles, dtype=I32)[None, :]
    member = (tid >= tile_start[:, None]) & (tid < tile_end[:, None])
    pick = lambda per_bucket: jnp.sum(jnp.where(member, per_bucket, 0), axis=0).astype(I32)
    bidx = np.arange(N_BUCKETS, dtype=np.int32)
    ea_of = jnp.asarray((bidx // N_PAIRS) * EPG + _PAIR_A[bidx % N_PAIRS])[:, None]
    eb_of = jnp.asarray((bidx // N_PAIRS) * EPG + _PAIR_B[bidx % N_PAIRS])[:, None]
    last_used = jnp.arange(N_BUCKETS)[:, None] == jnp.max(jnp.where(tiles_per > 0, jnp.arange(N_BUCKETS), 0))
    unused = tid[0] >= nt
    nv = pick(jnp.clip(counts[:, None] - (tid - tile_start[:, None]) * TM, 0, TM))
    ea = jnp.where(unused, jnp.sum(jnp.where(last_used, ea_of, 0)), pick(ea_of)).astype(I32)
    eb = jnp.where(unused, jnp.sum(jnp.where(last_used, eb_of, 0)), pick(eb_of)).astype(I32)

    pad128 = lambda a: jnp.concatenate([a, jnp.zeros((LANES - a.shape[0],), I32)])
    row_start = jnp.concatenate([tile_start, nt.reshape(1)]) * TM
    perm = _perm(pad128(row_start), pad128(counts), bucket, rank, n_tiles * TM)

    wg, wu, wd = w_exp_gate[0].astype(BF16), w_exp_up[0].astype(BF16), w_exp_down[0].astype(BF16)
    gain = mod3[:, 5, :] * ln2_post
    out = _moe(x1ext, gain, wg, wu, wd, nt.reshape(1), nv, _weight_schedule(ea, eb), perm, n_tiles)
    return out.reshape(bsz, seq, d)
```

```python
import functools

import numpy as np
import jax
import jax.numpy as jnp
from jax import lax
from jax.experimental import pallas as pl
from jax.experimental.pallas import tpu as pltpu

F32 = jnp.float32
BF16 = jnp.bfloat16
I32 = jnp.int32

D_MODEL = 1024
ATTN_HEADS = 8
HEAD_DIM = 64
WINDOW = 128
ATTN_W = 512
KV_W = 128
HG_HEADS = 4
HG_DIM = 128
HG_W = 512
HG_CHUNK = 32
IN_W = 2816
N_GROUPS = 4
EPG = 8
N_EXPERTS = 32
FF = 256
N_PAIRS = EPG * (EPG - 1) // 2
N_BUCKETS = N_GROUPS * N_PAIRS
EPS = 1e-6
NEG = -1e30
LOG2E = 1.4426950408889634

LANES = 128
H2P_W = D_MODEL // 2
EXT_W = LANES
OFF_H2P = D_MODEL
OFF_EXT = D_MODEL + H2P_W
ROW_W = OFF_EXT + EXT_W
EXT_BATCH0 = 8
U32 = jnp.uint32
SUBLANES = 8
GROUP_ROWS = SUBLANES
ROUTER_ROWS = GROUP_ROWS + N_EXPERTS

TQ = 256
TM = 128
VMEM_LIMIT = 56 * 1024 * 1024

OFF_Q, OFF_K, OFF_V, OFF_HQ, OFF_HF, OFF_HI, OFF_HG = 0, 512, 640, 768, 1280, 1792, 2304

_PAIR_A = np.array([a for a in range(EPG) for b in range(a + 1, EPG)], np.int32)
_PAIR_B = np.array([b for a in range(EPG) for b in range(a + 1, EPG)], np.int32)


def _dot(a, b):
    return jnp.dot(a, b, preferred_element_type=F32)


def _dot_nt(a, b):
    return lax.dot_general(a, b, (((1,), (1,)), ((), ())), preferred_element_type=F32)


def _dot_tn(a, b):
    return lax.dot_general(a, b, (((0,), (0,)), ((), ())), preferred_element_type=F32)


def _split(a):
    hi = a.astype(BF16)
    lo = (a - hi.astype(F32)).astype(BF16)
    return hi, lo


def _rms(x):
    return x * lax.rsqrt(jnp.mean(x * x, axis=-1, keepdims=True) + EPS)


def _silu(x):
    half = 0.5 * x
    return half * jnp.tanh(half) + half


def _ada_kernel(c_ref, w_ref, b_ref, o_ref):
    c = c_ref[...]
    ca = _silu(c)
    c_hi, c_lo = _split(ca)
    w_hi, w_lo = _split(w_ref[...])
    o_ref[...] = _dot(c_hi, w_hi) + _dot(c_lo, w_hi) + _dot(c_hi, w_lo) + b_ref[...]


def _ada(c, w, b):
    bsz, d = c.shape
    n_out = w.shape[1]
    return pl.pallas_call(
        _ada_kernel,
        grid=(n_out // d,),
        in_specs=[pl.BlockSpec((bsz, d), lambda j: (0, 0)),
                  pl.BlockSpec((d, d), lambda j: (0, j)),
                  pl.BlockSpec((1, d), lambda j: (0, j))],
        out_specs=pl.BlockSpec((bsz, d), lambda j: (0, j)),
        out_shape=jax.ShapeDtypeStruct((bsz, n_out), F32),
        compiler_params=pltpu.CompilerParams(dimension_semantics=("arbitrary",),
                                             vmem_limit_bytes=VMEM_LIMIT),
        name="adaln",
    )(c, w, b.reshape(1, n_out))


def _attention(proj, kprev_ref, vprev_ref, sinks_ref, bias_ref, t, s_scr, m_scr, p_scr):
    tq = proj.shape[0]
    q = (proj[:, OFF_Q:OFF_Q + ATTN_W] * (HEAD_DIM ** -0.5 * LOG2E)).astype(BF16)
    kf = jnp.concatenate([kprev_ref[...], proj[:, OFF_K:OFF_K + KV_W]], axis=0)
    vf = jnp.concatenate([vprev_ref[...], proj[:, OFF_V:OFF_V + KV_W]], axis=0)
    kprev_ref[...] = proj[tq - WINDOW:, OFF_K:OFF_K + KV_W]
    vprev_ref[...] = proj[tq - WINDOW:, OFF_V:OFF_V + KV_W]

    lo = lax.broadcasted_iota(I32, kf.shape, 1) < HEAD_DIM
    kr = pltpu.roll(kf, HEAD_DIM, axis=1)
    vr = pltpu.roll(vf, HEAD_DIM, axis=1)

    def variants(a, ar):
        return [[jnp.where(lo, a, 0.0).astype(BF16), jnp.where(lo, 0.0, ar).astype(BF16)],
                [jnp.where(lo, ar, 0.0).astype(BF16), jnp.where(lo, 0.0, a).astype(BF16)]]

    kvar = variants(kf, kr)
    vvar = variants(vf, vr)

    first = jnp.where(t > 0, 0, 1)

    nblk = tq // WINDOW
    idx = lambda j, h: j * ATTN_HEADS + h
    keys = lambda a, j: a[j * WINDOW:(j + 2) * WINDOW]

    def scores():
        for j in range(nblk):
            for h in range(ATTN_HEADS):
                p, par = h // 2, h % 2
                qp = q[j * WINDOW:(j + 1) * WINDOW, p * LANES:(p + 1) * LANES]
                s = _dot_nt(qp, keys(kvar[p // 2][par], j)) + bias_ref[first if j == 0 else 0, h]
                s_scr[idx(j, h)] = s
                m_scr[idx(j, h)] = jnp.maximum(jnp.max(s, axis=-1, keepdims=True), sinks_ref[h] * LOG2E)

    def exps():
        for j in range(nblk):
            for h in range(ATTN_HEADS):
                m = m_scr[idx(j, h)]
                pe = jnp.exp2(s_scr[idx(j, h)] - m)
                p_scr[idx(j, h)] = pe.astype(BF16)
                m_scr[idx(j, h)] = 1.0 / (jnp.sum(pe, axis=-1, keepdims=True) + jnp.exp2(sinks_ref[h] * LOG2E - m))

    def values():
        blocks = []
        for j in range(nblk):
            pairs = []
            for p in range(ATTN_HEADS // 2):
                acc = None
                for par in range(2):
                    h = 2 * p + par
                    o = _dot(p_scr[idx(j, h)], keys(vvar[p // 2][par], j)) * m_scr[idx(j, h)]
                    acc = o if acc is None else acc + o
                pairs.append(acc)
            blocks.append(jnp.concatenate(pairs, axis=1))
        return jnp.concatenate(blocks, axis=0)

    return scores, exps, values


def _hgrn2(proj, lb, st_ref, hnorm, u_scr, stb_scr):
    tq = proj.shape[0]
    nc = tq // HG_CHUNK
    qr = proj[:, OFF_HQ:OFF_HQ + HG_W]
    fr = proj[:, OFF_HF:OFF_HF + HG_W]
    iv = proj[:, OFF_HI:OFF_HI + HG_W]
    gr = proj[:, OFF_HG:OFF_HG + HG_W]
    qh = _silu(qr)
    half_gap = 0.5 * (1.0 - lb)
    f = (lb + half_gap) + half_gap * jnp.tanh(0.5 * fr)
    kk = 1.0 - f
    logf = jnp.log(f)

    rmod = lax.broadcasted_iota(I32, (tq, HG_W), 0) & (HG_CHUNK - 1)
    bc = logf
    s = 1
    while s < HG_CHUNK:
        bc = bc + jnp.where(rmod >= s, pltpu.roll(bc, s, axis=0), 0.0)
        s *= 2

    b3 = bc.reshape(nc, HG_CHUNK, HG_W)
    blast = b3[:, HG_CHUNK - 1:HG_CHUNK, :]
    kend = (kk.reshape(nc, HG_CHUNK, HG_W) * jnp.exp(blast - b3)).reshape(tq, HG_W)
    decay = jnp.exp(blast).reshape(nc, HG_W)
    qdec = (qh * jnp.exp(bc)).astype(BF16)
    kdec = (kk * jnp.exp(-bc)).astype(BF16)
    kend = kend.astype(BF16)
    ivb = iv.astype(BF16)

    ri = lax.broadcasted_iota(I32, (tq, tq), 0)
    ci = lax.broadcasted_iota(I32, (tq, tq), 1)
    cmask = ((ri // HG_CHUNK) == (ci // HG_CHUNK)) & (ri >= ci)

    heads = [slice(hh * HG_DIM, (hh + 1) * HG_DIM) for hh in range(HG_HEADS)]
    chunks = [slice(n * HG_CHUNK, (n + 1) * HG_CHUNK) for n in range(nc)]

    lane_head = lax.broadcasted_iota(I32, (HG_CHUNK, HG_W), 1) // HG_DIM
    for n, rs in enumerate(chunks):
        vstack = jnp.concatenate([ivb[rs, sl] for sl in heads], axis=0)
        kblk = jnp.concatenate([jnp.where(lane_head == hh, kend[rs], 0.0).astype(BF16)
                                for hh in range(HG_HEADS)], axis=0)
        u_scr[n] = _dot_tn(vstack, kblk)

    st = st_ref[...]
    for n in range(nc):
        stb_scr[n] = st.astype(BF16)
        st = st * decay[n:n + 1] + u_scr[n]
    st_ref[...] = st

    outs = []
    for hh, sl in enumerate(heads):
        a = _dot_nt(qdec[:, sl], kdec[:, sl])
        a = jnp.where(cmask, a, 0.0).astype(BF16)
        o_intra = _dot(a, ivb[:, sl])
        inter = [_dot_nt(qdec[rs, sl], stb_scr[n, :, sl]) for n, rs in enumerate(chunks)]
        o = o_intra + jnp.concatenate(inter, axis=0)
        o = _rms(o) * hnorm[:, sl]
        g = gr[:, sl]
        outs.append(o * _silu(g))
    return jnp.concatenate(outs, axis=1)


def _route_topk(h2, wr_hi, wr_lo, br):
    tq = h2.shape[0]
    h_hi, h_lo = _split(h2)
    logits = _dot(h_hi, wr_hi) + _dot(h_lo, wr_hi) + _dot(h_hi, wr_lo)
    lt = logits.T[0:ROUTER_ROWS] + br
    sub = lax.broadcasted_iota(I32, (SUBLANES, tq), 0).astype(F32)
    none = float(SUBLANES)

    gl = lt[0:GROUP_ROWS]
    gm = jnp.max(gl, axis=0, keepdims=True)
    gidx = jnp.min(jnp.where(gl == gm, sub, none), axis=0, keepdims=True)
    g_w = 1.0 / jnp.sum(jnp.exp(gl - gm), axis=0, keepdims=True)

    group_rows = lambda g: lt[GROUP_ROWS + EPG * g:GROUP_ROWS + EPG * (g + 1)]
    es = group_rows(0)
    for g in range(1, N_GROUPS):
        es = jnp.where(gidx == float(g), group_rows(g), es)
    m1 = jnp.max(es, axis=0, keepdims=True)
    i1 = jnp.min(jnp.where(es == m1, sub, none), axis=0, keepdims=True)
    e2 = jnp.where(sub == i1, NEG, es)
    m2 = jnp.max(e2, axis=0, keepdims=True)
    i2 = jnp.min(jnp.where(e2 == m2, sub, none), axis=0, keepdims=True)
    dd = jnp.exp(m2 - m1)
    w1 = g_w / (1.0 + dd)
    w2 = g_w * dd / (1.0 + dd)
    first_low = i1 < i2
    ea = jnp.minimum(i1, i2)
    eb = jnp.maximum(i1, i2)
    w_lo = jnp.where(first_low, w1, w2)
    w_hi = jnp.where(first_low, w2, w1)
    pair = ea * (2.0 * EPG - 1.0 - ea) * 0.5 + (eb - ea - 1.0)
    bucket = gidx * float(N_PAIRS) + pair
    return bucket, w_lo, w_hi


def _route_rank(bucket, w_lo, w_hi, carry_ref, bidx, live):
    tq = bucket.shape[1]
    brow = lax.broadcasted_iota(I32, (LANES, tq), 0).astype(F32)
    onehot = brow == bucket
    oh = jnp.where(onehot, live, 0.0)
    ti = lax.broadcasted_iota(I32, (tq, tq), 0)
    tj = lax.broadcasted_iota(I32, (tq, tq), 1)
    upper = jnp.where(ti < tj, 1.0, 0.0).astype(BF16)
    before = _dot(oh.astype(BF16), upper) + carry_ref[...]
    rank = jnp.sum(jnp.where(onehot, before, 0.0), axis=0, keepdims=True)
    carry_ref[...] = carry_ref[...] + jnp.sum(oh, axis=1, keepdims=True)

    lane_row = lax.broadcasted_iota(I32, (LANES - EXT_BATCH0, tq), 0)
    onehot_b = jnp.where(lane_row == bidx, 1.0, 0.0)
    info = jnp.concatenate([w_lo, w_hi, jnp.zeros((EXT_BATCH0 - 2, tq), F32), onehot_b], axis=0)
    return rank, info.T


def _mixer_kernel(sinks_ref, x_ref, mod_ref, ln1pre_ref, ln1post_ref, ln2pre_ref, anorm_ref, hnorm_ref,
                  lb_ref, win_ref, wout_ref, wrhi_ref, wrlo_ref, br_ref, bias_ref,
                  rows_hbm, info_ref, cnt_ref,
                  kprev_ref, vprev_ref, st_ref, carry_ref, s_scr, m_scr, p_scr, u_scr, stb_scr,
                  h2_scr, proj_scr, rowbuf, rsem, *, tiles_per_seq):
    s = pl.program_id(0)
    n_tiles = pl.num_programs(0) - 1
    tq = x_ref.shape[0]
    t = lax.rem(jnp.minimum(s, n_tiles - 1), tiles_per_seq)
    slot = lax.rem(s, 2)
    bits = lambda a: pltpu.bitcast(a, U32)

    def row_copy(buf_slot, tile):
        return pltpu.make_async_copy(rowbuf.at[buf_slot], rows_hbm.at[pl.ds(pl.multiple_of(tile * tq, tq), tq), 0],
                                     rsem.at[buf_slot])

    @pl.when(s == 0)
    def _():
        carry_ref[...] = jnp.zeros_like(carry_ref)
        h2_scr[...] = jnp.zeros_like(h2_scr)

    @pl.when(t == 0)
    def _():
        st_ref[...] = jnp.zeros_like(st_ref)
        kprev_ref[...] = jnp.zeros_like(kprev_ref)
        vprev_ref[...] = jnp.zeros_like(vprev_ref)

    x = x_ref[...]
    mod = mod_ref[0]
    sh1, sc1, ga1, sh2, sc2 = mod[0:1], mod[1:2], mod[2:3], mod[3:4], mod[4:5]
    prev = jnp.maximum(s - 1, 0)

    @pl.when(s <= n_tiles)
    def _():
        bucket, w_lo, w_hi = _route_topk(h2_scr[...], wrhi_ref[...], wrlo_ref[...], br_ref[...])
        h = _rms(x) * (ln1pre_ref[...] * (1.0 + sc1)) + sh1
        proj_scr[...] = _dot(h.astype(BF16), win_ref[...])
        live = jnp.where(s >= 1, 1.0, 0.0)
        rank, ext = _route_rank(bucket, w_lo, w_hi, carry_ref, prev // tiles_per_seq, live)
        rowbuf[1 - slot, :, OFF_EXT:ROW_W] = bits(ext)
        info_ref[0] = jnp.concatenate([bucket, rank, jnp.zeros((6, tq), F32)], axis=0).astype(I32)
        cnt_ref[...] = jnp.broadcast_to(carry_ref[...], cnt_ref.shape)

    proj = proj_scr[...]
    scores, exps, values = _attention(proj, kprev_ref, vprev_ref, sinks_ref, bias_ref, t, s_scr, m_scr, p_scr)
    scores()
    exps()
    attn = _rms(values()) * anorm_ref[...]

    lbr = lb_ref[...]
    le = jnp.exp(lbr - jnp.max(lbr, axis=0, keepdims=True))
    lb = le[0:1] / jnp.sum(le, axis=0, keepdims=True)
    hg = _hgrn2(proj, lb, st_ref, hnorm_ref[...], u_scr, stb_scr)

    mix = _dot(jnp.concatenate([attn, hg], axis=1).astype(BF16), wout_ref[...])
    x1 = x + _rms(mix) * (ga1 * ln1post_ref[...])

    h2 = _rms(x1) * (ln2pre_ref[...] * (1.0 + sc2)) + sh2
    h2r = h2.astype(BF16).astype(F32)

    @pl.when(s >= 2)
    def _():
        row_copy(slot, s - 2).wait()

    rowbuf[slot, :, 0:D_MODEL] = bits(x1)
    rowbuf[slot, :, OFF_H2P:OFF_EXT] = ((bits(h2r[:, 0:H2P_W]) >> 16)
                                        | (bits(h2r[:, H2P_W:D_MODEL]) & jnp.uint32(0xFFFF0000)))
    h2_scr[...] = h2

    @pl.when(s >= 1)
    def _():
        row_copy(1 - slot, prev).start()

    @pl.when(s == n_tiles)
    def _():
        row_copy(1 - slot, prev).wait()


def _attn_bias():
    qi = np.arange(WINDOW)[:, None]
    kj = np.arange(2 * WINDOW)[None, :]
    dist = qi + WINDOW - kj
    in_win = (dist >= 0) & (dist < WINDOW)
    slopes = 2.0 ** (-8.0 * (np.arange(ATTN_HEADS) + 1.0) / ATTN_HEADS)
    b = np.where(in_win[None], -slopes[:, None, None] * dist[None] * LOG2E, NEG)
    b_first = np.where((kj >= WINDOW)[None], b, NEG)
    return jnp.asarray(np.stack([b, b_first]).astype(np.float32))


def _mixer(x2, mod3, sinks, ln1pre, ln1post, ln2pre, anorm, hnorm, lb, win, wout, wr_hi, wr_lo, br,
           bsz, seq):
    bias = _attn_bias()
    n = bsz * seq
    nt = seq // TQ
    n_tiles = bsz * nt
    n_sc = (TQ // WINDOW) * ATTN_HEADS
    cur = lambda s: jnp.minimum(s, n_tiles - 1)
    const = lambda s: (0, 0)
    full = lambda a: pl.BlockSpec(a.shape, const)
    return pl.pallas_call(
        functools.partial(_mixer_kernel, tiles_per_seq=nt),
        grid=(n_tiles + 1,),
        in_specs=[pl.BlockSpec(memory_space=pltpu.SMEM),
                  pl.BlockSpec((TQ, D_MODEL), lambda s: (cur(s), 0)),
                  pl.BlockSpec((1, 6, D_MODEL), lambda s: (cur(s) // nt, 0, 0)),
                  full(ln1pre), full(ln1post), full(ln2pre), full(anorm), full(hnorm), full(lb),
                  full(win), full(wout), full(wr_hi), full(wr_lo), full(br),
                  pl.BlockSpec(bias.shape, lambda s: (0, 0, 0, 0))],
        out_specs=[pl.BlockSpec(memory_space=pl.ANY),
                   pl.BlockSpec((1, 8, TQ), lambda s: (jnp.maximum(s - 1, 0), 0, 0)),
                   pl.BlockSpec((LANES, LANES), const)],
        out_shape=[jax.ShapeDtypeStruct((n, 1, ROW_W), U32),
                   jax.ShapeDtypeStruct((n // TQ, 8, TQ), I32),
                   jax.ShapeDtypeStruct((LANES, LANES), F32)],
        scratch_shapes=[pltpu.VMEM((WINDOW, KV_W), F32),
                        pltpu.VMEM((WINDOW, KV_W), F32),
                        pltpu.VMEM((HG_DIM, HG_W), F32),
                        pltpu.VMEM((LANES, 1), F32),
                        pltpu.VMEM((n_sc, WINDOW, 2 * WINDOW), F32),
                        pltpu.VMEM((n_sc, WINDOW, 1), F32),
                        pltpu.VMEM((n_sc, WINDOW, 2 * WINDOW), BF16),
                        pltpu.VMEM((TQ // HG_CHUNK, HG_DIM, HG_W), F32),
                        pltpu.VMEM((TQ // HG_CHUNK, HG_DIM, HG_W), BF16),
                        pltpu.VMEM((TQ, D_MODEL), F32),
                        pltpu.VMEM((TQ, IN_W), F32),
                        pltpu.VMEM((2, TQ, ROW_W), U32),
                        pltpu.SemaphoreType.DMA((2,))],
        compiler_params=pltpu.CompilerParams(dimension_semantics=("arbitrary",),
                                             vmem_limit_bytes=VMEM_LIMIT),
        name="mixer",
    )(sinks, x2, mod3, ln1pre, ln1post, ln2pre, anorm, hnorm, lb, win, wout, wr_hi, wr_lo, br, bias)


PERM_STEPS = 8
PERM_ROWS = 8
PERM_UNROLL = 16


def _perm_kernel(rs_ref, cnt_ref, bucket_ref, rank_ref, perm_ref, pos_vmem, pos_smem, sem):
    pid = pl.program_id(0)
    rows, cols = pos_vmem.shape

    b = bucket_ref[0]
    start = jnp.zeros_like(b)
    for k in range(N_BUCKETS):
        start = jnp.where(b == k, rs_ref[k], start)
    pos_vmem[...] = start + rank_ref[0]
    copies = [pltpu.make_async_copy(pos_vmem.at[r], pos_smem.at[pl.ds(r * cols, cols)], sem) for r in range(rows)]
    for cp in copies:
        cp.start()

    @pl.when(pid == 0)
    def _():
        def per_bucket(k, carry):
            first = rs_ref[k]
            cnt = cnt_ref[k]

            def pad(r, c2):
                perm_ref[first + r] = 0
                return c2

            lax.fori_loop(cnt, ((cnt + TM - 1) // TM) * TM, pad, 0)
            return carry

        lax.fori_loop(0, N_BUCKETS, per_bucket, 0)

        def tail(blk, carry):
            for u in range(PERM_UNROLL):
                perm_ref[blk * PERM_UNROLL + u] = 0
            return carry

        lax.fori_loop(rs_ref[N_BUCKETS] // PERM_UNROLL, perm_ref.shape[0] // PERM_UNROLL, tail, 0)

    for cp in copies:
        cp.wait()
    base = pid * (rows * cols)

    def body(j, carry):
        i0 = j * PERM_UNROLL
        positions = [pos_smem[i0 + u] for u in range(PERM_UNROLL)]
        for u in range(PERM_UNROLL):
            perm_ref[positions[u]] = base + i0 + u
        return carry

    lax.fori_loop(0, rows * cols // PERM_UNROLL, body, 0)


def _perm(row_start, counts, bucket, rank, n_rows):
    n = bucket.shape[0]
    cols = n // (PERM_STEPS * PERM_ROWS)
    assert n % (PERM_STEPS * PERM_ROWS * PERM_UNROLL) == 0
    chunked = lambda a: a.reshape(PERM_STEPS, PERM_ROWS, cols)
    chunk_spec = pl.BlockSpec((1, PERM_ROWS, cols), lambda i: (i, 0, 0))
    return pl.pallas_call(
        _perm_kernel,
        grid=(PERM_STEPS,),
        in_specs=[pl.BlockSpec(memory_space=pltpu.SMEM),
                  pl.BlockSpec(memory_space=pltpu.SMEM),
                  chunk_spec, chunk_spec],
        out_specs=pl.BlockSpec(memory_space=pltpu.SMEM),
        out_shape=jax.ShapeDtypeStruct((n_rows,), I32),
        scratch_shapes=[pltpu.VMEM((PERM_ROWS, cols), I32), pltpu.SMEM((PERM_ROWS * cols,), I32),
                        pltpu.SemaphoreType.DMA(())],
        compiler_params=pltpu.CompilerParams(dimension_semantics=("arbitrary",)),
        name="perm",
    )(row_start, counts, chunked(bucket), chunked(rank))


GATHER_DEPTH = 3
OUT_SLOTS = 3


SCHED_EXPERT, SCHED_RUN_START, SCHED_SLOT, SCHED_NEXT_EXPERT = range(4)


def _moe_kernel(nt_ref, nv_ref, sched_ref, perm_ref,
                rows_hbm, gain_ref, wg_hbm, wu_hbm, wd_hbm,
                out_hbm, xbuf, x2d, obuf, wgu_buf, wd_buf, gsem, ssem, wsem):
    i = pl.program_id(0)
    nt = nt_ref[0]
    n_steps = pl.num_programs(0)
    last_tile = n_steps - 1
    sched = lambda field, side: sched_ref[(field * 2 + side) * n_steps + i]

    def weight_copies(side, expert, slot):
        return (pltpu.make_async_copy(wg_hbm.at[expert], wgu_buf.at[side, slot, :, 0:FF], wsem.at[side, slot]),
                pltpu.make_async_copy(wu_hbm.at[expert], wgu_buf.at[side, slot, :, FF:2 * FF], wsem.at[side, slot]),
                pltpu.make_async_copy(wd_hbm.at[expert], wd_buf.at[side, slot], wsem.at[side, slot]))
    nbuf = xbuf.shape[0] // TM

    def start_gather(tile, pred):
        sl = lax.rem(tile, nbuf)
        base = jnp.minimum(tile, last_tile) * TM
        for r in range(TM):
            @pl.when(pred)
            def _():
                tok = perm_ref[base + r]
                pltpu.make_async_copy(rows_hbm.at[tok], xbuf.at[sl * TM + r], gsem.at[sl]).start()

    def wait_gather(tile):
        sl = lax.rem(tile, nbuf)
        pltpu.make_async_copy(rows_hbm.at[pl.ds(0, TM)], xbuf.at[pl.ds(sl * TM, TM)], gsem.at[sl]).wait()

    def wait_scatter(sl, nv):
        @pl.when(nv == TM)
        def _():
            pltpu.make_async_copy(obuf.at[sl], out_hbm.at[pl.ds(0, TM)], ssem.at[sl]).wait()

        @pl.when(nv < TM)
        def _():
            def one(r, carry):
                pltpu.make_async_copy(obuf.at[sl, pl.ds(0, 1)], out_hbm.at[pl.ds(0, 1)], ssem.at[sl]).wait()
                return carry

            lax.fori_loop(0, nv, one, 0)

    def compute(xb, slot_a, slot_b):
        x1 = pltpu.bitcast(xb[:, 0:D_MODEL], F32)
        hp = xb[:, OFF_H2P:OFF_EXT]
        h2a = pltpu.bitcast(hp << 16, F32).astype(BF16)
        h2b = pltpu.bitcast(hp & jnp.uint32(0xFFFF0000), F32).astype(BF16)
        ext = pltpu.bitcast(xb[:, OFF_EXT:ROW_W], F32)
        w_lo, w_hi = ext[:, 0:1], ext[:, 1:2]
        ga2 = jnp.zeros((TM, D_MODEL), F32)
        for k in range(gain_ref.shape[0]):
            ga2 = jnp.where(ext[:, EXT_BATCH0 + k:EXT_BATCH0 + k + 1] > 0.5, gain_ref[k:k + 1, :], ga2)

        def hidden(side, slot, w):
            gu = (_dot(h2a, wgu_buf[side, slot, 0:H2P_W])
                  + _dot(h2b, wgu_buf[side, slot, H2P_W:D_MODEL]))
            hg, hu = gu[:, 0:FF], gu[:, FF:2 * FF]
            return (w * (_silu(hg) * hu)).astype(BF16)

        act = jnp.concatenate([hidden(0, slot_a, w_lo), hidden(1, slot_b, w_hi)], axis=1)
        wd = jnp.concatenate([wd_buf[0, slot_a], wd_buf[1, slot_b]], axis=0)
        y = _dot(act, wd)
        return x1 + ga2 * _rms(y)

    @pl.when(i == 0)
    def _():
        for side in range(2):
            for cp in weight_copies(side, sched(SCHED_EXPERT, side), 0):
                cp.start()
        for d in range(GATHER_DEPTH):
            start_gather(d, d < nt)

    @pl.when(i < nt)
    def _():
        nv = nv_ref[i]
        osl = lax.rem(i, OUT_SLOTS)
        slots = []
        for side in range(2):
            slot = sched(SCHED_SLOT, side)
            slots.append(slot)

            @pl.when(sched(SCHED_RUN_START, side) == 1)
            def _():
                for cp in weight_copies(side, sched(SCHED_EXPERT, side), slot):
                    cp.wait()
                nxt = sched(SCHED_NEXT_EXPERT, side)

                @pl.when(nxt >= 0)
                def _():
                    for cp in weight_copies(side, nxt, 1 - slot):
                        cp.start()

        wait_gather(i)

        @pl.when(i >= OUT_SLOTS)
        def _():
            wait_scatter(osl, nv_ref[jnp.maximum(i - OUT_SLOTS, 0)])

        x2d[...] = xbuf[pl.ds(lax.rem(i, nbuf) * TM, TM)].reshape(TM, ROW_W)
        result = compute(x2d[...], *slots)
        start_gather(i + GATHER_DEPTH, i + GATHER_DEPTH < nt)

        def scatter_row(k, r):
            tok = perm_ref[i * TM + r]
            pltpu.make_async_copy(obuf.at[k, pl.ds(r, 1)], out_hbm.at[pl.ds(tok, 1)], ssem.at[k]).start()

        for k in range(OUT_SLOTS):
            @pl.when((osl == k) & (nv == TM))
            def _():
                obuf[k] = result
                for r in range(TM):
                    scatter_row(k, r)

            @pl.when((osl == k) & (nv < TM))
            def _():
                obuf[k] = result
                for r in range(TM):
                    pl.when(r < nv)(functools.partial(scatter_row, k, r))

        @pl.when(i == nt - 1)
        def _():
            wait_scatter(osl, nv)
            for back in range(1, OUT_SLOTS):
                @pl.when(i >= back)
                def _():
                    wait_scatter(lax.rem(i - back + OUT_SLOTS, OUT_SLOTS), nv_ref[jnp.maximum(i - back, 0)])


def _moe(rows, gain, wg, wu, wd, nt, nv, sched, perm, n_tiles):
    n = rows.shape[0]
    const2 = lambda i, *_: (0, 0)
    grid_spec = pltpu.PrefetchScalarGridSpec(
        num_scalar_prefetch=4,
        grid=(n_tiles,),
        in_specs=[pl.BlockSpec(memory_space=pl.ANY),
                  pl.BlockSpec(gain.shape, const2),
                  pl.BlockSpec(memory_space=pl.ANY),
                  pl.BlockSpec(memory_space=pl.ANY),
                  pl.BlockSpec(memory_space=pl.ANY)],
        out_specs=pl.BlockSpec(memory_space=pl.ANY),
        scratch_shapes=[pltpu.VMEM(((GATHER_DEPTH + 1) * TM, 1, ROW_W), U32),
                        pltpu.VMEM((TM, ROW_W), U32),
                        pltpu.VMEM((OUT_SLOTS, TM, D_MODEL), F32),
                        pltpu.VMEM((2, 2, D_MODEL, 2 * FF), BF16),
                        pltpu.VMEM((2, 2, FF, D_MODEL), BF16),
                        pltpu.SemaphoreType.DMA((GATHER_DEPTH + 1,)),
                        pltpu.SemaphoreType.DMA((OUT_SLOTS,)),
                        pltpu.SemaphoreType.DMA((2, 2))],
    )
    return pl.pallas_call(
        _moe_kernel,
        grid_spec=grid_spec,
        out_shape=jax.ShapeDtypeStruct((n, D_MODEL), F32),
        compiler_params=pltpu.CompilerParams(dimension_semantics=("arbitrary",),
                                             vmem_limit_bytes=VMEM_LIMIT),
        name="moe",
    )(nt, nv, sched, perm, rows, gain, wg, wu, wd)


def _weight_schedule(ea, eb):
    n_tiles = ea.shape[0]
    tid = jnp.arange(n_tiles, dtype=I32)
    fields = [[], [], [], []]
    for e in (ea, eb):
        start = jnp.concatenate([jnp.ones((1,), I32), (e[1:] != e[:-1]).astype(I32)])
        slot = (jnp.cumsum(start) - 1) % 2
        start_idx = jnp.where(start == 1, tid, n_tiles)
        next_start = jnp.concatenate([lax.cummin(start_idx, reverse=True)[1:], jnp.full((1,), n_tiles, I32)])
        nxt = jnp.where(next_start < n_tiles, e[jnp.minimum(next_start, n_tiles - 1)], -1)
        for f, v in zip(fields, (e, start, slot, nxt)):
            f.append(v.astype(I32))
    return jnp.concatenate([v for f in fields for v in f])


def kernel(x, c, ln1_pre, ln1_post, ln2_pre, ln2_post, w_ada, b_ada, w_in, attn_sinks, attn_out_norm,
           hgrn_lb, hgrn_out_norm, w_out, w_router_group, b_router_group, w_router_expert,
           b_router_expert, w_exp_gate, w_exp_up, w_exp_down):
    bsz, seq, d = x.shape
    assert d == D_MODEL and seq % TQ == 0 and w_ada.shape[0] == 1 and hgrn_lb.shape[0] == 2
    assert bsz <= LANES - EXT_BATCH0 and (bsz * seq) % TM == 0
    n = bsz * seq

    mod = _ada(c, w_ada[0], b_ada[0])
    mod3 = mod.reshape(bsz, 6, d)

    wr = jnp.concatenate([w_router_group[0], jnp.zeros((d, GROUP_ROWS - N_GROUPS), F32), w_router_expert[0],
                          jnp.zeros((d, LANES - ROUTER_ROWS), F32)], axis=1)
    br = jnp.concatenate([b_router_group[0], jnp.full((GROUP_ROWS - N_GROUPS,), NEG, F32), b_router_expert[0]])
    wr_hi = wr.astype(BF16)
    wr_lo = (wr - wr_hi.astype(F32)).astype(BF16)

    x1ext, info, cnt = _mixer(
        x.reshape(n, d), mod3, attn_sinks[0], ln1_pre, ln1_post, ln2_pre, attn_out_norm, hgrn_out_norm,
        hgrn_lb, w_in[0].astype(BF16), w_out[0].astype(BF16), wr_hi, wr_lo, br.reshape(ROUTER_ROWS, 1),
        bsz, seq)

    n_tiles = n // TM + N_BUCKETS
    counts = cnt[:N_BUCKETS, 0].astype(I32)
    tiles_per = (counts + TM - 1) // TM
    tile_end = jnp.cumsum(tiles_per)
    tile_start = tile_end - tiles_per
    nt = tile_end[-1]
    bucket = info[:, 0, :].reshape(n)
    rank = info[:, 1, :].reshape(n)
    tid = jnp.arange(n_tiles, dtype=I32)[None, :]
    member = (tid >= tile_start[:, None]) & (tid < tile_end[:, None])
    pick = lambda per_bucket: jnp.sum(jnp.where(member, per_bucket, 0), axis=0).astype(I32)
    bidx = np.arange(N_BUCKETS, dtype=np.int32)
    ea_of = jnp.asarray((bidx // N_PAIRS) * EPG + _PAIR_A[bidx % N_PAIRS])[:, None]
    eb_of = jnp.asarray((bidx // N_PAIRS) * EPG + _PAIR_B[bidx % N_PAIRS])[:, None]
    last_used = jnp.arange(N_BUCKETS)[:, None] == jnp.max(jnp.where(tiles_per > 0, jnp.arange(N_BUCKETS), 0))
    unused = tid[0] >= nt
    nv = pick(jnp.clip(counts[:, None] - (tid - tile_start[:, None]) * TM, 0, TM))
    ea = jnp.where(unused, jnp.sum(jnp.where(last_used, ea_of, 0)), pick(ea_of)).astype(I32)
    eb = jnp.where(unused, jnp.sum(jnp.where(last_used, eb_of, 0)), pick(eb_of)).astype(I32)

    pad128 = lambda a: jnp.concatenate([a, jnp.zeros((LANES - a.shape[0],), I32)])
    row_start = jnp.concatenate([tile_start, nt.reshape(1)]) * TM
    perm = _perm(pad128(row_start), pad128(counts), bucket, rank, n_tiles * TM)

    wg, wu, wd = w_exp_gate[0].astype(BF16), w_exp_up[0].astype(BF16), w_exp_down[0].astype(BF16)
    gain = mod3[:, 5, :] * ln2_post
    out = _moe(x1ext, gain, wg, wu, wd, nt.reshape(1), nv, _weight_schedule(ea, eb), perm, n_tiles)
    return out.reshape(bsz, seq, d)
```

```python
import functools

import numpy as np
import jax
import jax.numpy as jnp
from jax import lax
from jax.experimental import pallas as pl
from jax.experimental.pallas import tpu as pltpu

F32 = jnp.float32
BF16 = jnp.bfloat16
I32 = jnp.int32

D_MODEL = 1024
ATTN_HEADS = 8
HEAD_DIM = 64
WINDOW = 128
ATTN_W = 512
KV_W = 128
HG_HEADS = 4
HG_DIM = 128
HG_W = 512
HG_CHUNK = 32
IN_W = 2816
N_GROUPS = 4
EPG = 8
N_EXPERTS = 32
FF = 256
N_PAIRS = EPG * (EPG - 1) // 2
N_BUCKETS = N_GROUPS * N_PAIRS
EPS = 1e-6
NEG = -1e30
LOG2E = 1.4426950408889634

LANES = 128
H2P_W = D_MODEL // 2
EXT_W = LANES
OFF_H2P = D_MODEL
OFF_EXT = D_MODEL + H2P_W
ROW_W = OFF_EXT + EXT_W
EXT_BATCH0 = 8
U32 = jnp.uint32
SUBLANES = 8
GROUP_ROWS = SUBLANES
ROUTER_ROWS = GROUP_ROWS + N_EXPERTS

TQ = 256
TM = 128
VMEM_LIMIT = 56 * 1024 * 1024

OFF_Q, OFF_K, OFF_V, OFF_HQ, OFF_HF, OFF_HI, OFF_HG = 0, 512, 640, 768, 1280, 1792, 2304

_PAIR_A = np.array([a for a in range(EPG) for b in range(a + 1, EPG)], np.int32)
_PAIR_B = np.array([b for a in range(EPG) for b in range(a + 1, EPG)], np.int32)


def _dot(a, b):
    return jnp.dot(a, b, preferred_element_type=F32)


def _dot_nt(a, b):
    return lax.dot_general(a, b, (((1,), (1,)), ((), ())), preferred_element_type=F32)


def _dot_tn(a, b):
    return lax.dot_general(a, b, (((0,), (0,)), ((), ())), preferred_element_type=F32)


def _split(a):
    hi = a.astype(BF16)
    lo = (a - hi.astype(F32)).astype(BF16)
    return hi, lo


def _rms(x):
    return x * lax.rsqrt(jnp.mean(x * x, axis=-1, keepdims=True) + EPS)


def _silu(x):
    half = 0.5 * x
    return half * jnp.tanh(half) + half


def _ada_kernel(c_ref, w_ref, b_ref, o_ref):
    c = c_ref[...]
    ca = _silu(c)
    c_hi, c_lo = _split(ca)
    w_hi, w_lo = _split(w_ref[...])
    o_ref[...] = _dot(c_hi, w_hi) + _dot(c_lo, w_hi) + _dot(c_hi, w_lo) + b_ref[...]


def _ada(c, w, b):
    bsz, d = c.shape
    n_out = w.shape[1]
    return pl.pallas_call(
        _ada_kernel,
        grid=(n_out // d,),
        in_specs=[pl.BlockSpec((bsz, d), lambda j: (0, 0)),
                  pl.BlockSpec((d, d), lambda j: (0, j)),
                  pl.BlockSpec((1, d), lambda j: (0, j))],
        out_specs=pl.BlockSpec((bsz, d), lambda j: (0, j)),
        out_shape=jax.ShapeDtypeStruct((bsz, n_out), F32),
        compiler_params=pltpu.CompilerParams(dimension_semantics=("arbitrary",),
                                             vmem_limit_bytes=VMEM_LIMIT),
        name="adaln",
    )(c, w, b.reshape(1, n_out))


def _attention(proj, kprev_ref, vprev_ref, sinks_ref, bias_ref, t, s_scr, m_scr, p_scr):
    tq = proj.shape[0]
    q = (proj[:, OFF_Q:OFF_Q + ATTN_W] * (HEAD_DIM ** -0.5 * LOG2E)).astype(BF16)
    kf = jnp.concatenate([kprev_ref[...], proj[:, OFF_K:OFF_K + KV_W]], axis=0)
    vf = jnp.concatenate([vprev_ref[...], proj[:, OFF_V:OFF_V + KV_W]], axis=0)
    kprev_ref[...] = proj[tq - WINDOW:, OFF_K:OFF_K + KV_W]
    vprev_ref[...] = proj[tq - WINDOW:, OFF_V:OFF_V + KV_W]

    lo = lax.broadcasted_iota(I32, kf.shape, 1) < HEAD_DIM
    kr = pltpu.roll(kf, HEAD_DIM, axis=1)
    vr = pltpu.roll(vf, HEAD_DIM, axis=1)

    def variants(a, ar):
        return [[jnp.where(lo, a, 0.0).astype(BF16), jnp.where(lo, 0.0, ar).astype(BF16)],
                [jnp.where(lo, ar, 0.0).astype(BF16), jnp.where(lo, 0.0, a).astype(BF16)]]

    kvar = variants(kf, kr)
    vvar = variants(vf, vr)

    first = jnp.where(t > 0, 0, 1)

    nblk = tq // WINDOW
    idx = lambda j, h: j * ATTN_HEADS + h
    keys = lambda a, j: a[j * WINDOW:(j + 2) * WINDOW]

    def scores():
        for j in range(nblk):
            for h in range(ATTN_HEADS):
                p, par = h // 2, h % 2
                qp = q[j * WINDOW:(j + 1) * WINDOW, p * LANES:(p + 1) * LANES]
                s = _dot_nt(qp, keys(kvar[p // 2][par], j)) + bias_ref[first if j == 0 else 0, h]
                s_scr[idx(j, h)] = s
                m_scr[idx(j, h)] = jnp.maximum(jnp.max(s, axis=-1, keepdims=True), sinks_ref[h] * LOG2E)

    def exps():
        for j in range(nblk):
            for h in range(ATTN_HEADS):
                m = m_scr[idx(j, h)]
                pe = jnp.exp2(s_scr[idx(j, h)] - m)
                p_scr[idx(j, h)] = pe.astype(BF16)
                m_scr[idx(j, h)] = 1.0 / (jnp.sum(pe, axis=-1, keepdims=True) + jnp.exp2(sinks_ref[h] * LOG2E - m))

    def values():
        blocks = []
        for j in range(nblk):
            pairs = []
            for p in range(ATTN_HEADS // 2):
                acc = None
                for par in range(2):
                    h = 2 * p + par
                    o = _dot(p_scr[idx(j, h)], keys(vvar[p // 2][par], j)) * m_scr[idx(j, h)]
                    acc = o if acc is None else acc + o
                pairs.append(acc)
            blocks.append(jnp.concatenate(pairs, axis=1))
        return jnp.concatenate(blocks, axis=0)

    return scores, exps, values


def _hgrn2(proj, lb, st_ref, hnorm, u_scr, stb_scr):
    tq = proj.shape[0]
    nc = tq // HG_CHUNK
    qr = proj[:, OFF_HQ:OFF_HQ + HG_W]
    fr = proj[:, OFF_HF:OFF_HF + HG_W]
    iv = proj[:, OFF_HI:OFF_HI + HG_W]
    gr = proj[:, OFF_HG:OFF_HG + HG_W]
    qh = _silu(qr)
    half_gap = 0.5 * (1.0 - lb)
    f = (lb + half_gap) + half_gap * jnp.tanh(0.5 * fr)
    kk = 1.0 - f
    logf = jnp.log(f)

    rmod = lax.broadcasted_iota(I32, (tq, HG_W), 0) & (HG_CHUNK - 1)
    bc = logf
    s = 1
    while s < HG_CHUNK:
        bc = bc + jnp.where(rmod >= s, pltpu.roll(bc, s, axis=0), 0.0)
        s *= 2

    b3 = bc.reshape(nc, HG_CHUNK, HG_W)
    blast = b3[:, HG_CHUNK - 1:HG_CHUNK, :]
    kend = (kk.reshape(nc, HG_CHUNK, HG_W) * jnp.exp(blast - b3)).reshape(tq, HG_W)
    decay = jnp.exp(blast).reshape(nc, HG_W)
    qdec = (qh * jnp.exp(bc)).astype(BF16)
    kdec = (kk * jnp.exp(-bc)).astype(BF16)
    kend = kend.astype(BF16)
    ivb = iv.astype(BF16)

    ri = lax.broadcasted_iota(I32, (tq, tq), 0)
    ci = lax.broadcasted_iota(I32, (tq, tq), 1)
    cmask = ((ri // HG_CHUNK) == (ci // HG_CHUNK)) & (ri >= ci)

    heads = [slice(hh * HG_DIM, (hh + 1) * HG_DIM) for hh in range(HG_HEADS)]
    chunks = [slice(n * HG_CHUNK, (n + 1) * HG_CHUNK) for n in range(nc)]

    lane_head = lax.broadcasted_iota(I32, (HG_CHUNK, HG_W), 1) // HG_DIM
    for n, rs in enumerate(chunks):
        vstack = jnp.concatenate([ivb[rs, sl] for sl in heads], axis=0)
        kblk = jnp.concatenate([jnp.where(lane_head == hh, kend[rs], 0.0).astype(BF16)
                                for hh in range(HG_HEADS)], axis=0)
        u_scr[n] = _dot_tn(vstack, kblk)

    st = st_ref[...]
    for n in range(nc):
        stb_scr[n] = st.astype(BF16)
        st = st * decay[n:n + 1] + u_scr[n]
    st_ref[...] = st

    outs = []
    for hh, sl in enumerate(heads):
        a = _dot_nt(qdec[:, sl], kdec[:, sl])
        a = jnp.where(cmask, a, 0.0).astype(BF16)
        o_intra = _dot(a, ivb[:, sl])
        inter = [_dot_nt(qdec[rs, sl], stb_scr[n, :, sl]) for n, rs in enumerate(chunks)]
        o = o_intra + jnp.concatenate(inter, axis=0)
        o = _rms(o) * hnorm[:, sl]
        g = gr[:, sl]
        outs.append(o * _silu(g))
    return jnp.concatenate(outs, axis=1)


def _route_topk(h2, wr_hi, wr_lo, br):
    tq = h2.shape[0]
    h_hi, h_lo = _split(h2)
    logits = _dot(h_hi, wr_hi) + _dot(h_lo, wr_hi) + _dot(h_hi, wr_lo)
    lt = logits.T[0:ROUTER_ROWS] + br
    sub = lax.broadcasted_iota(I32, (SUBLANES, tq), 0).astype(F32)
    none = float(SUBLANES)

    gl = lt[0:GROUP_ROWS]
    gm = jnp.max(gl, axis=0, keepdims=True)
    gidx = jnp.min(jnp.where(gl == gm, sub, none), axis=0, keepdims=True)
    g_w = 1.0 / jnp.sum(jnp.exp(gl - gm), axis=0, keepdims=True)

    group_rows = lambda g: lt[GROUP_ROWS + EPG * g:GROUP_ROWS + EPG * (g + 1)]
    es = group_rows(0)
    for g in range(1, N_GROUPS):
        es = jnp.where(gidx == float(g), group_rows(g), es)
    m1 = jnp.max(es, axis=0, keepdims=True)
    i1 = jnp.min(jnp.where(es == m1, sub, none), axis=0, keepdims=True)
    e2 = jnp.where(sub == i1, NEG, es)
    m2 = jnp.max(e2, axis=0, keepdims=True)
    i2 = jnp.min(jnp.where(e2 == m2, sub, none), axis=0, keepdims=True)
    dd = jnp.exp(m2 - m1)
    w1 = g_w / (1.0 + dd)
    w2 = g_w * dd / (1.0 + dd)
    first_low = i1 < i2
    ea = jnp.minimum(i1, i2)
    eb = jnp.maximum(i1, i2)
    w_lo = jnp.where(first_low, w1, w2)
    w_hi = jnp.where(first_low, w2, w1)
    pair = ea * (2.0 * EPG - 1.0 - ea) * 0.5 + (eb - ea - 1.0)
    bucket = gidx * float(N_PAIRS) + pair
    return bucket, w_lo, w_hi


def _route_rank(bucket, w_lo, w_hi, carry_ref, bidx, live):
    tq = bucket.shape[1]
    brow = lax.broadcasted_iota(I32, (LANES, tq), 0).astype(F32)
    onehot = brow == bucket
    oh = jnp.where(onehot, live, 0.0)
    ti = lax.broadcasted_iota(I32, (tq, tq), 0)
    tj = lax.broadcasted_iota(I32, (tq, tq), 1)
    upper = jnp.where(ti < tj, 1.0, 0.0).astype(BF16)
    before = _dot(oh.astype(BF16), upper) + carry_ref[...]
    rank = jnp.sum(jnp.where(onehot, before, 0.0), axis=0, keepdims=True)
    carry_ref[...] = carry_ref[...] + jnp.sum(oh, axis=1, keepdims=True)

    lane_row = lax.broadcasted_iota(I32, (LANES - EXT_BATCH0, tq), 0)
    onehot_b = jnp.where(lane_row == bidx, 1.0, 0.0)
    info = jnp.concatenate([w_lo, w_hi, jnp.zeros((EXT_BATCH0 - 2, tq), F32), onehot_b], axis=0)
    return rank, info.T


def _mixer_kernel(sinks_ref, x_ref, mod_ref, ln1pre_ref, ln1post_ref, ln2pre_ref, anorm_ref, hnorm_ref,
                  lb_ref, win_ref, wout_ref, wrhi_ref, wrlo_ref, br_ref, bias_ref,
                  rows_hbm, info_ref, cnt_ref,
                  kprev_ref, vprev_ref, st_ref, carry_ref, s_scr, m_scr, p_scr, u_scr, stb_scr,
                  h2_scr, proj_scr, rowbuf, rsem, *, tiles_per_seq):
    s = pl.program_id(0)
    n_tiles = pl.num_programs(0) - 1
    tq = x_ref.shape[0]
    t = lax.rem(jnp.minimum(s, n_tiles - 1), tiles_per_seq)
    slot = lax.rem(s, 2)
    bits = lambda a: pltpu.bitcast(a, U32)

    def row_copy(buf_slot, tile):
        return pltpu.make_async_copy(rowbuf.at[buf_slot], rows_hbm.at[pl.ds(pl.multiple_of(tile * tq, tq), tq), 0],
                                     rsem.at[buf_slot])

    @pl.when(s == 0)
    def _():
        carry_ref[...] = jnp.zeros_like(carry_ref)
        h2_scr[...] = jnp.zeros_like(h2_scr)

    @pl.when(t == 0)
    def _():
        st_ref[...] = jnp.zeros_like(st_ref)
        kprev_ref[...] = jnp.zeros_like(kprev_ref)
        vprev_ref[...] = jnp.zeros_like(vprev_ref)

    x = x_ref[...]
    mod = mod_ref[0]
    sh1, sc1, ga1, sh2, sc2 = mod[0:1], mod[1:2], mod[2:3], mod[3:4], mod[4:5]
    prev = jnp.maximum(s - 1, 0)

    @pl.when(s <= n_tiles)
    def _():
        bucket, w_lo, w_hi = _route_topk(h2_scr[...], wrhi_ref[...], wrlo_ref[...], br_ref[...])
        h = _rms(x) * (ln1pre_ref[...] * (1.0 + sc1)) + sh1
        proj_scr[...] = _dot(h.astype(BF16), win_ref[...])
        live = jnp.where(s >= 1, 1.0, 0.0)
        rank, ext = _route_rank(bucket, w_lo, w_hi, carry_ref, prev // tiles_per_seq, live)
        rowbuf[1 - slot, :, OFF_EXT:ROW_W] = bits(ext)
        info_ref[0] = jnp.concatenate([bucket, rank, jnp.zeros((6, tq), F32)], axis=0).astype(I32)
        cnt_ref[...] = jnp.broadcast_to(carry_ref[...], cnt_ref.shape)

    proj = proj_scr[...]
    scores, exps, values = _attention(proj, kprev_ref, vprev_ref, sinks_ref, bias_ref, t, s_scr, m_scr, p_scr)
    scores()
    exps()
    attn = _rms(values()) * anorm_ref[...]

    lbr = lb_ref[...]
    le = jnp.exp(lbr - jnp.max(lbr, axis=0, keepdims=True))
    lb = le[0:1] / jnp.sum(le, axis=0, keepdims=True)
    hg = _hgrn2(proj, lb, st_ref, hnorm_ref[...], u_scr, stb_scr)

    mix = _dot(jnp.concatenate([attn, hg], axis=1).astype(BF16), wout_ref[...])
    x1 = x + _rms(mix) * (ga1 * ln1post_ref[...])

    h2 = _rms(x1) * (ln2pre_ref[...] * (1.0 + sc2)) + sh2
    h2r = h2.astype(BF16).astype(F32)

    @pl.when(s >= 2)
    def _():
        row_copy(slot, s - 2).wait()

    rowbuf[slot, :, 0:D_MODEL] = bits(x1)
    rowbuf[slot, :, OFF_H2P:OFF_EXT] = ((bits(h2r[:, 0:H2P_W]) >> 16)
                                        | (bits(h2r[:, H2P_W:D_MODEL]) & jnp.uint32(0xFFFF0000)))
    h2_scr[...] = h2

    @pl.when(s >= 1)
    def _():
        row_copy(1 - slot, prev).start()

    @pl.when(s == n_tiles)
    def _():
        row_copy(1 - slot, prev).wait()


def _attn_bias():
    qi = np.arange(WINDOW)[:, None]
    kj = np.arange(2 * WINDOW)[None, :]
    dist = qi + WINDOW - kj
    in_win = (dist >= 0) & (dist < WINDOW)
    slopes = 2.0 ** (-8.0 * (np.arange(ATTN_HEADS) + 1.0) / ATTN_HEADS)
    b = np.where(in_win[None], -slopes[:, None, None] * dist[None] * LOG2E, NEG)
    b_first = np.where((kj >= WINDOW)[None], b, NEG)
    return jnp.asarray(np.stack([b, b_first]).astype(np.float32))


def _mixer(x2, mod3, sinks, ln1pre, ln1post, ln2pre, anorm, hnorm, lb, win, wout, wr_hi, wr_lo, br,
           bsz, seq):
    bias = _attn_bias()
    n = bsz * seq
    nt = seq // TQ
    n_tiles = bsz * nt
    n_sc = (TQ // WINDOW) * ATTN_HEADS
    cur = lambda s: jnp.minimum(s, n_tiles - 1)
    const = lambda s: (0, 0)
    full = lambda a: pl.BlockSpec(a.shape, const)
    return pl.pallas_call(
        functools.partial(_mixer_kernel, tiles_per_seq=nt),
        grid=(n_tiles + 1,),
        in_specs=[pl.BlockSpec(memory_space=pltpu.SMEM),
                  pl.BlockSpec((TQ, D_MODEL), lambda s: (cur(s), 0)),
                  pl.BlockSpec((1, 6, D_MODEL), lambda s: (cur(s) // nt, 0, 0)),
                  full(ln1pre), full(ln1post), full(ln2pre), full(anorm), full(hnorm), full(lb),
                  full(win), full(wout), full(wr_hi), full(wr_lo), full(br),
                  pl.BlockSpec(bias.shape, lambda s: (0, 0, 0, 0))],
        out_specs=[pl.BlockSpec(memory_space=pl.ANY),
                   pl.BlockSpec((1, 8, TQ), lambda s: (jnp.maximum(s - 1, 0), 0, 0)),
                   pl.BlockSpec((LANES, LANES), const)],
        out_shape=[jax.ShapeDtypeStruct((n, 1, ROW_W), U32),
                   jax.ShapeDtypeStruct((n // TQ, 8, TQ), I32),
                   jax.ShapeDtypeStruct((LANES, LANES), F32)],
        scratch_shapes=[pltpu.VMEM((WINDOW, KV_W), F32),
                        pltpu.VMEM((WINDOW, KV_W), F32),
                        pltpu.VMEM((HG_DIM, HG_W), F32),
                        pltpu.VMEM((LANES, 1), F32),
                        pltpu.VMEM((n_sc, WINDOW, 2 * WINDOW), F32),
                        pltpu.VMEM((n_sc, WINDOW, 1), F32),
                        pltpu.VMEM((n_sc, WINDOW, 2 * WINDOW), BF16),
                        pltpu.VMEM((TQ // HG_CHUNK, HG_DIM, HG_W), F32),
                        pltpu.VMEM((TQ // HG_CHUNK, HG_DIM, HG_W), BF16),
                        pltpu.VMEM((TQ, D_MODEL), F32),
                        pltpu.VMEM((TQ, IN_W), F32),
                        pltpu.VMEM((2, TQ, ROW_W), U32),
                        pltpu.SemaphoreType.DMA((2,))],
        compiler_params=pltpu.CompilerParams(dimension_semantics=("arbitrary",),
                                             vmem_limit_bytes=VMEM_LIMIT),
        name="mixer",
    )(sinks, x2, mod3, ln1pre, ln1post, ln2pre, anorm, hnorm, lb, win, wout, wr_hi, wr_lo, br, bias)


PERM_STEPS = 8
PERM_ROWS = 8
PERM_UNROLL = 16


def _perm_kernel(rs_ref, cnt_ref, bucket_ref, rank_ref, perm_ref, pos_vmem, pos_smem, sem):
    pid = pl.program_id(0)
    rows, cols = pos_vmem.shape

    b = bucket_ref[0]
    start = jnp.zeros_like(b)
    for k in range(N_BUCKETS):
        start = jnp.where(b == k, rs_ref[k], start)
    pos_vmem[...] = start + rank_ref[0]
    copies = [pltpu.make_async_copy(pos_vmem.at[r], pos_smem.at[pl.ds(r * cols, cols)], sem) for r in range(rows)]
    for cp in copies:
        cp.start()

    @pl.when(pid == 0)
    def _():
        def per_bucket(k, carry):
            first = rs_ref[k]
            cnt = cnt_ref[k]

            def pad(r, c2):
                perm_ref[first + r] = 0
                return c2

            lax.fori_loop(cnt, ((cnt + TM - 1) // TM) * TM, pad, 0)
            return carry

        lax.fori_loop(0, N_BUCKETS, per_bucket, 0)

        def tail(blk, carry):
            for u in range(PERM_UNROLL):
                perm_ref[blk * PERM_UNROLL + u] = 0
            return carry

        lax.fori_loop(rs_ref[N_BUCKETS] // PERM_UNROLL, perm_ref.shape[0] // PERM_UNROLL, tail, 0)

    for cp in copies:
        cp.wait()
    base = pid * (rows * cols)

    def body(j, carry):
        i0 = j * PERM_UNROLL
        positions = [pos_smem[i0 + u] for u in range(PERM_UNROLL)]
        for u in range(PERM_UNROLL):
            perm_ref[positions[u]] = base + i0 + u
        return carry

    lax.fori_loop(0, rows * cols // PERM_UNROLL, body, 0)


def _perm(row_start, counts, bucket, rank, n_rows):
    n = bucket.shape[0]
    cols = n // (PERM_STEPS * PERM_ROWS)
    assert n % (PERM_STEPS * PERM_ROWS * PERM_UNROLL) == 0
    chunked = lambda a: a.reshape(PERM_STEPS, PERM_ROWS, cols)
    chunk_spec = pl.BlockSpec((1, PERM_ROWS, cols), lambda i: (i, 0, 0))
    return pl.pallas_call(
        _perm_kernel,
        grid=(PERM_STEPS,),
        in_specs=[pl.BlockSpec(memory_space=pltpu.SMEM),
                  pl.BlockSpec(memory_space=pltpu.SMEM),
                  chunk_spec, chunk_spec],
        out_specs=pl.BlockSpec(memory_space=pltpu.SMEM),
        out_shape=jax.ShapeDtypeStruct((n_rows,), I32),
        scratch_shapes=[pltpu.VMEM((PERM_ROWS, cols), I32), pltpu.SMEM((PERM_ROWS * cols,), I32),
                        pltpu.SemaphoreType.DMA(())],
        compiler_params=pltpu.CompilerParams(dimension_semantics=("arbitrary",)),
        name="perm",
    )(row_start, counts, chunked(bucket), chunked(rank))


GATHER_DEPTH = 5
OUT_SLOTS = 3


SCHED_EXPERT, SCHED_RUN_START, SCHED_SLOT, SCHED_NEXT_EXPERT = range(4)


def _moe_kernel(nt_ref, nv_ref, sched_ref, perm_ref,
                rows_hbm, gain_ref, wg_hbm, wu_hbm, wd_hbm,
                out_hbm, xbuf, x2d, obuf, wgu_buf, wd_buf, gsem, ssem, wsem):
    i = pl.program_id(0)
    nt = nt_ref[0]
    n_steps = pl.num_programs(0)
    last_tile = n_steps - 1
    sched = lambda field, side: sched_ref[(field * 2 + side) * n_steps + i]

    def weight_copies(side, expert, slot):
        return (pltpu.make_async_copy(wg_hbm.at[expert], wgu_buf.at[side, slot, :, 0:FF], wsem.at[side, slot]),
                pltpu.make_async_copy(wu_hbm.at[expert], wgu_buf.at[side, slot, :, FF:2 * FF], wsem.at[side, slot]),
                pltpu.make_async_copy(wd_hbm.at[expert], wd_buf.at[side, slot], wsem.at[side, slot]))
    nbuf = xbuf.shape[0] // TM

    def start_gather(tile, pred):
        sl = lax.rem(tile, nbuf)
        base = jnp.minimum(tile, last_tile) * TM
        for r in range(TM):
            @pl.when(pred)
            def _():
                tok = perm_ref[base + r]
                pltpu.make_async_copy(rows_hbm.at[tok], xbuf.at[sl * TM + r], gsem.at[sl]).start()

    def wait_gather(tile):
        sl = lax.rem(tile, nbuf)
        pltpu.make_async_copy(rows_hbm.at[pl.ds(0, TM)], xbuf.at[pl.ds(sl * TM, TM)], gsem.at[sl]).wait()

    def wait_scatter(sl, nv):
        @pl.when(nv == TM)
        def _():
            pltpu.make_async_copy(obuf.at[sl], out_hbm.at[pl.ds(0, TM)], ssem.at[sl]).wait()

        @pl.when(nv < TM)
        def _():
            def one(r, carry):
                pltpu.make_async_copy(obuf.at[sl, pl.ds(0, 1)], out_hbm.at[pl.ds(0, 1)], ssem.at[sl]).wait()
                return carry

            lax.fori_loop(0, nv, one, 0)

    def compute(xb, slot_a, slot_b):
        x1 = pltpu.bitcast(xb[:, 0:D_MODEL], F32)
        hp = xb[:, OFF_H2P:OFF_EXT]
        h2a = pltpu.bitcast(hp << 16, F32).astype(BF16)
        h2b = pltpu.bitcast(hp & jnp.uint32(0xFFFF0000), F32).astype(BF16)
        ext = pltpu.bitcast(xb[:, OFF_EXT:ROW_W], F32)
        w_lo, w_hi = ext[:, 0:1], ext[:, 1:2]
        ga2 = jnp.zeros((TM, D_MODEL), F32)
        for k in range(gain_ref.shape[0]):
            ga2 = jnp.where(ext[:, EXT_BATCH0 + k:EXT_BATCH0 + k + 1] > 0.5, gain_ref[k:k + 1, :], ga2)

        def hidden(side, slot, w):
            gu = (_dot(h2a, wgu_buf[side, slot, 0:H2P_W])
                  + _dot(h2b, wgu_buf[side, slot, H2P_W:D_MODEL]))
            hg, hu = gu[:, 0:FF], gu[:, FF:2 * FF]
            return (w * (_silu(hg) * hu)).astype(BF16)

        act = jnp.concatenate([hidden(0, slot_a, w_lo), hidden(1, slot_b, w_hi)], axis=1)
        wd = jnp.concatenate([wd_buf[0, slot_a], wd_buf[1, slot_b]], axis=0)
        y = _dot(act, wd)
        return x1 + ga2 * _rms(y)

    @pl.when(i == 0)
    def _():
        for side in range(2):
            for cp in weight_copies(side, sched(SCHED_EXPERT, side), 0):
                cp.start()
        for d in range(GATHER_DEPTH):
            start_gather(d, d < nt)

    @pl.when(i < nt)
    def _():
        nv = nv_ref[i]
        osl = lax.rem(i, OUT_SLOTS)
        slots = []
        for side in range(2):
            slot = sched(SCHED_SLOT, side)
            slots.append(slot)

            @pl.when(sched(SCHED_RUN_START, side) == 1)
            def _():
                for cp in weight_copies(side, sched(SCHED_EXPERT, side), slot):
                    cp.wait()
                nxt = sched(SCHED_NEXT_EXPERT, side)

                @pl.when(nxt >= 0)
                def _():
                    for cp in weight_copies(side, nxt, 1 - slot):
                        cp.start()

        wait_gather(i)

        @pl.when(i >= OUT_SLOTS)
        def _():
            wait_scatter(osl, nv_ref[jnp.maximum(i - OUT_SLOTS, 0)])

        start_gather(i + GATHER_DEPTH, i + GATHER_DEPTH < nt)
        x2d[...] = xbuf[pl.ds(lax.rem(i, nbuf) * TM, TM)].reshape(TM, ROW_W)
        result = compute(x2d[...], *slots)

        def scatter_row(k, r):
            tok = perm_ref[i * TM + r]
            pltpu.make_async_copy(obuf.at[k, pl.ds(r, 1)], out_hbm.at[pl.ds(tok, 1)], ssem.at[k]).start()

        for k in range(OUT_SLOTS):
            @pl.when((osl == k) & (nv == TM))
            def _():
                obuf[k] = result
                for r in range(TM):
                    scatter_row(k, r)

            @pl.when((osl == k) & (nv < TM))
            def _():
                obuf[k] = result
                for r in range(TM):
                    pl.when(r < nv)(functools.partial(scatter_row, k, r))

        @pl.when(i == nt - 1)
        def _():
            wait_scatter(osl, nv)
            for back in range(1, OUT_SLOTS):
                @pl.when(i >= back)
                def _():
                    wait_scatter(lax.rem(i - back + OUT_SLOTS, OUT_SLOTS), nv_ref[jnp.maximum(i - back, 0)])


def _moe(rows, gain, wg, wu, wd, nt, nv, sched, perm, n_tiles):
    n = rows.shape[0]
    const2 = lambda i, *_: (0, 0)
    grid_spec = pltpu.PrefetchScalarGridSpec(
        num_scalar_prefetch=4,
        grid=(n_tiles,),
        in_specs=[pl.BlockSpec(memory_space=pl.ANY),
                  pl.BlockSpec(gain.shape, const2),
                  pl.BlockSpec(memory_space=pl.ANY),
                  pl.BlockSpec(memory_space=pl.ANY),
                  pl.BlockSpec(memory_space=pl.ANY)],
        out_specs=pl.BlockSpec(memory_space=pl.ANY),
        scratch_shapes=[pltpu.VMEM(((GATHER_DEPTH + 1) * TM, 1, ROW_W), U32),
                        pltpu.VMEM((TM, ROW_W), U32),
                        pltpu.VMEM((OUT_SLOTS, TM, D_MODEL), F32),
                        pltpu.VMEM((2, 2, D_MODEL, 2 * FF), BF16),
                        pltpu.VMEM((2, 2, FF, D_MODEL), BF16),
                        pltpu.SemaphoreType.DMA((GATHER_DEPTH + 1,)),
                        pltpu.SemaphoreType.DMA((OUT_SLOTS,)),
                        pltpu.SemaphoreType.DMA((2, 2))],
    )
    return pl.pallas_call(
        _moe_kernel,
        grid_spec=grid_spec,
        out_shape=jax.ShapeDtypeStruct((n, D_MODEL), F32),
        compiler_params=pltpu.CompilerParams(dimension_semantics=("arbitrary",),
                                             vmem_limit_bytes=VMEM_LIMIT),
        name="moe",
    )(nt, nv, sched, perm, rows, gain, wg, wu, wd)


def _weight_schedule(ea, eb):
    n_tiles = ea.shape[0]
    tid = jnp.arange(n_tiles, dtype=I32)
    fields = [[], [], [], []]
    for e in (ea, eb):
        start = jnp.concatenate([jnp.ones((1,), I32), (e[1:] != e[:-1]).astype(I32)])
        slot = (jnp.cumsum(start) - 1) % 2
        start_idx = jnp.where(start == 1, tid, n_tiles)
        next_start = jnp.concatenate([lax.cummin(start_idx, reverse=True)[1:], jnp.full((1,), n_tiles, I32)])
        nxt = jnp.where(next_start < n_tiles, e[jnp.minimum(next_start, n_tiles - 1)], -1)
        for f, v in zip(fields, (e, start, slot, nxt)):
            f.append(v.astype(I32))
    return jnp.concatenate([v for f in fields for v in f])


def kernel(x, c, ln1_pre, ln1_post, ln2_pre, ln2_post, w_ada, b_ada, w_in, attn_sinks, attn_out_norm,
           hgrn_lb, hgrn_out_norm, w_out, w_router_group, b_router_group, w_router_expert,
           b_router_expert, w_exp_gate, w_exp_up, w_exp_down):
    bsz, seq, d = x.shape
    assert d == D_MODEL and seq % TQ == 0 and w_ada.shape[0] == 1 and hgrn_lb.shape[0] == 2
    assert bsz <= LANES - EXT_BATCH0 and (bsz * seq) % TM == 0
    n = bsz * seq

    mod = _ada(c, w_ada[0], b_ada[0])
    mod3 = mod.reshape(bsz, 6, d)

    wr = jnp.concatenate([w_router_group[0], jnp.zeros((d, GROUP_ROWS - N_GROUPS), F32), w_router_expert[0],
                          jnp.zeros((d, LANES - ROUTER_ROWS), F32)], axis=1)
    br = jnp.concatenate([b_router_group[0], jnp.full((GROUP_ROWS - N_GROUPS,), NEG, F32), b_router_expert[0]])
    wr_hi = wr.astype(BF16)
    wr_lo = (wr - wr_hi.astype(F32)).astype(BF16)

    x1ext, info, cnt = _mixer(
        x.reshape(n, d), mod3, attn_sinks[0], ln1_pre, ln1_post, ln2_pre, attn_out_norm, hgrn_out_norm,
        hgrn_lb, w_in[0].astype(BF16), w_out[0].astype(BF16), wr_hi, wr_lo, br.reshape(ROUTER_ROWS, 1),
        bsz, seq)

    n_tiles = n // TM + N_BUCKETS
    counts = cnt[:N_BUCKETS, 0].astype(I32)
    tiles_per = (counts + TM - 1) // TM
    tile_end = jnp.cumsum(tiles_per)
    tile_start = tile_end - tiles_per
    nt = tile_end[-1]
    bucket = info[:, 0, :].reshape(n)
    rank = info[:, 1, :].reshape(n)
    tid = jnp.arange(n_tiles, dtype=I32)[None, :]
    member = (tid >= tile_start[:, None]) & (tid < tile_end[:, None])
    pick = lambda per_bucket: jnp.sum(jnp.where(member, per_bucket, 0), axis=0).astype(I32)
    bidx = np.arange(N_BUCKETS, dtype=np.int32)
    ea_of = jnp.asarray((bidx // N_PAIRS) * EPG + _PAIR_A[bidx % N_PAIRS])[:, None]
    eb_of = jnp.asarray((bidx // N_PAIRS) * EPG + _PAIR_B[bidx % N_PAIRS])[:, None]
    last_used = jnp.arange(N_BUCKETS)[:, None] == jnp.max(jnp.where(tiles_per > 0, jnp.arange(N_BUCKETS), 0))
    unused = tid[0] >= nt
    nv = pick(jnp.clip(counts[:, None] - (tid - tile_start[:, None]) * TM, 0, TM))
    ea = jnp.where(unused, jnp.sum(jnp.where(last_used, ea_of, 0)), pick(ea_of)).astype(I32)
    eb = jnp.where(unused, jnp.sum(jnp.where(last_used, eb_of, 0)), pick(eb_of)).astype(I32)

    pad128 = lambda a: jnp.concatenate([a, jnp.zeros((LANES - a.shape[0],), I32)])
    row_start = jnp.concatenate([tile_start, nt.reshape(1)]) * TM
    perm = _perm(pad128(row_start), pad128(counts), bucket, rank, n_tiles * TM)

    wg, wu, wd = w_exp_gate[0].astype(BF16), w_exp_up[0].astype(BF16), w_exp_down[0].astype(BF16)
    gain = mod3[:, 5, :] * ln2_post
    out = _moe(x1ext, gain, wg, wu, wd, nt.reshape(1), nv, _weight_schedule(ea, eb), perm, n_tiles)
    return out.reshape(bsz, seq, d)
```

```python
import functools

import numpy as np
import jax
import jax.numpy as jnp
from jax import lax
from jax.experimental import pallas as pl
from jax.experimental.pallas import tpu as pltpu

F32 = jnp.float32
BF16 = jnp.bfloat16
I32 = jnp.int32

D_MODEL = 1024
ATTN_HEADS = 8
HEAD_DIM = 64
WINDOW = 128
ATTN_W = 512
KV_W = 128
HG_HEADS = 4
HG_DIM = 128
HG_W = 512
HG_CHUNK = 32
IN_W = 2816
N_GROUPS = 4
EPG = 8
N_EXPERTS = 32
FF = 256
N_PAIRS = EPG * (EPG - 1) // 2
N_BUCKETS = N_GROUPS * N_PAIRS
EPS = 1e-6
NEG = -1e30
LOG2E = 1.4426950408889634

LANES = 128
H2P_W = D_MODEL // 2
EXT_W = LANES
OFF_H2P = D_MODEL
OFF_EXT = D_MODEL + H2P_W
ROW_W = OFF_EXT + EXT_W
EXT_BATCH0 = 8
U32 = jnp.uint32
SUBLANES = 8
GROUP_ROWS = SUBLANES
ROUTER_ROWS = GROUP_ROWS + N_EXPERTS

TQ = 256
TM = 128
VMEM_LIMIT = 56 * 1024 * 1024

OFF_Q, OFF_K, OFF_V, OFF_HQ, OFF_HF, OFF_HI, OFF_HG = 0, 512, 640, 768, 1280, 1792, 2304

_PAIR_A = np.array([a for a in range(EPG) for b in range(a + 1, EPG)], np.int32)
_PAIR_B = np.array([b for a in range(EPG) for b in range(a + 1, EPG)], np.int32)


def _dot(a, b):
    return jnp.dot(a, b, preferred_element_type=F32)


def _dot_nt(a, b):
    return lax.dot_general(a, b, (((1,), (1,)), ((), ())), preferred_element_type=F32)


def _dot_tn(a, b):
    return lax.dot_general(a, b, (((0,), (0,)), ((), ())), preferred_element_type=F32)


def _split(a):
    hi = a.astype(BF16)
    lo = (a - hi.astype(F32)).astype(BF16)
    return hi, lo


def _rms(x):
    return x * lax.rsqrt(jnp.mean(x * x, axis=-1, keepdims=True) + EPS)


def _silu(x):
    half = 0.5 * x
    return half * jnp.tanh(half) + half


def _ada_kernel(c_ref, w_ref, b_ref, o_ref):
    c = c_ref[...]
    ca = _silu(c)
    c_hi, c_lo = _split(ca)
    w_hi, w_lo = _split(w_ref[...])
    o_ref[...] = _dot(c_hi, w_hi) + _dot(c_lo, w_hi) + _dot(c_hi, w_lo) + b_ref[...]


def _ada(c, w, b):
    bsz, d = c.shape
    n_out = w.shape[1]
    return pl.pallas_call(
        _ada_kernel,
        grid=(n_out // d,),
        in_specs=[pl.BlockSpec((bsz, d), lambda j: (0, 0)),
                  pl.BlockSpec((d, d), lambda j: (0, j)),
                  pl.BlockSpec((1, d), lambda j: (0, j))],
        out_specs=pl.BlockSpec((bsz, d), lambda j: (0, j)),
        out_shape=jax.ShapeDtypeStruct((bsz, n_out), F32),
        compiler_params=pltpu.CompilerParams(dimension_semantics=("arbitrary",),
                                             vmem_limit_bytes=VMEM_LIMIT),
        name="adaln",
    )(c, w, b.reshape(1, n_out))


def _attention(proj, kprev_ref, vprev_ref, sinks_ref, bias_ref, t, s_scr, m_scr, p_scr):
    tq = proj.shape[0]
    q = (proj[:, OFF_Q:OFF_Q + ATTN_W] * (HEAD_DIM ** -0.5 * LOG2E)).astype(BF16)
    kf = jnp.concatenate([kprev_ref[...], proj[:, OFF_K:OFF_K + KV_W]], axis=0)
    vf = jnp.concatenate([vprev_ref[...], proj[:, OFF_V:OFF_V + KV_W]], axis=0)
    kprev_ref[...] = proj[tq - WINDOW:, OFF_K:OFF_K + KV_W]
    vprev_ref[...] = proj[tq - WINDOW:, OFF_V:OFF_V + KV_W]

    lo = lax.broadcasted_iota(I32, kf.shape, 1) < HEAD_DIM
    kr = pltpu.roll(kf, HEAD_DIM, axis=1)
    vr = pltpu.roll(vf, HEAD_DIM, axis=1)

    def variants(a, ar):
        return [[jnp.where(lo, a, 0.0).astype(BF16), jnp.where(lo, 0.0, ar).astype(BF16)],
                [jnp.where(lo, ar, 0.0).astype(BF16), jnp.where(lo, 0.0, a).astype(BF16)]]

    kvar = variants(kf, kr)
    vvar = variants(vf, vr)

    first = jnp.where(t > 0, 0, 1)

    nblk = tq // WINDOW
    idx = lambda j, h: j * ATTN_HEADS + h
    keys = lambda a, j: a[j * WINDOW:(j + 2) * WINDOW]

    def scores():
        for j in range(nblk):
            for h in range(ATTN_HEADS):
                p, par = h // 2, h % 2
                qp = q[j * WINDOW:(j + 1) * WINDOW, p * LANES:(p + 1) * LANES]
                s = _dot_nt(qp, keys(kvar[p // 2][par], j)) + bias_ref[first if j == 0 else 0, h]
                s_scr[idx(j, h)] = s
                m_scr[idx(j, h)] = jnp.maximum(jnp.max(s, axis=-1, keepdims=True), sinks_ref[h] * LOG2E)

    def exps():
        for j in range(nblk):
            for h in range(ATTN_HEADS):
                m = m_scr[idx(j, h)]
                pe = jnp.exp2(s_scr[idx(j, h)] - m)
                p_scr[idx(j, h)] = pe.astype(BF16)
                m_scr[idx(j, h)] = 1.0 / (jnp.sum(pe, axis=-1, keepdims=True) + jnp.exp2(sinks_ref[h] * LOG2E - m))

    def values():
        blocks = []
        for j in range(nblk):
            pairs = []
            for p in range(ATTN_HEADS // 2):
                acc = None
                for par in range(2):
                    h = 2 * p + par
                    o = _dot(p_scr[idx(j, h)], keys(vvar[p // 2][par], j)) * m_scr[idx(j, h)]
                    acc = o if acc is None else acc + o
                pairs.append(acc)
            blocks.append(jnp.concatenate(pairs, axis=1))
        return jnp.concatenate(blocks, axis=0)

    return scores, exps, values


def _hgrn2(proj, lb, st_ref, hnorm, u_scr, stb_scr):
    tq = proj.shape[0]
    nc = tq // HG_CHUNK
    qr = proj[:, OFF_HQ:OFF_HQ + HG_W]
    fr = proj[:, OFF_HF:OFF_HF + HG_W]
    iv = proj[:, OFF_HI:OFF_HI + HG_W]
    gr = proj[:, OFF_HG:OFF_HG + HG_W]
    qh = _silu(qr)
    half_gap = 0.5 * (1.0 - lb)
    f = (lb + half_gap) + half_gap * jnp.tanh(0.5 * fr)
    kk = 1.0 - f
    logf = jnp.log(f)

    rmod = lax.broadcasted_iota(I32, (tq, HG_W), 0) & (HG_CHUNK - 1)
    bc = logf
    s = 1
    while s < HG_CHUNK:
        bc = bc + jnp.where(rmod >= s, pltpu.roll(bc, s, axis=0), 0.0)
        s *= 2

    b3 = bc.reshape(nc, HG_CHUNK, HG_W)
    blast = b3[:, HG_CHUNK - 1:HG_CHUNK, :]
    kend = (kk.reshape(nc, HG_CHUNK, HG_W) * jnp.exp(blast - b3)).reshape(tq, HG_W)
    decay = jnp.exp(blast).reshape(nc, HG_W)
    qdec = (qh * jnp.exp(bc)).astype(BF16)
    kdec = (kk * jnp.exp(-bc)).astype(BF16)
    kend = kend.astype(BF16)
    ivb = iv.astype(BF16)

    ri = lax.broadcasted_iota(I32, (tq, tq), 0)
    ci = lax.broadcasted_iota(I32, (tq, tq), 1)
    cmask = ((ri // HG_CHUNK) == (ci // HG_CHUNK)) & (ri >= ci)

    heads = [slice(hh * HG_DIM, (hh + 1) * HG_DIM) for hh in range(HG_HEADS)]
    chunks = [slice(n * HG_CHUNK, (n + 1) * HG_CHUNK) for n in range(nc)]

    lane_head = lax.broadcasted_iota(I32, (HG_CHUNK, HG_W), 1) // HG_DIM
    for n, rs in enumerate(chunks):
        vstack = jnp.concatenate([ivb[rs, sl] for sl in heads], axis=0)
        kblk = jnp.concatenate([jnp.where(lane_head == hh, kend[rs], 0.0).astype(BF16)
                                for hh in range(HG_HEADS)], axis=0)
        u_scr[n] = _dot_tn(vstack, kblk)

    st = st_ref[...]
    for n in range(nc):
        stb_scr[n] = st.astype(BF16)
        st = st * decay[n:n + 1] + u_scr[n]
    st_ref[...] = st

    outs = []
    for hh, sl in enumerate(heads):
        a = _dot_nt(qdec[:, sl], kdec[:, sl])
        a = jnp.where(cmask, a, 0.0).astype(BF16)
        o_intra = _dot(a, ivb[:, sl])
        inter = [_dot_nt(qdec[rs, sl], stb_scr[n, :, sl]) for n, rs in enumerate(chunks)]
        o = o_intra + jnp.concatenate(inter, axis=0)
        o = _rms(o) * hnorm[:, sl]
        g = gr[:, sl]
        outs.append(o * _silu(g))
    return jnp.concatenate(outs, axis=1)


def _route_topk(h2, wr_hi, wr_lo, br):
    tq = h2.shape[0]
    h_hi, h_lo = _split(h2)
    logits = _dot(h_hi, wr_hi) + _dot(h_lo, wr_hi) + _dot(h_hi, wr_lo)
    lt = logits.T[0:ROUTER_ROWS] + br
    sub = lax.broadcasted_iota(I32, (SUBLANES, tq), 0).astype(F32)
    none = float(SUBLANES)

    gl = lt[0:GROUP_ROWS]
    gm = jnp.max(gl, axis=0, keepdims=True)
    gidx = jnp.min(jnp.where(gl == gm, sub, none), axis=0, keepdims=True)
    g_w = 1.0 / jnp.sum(jnp.exp(gl - gm), axis=0, keepdims=True)

    group_rows = lambda g: lt[GROUP_ROWS + EPG * g:GROUP_ROWS + EPG * (g + 1)]
    es = group_rows(0)
    for g in range(1, N_GROUPS):
        es = jnp.where(gidx == float(g), group_rows(g), es)
    m1 = jnp.max(es, axis=0, keepdims=True)
    i1 = jnp.min(jnp.where(es == m1, sub, none), axis=0, keepdims=True)
    e2 = jnp.where(sub == i1, NEG, es)
    m2 = jnp.max(e2, axis=0, keepdims=True)
    i2 = jnp.min(jnp.where(e2 == m2, sub, none), axis=0, keepdims=True)
    dd = jnp.exp(m2 - m1)
    w1 = g_w / (1.0 + dd)
    w2 = g_w * dd / (1.0 + dd)
    first_low = i1 < i2
    ea = jnp.minimum(i1, i2)
    eb = jnp.maximum(i1, i2)
    w_lo = jnp.where(first_low, w1, w2)
    w_hi = jnp.where(first_low, w2, w1)
    pair = ea * (2.0 * EPG - 1.0 - ea) * 0.5 + (eb - ea - 1.0)
    bucket = gidx * float(N_PAIRS) + pair
    return bucket, w_lo, w_hi


def _route_rank(bucket, w_lo, w_hi, carry_ref, bidx, live):
    tq = bucket.shape[1]
    brow = lax.broadcasted_iota(I32, (LANES, tq), 0).astype(F32)
    onehot = brow == bucket
    oh = jnp.where(onehot, live, 0.0)
    ti = lax.broadcasted_iota(I32, (tq, tq), 0)
    tj = lax.broadcasted_iota(I32, (tq, tq), 1)
    upper = jnp.where(ti < tj, 1.0, 0.0).astype(BF16)
    before = _dot(oh.astype(BF16), upper) + carry_ref[...]
    rank = jnp.sum(jnp.where(onehot, before, 0.0), axis=0, keepdims=True)
    carry_ref[...] = carry_ref[...] + jnp.sum(oh, axis=1, keepdims=True)

    lane_row = lax.broadcasted_iota(I32, (LANES - EXT_BATCH0, tq), 0)
    onehot_b = jnp.where(lane_row == bidx, 1.0, 0.0)
    info = jnp.concatenate([w_lo, w_hi, jnp.zeros((EXT_BATCH0 - 2, tq), F32), onehot_b], axis=0)
    return rank, info.T


def _mixer_kernel(sinks_ref, x_ref, mod_ref, ln1pre_ref, ln1post_ref, ln2pre_ref, anorm_ref, hnorm_ref,
                  lb_ref, win_ref, wout_ref, wrhi_ref, wrlo_ref, br_ref, bias_ref,
                  rows_hbm, info_ref, cnt_ref,
                  kprev_ref, vprev_ref, st_ref, carry_ref, s_scr, m_scr, p_scr, u_scr, stb_scr,
                  h2_scr, proj_scr, rowbuf, rsem, *, tiles_per_seq):
    s = pl.program_id(0)
    n_tiles = pl.num_programs(0) - 1
    tq = x_ref.shape[0]
    t = lax.rem(jnp.minimum(s, n_tiles - 1), tiles_per_seq)
    slot = lax.rem(s, 2)
    bits = lambda a: pltpu.bitcast(a, U32)

    def row_copy(buf_slot, tile):
        return pltpu.make_async_copy(rowbuf.at[buf_slot], rows_hbm.at[pl.ds(pl.multiple_of(tile * tq, tq), tq), 0],
                                     rsem.at[buf_slot])

    @pl.when(s == 0)
    def _():
        carry_ref[...] = jnp.zeros_like(carry_ref)
        h2_scr[...] = jnp.zeros_like(h2_scr)

    @pl.when(t == 0)
    def _():
        st_ref[...] = jnp.zeros_like(st_ref)
        kprev_ref[...] = jnp.zeros_like(kprev_ref)
        vprev_ref[...] = jnp.zeros_like(vprev_ref)

    x = x_ref[...]
    mod = mod_ref[0]
    sh1, sc1, ga1, sh2, sc2 = mod[0:1], mod[1:2], mod[2:3], mod[3:4], mod[4:5]
    prev = jnp.maximum(s - 1, 0)

    @pl.when(s <= n_tiles)
    def _():
        bucket, w_lo, w_hi = _route_topk(h2_scr[...], wrhi_ref[...], wrlo_ref[...], br_ref[...])
        h = _rms(x) * (ln1pre_ref[...] * (1.0 + sc1)) + sh1
        proj_scr[...] = _dot(h.astype(BF16), win_ref[...])
        live = jnp.where(s >= 1, 1.0, 0.0)
        rank, ext = _route_rank(bucket, w_lo, w_hi, carry_ref, prev // tiles_per_seq, live)
        rowbuf[1 - slot, :, OFF_EXT:ROW_W] = bits(ext)
        info_ref[0] = jnp.concatenate([bucket, rank, jnp.zeros((6, tq), F32)], axis=0).astype(I32)
        cnt_ref[...] = jnp.broadcast_to(carry_ref[...], cnt_ref.shape)

    proj = proj_scr[...]
    scores, exps, values = _attention(proj, kprev_ref, vprev_ref, sinks_ref, bias_ref, t, s_scr, m_scr, p_scr)
    scores()
    exps()
    attn = _rms(values()) * anorm_ref[...]

    lbr = lb_ref[...]
    le = jnp.exp(lbr - jnp.max(lbr, axis=0, keepdims=True))
    lb = le[0:1] / jnp.sum(le, axis=0, keepdims=True)
    hg = _hgrn2(proj, lb, st_ref, hnorm_ref[...], u_scr, stb_scr)

    mix = _dot(jnp.concatenate([attn, hg], axis=1).astype(BF16), wout_ref[...])
    x1 = x + _rms(mix) * (ga1 * ln1post_ref[...])

    h2 = _rms(x1) * (ln2pre_ref[...] * (1.0 + sc2)) + sh2
    h2r = h2.astype(BF16).astype(F32)

    @pl.when(s >= 2)
    def _():
        row_copy(slot, s - 2).wait()

    rowbuf[slot, :, 0:D_MODEL] = bits(x1)
    rowbuf[slot, :, OFF_H2P:OFF_EXT] = ((bits(h2r[:, 0:H2P_W]) >> 16)
                                        | (bits(h2r[:, H2P_W:D_MODEL]) & jnp.uint32(0xFFFF0000)))
    h2_scr[...] = h2

    @pl.when(s >= 1)
    def _():
        row_copy(1 - slot, prev).start()

    @pl.when(s == n_tiles)
    def _():
        row_copy(1 - slot, prev).wait()


def _attn_bias():
    qi = np.arange(WINDOW)[:, None]
    kj = np.arange(2 * WINDOW)[None, :]
    dist = qi + WINDOW - kj
    in_win = (dist >= 0) & (dist < WINDOW)
    slopes = 2.0 ** (-8.0 * (np.arange(ATTN_HEADS) + 1.0) / ATTN_HEADS)
    b = np.where(in_win[None], -slopes[:, None, None] * dist[None] * LOG2E, NEG)
    b_first = np.where((kj >= WINDOW)[None], b, NEG)
    return jnp.asarray(np.stack([b, b_first]).astype(np.float32))


def _mixer(x2, mod3, sinks, ln1pre, ln1post, ln2pre, anorm, hnorm, lb, win, wout, wr_hi, wr_lo, br,
           bsz, seq):
    bias = _attn_bias()
    n = bsz * seq
    nt = seq // TQ
    n_tiles = bsz * nt
    n_sc = (TQ // WINDOW) * ATTN_HEADS
    cur = lambda s: jnp.minimum(s, n_tiles - 1)
    const = lambda s: (0, 0)
    full = lambda a: pl.BlockSpec(a.shape, const)
    return pl.pallas_call(
        functools.partial(_mixer_kernel, tiles_per_seq=nt),
        grid=(n_tiles + 1,),
        in_specs=[pl.BlockSpec(memory_space=pltpu.SMEM),
                  pl.BlockSpec((TQ, D_MODEL), lambda s: (cur(s), 0)),
                  pl.BlockSpec((1, 6, D_MODEL), lambda s: (cur(s) // nt, 0, 0)),
                  full(ln1pre), full(ln1post), full(ln2pre), full(anorm), full(hnorm), full(lb),
                  full(win), full(wout), full(wr_hi), full(wr_lo), full(br),
                  pl.BlockSpec(bias.shape, lambda s: (0, 0, 0, 0))],
        out_specs=[pl.BlockSpec(memory_space=pl.ANY),
                   pl.BlockSpec((1, 8, TQ), lambda s: (jnp.maximum(s - 1, 0), 0, 0)),
                   pl.BlockSpec((LANES, LANES), const)],
        out_shape=[jax.ShapeDtypeStruct((n, 1, ROW_W), U32),
                   jax.ShapeDtypeStruct((n // TQ, 8, TQ), I32),
                   jax.ShapeDtypeStruct((LANES, LANES), F32)],
        scratch_shapes=[pltpu.VMEM((WINDOW, KV_W), F32),
                        pltpu.VMEM((WINDOW, KV_W), F32),
                        pltpu.VMEM((HG_DIM, HG_W), F32),
                        pltpu.VMEM((LANES, 1), F32),
                        pltpu.VMEM((n_sc, WINDOW, 2 * WINDOW), F32),
                        pltpu.VMEM((n_sc, WINDOW, 1), F32),
                        pltpu.VMEM((n_sc, WINDOW, 2 * WINDOW), BF16),
                        pltpu.VMEM((TQ // HG_CHUNK, HG_DIM, HG_W), F32),
                        pltpu.VMEM((TQ // HG_CHUNK, HG_DIM, HG_W), BF16),
                        pltpu.VMEM((TQ, D_MODEL), F32),
                        pltpu.VMEM((TQ, IN_W), F32),
                        pltpu.VMEM((2, TQ, ROW_W), U32),
                        pltpu.SemaphoreType.DMA((2,))],
        compiler_params=pltpu.CompilerParams(dimension_semantics=("arbitrary",),
                                             vmem_limit_bytes=VMEM_LIMIT),
        name="mixer",
    )(sinks, x2, mod3, ln1pre, ln1post, ln2pre, anorm, hnorm, lb, win, wout, wr_hi, wr_lo, br, bias)


PERM_STEPS = 8
PERM_ROWS = 8
PERM_UNROLL = 16


def _perm_kernel(rs_ref, cnt_ref, bucket_ref, rank_ref, perm_ref, pos_vmem, pos_smem, sem):
    pid = pl.program_id(0)
    rows, cols = pos_vmem.shape

    b = bucket_ref[0]
    start = jnp.zeros_like(b)
    for k in range(N_BUCKETS):
        start = jnp.where(b == k, rs_ref[k], start)
    pos_vmem[...] = start + rank_ref[0]
    copies = [pltpu.make_async_copy(pos_vmem.at[r], pos_smem.at[pl.ds(r * cols, cols)], sem) for r in range(rows)]
    for cp in copies:
        cp.start()

    @pl.when(pid == 0)
    def _():
        def per_bucket(k, carry):
            first = rs_ref[k]
            cnt = cnt_ref[k]

            def pad(r, c2):
                perm_ref[first + r] = 0
                return c2

            lax.fori_loop(cnt, ((cnt + TM - 1) // TM) * TM, pad, 0)
            return carry

        lax.fori_loop(0, N_BUCKETS, per_bucket, 0)

        def tail(blk, carry):
            for u in range(PERM_UNROLL):
                perm_ref[blk * PERM_UNROLL + u] = 0
            return carry

        lax.fori_loop(rs_ref[N_BUCKETS] // PERM_UNROLL, perm_ref.shape[0] // PERM_UNROLL, tail, 0)

    for cp in copies:
        cp.wait()
    base = pid * (rows * cols)

    def body(j, carry):
        i0 = j * PERM_UNROLL
        positions = [pos_smem[i0 + u] for u in range(PERM_UNROLL)]
        for u in range(PERM_UNROLL):
            perm_ref[positions[u]] = base + i0 + u
        return carry

    lax.fori_loop(0, rows * cols // PERM_UNROLL, body, 0)


def _perm(row_start, counts, bucket, rank, n_rows):
    n = bucket.shape[0]
    cols = n // (PERM_STEPS * PERM_ROWS)
    assert n % (PERM_STEPS * PERM_ROWS * PERM_UNROLL) == 0
    chunked = lambda a: a.reshape(PERM_STEPS, PERM_ROWS, cols)
    chunk_spec = pl.BlockSpec((1, PERM_ROWS, cols), lambda i: (i, 0, 0))
    return pl.pallas_call(
        _perm_kernel,
        grid=(PERM_STEPS,),
        in_specs=[pl.BlockSpec(memory_space=pltpu.SMEM),
                  pl.BlockSpec(memory_space=pltpu.SMEM),
                  chunk_spec, chunk_spec],
        out_specs=pl.BlockSpec(memory_space=pltpu.SMEM),
        out_shape=jax.ShapeDtypeStruct((n_rows,), I32),
        scratch_shapes=[pltpu.VMEM((PERM_ROWS, cols), I32), pltpu.SMEM((PERM_ROWS * cols,), I32),
                        pltpu.SemaphoreType.DMA(())],
        compiler_params=pltpu.CompilerParams(dimension_semantics=("arbitrary",)),
        name="perm",
    )(row_start, counts, chunked(bucket), chunked(rank))


GATHER_DEPTH = 8
OUT_SLOTS = 4


SCHED_EXPERT, SCHED_RUN_START, SCHED_SLOT, SCHED_NEXT_EXPERT = range(4)


def _moe_kernel(nt_ref, nv_ref, sched_ref, perm_ref,
                rows_hbm, gain_ref, wg_hbm, wu_hbm, wd_hbm,
                out_hbm, xbuf, x2d, obuf, wgu_buf, wd_buf, gsem, ssem, wsem):
    i = pl.program_id(0)
    nt = nt_ref[0]
    n_steps = pl.num_programs(0)
    last_tile = n_steps - 1
    sched = lambda field, side: sched_ref[(field * 2 + side) * n_steps + i]

    def weight_copies(side, expert, slot):
        return (pltpu.make_async_copy(wg_hbm.at[expert], wgu_buf.at[side, slot, :, 0:FF], wsem.at[side, slot]),
                pltpu.make_async_copy(wu_hbm.at[expert], wgu_buf.at[side, slot, :, FF:2 * FF], wsem.at[side, slot]),
                pltpu.make_async_copy(wd_hbm.at[expert], wd_buf.at[side, slot], wsem.at[side, slot]))
    nbuf = xbuf.shape[0] // TM

    def start_gather(tile, pred):
        sl = lax.rem(tile, nbuf)
        base = jnp.minimum(tile, last_tile) * TM
        for r in range(TM):
            @pl.when(pred)
            def _():
                tok = perm_ref[base + r]
                pltpu.make_async_copy(rows_hbm.at[tok], xbuf.at[sl * TM + r], gsem.at[sl]).start()

    def wait_gather(tile):
        sl = lax.rem(tile, nbuf)
        pltpu.make_async_copy(rows_hbm.at[pl.ds(0, TM)], xbuf.at[pl.ds(sl * TM, TM)], gsem.at[sl]).wait()

    def wait_scatter(sl, nv):
        @pl.when(nv == TM)
        def _():
            pltpu.make_async_copy(obuf.at[sl], out_hbm.at[pl.ds(0, TM)], ssem.at[sl]).wait()

        @pl.when(nv < TM)
        def _():
            def one(r, carry):
                pltpu.make_async_copy(obuf.at[sl, pl.ds(0, 1)], out_hbm.at[pl.ds(0, 1)], ssem.at[sl]).wait()
                return carry

            lax.fori_loop(0, nv, one, 0)

    def compute(xb, slot_a, slot_b):
        x1 = pltpu.bitcast(xb[:, 0:D_MODEL], F32)
        hp = xb[:, OFF_H2P:OFF_EXT]
        h2a = pltpu.bitcast(hp << 16, F32).astype(BF16)
        h2b = pltpu.bitcast(hp & jnp.uint32(0xFFFF0000), F32).astype(BF16)
        ext = pltpu.bitcast(xb[:, OFF_EXT:ROW_W], F32)
        w_lo, w_hi = ext[:, 0:1], ext[:, 1:2]
        ga2 = jnp.zeros((TM, D_MODEL), F32)
        for k in range(gain_ref.shape[0]):
            ga2 = jnp.where(ext[:, EXT_BATCH0 + k:EXT_BATCH0 + k + 1] > 0.5, gain_ref[k:k + 1, :], ga2)

        def hidden(side, slot, w):
            gu = (_dot(h2a, wgu_buf[side, slot, 0:H2P_W])
                  + _dot(h2b, wgu_buf[side, slot, H2P_W:D_MODEL]))
            hg, hu = gu[:, 0:FF], gu[:, FF:2 * FF]
            return (w * (_silu(hg) * hu)).astype(BF16)

        act = jnp.concatenate([hidden(0, slot_a, w_lo), hidden(1, slot_b, w_hi)], axis=1)
        wd = jnp.concatenate([wd_buf[0, slot_a], wd_buf[1, slot_b]], axis=0)
        y = _dot(act, wd)
        return x1 + ga2 * _rms(y)

    @pl.when(i == 0)
    def _():
        for side in range(2):
            for cp in weight_copies(side, sched(SCHED_EXPERT, side), 0):
                cp.start()
        for d in range(GATHER_DEPTH):
            start_gather(d, d < nt)

    @pl.when(i < nt)
    def _():
        nv = nv_ref[i]
        osl = lax.rem(i, OUT_SLOTS)
        slots = []
        for side in range(2):
            slot = sched(SCHED_SLOT, side)
            slots.append(slot)

            @pl.when(sched(SCHED_RUN_START, side) == 1)
            def _():
                for cp in weight_copies(side, sched(SCHED_EXPERT, side), slot):
                    cp.wait()
                nxt = sched(SCHED_NEXT_EXPERT, side)

                @pl.when(nxt >= 0)
                def _():
                    for cp in weight_copies(side, nxt, 1 - slot):
                        cp.start()

        wait_gather(i)

        @pl.when(i >= OUT_SLOTS)
        def _():
            wait_scatter(osl, nv_ref[jnp.maximum(i - OUT_SLOTS, 0)])

        start_gather(i + GATHER_DEPTH, i + GATHER_DEPTH < nt)
        x2d[...] = xbuf[pl.ds(lax.rem(i, nbuf) * TM, TM)].reshape(TM, ROW_W)
        result = compute(x2d[...], *slots)

        def scatter_row(k, r):
            tok = perm_ref[i * TM + r]
            pltpu.make_async_copy(obuf.at[k, pl.ds(r, 1)], out_hbm.at[pl.ds(tok, 1)], ssem.at[k]).start()

        for k in range(OUT_SLOTS):
            @pl.when((osl == k) & (nv == TM))
            def _():
                obuf[k] = result
                for r in range(TM):
                    scatter_row(k, r)

            @pl.when((osl == k) & (nv < TM))
            def _():
                obuf[k] = result
                for r in range(TM):
                    pl.when(r < nv)(functools.partial(scatter_row, k, r))

        @pl.when(i == nt - 1)
        def _():
            wait_scatter(osl, nv)
            for back in range(1, OUT_SLOTS):
                @pl.when(i >= back)
                def _():
                    wait_scatter(lax.rem(i - back + OUT_SLOTS, OUT_SLOTS), nv_ref[jnp.maximum(i - back, 0)])


def _moe(rows, gain, wg, wu, wd, nt, nv, sched, perm, n_tiles):
    n = rows.shape[0]
    const2 = lambda i, *_: (0, 0)
    grid_spec = pltpu.PrefetchScalarGridSpec(
        num_scalar_prefetch=4,
        grid=(n_tiles,),
        in_specs=[pl.BlockSpec(memory_space=pl.ANY),
                  pl.BlockSpec(gain.shape, const2),
                  pl.BlockSpec(memory_space=pl.ANY),
                  pl.BlockSpec(memory_space=pl.ANY),
                  pl.BlockSpec(memory_space=pl.ANY)],
        out_specs=pl.BlockSpec(memory_space=pl.ANY),
        scratch_shapes=[pltpu.VMEM(((GATHER_DEPTH + 1) * TM, 1, ROW_W), U32),
                        pltpu.VMEM((TM, ROW_W), U32),
                        pltpu.VMEM((OUT_SLOTS, TM, D_MODEL), F32),
                        pltpu.VMEM((2, 2, D_MODEL, 2 * FF), BF16),
                        pltpu.VMEM((2, 2, FF, D_MODEL), BF16),
                        pltpu.SemaphoreType.DMA((GATHER_DEPTH + 1,)),
                        pltpu.SemaphoreType.DMA((OUT_SLOTS,)),
                        pltpu.SemaphoreType.DMA((2, 2))],
    )
    return pl.pallas_call(
        _moe_kernel,
        grid_spec=grid_spec,
        out_shape=jax.ShapeDtypeStruct((n, D_MODEL), F32),
        compiler_params=pltpu.CompilerParams(dimension_semantics=("arbitrary",),
                                             vmem_limit_bytes=VMEM_LIMIT),
        name="moe",
    )(nt, nv, sched, perm, rows, gain, wg, wu, wd)


def _weight_schedule(ea, eb):
    n_tiles = ea.shape[0]
    tid = jnp.arange(n_tiles, dtype=I32)
    fields = [[], [], [], []]
    for e in (ea, eb):
        start = jnp.concatenate([jnp.ones((1,), I32), (e[1:] != e[:-1]).astype(I32)])
        slot = (jnp.cumsum(start) - 1) % 2
        start_idx = jnp.where(start == 1, tid, n_tiles)
        next_start = jnp.concatenate([lax.cummin(start_idx, reverse=True)[1:], jnp.full((1,), n_tiles, I32)])
        nxt = jnp.where(next_start < n_tiles, e[jnp.minimum(next_start, n_tiles - 1)], -1)
        for f, v in zip(fields, (e, start, slot, nxt)):
            f.append(v.astype(I32))
    return jnp.concatenate([v for f in fields for v in f])


def kernel(x, c, ln1_pre, ln1_post, ln2_pre, ln2_post, w_ada, b_ada, w_in, attn_sinks, attn_out_norm,
           hgrn_lb, hgrn_out_norm, w_out, w_router_group, b_router_group, w_router_expert,
           b_router_expert, w_exp_gate, w_exp_up, w_exp_down):
    bsz, seq, d = x.shape
    assert d == D_MODEL and seq % TQ == 0 and w_ada.shape[0] == 1 and hgrn_lb.shape[0] == 2
    assert bsz <= LANES - EXT_BATCH0 and (bsz * seq) % TM == 0
    n = bsz * seq

    mod = _ada(c, w_ada[0], b_ada[0])
    mod3 = mod.reshape(bsz, 6, d)

    wr = jnp.concatenate([w_router_group[0], jnp.zeros((d, GROUP_ROWS - N_GROUPS), F32), w_router_expert[0],
                          jnp.zeros((d, LANES - ROUTER_ROWS), F32)], axis=1)
    br = jnp.concatenate([b_router_group[0], jnp.full((GROUP_ROWS - N_GROUPS,), NEG, F32), b_router_expert[0]])
    wr_hi = wr.astype(BF16)
    wr_lo = (wr - wr_hi.astype(F32)).astype(BF16)

    x1ext, info, cnt = _mixer(
        x.reshape(n, d), mod3, attn_sinks[0], ln1_pre, ln1_post, ln2_pre, attn_out_norm, hgrn_out_norm,
        hgrn_lb, w_in[0].astype(BF16), w_out[0].astype(BF16), wr_hi, wr_lo, br.reshape(ROUTER_ROWS, 1),
        bsz, seq)

    n_tiles = n // TM + N_BUCKETS
    counts = cnt[:N_BUCKETS, 0].astype(I32)
    tiles_per = (counts + TM - 1) // TM
    tile_end = jnp.cumsum(tiles_per)
    tile_start = tile_end - tiles_per
    nt = tile_end[-1]
    bucket = info[:, 0, :].reshape(n)
    rank = info[:, 1, :].reshape(n)
    tid = jnp.arange(n_tiles, dtype=I32)[None, :]
    member = (tid >= tile_start[:, None]) & (tid < tile_end[:, None])
    pick = lambda per_bucket: jnp.sum(jnp.where(member, per_bucket, 0), axis=0).astype(I32)
    bidx = np.arange(N_BUCKETS, dtype=np.int32)
    ea_of = jnp.asarray((bidx // N_PAIRS) * EPG + _PAIR_A[bidx % N_PAIRS])[:, None]
    eb_of = jnp.asarray((bidx // N_PAIRS) * EPG + _PAIR_B[bidx % N_PAIRS])[:, None]
    last_used = jnp.arange(N_BUCKETS)[:, None] == jnp.max(jnp.where(tiles_per > 0, jnp.arange(N_BUCKETS), 0))
    unused = tid[0] >= nt
    nv = pick(jnp.clip(counts[:, None] - (tid - tile_start[:, None]) * TM, 0, TM))
    ea = jnp.where(unused, jnp.sum(jnp.where(last_used, ea_of, 0)), pick(ea_of)).astype(I32)
    eb = jnp.where(unused, jnp.sum(jnp.where(last_used, eb_of, 0)), pick(eb_of)).astype(I32)

    pad128 = lambda a: jnp.concatenate([a, jnp.zeros((LANES - a.shape[0],), I32)])
    row_start = jnp.concatenate([tile_start, nt.reshape(1)]) * TM
    perm = _perm(pad128(row_start), pad128(counts), bucket, rank, n_tiles * TM)

    wg, wu, wd = w_exp_gate[0].astype(BF16), w_exp_up[0].astype(BF16), w_exp_down[0].astype(BF16)
    gain = mod3[:, 5, :] * ln2_post
    out = _moe(x1ext, gain, wg, wu, wd, nt.reshape(1), nv, _weight_schedule(ea, eb), perm, n_tiles)
    return out.reshape(bsz, seq, d)
```
